```python
import jax, jax.numpy as jnp
from jax import lax
import numpy as np

D_MODEL = 1024
BATCH = 8
SEQ = 4096
DEPTH = 1

CHUNK = 64
Q_BLOCK = 128
MLA_HEADS = 8
QK_NOPE_DIM = 64
QK_ROPE_DIM = 32
QK_HEAD_DIM = QK_NOPE_DIM + QK_ROPE_DIM
V_HEAD_DIM = 64
Q_LORA_RANK = 384
KV_LORA_RANK = 128
ROPE_THETA = 10000.0
ATTN_WIDTH = MLA_HEADS * V_HEAD_DIM
CONV_WIDTH = D_MODEL - ATTN_WIDTH
CONV_KERNEL = 31
IN_WIDTH = Q_LORA_RANK + KV_LORA_RANK + QK_ROPE_DIM + 2 * CONV_WIDTH
N_EXPERTS = 32
TOP_K = 4
D_FF = D_MODEL
SWIGLU_ALPHA = 1.702
SWIGLU_LIMIT = 7.0
MOE_BLOCK = 256
EPS = 1e-6

kernel_name = "hybrid_mla_conformer_moe_block"


def rms_norm(x, g):
    xf = x.astype(jnp.float32)
    y = xf * lax.rsqrt(jnp.mean(xf * xf, axis=-1, keepdims=True) + EPS)
    return (y * g.astype(jnp.float32)).astype(x.dtype)


def layer_norm(x, g, b):
    xf = x.astype(jnp.float32)
    mu = jnp.mean(xf, axis=-1, keepdims=True)
    xc = xf - mu
    y = xc * lax.rsqrt(jnp.mean(xc * xc, axis=-1, keepdims=True) + EPS)
    return (y * g.astype(jnp.float32) + b.astype(jnp.float32)).astype(x.dtype)


def rope_tables(positions):
    inv_freq = 1.0 / (ROPE_THETA ** (jnp.arange(0, QK_ROPE_DIM, 2, dtype=jnp.float32) / QK_ROPE_DIM))
    ang = positions.astype(jnp.float32)[..., None] * inv_freq
    return jnp.cos(ang)[:, :, None, :], jnp.sin(ang)[:, :, None, :]


def rope_tail(t, cos, sin):
    nope, pe = t[..., :QK_NOPE_DIM], t[..., QK_NOPE_DIM:].astype(jnp.float32)
    half = QK_ROPE_DIM // 2
    x1, x2 = pe[..., :half], pe[..., half:]
    rot = jnp.concatenate([x1 * cos - x2 * sin, x2 * cos + x1 * sin], axis=-1)
    return jnp.concatenate([nope, rot.astype(t.dtype)], axis=-1)


def attend_block(qb, kb, vb, q0):
    scale = QK_HEAD_DIM ** -0.5
    s = jnp.einsum('bqhd,bkhd->bhqk', qb, kb, preferred_element_type=jnp.float32) * scale
    q_chunk = (q0 + jnp.arange(qb.shape[1])) // CHUNK
    k_chunk = jnp.arange(kb.shape[1]) // CHUNK
    s = jnp.where(k_chunk[None, :] <= q_chunk[:, None], s, -jnp.inf)
    p = jax.nn.softmax(s, axis=-1).astype(vb.dtype)
    return jnp.einsum('bhqk,bkhd->bqhd', p, vb)


def mla(c_q, c_kv, k_pe, cos, sin, q_latent_g, w_uq, kv_latent_g, w_ukv, q_head_g, k_head_g):
    B, S, _ = c_q.shape
    q = (rms_norm(c_q, q_latent_g) @ w_uq).reshape(B, S, MLA_HEADS, QK_HEAD_DIM)
    kv = (rms_norm(c_kv, kv_latent_g) @ w_ukv).reshape(B, S, MLA_HEADS, QK_NOPE_DIM + V_HEAD_DIM)
    k_nope, v = kv[..., :QK_NOPE_DIM], kv[..., QK_NOPE_DIM:]
    k = jnp.concatenate(
        [k_nope, jnp.broadcast_to(k_pe[:, :, None, :], (B, S, MLA_HEADS, QK_ROPE_DIM))], axis=-1)
    q = rope_tail(rms_norm(q, q_head_g), cos, sin)
    k = rope_tail(rms_norm(k, k_head_g), cos, sin)
    outs = []
    for i in range(S // Q_BLOCK):
        q0, end = i * Q_BLOCK, (i + 1) * Q_BLOCK
        outs.append(attend_block(q[:, q0:end], k[:, :end], v[:, :end], q0))
    return jnp.concatenate(outs, axis=1).reshape(B, S, ATTN_WIDTH)


def conformer_conv(u, dw_w, dw_b, ln_g, ln_b):
    a, g = u[..., :CONV_WIDTH], u[..., CONV_WIDTH:]
    y = a * jax.nn.sigmoid(g)
    y = lax.conv_general_dilated(
        y, dw_w[:, None, :].astype(y.dtype), window_strides=(1,),
        padding=[(CONV_KERNEL - 1, 0)], dimension_numbers=('NWC', 'WIO', 'NWC'),
        feature_group_count=CONV_WIDTH) + dw_b
    return jax.nn.silu(layer_norm(y, ln_g, ln_b))


def moe(h, router_w, router_b, w_gate_up, b_gate_up, w_down, b_down):
    B, S, D = h.shape
    T = B * S
    hf = h.reshape(T, D)
    logits = (hf @ router_w).astype(jnp.float32) + router_b.astype(jnp.float32)
    top_val, top_idx = lax.top_k(logits, TOP_K)
    gates = jax.nn.softmax(top_val, axis=-1)
    n_assign = T * TOP_K
    flat_e = top_idx.reshape(-1).astype(jnp.int32)
    flat_tok = jnp.arange(n_assign, dtype=jnp.int32) // TOP_K
    order = jnp.argsort(flat_e)
    se, stok, sw = flat_e[order], flat_tok[order], gates.reshape(-1)[order]
    counts = jnp.bincount(flat_e, length=N_EXPERTS)
    starts = jnp.cumsum(counts) - counts
    pcounts = (counts + MOE_BLOCK - 1) // MOE_BLOCK * MOE_BLOCK
    pends = jnp.cumsum(pcounts)
    pstarts = pends - pcounts
    dest = pstarts[se] + jnp.arange(n_assign, dtype=jnp.int32) - starts[se]
    P = n_assign + N_EXPERTS * MOE_BLOCK
    n_blocks = P // MOE_BLOCK
    slot_tok = jnp.full((P,), T, jnp.int32).at[dest].set(stok)
    slot_w = jnp.zeros((P,), jnp.float32).at[dest].set(sw)
    block_e = jnp.minimum(
        jnp.searchsorted(pends, jnp.arange(n_blocks) * MOE_BLOCK, side='right'), N_EXPERTS - 1)
    x_pad = jnp.concatenate([hf, jnp.zeros((1, D), hf.dtype)], axis=0)
    xs = x_pad[slot_tok].reshape(n_blocks, MOE_BLOCK, D)

    def expert_block(args):
        xb, e = args
        gu = xb @ w_gate_up[e] + b_gate_up[e]
        gate = jnp.minimum(gu[:, :D_FF], SWIGLU_LIMIT)
        up = jnp.clip(gu[:, D_FF:], -SWIGLU_LIMIT, SWIGLU_LIMIT)
        glu = gate * jax.nn.sigmoid(gate * SWIGLU_ALPHA)
        return ((up + 1.0) * glu) @ w_down[e] + b_down[e]

    ys = lax.map(expert_block, (xs, block_e)).reshape(P, D)
    out = jnp.zeros((T + 1, D), jnp.float32).at[slot_tok].add(ys.astype(jnp.float32) * slot_w[:, None])
    return out[:T].astype(h.dtype).reshape(B, S, D)


def setup_inputs(seed: int = 0) -> dict:
    key = jax.random.key(seed)
    ks = jax.random.split(key, 26)
    L = DEPTH
    f32 = jnp.float32

    def w(k, shape, fan_in):
        return jax.random.normal(k, shape, f32) * (fan_in ** -0.5)

    def gain(k, shape):
        return 1.0 + 0.02 * jax.random.normal(k, shape, f32)

    def bias(k, shape):
        return 0.01 * jax.random.normal(k, shape, f32)

    start = jax.random.randint(ks[1], (BATCH, 1), 0, 65536, dtype=jnp.int32)
    return {
        "x": jax.random.normal(ks[0], (BATCH, SEQ, D_MODEL), f32),
        "positions": start + jnp.arange(SEQ, dtype=jnp.int32)[None, :],
        "norm_mix_g": gain(ks[2], (L, D_MODEL)),
        "w_in": w(ks[3], (L, D_MODEL, IN_WIDTH), D_MODEL),
        "q_latent_g": gain(ks[4], (L, Q_LORA_RANK)),
        "w_uq": w(ks[5], (L, Q_LORA_RANK, MLA_HEADS * QK_HEAD_DIM), Q_LORA_RANK),
        "kv_latent_g": gain(ks[6], (L, KV_LORA_RANK)),
        "w_ukv": w(ks[7], (L, KV_LORA_RANK, MLA_HEADS * (QK_NOPE_DIM + V_HEAD_DIM)), KV_LORA_RANK),
        "q_head_g": gain(ks[8], (L, QK_HEAD_DIM)),
        "k_head_g": gain(ks[9], (L, QK_HEAD_DIM)),
        "conv_dw_w": w(ks[10], (L, CONV_KERNEL, CONV_WIDTH), CONV_KERNEL),
        "conv_dw_b": bias(ks[11], (L, CONV_WIDTH)),
        "conv_ln_g": gain(ks[12], (L, CONV_WIDTH)),
        "conv_ln_b": bias(ks[13], (L, CONV_WIDTH)),
        "attn_out_g": gain(ks[14], (L, ATTN_WIDTH)),
        "conv_out_g": gain(ks[15], (L, CONV_WIDTH)),
        "w_out": w(ks[16], (L, D_MODEL, D_MODEL), D_MODEL),
        "norm_ffn_g": gain(ks[17], (L, D_MODEL)),
        "router_w": w(ks[18], (L, D_MODEL, N_EXPERTS), D_MODEL),
        "router_b": bias(ks[19], (L, N_EXPERTS)),
        "w_gate_up": w(ks[20], (L, N_EXPERTS, D_MODEL, 2 * D_FF), D_MODEL),
        "b_gate_up": bias(ks[21], (L, N_EXPERTS, 2 * D_FF)),
        "w_down": w(ks[22], (L, N_EXPERTS, D_FF, D_MODEL), D_FF),
        "b_down": bias(ks[23], (L, N_EXPERTS, D_MODEL)),
    }


def reference(x, positions, norm_mix_g, w_in, q_latent_g, w_uq, kv_latent_g, w_ukv, q_head_g,
              k_head_g, conv_dw_w, conv_dw_b, conv_ln_g, conv_ln_b, attn_out_g, conv_out_g, w_out,
              norm_ffn_g, router_w, router_b, w_gate_up, b_gate_up, w_down, b_down):
    cos, sin = rope_tables(positions)
    o1 = Q_LORA_RANK
    o2 = o1 + KV_LORA_RANK
    o3 = o2 + QK_ROPE_DIM
    for l in range(DEPTH):
        h = rms_norm(x, norm_mix_g[l])
        p = h @ w_in[l]
        attn = mla(p[..., :o1], p[..., o1:o2], p[..., o2:o3], cos, sin,
                   q_latent_g[l], w_uq[l], kv_latent_g[l], w_ukv[l], q_head_g[l], k_head_g[l])
        conv = conformer_conv(p[..., o3:], conv_dw_w[l], conv_dw_b[l], conv_ln_g[l], conv_ln_b[l])
        mixed = jnp.concatenate([rms_norm(attn, attn_out_g[l]), rms_norm(conv, conv_out_g[l])], axis=-1)
        x = x + mixed @ w_out[l]
        h = rms_norm(x, norm_ffn_g[l])
        x = x + moe(h, router_w[l], router_b[l], w_gate_up[l], b_gate_up[l], w_down[l], b_down[l])
    return x
```

```python
import functools

import jax
import jax.numpy as jnp
from jax import lax
from jax.experimental import pallas as pl
from jax.experimental.pallas import tpu as pltpu

D_MODEL = 1024
N_HEADS = 8
NOPE_DIM = 64
ROPE_DIM = 32
QK_DIM = NOPE_DIM + ROPE_DIM
V_DIM = 64
Q_LORA = 384
KV_LORA = 128
CONV_WIDTH = 512
CONV_TAPS = 31
N_EXPERTS = 32
TOP_K = 4
D_FF = 1024
CHUNK = 64
ROPE_THETA = 10000.0
EPS = 1e-6
SWIGLU_ALPHA = 1.702
SWIGLU_LIMIT = 7.0

LANES = 128
SUBLANES = 8
HEAD_PAD = LANES
HALF_ROPE = ROPE_DIM // 2
VMEM_LIMIT_BYTES = 56 * 1024 * 1024

IN_ROWS = 256
ATT_Q = 256
ATT_K = 256
CONV_ROWS = 64
CONV_HALO = 32
MIX_ROWS = 512
DISPATCH_ROWS = 256
MOE_ROWS = 512
COMBINE_ROWS = 256

F32 = jnp.float32
BF16 = jnp.bfloat16


def _params(*semantics):
    return pltpu.CompilerParams(dimension_semantics=semantics, vmem_limit_bytes=VMEM_LIMIT_BYTES)


def _rms(x, g):
    return x * lax.rsqrt(jnp.mean(x * x, axis=-1, keepdims=True) + EPS) * g


def _full(shape):
    return pl.BlockSpec(shape, lambda *_: (0,) * len(shape))


def _rope(x, c, s_lo, s_hi):
    return x * c + pltpu.roll(x, HEAD_PAD - HALF_ROPE, 1) * s_lo + pltpu.roll(x, HALF_ROPE, 1) * s_hi


def _in_kernel(x_ref, pos_ref, gmix_ref, win_ref, gql_ref, wuq_ref, gkvl_ref, wuk_ref, wuv_ref,
               gq_ref, gk_ref, invf_ref, q_ref, k_ref, v_ref, y_ref):
    h = _rms(x_ref[...], gmix_ref[...]).astype(BF16)
    p = jnp.dot(h, win_ref[...], preferred_element_type=F32)
    o_kv = Q_LORA
    o_pe = o_kv + KV_LORA
    o_a = o_pe + HEAD_PAD
    o_g = o_a + CONV_WIDTH
    cq, ckv, kpe = p[:, :o_kv], p[:, o_kv:o_pe], p[:, o_pe:o_a]
    y_ref[...] = p[:, o_a:o_g] * jax.nn.sigmoid(p[:, o_g:])

    ang = pos_ref[...].astype(F32) * invf_ref[...]
    cos, sin = jnp.cos(ang), jnp.sin(ang)
    lane = lax.broadcasted_iota(jnp.int32, (1, HEAD_PAD), 1)
    lo = (lane >= NOPE_DIM) & (lane < NOPE_DIM + HALF_ROPE)
    hi = (lane >= NOPE_DIM + HALF_ROPE) & (lane < QK_DIM)
    c = jnp.where(lane < NOPE_DIM, 1.0, jnp.where(lane < QK_DIM, cos, 0.0))
    s_lo = jnp.where(lo, -sin, 0.0)
    s_hi = jnp.where(hi, sin, 0.0)

    q = jnp.dot(_rms(cq, gql_ref[...]).astype(BF16), wuq_ref[...], preferred_element_type=F32)
    ckvn = _rms(ckv, gkvl_ref[...]).astype(BF16)
    kn = jnp.dot(ckvn, wuk_ref[...], preferred_element_type=F32)
    v_ref[...] = jnp.dot(ckvn, wuv_ref[...], preferred_element_type=F32).astype(BF16)

    scale = QK_DIM ** -0.5
    gq, gk = gq_ref[...], gk_ref[...]
    ss_pe = jnp.sum(kpe * kpe, axis=-1, keepdims=True)
    for hd in range(N_HEADS):
        sl = slice(hd * HEAD_PAD, (hd + 1) * HEAD_PAD)
        qh = q[:, sl]
        rq = lax.rsqrt(jnp.sum(qh * qh, axis=-1, keepdims=True) * (1.0 / QK_DIM) + EPS)
        q_ref[:, sl] = (_rope(qh * rq * gq, c, s_lo, s_hi) * scale).astype(BF16)
        kh = kn[:, sl]
        rk = lax.rsqrt((jnp.sum(kh * kh, axis=-1, keepdims=True) + ss_pe) * (1.0 / QK_DIM) + EPS)
        k_ref[:, sl] = _rope((kh + kpe) * rk * gk, c, s_lo, s_hi).astype(BF16)


def _input_stage(x2, pos2, gmix, win, gql, wuq, gkvl, wuk, wuv, gq, gk, invf):
    t = x2.shape[0]
    wide = N_HEADS * HEAD_PAD
    row = lambda w: pl.BlockSpec((IN_ROWS, w), lambda i: (i, 0))
    return pl.pallas_call(
        _in_kernel,
        grid=(t // IN_ROWS,),
        in_specs=[row(D_MODEL), row(1), _full(gmix.shape), _full(win.shape), _full(gql.shape),
                  _full(wuq.shape), _full(gkvl.shape), _full(wuk.shape), _full(wuv.shape),
                  _full(gq.shape), _full(gk.shape), _full(invf.shape)],
        out_specs=[row(wide), row(wide), row(wide), row(CONV_WIDTH)],
        out_shape=[jax.ShapeDtypeStruct((t, wide), BF16), jax.ShapeDtypeStruct((t, wide), BF16),
                   jax.ShapeDtypeStruct((t, wide), BF16), jax.ShapeDtypeStruct((t, CONV_WIDTH), F32)],
        compiler_params=_params("parallel"),
        name="input_stage",
    )(x2, pos2, gmix, win, gql, wuq, gkvl, wuk, wuv, gq, gk, invf)


def _attn_kernel(q_ref, k_ref, v_ref, o_ref):
    i = pl.program_id(2)
    q = q_ref[...]
    dn = (((1,), (1,)), ((), ()))

    def block(j, carry, masked):
        m, l, acc = carry
        r0 = pl.multiple_of(j * ATT_K, ATT_K)
        s = lax.dot_general(q, k_ref[pl.ds(r0, ATT_K), :], dn, preferred_element_type=F32)
        if masked:
            rc = lax.broadcasted_iota(jnp.int32, (ATT_Q, ATT_K), 0) // CHUNK
            cc = lax.broadcasted_iota(jnp.int32, (ATT_Q, ATT_K), 1) // CHUNK
            s = jnp.where(cc <= rc, s, -jnp.inf)
        m_new = jnp.maximum(m, jnp.max(s, axis=-1, keepdims=True))
        alpha = jnp.exp(m - m_new)
        p = jnp.exp(s - m_new)
        l = alpha * l + jnp.sum(p, axis=-1, keepdims=True)
        acc = alpha * acc + jnp.dot(p.astype(BF16), v_ref[pl.ds(r0, ATT_K), :], preferred_element_type=F32)
        return m_new, l, acc

    init = (jnp.full((ATT_Q, 1), -jnp.inf, F32), jnp.zeros((ATT_Q, 1), F32), jnp.zeros((ATT_Q, HEAD_PAD), F32))
    carry = lax.fori_loop(0, i, lambda j, cr: block(j, cr, False), init)
    m, l, acc = block(i, carry, True)
    o_ref[...] = (acc / l).astype(BF16)


def _attention(q, k, v, batch, seq):
    assert ATT_Q == ATT_K and seq % ATT_Q == 0 and ATT_Q % CHUNK == 0
    nq = seq // ATT_Q
    qspec = pl.BlockSpec((ATT_Q, HEAD_PAD), lambda b, h, i: (b * nq + i, h))
    kvspec = pl.BlockSpec((seq, HEAD_PAD), lambda b, h, i: (b, h))
    return pl.pallas_call(
        _attn_kernel,
        grid=(batch, N_HEADS, nq),
        in_specs=[qspec, kvspec, kvspec],
        out_specs=qspec,
        out_shape=jax.ShapeDtypeStruct(q.shape, BF16),
        compiler_params=_params("parallel", "parallel", "arbitrary"),
        name="attention",
    )(q, k, v)


def _conv_kernel(y_ref, w_ref, b_ref, lng_ref, lnb_ref, og_ref, o_ref, pad_ref):
    seq = y_ref.shape[0]
    pad_ref[0:CONV_HALO, :] = jnp.zeros((CONV_HALO, CONV_WIDTH), F32)
    pad_ref[CONV_HALO:, :] = y_ref[...]
    first = CONV_HALO - (CONV_TAPS - 1)

    def body(i, _):
        r0 = pl.multiple_of(i * CONV_ROWS, CONV_ROWS)
        win = pad_ref[pl.ds(r0, CONV_ROWS + CONV_HALO), :]
        acc = jnp.zeros((CONV_ROWS, CONV_WIDTH), F32)
        for s in range(SUBLANES):
            offs = [o for o in range(first, first + CONV_TAPS) if o % SUBLANES == s]
            span = max(offs) - s + CONV_ROWS
            shifted = win[s:s + span, :]
            for o in offs:
                acc = acc + w_ref[o - first:o - first + 1, :] * shifted[o - s:o - s + CONV_ROWS, :]
        acc = acc + b_ref[...]
        xc = acc - jnp.mean(acc, axis=-1, keepdims=True)
        ln = xc * lax.rsqrt(jnp.mean(xc * xc, axis=-1, keepdims=True) + EPS) * lng_ref[...] + lnb_ref[...]
        z = ln * jax.nn.sigmoid(ln)
        o_ref[pl.ds(r0, CONV_ROWS), :] = _rms(z, og_ref[...]).astype(BF16)
        return 0

    lax.fori_loop(0, seq // CONV_ROWS, body, 0)


def _conv_branch(y, w, b, lng, lnb, og, batch, seq):
    spec = pl.BlockSpec((seq, CONV_WIDTH), lambda bi: (bi, 0))
    return pl.pallas_call(
        _conv_kernel,
        grid=(batch,),
        in_specs=[spec, _full(w.shape), _full(b.shape), _full(lng.shape), _full(lnb.shape), _full(og.shape)],
        out_specs=spec,
        out_shape=jax.ShapeDtypeStruct(y.shape, BF16),
        scratch_shapes=[pltpu.VMEM((seq + CONV_HALO, CONV_WIDTH), F32)],
        compiler_params=_params("parallel"),
        name="conv_branch",
    )(y, w, b, lng, lnb, og)


def _mix_kernel(attn_ref, conv_ref, x_ref, ga_ref, woa_ref, woc_ref, gffn_ref, rwh_ref, rwl_ref, rb_ref,
                tri_ref, x1_ref, h2_ref, idx_ref, gate_ref, rank_ref, cnt_ref, base_ref):
    @pl.when(pl.program_id(0) == 0)
    def _():
        base_ref[...] = jnp.zeros(base_ref.shape, F32)

    a = attn_ref[...].astype(F32)
    attn_width = N_HEADS * V_DIM
    an = a * lax.rsqrt(jnp.sum(a * a, axis=-1, keepdims=True) * (1.0 / attn_width) + EPS) * ga_ref[...]
    x1 = (x_ref[...] + jnp.dot(an.astype(BF16), woa_ref[...], preferred_element_type=F32)
          + jnp.dot(conv_ref[...], woc_ref[...], preferred_element_type=F32))
    x1_ref[...] = x1
    h2 = _rms(x1, gffn_ref[...])
    h2_ref[...] = h2

    hi = h2.astype(BF16)
    lo = (h2 - hi.astype(F32)).astype(BF16)
    dn = (((1,), (1,)), ((), ()))
    rwh, rwl = rwh_ref[...], rwl_ref[...]
    logits = (lax.dot_general(rwh, hi, dn, preferred_element_type=F32)
              + lax.dot_general(rwh, lo, dn, preferred_element_type=F32)
              + lax.dot_general(rwl, hi, dn, preferred_element_type=F32)) + rb_ref[...]

    rows = logits.shape[1]
    eidx = lax.broadcasted_iota(jnp.int32, (N_EXPERTS, rows), 0).astype(F32)
    work = logits
    sels, vals = [], []
    for k in range(TOP_K):
        mx = jnp.max(work, axis=0, keepdims=True)
        first = jnp.min(jnp.where(work == mx, eidx, float(N_EXPERTS)), axis=0, keepdims=True)
        sel = eidx == first
        work = jnp.where(sel, -jnp.inf, work)
        sels.append(sel)
        vals.append(mx)
        idx_ref[k:k + 1, :] = first.astype(jnp.int32)
    exps = [jnp.exp(v - vals[0]) for v in vals]
    denom = exps[0] + exps[1] + exps[2] + exps[3]
    for k in range(TOP_K):
        gate_ref[k:k + 1, :] = exps[k] / denom

    member = jnp.where(sels[0] | sels[1] | sels[2] | sels[3], 1.0, 0.0)
    before = jnp.dot(member.astype(BF16), tri_ref[...], preferred_element_type=F32)
    posn = base_ref[:, 0:1] + before
    for k in range(TOP_K):
        rank_ref[k:k + 1, :] = jnp.sum(jnp.where(sels[k], posn, 0.0), axis=0, keepdims=True).astype(jnp.int32)
    base_ref[...] = base_ref[...] + jnp.sum(member, axis=1, keepdims=True)
    cnt_ref[...] = base_ref[...].astype(jnp.int32)


def _mix_stage(attn, conv, x2, ga, woa, woc, gffn, rwh, rwl, rb, tri):
    t = x2.shape[0]
    row = lambda w: pl.BlockSpec((MIX_ROWS, w), lambda i: (i, 0))
    col = pl.BlockSpec((TOP_K, MIX_ROWS), lambda i: (0, i))
    return pl.pallas_call(
        _mix_kernel,
        grid=(t // MIX_ROWS,),
        in_specs=[row(attn.shape[1]), row(CONV_WIDTH), row(D_MODEL), _full(ga.shape), _full(woa.shape),
                  _full(woc.shape), _full(gffn.shape), _full(rwh.shape), _full(rwl.shape), _full(rb.shape),
                  _full(tri.shape)],
        out_specs=[row(D_MODEL), row(D_MODEL), col, col, col, _full((N_EXPERTS, LANES))],
        out_shape=[jax.ShapeDtypeStruct((t, D_MODEL), F32), jax.ShapeDtypeStruct((t, D_MODEL), F32),
                   jax.ShapeDtypeStruct((TOP_K, t), jnp.int32), jax.ShapeDtypeStruct((TOP_K, t), F32),
                   jax.ShapeDtypeStruct((TOP_K, t), jnp.int32),
                   jax.ShapeDtypeStruct((N_EXPERTS, LANES), jnp.int32)],
        scratch_shapes=[pltpu.VMEM((N_EXPERTS, LANES), F32)],
        compiler_params=_params("arbitrary"),
        name="mix_router",
    )(attn, conv, x2, ga, woa, woc, gffn, rwh, rwl, rb, tri)


def _dispatch_kernel(dest_ref, h_ref, xs_in_ref, xs_ref, sem):
    del xs_in_ref
    rows = h_ref.shape[0]

    def issue(t, _):
        for k in range(TOP_K):
            pltpu.make_async_copy(h_ref.at[pl.ds(t, 1)], xs_ref.at[pl.ds(dest_ref[k, t], 1)], sem).start()
        return 0

    lax.fori_loop(0, rows, issue, 0)
    for _ in range(TOP_K):
        pltpu.make_async_copy(h_ref, xs_ref.at[pl.ds(0, rows)], sem).wait()


def _dispatch(dest, h2, xs_init):
    t = h2.shape[0]
    return pl.pallas_call(
        _dispatch_kernel,
        grid=(t // DISPATCH_ROWS,),
        in_specs=[pl.BlockSpec((TOP_K, DISPATCH_ROWS), lambda i: (0, i), memory_space=pltpu.SMEM),
                  pl.BlockSpec((DISPATCH_ROWS, D_MODEL), lambda i: (i, 0)),
                  pl.BlockSpec(memory_space=pl.ANY)],
        out_specs=pl.BlockSpec(memory_space=pl.ANY),
        out_shape=jax.ShapeDtypeStruct(xs_init.shape, xs_init.dtype),
        scratch_shapes=[pltpu.SemaphoreType.DMA(())],
        input_output_aliases={2: 0},
        compiler_params=_params("arbitrary"),
        name="dispatch_rows",
    )(dest, h2, xs_init)


def _moe_kernel(be_ref, nact_ref, xs_ref, wgu_ref, bgu_ref, wd_ref, bd_ref, ys_ref):
    del be_ref
    active = pl.program_id(0) < nact_ref[0]

    @pl.when(jnp.logical_not(active))
    def _():
        ys_ref[...] = jnp.zeros(ys_ref.shape, F32)

    @pl.when(active)
    def _():
        gu = jnp.dot(xs_ref[...].astype(BF16), wgu_ref[0], preferred_element_type=F32) + bgu_ref[0]
        gate = jnp.minimum(gu[:, :D_FF], SWIGLU_LIMIT)
        up = jnp.clip(gu[:, D_FF:], -SWIGLU_LIMIT, SWIGLU_LIMIT)
        mid = (up + 1.0) * (gate * jax.nn.sigmoid(gate * SWIGLU_ALPHA))
        ys_ref[...] = jnp.dot(mid.astype(BF16), wd_ref[0], preferred_element_type=F32) + bd_ref[0]


def _moe(block_expert, n_active, xs, wgu, bgu, wd, bd):
    n_blocks = xs.shape[0] // MOE_ROWS
    grid_spec = pltpu.PrefetchScalarGridSpec(
        num_scalar_prefetch=2,
        grid=(n_blocks,),
        in_specs=[
            pl.BlockSpec((MOE_ROWS, D_MODEL), lambda i, be, na: (jnp.minimum(i, na[0] - 1), 0)),
            pl.BlockSpec((1, D_MODEL, 2 * D_FF), lambda i, be, na: (be[i], 0, 0)),
            pl.BlockSpec((1, 1, 2 * D_FF), lambda i, be, na: (be[i], 0, 0)),
            pl.BlockSpec((1, D_FF, D_MODEL), lambda i, be, na: (be[i], 0, 0)),
            pl.BlockSpec((1, 1, D_MODEL), lambda i, be, na: (be[i], 0, 0)),
        ],
        out_specs=pl.BlockSpec((MOE_ROWS, D_MODEL), lambda i, be, na: (i, 0)),
    )
    return pl.pallas_call(
        _moe_kernel,
        grid_spec=grid_spec,
        out_shape=jax.ShapeDtypeStruct((n_blocks * MOE_ROWS, D_MODEL), F32),
        compiler_params=_params("arbitrary"),
        name="expert_mlp",
    )(block_expert, n_active, xs, wgu, bgu, wd, bd)


def _combine_kernel(dest_ref, x1_ref, gate_ref, ys_ref, o_ref, buf_ref, sem):
    rows = x1_ref.shape[0]

    def issue(t, _):
        for k in range(TOP_K):
            pltpu.make_async_copy(ys_ref.at[pl.ds(dest_ref[k, t], 1)], buf_ref.at[k, pl.ds(t, 1)], sem).start()
        return 0

    lax.fori_loop(0, rows, issue, 0)
    for k in range(TOP_K):
        pltpu.make_async_copy(ys_ref.at[pl.ds(0, rows)], buf_ref.at[k], sem).wait()
    acc = x1_ref[...]
    for k in range(TOP_K):
        acc = acc + gate_ref[:, k:k + 1] * buf_ref[k]
    o_ref[...] = acc


def _combine(dest, x1, gate_rows, ys):
    t = x1.shape[0]
    row = lambda w: pl.BlockSpec((COMBINE_ROWS, w), lambda i: (i, 0))
    return pl.pallas_call(
        _combine_kernel,
        grid=(t // COMBINE_ROWS,),
        in_specs=[pl.BlockSpec((TOP_K, COMBINE_ROWS), lambda i: (0, i), memory_space=pltpu.SMEM),
                  row(D_MODEL), row(TOP_K), pl.BlockSpec(memory_space=pl.ANY)],
        out_specs=row(D_MODEL),
        out_shape=jax.ShapeDtypeStruct(x1.shape, F32),
        scratch_shapes=[pltpu.VMEM((TOP_K, COMBINE_ROWS, D_MODEL), F32), pltpu.SemaphoreType.DMA(())],
        compiler_params=_params("arbitrary"),
        name="combine_rows",
    )(dest, x1, gate_rows, ys)


def _head_blocks(w, width):
    r = w.shape[0]
    w = w.reshape(r, N_HEADS, width)
    return jnp.pad(w, ((0, 0), (0, 0), (0, HEAD_PAD - width))).reshape(r, N_HEADS * HEAD_PAD)


def _lane_row(v, offset=0):
    return jnp.pad(v, (offset, HEAD_PAD - offset - v.shape[0])).reshape(1, HEAD_PAD)


def kernel(x, positions, norm_mix_g, w_in, q_latent_g, w_uq, kv_latent_g, w_ukv, q_head_g, k_head_g, conv_dw_w, conv_dw_b, conv_ln_g, conv_ln_b, attn_out_g, conv_out_g, w_out, norm_ffn_g, router_w, router_b, w_gate_up, b_gate_up, w_down, b_down):
    batch, seq, _ = x.shape
    t = batch * seq
    depth = norm_mix_g.shape[0]
    x2 = x.reshape(t, D_MODEL)
    pos2 = positions.reshape(t, 1)
    inv_freq = 1.0 / (ROPE_THETA ** (jnp.arange(0, ROPE_DIM, 2, dtype=F32) / ROPE_DIM))
    invf = _lane_row(jnp.concatenate([inv_freq, inv_freq]), NOPE_DIM)
    tri = jnp.triu(jnp.ones((MIX_ROWS, MIX_ROWS), BF16), 1)
    o_kv = Q_LORA
    o_pe = o_kv + KV_LORA
    o_u = o_pe + ROPE_DIM

    for l in range(depth):
        wi = w_in[l]
        win = jnp.concatenate(
            [wi[:, :o_pe], jnp.pad(wi[:, o_pe:o_u], ((0, 0), (NOPE_DIM, HEAD_PAD - QK_DIM))), wi[:, o_u:]],
            axis=1).astype(BF16)
        wuq = _head_blocks(w_uq[l], QK_DIM).astype(BF16)
        wkv = w_ukv[l].reshape(KV_LORA, N_HEADS, NOPE_DIM + V_DIM)
        wuk = _head_blocks(wkv[:, :, :NOPE_DIM].reshape(KV_LORA, -1), NOPE_DIM).astype(BF16)
        wuv = _head_blocks(wkv[:, :, NOPE_DIM:].reshape(KV_LORA, -1), V_DIM).astype(BF16)
        ga = _head_blocks(attn_out_g[l].reshape(1, -1), V_DIM)
        wo = w_out[l]
        woa = jnp.pad(wo[:N_HEADS * V_DIM].reshape(N_HEADS, V_DIM, D_MODEL),
                      ((0, 0), (0, HEAD_PAD - V_DIM), (0, 0))).reshape(N_HEADS * HEAD_PAD, D_MODEL).astype(BF16)
        woc = wo[N_HEADS * V_DIM:].astype(BF16)
        rwt = router_w[l].T
        rwh = rwt.astype(BF16)
        rwl = (rwt - rwh.astype(F32)).astype(BF16)

        q, k, v, y = _input_stage(
            x2, pos2, norm_mix_g[l].reshape(1, -1), win, q_latent_g[l].reshape(1, -1), wuq,
            kv_latent_g[l].reshape(1, -1), wuk, wuv, _lane_row(q_head_g[l]), _lane_row(k_head_g[l]), invf)
        attn = _attention(q, k, v, batch, seq)
        conv = _conv_branch(y, conv_dw_w[l], conv_dw_b[l].reshape(1, -1), conv_ln_g[l].reshape(1, -1),
                            conv_ln_b[l].reshape(1, -1), conv_out_g[l].reshape(1, -1), batch, seq)
        x1, h2, idx, gate, rank, cnt = _mix_stage(
            attn, conv, x2, ga, woa, woc, norm_ffn_g[l].reshape(1, -1), rwh, rwl,
            router_b[l].reshape(-1, 1), tri)

        counts = cnt[:, 0]
        padded = (counts + MOE_ROWS - 1) // MOE_ROWS * MOE_ROWS
        ends = jnp.cumsum(padded)
        starts = ends - padded
        dest = starts[idx] + rank
        n_blocks = (t * TOP_K + N_EXPERTS * (MOE_ROWS - 1) + MOE_ROWS - 1) // MOE_ROWS
        n_active = (ends[-1] // MOE_ROWS).astype(jnp.int32)
        blk = jnp.arange(n_blocks, dtype=jnp.int32)
        be = jnp.minimum(jnp.searchsorted(ends, blk * MOE_ROWS, side="right"), N_EXPERTS - 1).astype(jnp.int32)
        be = jnp.where(blk < n_active, be, be[jnp.maximum(n_active - 1, 0)])

        xs = _dispatch(dest, h2, jnp.zeros((n_blocks * MOE_ROWS, D_MODEL), F32))
        ys = _moe(be, n_active.reshape(1), xs, w_gate_up[l].astype(BF16),
                  b_gate_up[l].reshape(N_EXPERTS, 1, -1), w_down[l].astype(BF16),
                  b_down[l].reshape(N_EXPERTS, 1, -1))
        x2 = _combine(dest, x1, gate.T, ys)
    return x2.reshape(batch, seq, D_MODEL)
```

```python
import functools

import jax
import jax.numpy as jnp
from jax import lax
from jax.experimental import pallas as pl
from jax.experimental.pallas import tpu as pltpu

D_MODEL = 1024
N_HEADS = 8
NOPE_DIM = 64
ROPE_DIM = 32
QK_DIM = NOPE_DIM + ROPE_DIM
V_DIM = 64
Q_LORA = 384
KV_LORA = 128
CONV_WIDTH = 512
CONV_TAPS = 31
N_EXPERTS = 32
TOP_K = 4
D_FF = 1024
CHUNK = 64
ROPE_THETA = 10000.0
EPS = 1e-6
SWIGLU_ALPHA = 1.702
SWIGLU_LIMIT = 7.0
LOG2_E = 1.4426950408889634

LANES = 128
SUBLANES = 8
HEAD_PAD = LANES
HALF_ROPE = ROPE_DIM // 2
VMEM_LIMIT_BYTES = 56 * 1024 * 1024

IN_ROWS = 256
ATT_Q = 256
ATT_K = 256
CONV_ROWS = 64
CONV_HALO = 32
MIX_ROWS = 512
DISPATCH_ROWS = 256
MOE_ROWS = 512
COMBINE_ROWS = 256

F32 = jnp.float32
BF16 = jnp.bfloat16


def _params(*semantics):
    return pltpu.CompilerParams(dimension_semantics=semantics, vmem_limit_bytes=VMEM_LIMIT_BYTES)


def _rms(x, g):
    return x * lax.rsqrt(jnp.mean(x * x, axis=-1, keepdims=True) + EPS) * g


def _full(shape):
    return pl.BlockSpec(shape, lambda *_: (0,) * len(shape))


def _rope(x, c, s_lo, s_hi):
    return x * c + pltpu.roll(x, HEAD_PAD - HALF_ROPE, 1) * s_lo + pltpu.roll(x, HALF_ROPE, 1) * s_hi


def _in_kernel(x_ref, pos_ref, gmix_ref, win_ref, gql_ref, wuq_ref, gkvl_ref, wuk_ref, wuv_ref,
               gq_ref, gk_ref, invf_ref, q_ref, k_ref, v_ref, y_ref):
    h = _rms(x_ref[...], gmix_ref[...]).astype(BF16)
    p = jnp.dot(h, win_ref[...], preferred_element_type=F32)
    o_kv = Q_LORA
    o_pe = o_kv + KV_LORA
    o_a = o_pe + HEAD_PAD
    o_g = o_a + CONV_WIDTH
    cq, ckv, kpe = p[:, :o_kv], p[:, o_kv:o_pe], p[:, o_pe:o_a]
    y_ref[...] = p[:, o_a:o_g] * jax.nn.sigmoid(p[:, o_g:])

    ang = pos_ref[...].astype(F32) * invf_ref[...]
    cos, sin = jnp.cos(ang), jnp.sin(ang)
    lane = lax.broadcasted_iota(jnp.int32, (1, HEAD_PAD), 1)
    lo = (lane >= NOPE_DIM) & (lane < NOPE_DIM + HALF_ROPE)
    hi = (lane >= NOPE_DIM + HALF_ROPE) & (lane < QK_DIM)
    c = jnp.where(lane < NOPE_DIM, 1.0, jnp.where(lane < QK_DIM, cos, 0.0))
    s_lo = jnp.where(lo, -sin, 0.0)
    s_hi = jnp.where(hi, sin, 0.0)

    q = jnp.dot(_rms(cq, gql_ref[...]).astype(BF16), wuq_ref[...], preferred_element_type=F32)
    ckvn = _rms(ckv, gkvl_ref[...]).astype(BF16)
    kn = jnp.dot(ckvn, wuk_ref[...], preferred_element_type=F32)
    wide_lane = lax.broadcasted_iota(jnp.int32, (1, N_HEADS * HEAD_PAD), 1)
    ones_lane = jnp.where(wide_lane % HEAD_PAD == V_DIM, 1.0, 0.0)
    v_ref[...] = (jnp.dot(ckvn, wuv_ref[...], preferred_element_type=F32) + ones_lane).astype(BF16)

    scale = QK_DIM ** -0.5 * LOG2_E
    gq, gk = gq_ref[...], gk_ref[...]
    ss_pe = jnp.sum(kpe * kpe, axis=-1, keepdims=True)
    for hd in range(N_HEADS):
        sl = slice(hd * HEAD_PAD, (hd + 1) * HEAD_PAD)
        qh = q[:, sl]
        rq = lax.rsqrt(jnp.sum(qh * qh, axis=-1, keepdims=True) * (1.0 / QK_DIM) + EPS)
        q_ref[:, sl] = (_rope(qh * rq * gq, c, s_lo, s_hi) * scale).astype(BF16)
        kh = kn[:, sl]
        rk = lax.rsqrt((jnp.sum(kh * kh, axis=-1, keepdims=True) + ss_pe) * (1.0 / QK_DIM) + EPS)
        k_ref[:, sl] = _rope((kh + kpe) * rk * gk, c, s_lo, s_hi).astype(BF16)


def _input_stage(x2, pos2, gmix, win, gql, wuq, gkvl, wuk, wuv, gq, gk, invf):
    t = x2.shape[0]
    wide = N_HEADS * HEAD_PAD
    row = lambda w: pl.BlockSpec((IN_ROWS, w), lambda i: (i, 0))
    return pl.pallas_call(
        _in_kernel,
        grid=(t // IN_ROWS,),
        in_specs=[row(D_MODEL), row(1), _full(gmix.shape), _full(win.shape), _full(gql.shape),
                  _full(wuq.shape), _full(gkvl.shape), _full(wuk.shape), _full(wuv.shape),
                  _full(gq.shape), _full(gk.shape), _full(invf.shape)],
        out_specs=[row(wide), row(wide), row(wide), row(CONV_WIDTH)],
        out_shape=[jax.ShapeDtypeStruct((t, wide), BF16), jax.ShapeDtypeStruct((t, wide), BF16),
                   jax.ShapeDtypeStruct((t, wide), BF16), jax.ShapeDtypeStruct((t, CONV_WIDTH), F32)],
        compiler_params=_params("parallel"),
        name="input_stage",
    )(x2, pos2, gmix, win, gql, wuq, gkvl, wuk, wuv, gq, gk, invf)


def _attn_kernel(q_ref, k_ref, v_ref, o_ref, *scratch):
    m_refs, acc_refs = scratch[:N_HEADS], scratch[N_HEADS:]
    i = pl.program_id(1)
    dn = (((1,), (1,)), ((), ()))

    def step(r0, diagonal):
        old = None if diagonal else [(m_refs[hd][...], acc_refs[hd][...]) for hd in range(N_HEADS)]
        new = []
        for hd in range(N_HEADS):
            sl = slice(hd * HEAD_PAD, (hd + 1) * HEAD_PAD)
            s = lax.dot_general(q_ref[:, sl], k_ref[pl.ds(r0, ATT_K), sl], dn, preferred_element_type=F32)
            if diagonal:
                rc = lax.broadcasted_iota(jnp.int32, (ATT_Q, ATT_K), 0) // CHUNK
                cc = lax.broadcasted_iota(jnp.int32, (ATT_Q, ATT_K), 1) // CHUNK
                s = jnp.where(cc <= rc, s, -jnp.inf)
            m_new = jnp.broadcast_to(jnp.max(s, axis=-1, keepdims=True), (ATT_Q, HEAD_PAD))
            if not diagonal:
                m_new = jnp.maximum(old[hd][0], m_new)
            m_wide = jnp.concatenate([m_new] * (ATT_K // HEAD_PAD), axis=1)
            acc = jnp.dot(jnp.exp2(s - m_wide).astype(BF16), v_ref[pl.ds(r0, ATT_K), sl],
                          preferred_element_type=F32)
            if not diagonal:
                acc = jnp.exp2(old[hd][0] - m_new) * old[hd][1] + acc
            new.append((m_new, acc))
        for hd in range(N_HEADS):
            m_refs[hd][...], acc_refs[hd][...] = new[hd]

    step(pl.multiple_of(i * ATT_Q, ATT_Q), True)

    def body(j, _):
        step(pl.multiple_of(j * ATT_K, ATT_K), False)
        return 0

    lax.fori_loop(0, i, body, 0)
    lane = lax.broadcasted_iota(jnp.int32, (1, HEAD_PAD), 1)
    for hd in range(N_HEADS):
        sl = slice(hd * HEAD_PAD, (hd + 1) * HEAD_PAD)
        acc = acc_refs[hd][...]
        o_ref[:, sl] = jnp.where(lane < V_DIM, acc / acc[:, V_DIM:V_DIM + 1], 0.0).astype(BF16)


def _attention(q, k, v, batch, seq):
    assert ATT_Q == ATT_K and seq % ATT_Q == 0 and ATT_Q % CHUNK == 0
    nq = seq // ATT_Q
    wide = N_HEADS * HEAD_PAD
    qspec = pl.BlockSpec((ATT_Q, wide), lambda b, i: (b * nq + i, 0))
    kvspec = pl.BlockSpec((seq, wide), lambda b, i: (b, 0))
    return pl.pallas_call(
        _attn_kernel,
        grid=(batch, nq),
        in_specs=[qspec, kvspec, kvspec],
        out_specs=qspec,
        out_shape=jax.ShapeDtypeStruct(q.shape, BF16),
        scratch_shapes=[pltpu.VMEM((ATT_Q, HEAD_PAD), F32)] * (2 * N_HEADS),
        compiler_params=_params("parallel", "arbitrary"),
        name="attention",
    )(q, k, v)


def _conv_kernel(y_ref, w_ref, b_ref, lng_ref, lnb_ref, og_ref, o_ref, pad_ref):
    seq = y_ref.shape[0]
    pad_ref[0:CONV_HALO, :] = jnp.zeros((CONV_HALO, CONV_WIDTH), F32)
    pad_ref[CONV_HALO:, :] = y_ref[...]
    first = CONV_HALO - (CONV_TAPS - 1)

    def body(i, _):
        r0 = pl.multiple_of(i * CONV_ROWS, CONV_ROWS)
        win = pad_ref[pl.ds(r0, CONV_ROWS + CONV_HALO), :]
        acc = jnp.zeros((CONV_ROWS, CONV_WIDTH), F32)
        for s in range(SUBLANES):
            offs = [o for o in range(first, first + CONV_TAPS) if o % SUBLANES == s]
            span = max(offs) - s + CONV_ROWS
            shifted = win[s:s + span, :]
            for o in offs:
                acc = acc + w_ref[o - first:o - first + 1, :] * shifted[o - s:o - s + CONV_ROWS, :]
        acc = acc + b_ref[...]
        xc = acc - jnp.mean(acc, axis=-1, keepdims=True)
        ln = xc * lax.rsqrt(jnp.mean(xc * xc, axis=-1, keepdims=True) + EPS) * lng_ref[...] + lnb_ref[...]
        z = ln * jax.nn.sigmoid(ln)
        o_ref[pl.ds(r0, CONV_ROWS), :] = _rms(z, og_ref[...]).astype(BF16)
        return 0

    lax.fori_loop(0, seq // CONV_ROWS, body, 0)


def _conv_branch(y, w, b, lng, lnb, og, batch, seq):
    spec = pl.BlockSpec((seq, CONV_WIDTH), lambda bi: (bi, 0))
    return pl.pallas_call(
        _conv_kernel,
        grid=(batch,),
        in_specs=[spec, _full(w.shape), _full(b.shape), _full(lng.shape), _full(lnb.shape), _full(og.shape)],
        out_specs=spec,
        out_shape=jax.ShapeDtypeStruct(y.shape, BF16),
        scratch_shapes=[pltpu.VMEM((seq + CONV_HALO, CONV_WIDTH), F32)],
        compiler_params=_params("parallel"),
        name="conv_branch",
    )(y, w, b, lng, lnb, og)


def _mix_kernel(attn_ref, conv_ref, x_ref, ga_ref, woa_ref, woc_ref, gffn_ref, rwh_ref, rwl_ref, rb_ref,
                tri_ref, x1_ref, h2_ref, idx_ref, gate_ref, rank_ref, cnt_ref, base_ref):
    @pl.when(pl.program_id(0) == 0)
    def _():
        base_ref[...] = jnp.zeros(base_ref.shape, F32)

    a = attn_ref[...].astype(F32)
    attn_width = N_HEADS * V_DIM
    an = a * lax.rsqrt(jnp.sum(a * a, axis=-1, keepdims=True) * (1.0 / attn_width) + EPS) * ga_ref[...]
    x1 = (x_ref[...] + jnp.dot(an.astype(BF16), woa_ref[...], preferred_element_type=F32)
          + jnp.dot(conv_ref[...], woc_ref[...], preferred_element_type=F32))
    x1_ref[...] = x1
    h2 = _rms(x1, gffn_ref[...])
    h2_ref[...] = h2

    hi = h2.astype(BF16)
    lo = (h2 - hi.astype(F32)).astype(BF16)
    dn = (((1,), (1,)), ((), ()))
    rwh, rwl = rwh_ref[...], rwl_ref[...]
    logits = (lax.dot_general(rwh, hi, dn, preferred_element_type=F32)
              + lax.dot_general(rwh, lo, dn, preferred_element_type=F32)
              + lax.dot_general(rwl, hi, dn, preferred_element_type=F32)) + rb_ref[...]

    rows = logits.shape[1]
    eidx = lax.broadcasted_iota(jnp.int32, (N_EXPERTS, rows), 0).astype(F32)
    work = logits
    sels, vals = [], []
    for k in range(TOP_K):
        mx = jnp.max(work, axis=0, keepdims=True)
        first = jnp.min(jnp.where(work == mx, eidx, float(N_EXPERTS)), axis=0, keepdims=True)
        sel = eidx == first
        work = jnp.where(sel, -jnp.inf, work)
        sels.append(sel)
        vals.append(mx)
        idx_ref[k:k + 1, :] = first.astype(jnp.int32)
    exps = [jnp.exp(v - vals[0]) for v in vals]
    denom = exps[0] + exps[1] + exps[2] + exps[3]
    for k in range(TOP_K):
        gate_ref[k:k + 1, :] = exps[k] / denom

    member = jnp.where(sels[0] | sels[1] | sels[2] | sels[3], 1.0, 0.0)
    before = jnp.dot(member.astype(BF16), tri_ref[...], preferred_element_type=F32)
    posn = base_ref[:, 0:1] + before
    for k in range(TOP_K):
        rank_ref[k:k + 1, :] = jnp.sum(jnp.where(sels[k], posn, 0.0), axis=0, keepdims=True).astype(jnp.int32)
    base_ref[...] = base_ref[...] + jnp.sum(member, axis=1, keepdims=True)
    cnt_ref[...] = base_ref[...].astype(jnp.int32)


def _mix_stage(attn, conv, x2, ga, woa, woc, gffn, rwh, rwl, rb, tri):
    t = x2.shape[0]
    row = lambda w: pl.BlockSpec((MIX_ROWS, w), lambda i: (i, 0))
    col = pl.BlockSpec((TOP_K, MIX_ROWS), lambda i: (0, i))
    return pl.pallas_call(
        _mix_kernel,
        grid=(t // MIX_ROWS,),
        in_specs=[row(attn.shape[1]), row(CONV_WIDTH), row(D_MODEL), _full(ga.shape), _full(woa.shape),
                  _full(woc.shape), _full(gffn.shape), _full(rwh.shape), _full(rwl.shape), _full(rb.shape),
                  _full(tri.shape)],
        out_specs=[row(D_MODEL), row(D_MODEL), col, col, col, _full((N_EXPERTS, LANES))],
        out_shape=[jax.ShapeDtypeStruct((t, D_MODEL), F32), jax.ShapeDtypeStruct((t, D_MODEL), F32),
                   jax.ShapeDtypeStruct((TOP_K, t), jnp.int32), jax.ShapeDtypeStruct((TOP_K, t), F32),
                   jax.ShapeDtypeStruct((TOP_K, t), jnp.int32),
                   jax.ShapeDtypeStruct((N_EXPERTS, LANES), jnp.int32)],
        scratch_shapes=[pltpu.VMEM((N_EXPERTS, LANES), F32)],
        compiler_params=_params("arbitrary"),
        name="mix_router",
    )(attn, conv, x2, ga, woa, woc, gffn, rwh, rwl, rb, tri)


def _dispatch_kernel(ends_ref, padded_ref, dest_ref, h_ref, xs_ref, zero_ref, sem, zsem):
    rows = h_ref.shape[0]
    n_blocks = xs_ref.shape[0] // MOE_ROWS

    @pl.when(pl.program_id(0) == 0)
    def _():
        zero_ref[...] = jnp.zeros(zero_ref.shape, F32)

        def zero_copy(r0):
            return pltpu.make_async_copy(zero_ref, xs_ref.at[pl.ds(pl.multiple_of(r0, MOE_ROWS), MOE_ROWS)], zsem)

        for start in (True, False):
            for e in range(N_EXPERTS):
                @pl.when(padded_ref[e] > 0)
                def _():
                    cp = zero_copy(ends_ref[e] - MOE_ROWS)
                    cp.start() if start else cp.wait()

            def tail(b, _):
                cp = zero_copy(b * MOE_ROWS)
                cp.start() if start else cp.wait()
                return 0

            lax.fori_loop(ends_ref[N_EXPERTS - 1] // MOE_ROWS, n_blocks, tail, 0)

    def issue(t, _):
        for k in range(TOP_K):
            pltpu.make_async_copy(h_ref.at[pl.ds(t, 1)], xs_ref.at[pl.ds(dest_ref[k, t], 1)], sem).start()
        return 0

    lax.fori_loop(0, rows, issue, 0)
    for _ in range(TOP_K):
        pltpu.make_async_copy(h_ref, xs_ref.at[pl.ds(0, rows)], sem).wait()


def _dispatch(ends, padded, dest, h2, n_blocks):
    t = h2.shape[0]
    grid_spec = pltpu.PrefetchScalarGridSpec(
        num_scalar_prefetch=2,
        grid=(t // DISPATCH_ROWS,),
        in_specs=[pl.BlockSpec((TOP_K, DISPATCH_ROWS), lambda i, *_: (0, i), memory_space=pltpu.SMEM),
                  pl.BlockSpec((DISPATCH_ROWS, D_MODEL), lambda i, *_: (i, 0))],
        out_specs=pl.BlockSpec(memory_space=pl.ANY),
        scratch_shapes=[pltpu.VMEM((MOE_ROWS, D_MODEL), F32), pltpu.SemaphoreType.DMA(()),
                        pltpu.SemaphoreType.DMA(())],
    )
    return pl.pallas_call(
        _dispatch_kernel,
        grid_spec=grid_spec,
        out_shape=jax.ShapeDtypeStruct((n_blocks * MOE_ROWS, D_MODEL), F32),
        compiler_params=_params("arbitrary"),
        name="dispatch_rows",
    )(ends, padded, dest, h2)


def _moe_kernel(be_ref, nact_ref, xs_ref, wgu_ref, bgu_ref, wd_ref, bd_ref, ys_ref):
    del be_ref
    active = pl.program_id(0) < nact_ref[0]

    @pl.when(jnp.logical_not(active))
    def _():
        ys_ref[...] = jnp.zeros(ys_ref.shape, F32)

    @pl.when(active)
    def _():
        gu = jnp.dot(xs_ref[...].astype(BF16), wgu_ref[0], preferred_element_type=F32) + bgu_ref[0]
        gate = jnp.minimum(gu[:, :D_FF], SWIGLU_LIMIT)
        up = jnp.clip(gu[:, D_FF:], -SWIGLU_LIMIT, SWIGLU_LIMIT)
        mid = (up + 1.0) * (gate * jax.nn.sigmoid(gate * SWIGLU_ALPHA))
        ys_ref[...] = jnp.dot(mid.astype(BF16), wd_ref[0], preferred_element_type=F32) + bd_ref[0]


def _moe(block_expert, n_active, xs, wgu, bgu, wd, bd):
    n_blocks = xs.shape[0] // MOE_ROWS
    grid_spec = pltpu.PrefetchScalarGridSpec(
        num_scalar_prefetch=2,
        grid=(n_blocks,),
        in_specs=[
            pl.BlockSpec((MOE_ROWS, D_MODEL), lambda i, be, na: (jnp.minimum(i, na[0] - 1), 0)),
            pl.BlockSpec((1, D_MODEL, 2 * D_FF), lambda i, be, na: (be[i], 0, 0)),
            pl.BlockSpec((1, 1, 2 * D_FF), lambda i, be, na: (be[i], 0, 0)),
            pl.BlockSpec((1, D_FF, D_MODEL), lambda i, be, na: (be[i], 0, 0)),
            pl.BlockSpec((1, 1, D_MODEL), lambda i, be, na: (be[i], 0, 0)),
        ],
        out_specs=pl.BlockSpec((MOE_ROWS, D_MODEL), lambda i, be, na: (i, 0)),
    )
    return pl.pallas_call(
        _moe_kernel,
        grid_spec=grid_spec,
        out_shape=jax.ShapeDtypeStruct((n_blocks * MOE_ROWS, D_MODEL), F32),
        compiler_params=_params("arbitrary"),
        name="expert_mlp",
    )(block_expert, n_active, xs, wgu, bgu, wd, bd)


def _combine_kernel(dest_ref, x1_ref, gate_ref, ys_ref, o_ref, buf_ref, sem):
    rows = x1_ref.shape[0]

    def issue(t, _):
        for k in range(TOP_K):
            pltpu.make_async_copy(ys_ref.at[pl.ds(dest_ref[k, t], 1)], buf_ref.at[k, pl.ds(t, 1)], sem).start()
        return 0

    lax.fori_loop(0, rows, issue, 0)
    for k in range(TOP_K):
        pltpu.make_async_copy(ys_ref.at[pl.ds(0, rows)], buf_ref.at[k], sem).wait()
    acc = x1_ref[...]
    for k in range(TOP_K):
        acc = acc + gate_ref[:, k:k + 1] * buf_ref[k]
    o_ref[...] = acc


def _combine(dest, x1, gate_rows, ys):
    t = x1.shape[0]
    row = lambda w: pl.BlockSpec((COMBINE_ROWS, w), lambda i: (i, 0))
    return pl.pallas_call(
        _combine_kernel,
        grid=(t // COMBINE_ROWS,),
        in_specs=[pl.BlockSpec((TOP_K, COMBINE_ROWS), lambda i: (0, i), memory_space=pltpu.SMEM),
                  row(D_MODEL), row(TOP_K), pl.BlockSpec(memory_space=pl.ANY)],
        out_specs=row(D_MODEL),
        out_shape=jax.ShapeDtypeStruct(x1.shape, F32),
        scratch_shapes=[pltpu.VMEM((TOP_K, COMBINE_ROWS, D_MODEL), F32), pltpu.SemaphoreType.DMA(())],
        compiler_params=_params("arbitrary"),
        name="combine_rows",
    )(dest, x1, gate_rows, ys)


def _head_blocks(w, width):
    r = w.shape[0]
    w = w.reshape(r, N_HEADS, width)
    return jnp.pad(w, ((0, 0), (0, 0), (0, HEAD_PAD - width))).reshape(r, N_HEADS * HEAD_PAD)


def _lane_row(v, offset=0):
    return jnp.pad(v, (offset, HEAD_PAD - offset - v.shape[0])).reshape(1, HEAD_PAD)


def kernel(x, positions, norm_mix_g, w_in, q_latent_g, w_uq, kv_latent_g, w_ukv, q_head_g, k_head_g, conv_dw_w, conv_dw_b, conv_ln_g, conv_ln_b, attn_out_g, conv_out_g, w_out, norm_ffn_g, router_w, router_b, w_gate_up, b_gate_up, w_down, b_down):
    batch, seq, _ = x.shape
    t = batch * seq
    depth = norm_mix_g.shape[0]
    x2 = x.reshape(t, D_MODEL)
    pos2 = positions.reshape(t, 1)
    inv_freq = 1.0 / (ROPE_THETA ** (jnp.arange(0, ROPE_DIM, 2, dtype=F32) / ROPE_DIM))
    invf = _lane_row(jnp.concatenate([inv_freq, inv_freq]), NOPE_DIM)
    tri = jnp.triu(jnp.ones((MIX_ROWS, MIX_ROWS), BF16), 1)
    o_kv = Q_LORA
    o_pe = o_kv + KV_LORA
    o_u = o_pe + ROPE_DIM

    for l in range(depth):
        wi = w_in[l]
        win = jnp.concatenate(
            [wi[:, :o_pe], jnp.pad(wi[:, o_pe:o_u], ((0, 0), (NOPE_DIM, HEAD_PAD - QK_DIM))), wi[:, o_u:]],
            axis=1).astype(BF16)
        wuq = _head_blocks(w_uq[l], QK_DIM).astype(BF16)
        wkv = w_ukv[l].reshape(KV_LORA, N_HEADS, NOPE_DIM + V_DIM)
        wuk = _head_blocks(wkv[:, :, :NOPE_DIM].reshape(KV_LORA, -1), NOPE_DIM).astype(BF16)
        wuv = _head_blocks(wkv[:, :, NOPE_DIM:].reshape(KV_LORA, -1), V_DIM).astype(BF16)
        ga = _head_blocks(attn_out_g[l].reshape(1, -1), V_DIM)
        wo = w_out[l]
        woa = jnp.pad(wo[:N_HEADS * V_DIM].reshape(N_HEADS, V_DIM, D_MODEL),
                      ((0, 0), (0, HEAD_PAD - V_DIM), (0, 0))).reshape(N_HEADS * HEAD_PAD, D_MODEL).astype(BF16)
        woc = wo[N_HEADS * V_DIM:].astype(BF16)
        rwt = router_w[l].T
        rwh = rwt.astype(BF16)
        rwl = (rwt - rwh.astype(F32)).astype(BF16)

        q, k, v, y = _input_stage(
            x2, pos2, norm_mix_g[l].reshape(1, -1), win, q_latent_g[l].reshape(1, -1), wuq,
            kv_latent_g[l].reshape(1, -1), wuk, wuv, _lane_row(q_head_g[l]), _lane_row(k_head_g[l]), invf)
        attn = _attention(q, k, v, batch, seq)
        conv = _conv_branch(y, conv_dw_w[l], conv_dw_b[l].reshape(1, -1), conv_ln_g[l].reshape(1, -1),
                            conv_ln_b[l].reshape(1, -1), conv_out_g[l].reshape(1, -1), batch, seq)
        x1, h2, idx, gate, rank, cnt = _mix_stage(
            attn, conv, x2, ga, woa, woc, norm_ffn_g[l].reshape(1, -1), rwh, rwl,
            router_b[l].reshape(-1, 1), tri)

        counts = cnt[:, 0]
        padded = (counts + MOE_ROWS - 1) // MOE_ROWS * MOE_ROWS
        ends = jnp.cumsum(padded)
        starts = ends - padded
        experts = jnp.arange(N_EXPERTS, dtype=jnp.int32)
        dest = rank + jnp.sum(jnp.where(idx[None] == experts[:, None, None], starts[:, None, None], 0), axis=0)
        n_blocks = (t * TOP_K + N_EXPERTS * (MOE_ROWS - 1)) // MOE_ROWS
        n_active = (ends[-1] // MOE_ROWS).astype(jnp.int32)
        blk = jnp.minimum(jnp.arange(n_blocks, dtype=jnp.int32), n_active - 1)
        be = jnp.minimum(jnp.sum((ends[None, :] <= (blk * MOE_ROWS)[:, None]).astype(jnp.int32), axis=1),
                         N_EXPERTS - 1)

        xs = _dispatch(ends.astype(jnp.int32), padded.astype(jnp.int32), dest, h2, n_blocks)
        ys = _moe(be, n_active.reshape(1), xs, w_gate_up[l].astype(BF16),
                  b_gate_up[l].reshape(N_EXPERTS, 1, -1), w_down[l].astype(BF16),
                  b_down[l].reshape(N_EXPERTS, 1, -1))
        x2 = _combine(dest, x1, gate.T, ys)
    return x2.reshape(batch, seq, D_MODEL)
```

```python
import functools

import jax
import jax.numpy as jnp
from jax import lax
from jax.experimental import pallas as pl
from jax.experimental.pallas import tpu as pltpu

D_MODEL = 1024
N_HEADS = 8
NOPE_DIM = 64
ROPE_DIM = 32
QK_DIM = NOPE_DIM + ROPE_DIM
V_DIM = 64
Q_LORA = 384
KV_LORA = 128
CONV_WIDTH = 512
CONV_TAPS = 31
N_EXPERTS = 32
TOP_K = 4
D_FF = 1024
CHUNK = 64
ROPE_THETA = 10000.0
EPS = 1e-6
SWIGLU_ALPHA = 1.702
SWIGLU_LIMIT = 7.0
LOG2_E = 1.4426950408889634

LANES = 128
SUBLANES = 8
HEAD_PAD = LANES
HALF_ROPE = ROPE_DIM // 2
VMEM_LIMIT_BYTES = 56 * 1024 * 1024

IN_ROWS = 256
ATT_Q = 256
ATT_K = 256
CONV_ROWS = 64
CONV_HALO = 32
MIX_ROWS = 512
DISPATCH_ROWS = 256
MOE_ROWS = 512
COMBINE_ROWS = 256
ISSUE_UNROLL = 4
DMA_PRIORITIES = 2

F32 = jnp.float32
BF16 = jnp.bfloat16


def _params(*semantics):
    return pltpu.CompilerParams(dimension_semantics=semantics, vmem_limit_bytes=VMEM_LIMIT_BYTES)


def _rms(x, g):
    return x * lax.rsqrt(jnp.mean(x * x, axis=-1, keepdims=True) + EPS) * g


def _full(shape):
    return pl.BlockSpec(shape, lambda *_: (0,) * len(shape))


ROW_TILES = D_MODEL // LANES
assert ROW_TILES == SUBLANES


def _load_tile_rows(ref, rows, lead=()):
    return jnp.concatenate([ref[lead + (pl.ds(c, rows, stride=ROW_TILES), slice(None))] for c in range(ROW_TILES)],
                           axis=1)


def _store_tile_rows(ref, value):
    rows = value.shape[0]
    for c in range(ROW_TILES):
        ref[pl.ds(c, rows, stride=ROW_TILES), :] = value[:, c * LANES:(c + 1) * LANES]


def _tile_row(ref, r, lead=()):
    return ref.at[lead + (pl.ds(pl.multiple_of(r * ROW_TILES, ROW_TILES), ROW_TILES),)]


def _rope_table_kernel(pos_ref, invf_ref, cos_ref, sin_ref):
    ang = pos_ref[...].astype(F32) * invf_ref[...]
    cos_ref[...] = jnp.cos(ang)
    sin_ref[...] = jnp.sin(ang)


def _rope_tables(positions):
    t = positions.size
    per_row = LANES // HALF_ROPE
    inv_freq = 1.0 / (ROPE_THETA ** (jnp.arange(0, ROPE_DIM, 2, dtype=F32) / ROPE_DIM))
    pos = jnp.repeat(positions.reshape(t // per_row, per_row), HALF_ROPE, axis=1)
    invf = jnp.tile(inv_freq, per_row).reshape(1, LANES)
    shape = jax.ShapeDtypeStruct(pos.shape, F32)
    cos, sin = pl.pallas_call(
        _rope_table_kernel,
        in_specs=[_full(pos.shape), _full(invf.shape)],
        out_specs=[_full(pos.shape), _full(pos.shape)],
        out_shape=[shape, shape],
        compiler_params=pltpu.CompilerParams(vmem_limit_bytes=VMEM_LIMIT_BYTES),
        name="rope_tables",
    )(pos, invf)
    return cos.reshape(t, HALF_ROPE), sin.reshape(t, HALF_ROPE)


def _in_kernel(x_ref, ctab_ref, stab_ref, gmix_ref, win_ref, gql_ref, wuq_ref, wuqs_ref, gkvl_ref, wuk_ref,
               wuv_ref, gq_ref, gqs_ref, gk_ref, gks_ref, q_ref, k_ref, v_ref, y_ref):
    h = _rms(x_ref[...], gmix_ref[...]).astype(BF16)
    p = jnp.dot(h, win_ref[...], preferred_element_type=F32)
    o_kv = Q_LORA
    o_pe = o_kv + KV_LORA
    o_ps = o_pe + HEAD_PAD
    o_a = o_ps + HEAD_PAD
    o_g = o_a + CONV_WIDTH
    cq, ckv, kpe, kpe_sw = p[:, :o_kv], p[:, o_kv:o_pe], p[:, o_pe:o_ps], p[:, o_ps:o_a]
    y_ref[...] = p[:, o_a:o_g] * jax.nn.sigmoid(p[:, o_g:])

    cqn = _rms(cq, gql_ref[...]).astype(BF16)
    q = jnp.dot(cqn, wuq_ref[...], preferred_element_type=F32)
    q_sw = jnp.dot(cqn, wuqs_ref[...], preferred_element_type=F32)
    ckvn = _rms(ckv, gkvl_ref[...]).astype(BF16)
    kn = jnp.dot(ckvn, wuk_ref[...], preferred_element_type=F32)
    wide_lane = lax.broadcasted_iota(jnp.int32, (1, N_HEADS * HEAD_PAD), 1)
    ones_lane = jnp.where(wide_lane % HEAD_PAD == V_DIM, 1.0, 0.0)
    v_ref[...] = (jnp.dot(ckvn, wuv_ref[...], preferred_element_type=F32) + ones_lane).astype(BF16)

    scale = QK_DIM ** -0.5 * LOG2_E
    cq_tab = ctab_ref[...] * (gq_ref[...] * scale)
    sq_tab = stab_ref[...] * (gqs_ref[...] * scale)
    ck_tab = ctab_ref[...] * gk_ref[...]
    sk_tab = stab_ref[...] * gks_ref[...]
    k_pe = kpe * ck_tab + kpe_sw * sk_tab
    ss_pe = jnp.sum(kpe * kpe, axis=-1, keepdims=True)
    for hd in range(N_HEADS):
        sl = slice(hd * HEAD_PAD, (hd + 1) * HEAD_PAD)
        qh = q[:, sl]
        rq = lax.rsqrt(jnp.sum(qh * qh, axis=-1, keepdims=True) * (1.0 / QK_DIM) + EPS)
        q_ref[:, sl] = ((qh * cq_tab + q_sw[:, sl] * sq_tab) * rq).astype(BF16)
        kh = kn[:, sl]
        rk = lax.rsqrt((jnp.sum(kh * kh, axis=-1, keepdims=True) + ss_pe) * (1.0 / QK_DIM) + EPS)
        k_ref[:, sl] = ((kh * ck_tab + k_pe) * rk).astype(BF16)


def _input_stage(x2, ctab, stab, gmix, win, gql, wuq, wuqs, gkvl, wuk, wuv, gq, gqs, gk, gks):
    t = x2.shape[0]
    wide = N_HEADS * HEAD_PAD
    row = lambda w: pl.BlockSpec((IN_ROWS, w), lambda i: (i, 0))
    consts = (gmix, win, gql, wuq, wuqs, gkvl, wuk, wuv, gq, gqs, gk, gks)
    return pl.pallas_call(
        _in_kernel,
        grid=(t // IN_ROWS,),
        in_specs=[row(D_MODEL), row(HEAD_PAD), row(HEAD_PAD)] + [_full(a.shape) for a in consts],
        out_specs=[row(wide), row(wide), row(wide), row(CONV_WIDTH)],
        out_shape=[jax.ShapeDtypeStruct((t, wide), BF16), jax.ShapeDtypeStruct((t, wide), BF16),
                   jax.ShapeDtypeStruct((t, wide), BF16), jax.ShapeDtypeStruct((t, CONV_WIDTH), F32)],
        compiler_params=_params("parallel"),
        name="input_stage",
    )(x2, ctab, stab, *consts)


def _attn_kernel(q_ref, k_ref, v_ref, o_ref, *scratch):
    m_refs, acc_refs = scratch[:N_HEADS], scratch[N_HEADS:]
    i = pl.program_id(1)
    dn = (((1,), (1,)), ((), ()))

    def step(r0, diagonal):
        old = None if diagonal else [(m_refs[hd][...], acc_refs[hd][...]) for hd in range(N_HEADS)]
        new = []
        for hd in range(N_HEADS):
            sl = slice(hd * HEAD_PAD, (hd + 1) * HEAD_PAD)
            s = lax.dot_general(q_ref[:, sl], k_ref[pl.ds(r0, ATT_K), sl], dn, preferred_element_type=F32)
            if diagonal:
                rc = lax.broadcasted_iota(jnp.int32, (ATT_Q, ATT_K), 0) // CHUNK
                cc = lax.broadcasted_iota(jnp.int32, (ATT_Q, ATT_K), 1) // CHUNK
                s = jnp.where(cc <= rc, s, -jnp.inf)
            m_new = jnp.broadcast_to(jnp.max(s, axis=-1, keepdims=True), (ATT_Q, HEAD_PAD))
            if not diagonal:
                m_new = jnp.maximum(old[hd][0], m_new)
            m_wide = jnp.concatenate([m_new] * (ATT_K // HEAD_PAD), axis=1)
            acc = jnp.dot(jnp.exp2(s - m_wide).astype(BF16), v_ref[pl.ds(r0, ATT_K), sl],
                          preferred_element_type=F32)
            if not diagonal:
                acc = jnp.exp2(old[hd][0] - m_new) * old[hd][1] + acc
            new.append((m_new, acc))
        for hd in range(N_HEADS):
            m_refs[hd][...], acc_refs[hd][...] = new[hd]

    step(pl.multiple_of(i * ATT_Q, ATT_Q), True)

    def body(j, _):
        step(pl.multiple_of(j * ATT_K, ATT_K), False)
        return 0

    lax.fori_loop(0, i, body, 0)
    lane = lax.broadcasted_iota(jnp.int32, (1, HEAD_PAD), 1)
    for hd in range(N_HEADS):
        sl = slice(hd * HEAD_PAD, (hd + 1) * HEAD_PAD)
        acc = acc_refs[hd][...]
        o_ref[:, sl] = jnp.where(lane < V_DIM, acc / acc[:, V_DIM:V_DIM + 1], 0.0).astype(BF16)


def _attention(q, k, v, batch, seq):
    assert ATT_Q == ATT_K and seq % ATT_Q == 0 and ATT_Q % CHUNK == 0
    nq = seq // ATT_Q
    wide = N_HEADS * HEAD_PAD
    qspec = pl.BlockSpec((ATT_Q, wide), lambda b, i: (b * nq + i, 0))
    kvspec = pl.BlockSpec((seq, wide), lambda b, i: (b, 0))
    return pl.pallas_call(
        _attn_kernel,
        grid=(batch, nq),
        in_specs=[qspec, kvspec, kvspec],
        out_specs=qspec,
        out_shape=jax.ShapeDtypeStruct(q.shape, BF16),
        scratch_shapes=[pltpu.VMEM((ATT_Q, HEAD_PAD), F32)] * (2 * N_HEADS),
        compiler_params=_params("parallel", "arbitrary"),
        name="attention",
    )(q, k, v)


def _conv_kernel(y_ref, w_ref, b_ref, lng_ref, lnb_ref, og_ref, o_ref, pad_ref, phase_ref):
    seq = y_ref.shape[0]
    pad_ref[0:CONV_HALO, :] = jnp.zeros((CONV_HALO, CONV_WIDTH), F32)
    pad_ref[CONV_HALO:, :] = y_ref[...]
    first = CONV_HALO - (CONV_TAPS - 1)

    def body(i, _):
        r0 = pl.multiple_of(i * CONV_ROWS, CONV_ROWS)
        win = pad_ref[pl.ds(r0, CONV_ROWS + CONV_HALO), :]
        acc = jnp.zeros((CONV_ROWS, CONV_WIDTH), F32)
        for s in range(SUBLANES):
            offs = [o for o in range(first, first + CONV_TAPS) if o % SUBLANES == s]
            span = max(offs) - s + CONV_ROWS
            if s:
                phase_ref[s, 0:span, :] = win[s:s + span, :]
            for o in offs:
                rows = (pad_ref[pl.ds(pl.multiple_of(r0 + o, SUBLANES), CONV_ROWS), :] if s == 0
                        else phase_ref[s, o - s:o - s + CONV_ROWS, :])
                acc = acc + w_ref[o - first:o - first + 1, :] * rows
        acc = acc + b_ref[...]
        xc = acc - jnp.mean(acc, axis=-1, keepdims=True)
        ln = xc * lax.rsqrt(jnp.mean(xc * xc, axis=-1, keepdims=True) + EPS) * lng_ref[...] + lnb_ref[...]
        z = ln * jax.nn.sigmoid(ln)
        o_ref[pl.ds(r0, CONV_ROWS), :] = _rms(z, og_ref[...]).astype(BF16)
        return 0

    lax.fori_loop(0, seq // CONV_ROWS, body, 0)


def _conv_branch(y, w, b, lng, lnb, og, batch, seq):
    spec = pl.BlockSpec((seq, CONV_WIDTH), lambda bi: (bi, 0))
    return pl.pallas_call(
        _conv_kernel,
        grid=(batch,),
        in_specs=[spec, _full(w.shape), _full(b.shape), _full(lng.shape), _full(lnb.shape), _full(og.shape)],
        out_specs=spec,
        out_shape=jax.ShapeDtypeStruct(y.shape, BF16),
        scratch_shapes=[pltpu.VMEM((seq + CONV_HALO, CONV_WIDTH), F32),
                        pltpu.VMEM((SUBLANES, CONV_ROWS + CONV_HALO, CONV_WIDTH), F32)],
        compiler_params=_params("parallel"),
        name="conv_branch",
    )(y, w, b, lng, lnb, og)


def _mix_kernel(attn_ref, conv_ref, x_ref, ga_ref, woa_ref, woc_ref, gffn_ref, rwh_ref, rwl_ref, rb_ref,
                tri_ref, x1_ref, h2_ref, idx_ref, gate_ref, rank_ref, cnt_ref, base_ref):
    @pl.when(pl.program_id(0) == 0)
    def _():
        base_ref[...] = jnp.zeros(base_ref.shape, F32)

    a = attn_ref[...].astype(F32)
    attn_width = N_HEADS * V_DIM
    an = a * lax.rsqrt(jnp.sum(a * a, axis=-1, keepdims=True) * (1.0 / attn_width) + EPS) * ga_ref[...]
    x1 = (x_ref[...] + jnp.dot(an.astype(BF16), woa_ref[...], preferred_element_type=F32)
          + jnp.dot(conv_ref[...], woc_ref[...], preferred_element_type=F32))
    x1_ref[...] = x1
    h2 = _rms(x1, gffn_ref[...])
    _store_tile_rows(h2_ref, h2)

    hi = h2.astype(BF16)
    lo = (h2 - hi.astype(F32)).astype(BF16)
    dn = (((1,), (1,)), ((), ()))
    rwh, rwl = rwh_ref[...], rwl_ref[...]
    logits = (lax.dot_general(rwh, hi, dn, preferred_element_type=F32)
              + lax.dot_general(rwh, lo, dn, preferred_element_type=F32)
              + lax.dot_general(rwl, hi, dn, preferred_element_type=F32)) + rb_ref[...]

    rows = logits.shape[1]
    eidx = lax.broadcasted_iota(jnp.int32, (N_EXPERTS, rows), 0).astype(F32)
    work = logits
    sels, vals = [], []
    for k in range(TOP_K):
        mx = jnp.max(work, axis=0, keepdims=True)
        first = jnp.min(jnp.where(work == mx, eidx, float(N_EXPERTS)), axis=0, keepdims=True)
        sel = eidx == first
        work = jnp.where(sel, -jnp.inf, work)
        sels.append(sel)
        vals.append(mx)
        idx_ref[k:k + 1, :] = first.astype(jnp.int32)
    exps = [jnp.exp(v - vals[0]) for v in vals]
    denom = exps[0] + exps[1] + exps[2] + exps[3]
    for k in range(TOP_K):
        gate_ref[k:k + 1, :] = exps[k] / denom

    member = jnp.where(sels[0] | sels[1] | sels[2] | sels[3], 1.0, 0.0)
    before = jnp.dot(member.astype(BF16), tri_ref[...], preferred_element_type=F32)
    posn = base_ref[:, 0:1] + before
    for k in range(TOP_K):
        rank_ref[k:k + 1, :] = jnp.sum(jnp.where(sels[k], posn, 0.0), axis=0, keepdims=True).astype(jnp.int32)
    base_ref[...] = base_ref[...] + jnp.sum(member, axis=1, keepdims=True)
    cnt_ref[...] = base_ref[...].astype(jnp.int32)


def _mix_stage(attn, conv, x2, ga, woa, woc, gffn, rwh, rwl, rb, tri):
    t = x2.shape[0]
    row = lambda w: pl.BlockSpec((MIX_ROWS, w), lambda i: (i, 0))
    col = pl.BlockSpec((TOP_K, MIX_ROWS), lambda i: (0, i))
    return pl.pallas_call(
        _mix_kernel,
        grid=(t // MIX_ROWS,),
        in_specs=[row(attn.shape[1]), row(CONV_WIDTH), row(D_MODEL), _full(ga.shape), _full(woa.shape),
                  _full(woc.shape), _full(gffn.shape), _full(rwh.shape), _full(rwl.shape), _full(rb.shape),
                  _full(tri.shape)],
        out_specs=[row(D_MODEL), pl.BlockSpec((MIX_ROWS * ROW_TILES, LANES), lambda i: (i, 0)), col, col, col,
                   _full((N_EXPERTS, LANES))],
        out_shape=[jax.ShapeDtypeStruct((t, D_MODEL), F32), jax.ShapeDtypeStruct((t * ROW_TILES, LANES), F32),
                   jax.ShapeDtypeStruct((TOP_K, t), jnp.int32), jax.ShapeDtypeStruct((TOP_K, t), F32),
                   jax.ShapeDtypeStruct((TOP_K, t), jnp.int32),
                   jax.ShapeDtypeStruct((N_EXPERTS, LANES), jnp.int32)],
        scratch_shapes=[pltpu.VMEM((N_EXPERTS, LANES), F32)],
        compiler_params=_params("arbitrary"),
        name="mix_router",
    )(attn, conv, x2, ga, woa, woc, gffn, rwh, rwl, rb, tri)


def _dispatch_kernel(ends_ref, padded_ref, dest_ref, h_ref, xs_ref, zero_ref, sem, zsem):
    rows = h_ref.shape[0] // ROW_TILES
    block = MOE_ROWS * ROW_TILES
    n_blocks = xs_ref.shape[0] // block

    @pl.when(pl.program_id(0) == 0)
    def _():
        zero_ref[...] = jnp.zeros(zero_ref.shape, F32)

        def zero_copy(r0):
            return pltpu.make_async_copy(
                zero_ref, xs_ref.at[pl.ds(pl.multiple_of(r0 * ROW_TILES, block), block)], zsem)

        for start in (True, False):
            for e in range(N_EXPERTS):
                @pl.when(padded_ref[e] > 0)
                def _():
                    cp = zero_copy(ends_ref[e] - MOE_ROWS)
                    cp.start() if start else cp.wait()

            def tail(b, _):
                cp = zero_copy(b * MOE_ROWS)
                cp.start() if start else cp.wait()
                return 0

            lax.fori_loop(ends_ref[N_EXPERTS - 1] // MOE_ROWS, n_blocks, tail, 0)

    def issue(t, _):
        for k in range(TOP_K):
            pltpu.make_async_copy(_tile_row(h_ref, t), _tile_row(xs_ref, dest_ref[k, t]), sem).start(
                priority=k % DMA_PRIORITIES)
        return 0

    lax.fori_loop(0, rows, issue, 0, unroll=ISSUE_UNROLL)
    for _ in range(TOP_K):
        pltpu.make_async_copy(h_ref, xs_ref.at[pl.ds(0, rows * ROW_TILES)], sem).wait()


def _dispatch(ends, padded, dest, h2, n_blocks):
    t = h2.shape[0] // ROW_TILES
    grid_spec = pltpu.PrefetchScalarGridSpec(
        num_scalar_prefetch=2,
        grid=(t // DISPATCH_ROWS,),
        in_specs=[pl.BlockSpec((TOP_K, DISPATCH_ROWS), lambda i, *_: (0, i), memory_space=pltpu.SMEM),
                  pl.BlockSpec((DISPATCH_ROWS * ROW_TILES, LANES), lambda i, *_: (i, 0))],
        out_specs=pl.BlockSpec(memory_space=pl.ANY),
        scratch_shapes=[pltpu.VMEM((MOE_ROWS * ROW_TILES, LANES), F32), pltpu.SemaphoreType.DMA(()),
                        pltpu.SemaphoreType.DMA(())],
    )
    return pl.pallas_call(
        _dispatch_kernel,
        grid_spec=grid_spec,
        out_shape=jax.ShapeDtypeStruct((n_blocks * MOE_ROWS * ROW_TILES, LANES), F32),
        compiler_params=_params("arbitrary"),
        name="dispatch_rows",
    )(ends, padded, dest, h2)


def _moe_kernel(be_ref, nact_ref, xs_ref, wgu_ref, bgu_ref, wd_ref, bd_ref, ys_ref, wgu_bf, wd_bf):
    i = pl.program_id(0)
    active = i < nact_ref[0]

    @pl.when(jnp.logical_not(active))
    def _():
        ys_ref[...] = jnp.zeros(ys_ref.shape, F32)

    @pl.when(active & ((i == 0) | (be_ref[i] != be_ref[jnp.maximum(i - 1, 0)])))
    def _():
        wgu_bf[...] = wgu_ref[0].astype(BF16)
        wd_bf[...] = wd_ref[0].astype(BF16)

    @pl.when(active)
    def _():
        x = _load_tile_rows(xs_ref, MOE_ROWS).astype(BF16)
        gu = jnp.dot(x, wgu_bf[...], preferred_element_type=F32) + bgu_ref[0]
        gate = jnp.minimum(gu[:, :D_FF], SWIGLU_LIMIT)
        up = jnp.clip(gu[:, D_FF:], -SWIGLU_LIMIT, SWIGLU_LIMIT)
        mid = (up + 1.0) * (gate * jax.nn.sigmoid(gate * SWIGLU_ALPHA))
        _store_tile_rows(ys_ref, jnp.dot(mid.astype(BF16), wd_bf[...], preferred_element_type=F32) + bd_ref[0])


def _moe(block_expert, n_active, xs, wgu, bgu, wd, bd):
    block = MOE_ROWS * ROW_TILES
    n_blocks = xs.shape[0] // block
    grid_spec = pltpu.PrefetchScalarGridSpec(
        num_scalar_prefetch=2,
        grid=(n_blocks,),
        in_specs=[
            pl.BlockSpec((block, LANES), lambda i, be, na: (jnp.minimum(i, na[0] - 1), 0)),
            pl.BlockSpec((1, D_MODEL, 2 * D_FF), lambda i, be, na: (be[i], 0, 0)),
            pl.BlockSpec((1, 1, 2 * D_FF), lambda i, be, na: (be[i], 0, 0)),
            pl.BlockSpec((1, D_FF, D_MODEL), lambda i, be, na: (be[i], 0, 0)),
            pl.BlockSpec((1, 1, D_MODEL), lambda i, be, na: (be[i], 0, 0)),
        ],
        out_specs=pl.BlockSpec((block, LANES), lambda i, be, na: (i, 0)),
        scratch_shapes=[pltpu.VMEM((D_MODEL, 2 * D_FF), BF16), pltpu.VMEM((D_FF, D_MODEL), BF16)],
    )
    return pl.pallas_call(
        _moe_kernel,
        grid_spec=grid_spec,
        out_shape=jax.ShapeDtypeStruct(xs.shape, F32),
        compiler_params=_params("arbitrary"),
        name="expert_mlp",
    )(block_expert, n_active, xs, wgu, bgu, wd, bd)


def _combine_kernel(dest_ref, x1_ref, gate_ref, ys_ref, o_ref, buf_ref, sem):
    rows = x1_ref.shape[0]

    def issue(t, _):
        for k in range(TOP_K):
            pltpu.make_async_copy(_tile_row(ys_ref, dest_ref[k, t]), _tile_row(buf_ref, t, (k,)), sem).start(
                priority=k % DMA_PRIORITIES)
        return 0

    lax.fori_loop(0, rows, issue, 0, unroll=ISSUE_UNROLL)
    for k in range(TOP_K):
        pltpu.make_async_copy(ys_ref.at[pl.ds(0, rows * ROW_TILES)], buf_ref.at[k], sem).wait()
    acc = x1_ref[...]
    for k in range(TOP_K):
        acc = acc + gate_ref[:, k:k + 1] * _load_tile_rows(buf_ref, rows, (k,))
    o_ref[...] = acc


def _combine(dest, x1, gate_rows, ys):
    t = x1.shape[0]
    row = lambda w: pl.BlockSpec((COMBINE_ROWS, w), lambda i: (i, 0))
    return pl.pallas_call(
        _combine_kernel,
        grid=(t // COMBINE_ROWS,),
        in_specs=[pl.BlockSpec((TOP_K, COMBINE_ROWS), lambda i: (0, i), memory_space=pltpu.SMEM),
                  row(D_MODEL), row(TOP_K), pl.BlockSpec(memory_space=pl.ANY)],
        out_specs=row(D_MODEL),
        out_shape=jax.ShapeDtypeStruct(x1.shape, F32),
        scratch_shapes=[pltpu.VMEM((TOP_K, COMBINE_ROWS * ROW_TILES, LANES), F32), pltpu.SemaphoreType.DMA(())],
        compiler_params=_params("arbitrary"),
        name="combine_rows",
    )(dest, x1, gate_rows, ys)


def _head_blocks(w, width):
    r = w.shape[0]
    w = w.reshape(r, N_HEADS, width)
    return jnp.pad(w, ((0, 0), (0, 0), (0, HEAD_PAD - width))).reshape(r, N_HEADS * HEAD_PAD)


def _lane_row(v, offset=0):
    return jnp.pad(v, (offset, HEAD_PAD - offset - v.shape[0])).reshape(1, HEAD_PAD)


def _swap_rope(w, sign):
    lo, hi = w[..., NOPE_DIM:NOPE_DIM + HALF_ROPE], w[..., NOPE_DIM + HALF_ROPE:QK_DIM]
    return jnp.concatenate([jnp.zeros_like(w[..., :NOPE_DIM]), sign * hi, lo], axis=-1)


def kernel(x, positions, norm_mix_g, w_in, q_latent_g, w_uq, kv_latent_g, w_ukv, q_head_g, k_head_g, conv_dw_w, conv_dw_b, conv_ln_g, conv_ln_b, attn_out_g, conv_out_g, w_out, norm_ffn_g, router_w, router_b, w_gate_up, b_gate_up, w_down, b_down):
    batch, seq, _ = x.shape
    t = batch * seq
    depth = norm_mix_g.shape[0]
    x2 = x.reshape(t, D_MODEL)
    cos, sin = _rope_tables(positions)
    ctab = jnp.concatenate([jnp.ones((t, NOPE_DIM), F32), cos, cos, jnp.zeros((t, HEAD_PAD - QK_DIM), F32)], axis=1)
    stab = jnp.concatenate([jnp.zeros((t, NOPE_DIM), F32), sin, sin, jnp.zeros((t, HEAD_PAD - QK_DIM), F32)], axis=1)
    tri = jnp.triu(jnp.ones((MIX_ROWS, MIX_ROWS), BF16), 1)
    o_kv = Q_LORA
    o_pe = o_kv + KV_LORA
    o_u = o_pe + ROPE_DIM

    for l in range(depth):
        wi = w_in[l]
        w_pe = wi[:, o_pe:o_u]
        w_pe_sw = jnp.concatenate([-w_pe[:, HALF_ROPE:], w_pe[:, :HALF_ROPE]], axis=1)
        pe_block = lambda w: jnp.pad(w, ((0, 0), (NOPE_DIM, HEAD_PAD - QK_DIM)))
        win = jnp.concatenate([wi[:, :o_pe], pe_block(w_pe), pe_block(w_pe_sw), wi[:, o_u:]], axis=1).astype(BF16)
        wq = w_uq[l].reshape(Q_LORA, N_HEADS, QK_DIM)
        wuq = _head_blocks(w_uq[l], QK_DIM).astype(BF16)
        wuqs = _head_blocks(_swap_rope(wq, -1.0).reshape(Q_LORA, -1), QK_DIM).astype(BF16)
        wkv = w_ukv[l].reshape(KV_LORA, N_HEADS, NOPE_DIM + V_DIM)
        wuk = _head_blocks(wkv[:, :, :NOPE_DIM].reshape(KV_LORA, -1), NOPE_DIM).astype(BF16)
        wuv = _head_blocks(wkv[:, :, NOPE_DIM:].reshape(KV_LORA, -1), V_DIM).astype(BF16)
        ga = _head_blocks(attn_out_g[l].reshape(1, -1), V_DIM)
        wo = w_out[l]
        woa = jnp.pad(wo[:N_HEADS * V_DIM].reshape(N_HEADS, V_DIM, D_MODEL),
                      ((0, 0), (0, HEAD_PAD - V_DIM), (0, 0))).reshape(N_HEADS * HEAD_PAD, D_MODEL).astype(BF16)
        woc = wo[N_HEADS * V_DIM:].astype(BF16)
        rwt = router_w[l].T
        rwh = rwt.astype(BF16)
        rwl = (rwt - rwh.astype(F32)).astype(BF16)

        q, k, v, y = _input_stage(
            x2, ctab, stab, norm_mix_g[l].reshape(1, -1), win, q_latent_g[l].reshape(1, -1), wuq, wuqs,
            kv_latent_g[l].reshape(1, -1), wuk, wuv, _lane_row(q_head_g[l]),
            _lane_row(_swap_rope(q_head_g[l], 1.0)), _lane_row(k_head_g[l]), _lane_row(_swap_rope(k_head_g[l], 1.0)))
        attn = _attention(q, k, v, batch, seq)
        conv = _conv_branch(y, conv_dw_w[l], conv_dw_b[l].reshape(1, -1), conv_ln_g[l].reshape(1, -1),
                            conv_ln_b[l].reshape(1, -1), conv_out_g[l].reshape(1, -1), batch, seq)
        x1, h2, idx, gate, rank, cnt = _mix_stage(
            attn, conv, x2, ga, woa, woc, norm_ffn_g[l].reshape(1, -1), rwh, rwl,
            router_b[l].reshape(-1, 1), tri)

        counts = cnt[:, 0]
        padded = (counts + MOE_ROWS - 1) // MOE_ROWS * MOE_ROWS
        ends = jnp.cumsum(padded)
        starts = ends - padded
        experts = jnp.arange(N_EXPERTS, dtype=jnp.int32)
        dest = rank + jnp.sum(jnp.where(idx[None] == experts[:, None, None], starts[:, None, None], 0), axis=0)
        n_blocks = (t * TOP_K + N_EXPERTS * (MOE_ROWS - 1)) // MOE_ROWS
        n_active = (ends[-1] // MOE_ROWS).astype(jnp.int32)
        blk = jnp.minimum(jnp.arange(n_blocks, dtype=jnp.int32), n_active - 1)
        be = jnp.minimum(jnp.sum((ends[None, :] <= (blk * MOE_ROWS)[:, None]).astype(jnp.int32), axis=1),
                         N_EXPERTS - 1)

        xs = _dispatch(ends.astype(jnp.int32), padded.astype(jnp.int32), dest, h2, n_blocks)
        ys = _moe(be, n_active.reshape(1), xs, w_gate_up[l], b_gate_up[l].reshape(N_EXPERTS, 1, -1),
                  w_down[l], b_down[l].reshape(N_EXPERTS, 1, -1))
        x2 = _combine(dest, x1, gate.T, ys)
    return x2.reshape(batch, seq, D_MODEL)
```

```python
import functools

import jax
import jax.numpy as jnp
from jax import lax
from jax.experimental import pallas as pl
from jax.experimental.pallas import tpu as pltpu

D_MODEL = 1024
N_HEADS = 8
NOPE_DIM = 64
ROPE_DIM = 32
QK_DIM = NOPE_DIM + ROPE_DIM
V_DIM = 64
Q_LORA = 384
KV_LORA = 128
CONV_WIDTH = 512
CONV_TAPS = 31
N_EXPERTS = 32
TOP_K = 4
D_FF = 1024
CHUNK = 64
ROPE_THETA = 10000.0
EPS = 1e-6
SWIGLU_ALPHA = 1.702
SWIGLU_LIMIT = 7.0
LOG2_E = 1.4426950408889634

LANES = 128
SUBLANES = 8
HEAD_PAD = LANES
HALF_ROPE = ROPE_DIM // 2
VMEM_LIMIT_BYTES = 56 * 1024 * 1024

IN_ROWS = 256
ATT_Q = 256
ATT_K = 256
ATT_AHEAD = 3
CONV_ROWS = 64
CONV_HALO = 32
MIX_ROWS = 512
DISPATCH_ROWS = 256
MOE_ROWS = 512
COMBINE_ROWS = 256
ISSUE_UNROLL = 4
DMA_PRIORITIES = 2

F32 = jnp.float32
BF16 = jnp.bfloat16


def _params(*semantics, flags=None):
    return pltpu.CompilerParams(dimension_semantics=semantics, vmem_limit_bytes=VMEM_LIMIT_BYTES, flags=flags)


def _rms(x, g):
    return x * lax.rsqrt(jnp.mean(x * x, axis=-1, keepdims=True) + EPS) * g


def _full(shape):
    return pl.BlockSpec(shape, lambda *_: (0,) * len(shape))


ROW_TILES = D_MODEL // LANES
assert ROW_TILES == SUBLANES


def _load_tile_rows(ref, rows, lead=()):
    return jnp.concatenate([ref[lead + (pl.ds(c, rows, stride=ROW_TILES), slice(None))] for c in range(ROW_TILES)],
                           axis=1)


def _store_tile_rows(ref, value):
    rows = value.shape[0]
    for c in range(ROW_TILES):
        ref[pl.ds(c, rows, stride=ROW_TILES), :] = value[:, c * LANES:(c + 1) * LANES]


def _tile_row(ref, r, lead=()):
    return ref.at[lead + (pl.ds(pl.multiple_of(r * ROW_TILES, ROW_TILES), ROW_TILES),)]


def _rope_table_kernel(pos_ref, invf_ref, cos_ref, sin_ref):
    ang = pos_ref[...].astype(F32) * invf_ref[...]
    cos_ref[...] = jnp.cos(ang)
    sin_ref[...] = jnp.sin(ang)


def _rope_tables(positions):
    t = positions.size
    per_row = LANES // HALF_ROPE
    inv_freq = 1.0 / (ROPE_THETA ** (jnp.arange(0, ROPE_DIM, 2, dtype=F32) / ROPE_DIM))
    pos = jnp.repeat(positions.reshape(t // per_row, per_row), HALF_ROPE, axis=1)
    invf = jnp.tile(inv_freq, per_row).reshape(1, LANES)
    shape = jax.ShapeDtypeStruct(pos.shape, F32)
    cos, sin = pl.pallas_call(
        _rope_table_kernel,
        in_specs=[_full(pos.shape), _full(invf.shape)],
        out_specs=[_full(pos.shape), _full(pos.shape)],
        out_shape=[shape, shape],
        compiler_params=pltpu.CompilerParams(vmem_limit_bytes=VMEM_LIMIT_BYTES),
        name="rope_tables",
    )(pos, invf)
    return cos.reshape(t, HALF_ROPE), sin.reshape(t, HALF_ROPE)


def _in_kernel(x_ref, ctab_ref, stab_ref, gmix_ref, win_ref, gql_ref, wuq_ref, wuqs_ref, gkvl_ref, wuk_ref,
               wuv_ref, gq_ref, gqs_ref, gk_ref, gks_ref, q_ref, k_ref, v_ref, y_ref):
    h = _rms(x_ref[...], gmix_ref[...]).astype(BF16)
    p = jnp.dot(h, win_ref[...], preferred_element_type=F32)
    o_kv = Q_LORA
    o_pe = o_kv + KV_LORA
    o_ps = o_pe + HEAD_PAD
    o_a = o_ps + HEAD_PAD
    o_g = o_a + CONV_WIDTH
    cq, ckv, kpe, kpe_sw = p[:, :o_kv], p[:, o_kv:o_pe], p[:, o_pe:o_ps], p[:, o_ps:o_a]
    y_ref[...] = p[:, o_a:o_g] * jax.nn.sigmoid(p[:, o_g:])

    cqn = _rms(cq, gql_ref[...]).astype(BF16)
    q = jnp.dot(cqn, wuq_ref[...], preferred_element_type=F32)
    q_sw = jnp.dot(cqn, wuqs_ref[...], preferred_element_type=F32)
    ckvn = _rms(ckv, gkvl_ref[...]).astype(BF16)
    kn = jnp.dot(ckvn, wuk_ref[...], preferred_element_type=F32)
    wide_lane = lax.broadcasted_iota(jnp.int32, (1, N_HEADS * HEAD_PAD), 1)
    ones_lane = jnp.where(wide_lane % HEAD_PAD == V_DIM, 1.0, 0.0)
    v_ref[...] = (jnp.dot(ckvn, wuv_ref[...], preferred_element_type=F32) + ones_lane).astype(BF16)

    scale = QK_DIM ** -0.5 * LOG2_E
    cq_tab = ctab_ref[...] * (gq_ref[...] * scale)
    sq_tab = stab_ref[...] * (gqs_ref[...] * scale)
    ck_tab = ctab_ref[...] * gk_ref[...]
    sk_tab = stab_ref[...] * gks_ref[...]
    k_pe = kpe * ck_tab + kpe_sw * sk_tab
    ss_pe = jnp.sum(kpe * kpe, axis=-1, keepdims=True)
    for hd in range(N_HEADS):
        sl = slice(hd * HEAD_PAD, (hd + 1) * HEAD_PAD)
        qh = q[:, sl]
        rq = lax.rsqrt(jnp.sum(qh * qh, axis=-1, keepdims=True) * (1.0 / QK_DIM) + EPS)
        q_ref[:, sl] = ((qh * cq_tab + q_sw[:, sl] * sq_tab) * rq).astype(BF16)
        kh = kn[:, sl]
        rk = lax.rsqrt((jnp.sum(kh * kh, axis=-1, keepdims=True) + ss_pe) * (1.0 / QK_DIM) + EPS)
        k_ref[:, sl] = ((kh * ck_tab + k_pe) * rk).astype(BF16)


def _input_stage(x2, ctab, stab, gmix, win, gql, wuq, wuqs, gkvl, wuk, wuv, gq, gqs, gk, gks):
    t = x2.shape[0]
    wide = N_HEADS * HEAD_PAD
    row = lambda w: pl.BlockSpec((IN_ROWS, w), lambda i: (i, 0))
    consts = (gmix, win, gql, wuq, wuqs, gkvl, wuk, wuv, gq, gqs, gk, gks)
    return pl.pallas_call(
        _in_kernel,
        grid=(t // IN_ROWS,),
        in_specs=[row(D_MODEL), row(HEAD_PAD), row(HEAD_PAD)] + [_full(a.shape) for a in consts],
        out_specs=[row(wide), row(wide), row(wide), row(CONV_WIDTH)],
        out_shape=[jax.ShapeDtypeStruct((t, wide), BF16), jax.ShapeDtypeStruct((t, wide), BF16),
                   jax.ShapeDtypeStruct((t, wide), BF16), jax.ShapeDtypeStruct((t, CONV_WIDTH), F32)],
        compiler_params=_params("parallel"),
        name="input_stage",
    )(x2, ctab, stab, *consts)


def _attn_kernel(q_ref, k_ref, v_ref, o_ref, *scratch):
    m_refs, acc_refs = scratch[:N_HEADS], scratch[N_HEADS:]
    i = pl.program_id(1)
    dn = (((1,), (1,)), ((), ()))

    def step(r0, diagonal):
        old = None if diagonal else [(m_refs[hd][...], acc_refs[hd][...]) for hd in range(N_HEADS)]
        new = []
        def scores(hd):
            sl = slice(hd * HEAD_PAD, (hd + 1) * HEAD_PAD)
            return lax.dot_general(q_ref[:, sl], k_ref[pl.ds(r0, ATT_K), sl], dn, preferred_element_type=F32)

        ahead = [scores(hd) for hd in range(ATT_AHEAD)]
        for hd in range(N_HEADS):
            sl = slice(hd * HEAD_PAD, (hd + 1) * HEAD_PAD)
            s = ahead.pop(0)
            if hd + ATT_AHEAD < N_HEADS:
                ahead.append(scores(hd + ATT_AHEAD))
            if diagonal:
                rc = lax.broadcasted_iota(jnp.int32, (ATT_Q, ATT_K), 0) // CHUNK
                cc = lax.broadcasted_iota(jnp.int32, (ATT_Q, ATT_K), 1) // CHUNK
                s = jnp.where(cc <= rc, s, -jnp.inf)
            m_new = jnp.broadcast_to(jnp.max(s, axis=-1, keepdims=True), (ATT_Q, HEAD_PAD))
            if not diagonal:
                m_new = jnp.maximum(old[hd][0], m_new)
            m_wide = jnp.concatenate([m_new] * (ATT_K // HEAD_PAD), axis=1)
            acc = jnp.dot(jnp.exp2(s - m_wide).astype(BF16), v_ref[pl.ds(r0, ATT_K), sl],
                          preferred_element_type=F32)
            if not diagonal:
                acc = jnp.exp2(old[hd][0] - m_new) * old[hd][1] + acc
            new.append((m_new, acc))
        for hd in range(N_HEADS):
            m_refs[hd][...], acc_refs[hd][...] = new[hd]

    step(pl.multiple_of(i * ATT_Q, ATT_Q), True)

    def body(j, _):
        step(pl.multiple_of(j * ATT_K, ATT_K), False)
        return 0

    lax.fori_loop(0, i, body, 0)
    lane = lax.broadcasted_iota(jnp.int32, (1, HEAD_PAD), 1)
    for hd in range(N_HEADS):
        sl = slice(hd * HEAD_PAD, (hd + 1) * HEAD_PAD)
        acc = acc_refs[hd][...]
        o_ref[:, sl] = jnp.where(lane < V_DIM, acc / acc[:, V_DIM:V_DIM + 1], 0.0).astype(BF16)


def _attention(q, k, v, batch, seq):
    assert ATT_Q == ATT_K and seq % ATT_Q == 0 and ATT_Q % CHUNK == 0
    nq = seq // ATT_Q
    wide = N_HEADS * HEAD_PAD
    qspec = pl.BlockSpec((ATT_Q, wide), lambda b, i: (b * nq + i, 0))
    kvspec = pl.BlockSpec((seq, wide), lambda b, i: (b, 0))
    return pl.pallas_call(
        _attn_kernel,
        grid=(batch, nq),
        in_specs=[qspec, kvspec, kvspec],
        out_specs=qspec,
        out_shape=jax.ShapeDtypeStruct(q.shape, BF16),
        scratch_shapes=[pltpu.VMEM((ATT_Q, HEAD_PAD), F32)] * (2 * N_HEADS),
        compiler_params=_params("parallel", "arbitrary"),
        name="attention",
    )(q, k, v)


def _conv_kernel(y_ref, w_ref, b_ref, lng_ref, lnb_ref, og_ref, o_ref, pad_ref, phase_ref):
    seq = y_ref.shape[0]
    pad_ref[0:CONV_HALO, :] = jnp.zeros((CONV_HALO, CONV_WIDTH), F32)
    pad_ref[CONV_HALO:, :] = y_ref[...]
    first = CONV_HALO - (CONV_TAPS - 1)

    def body(i, _):
        r0 = pl.multiple_of(i * CONV_ROWS, CONV_ROWS)
        win = pad_ref[pl.ds(r0, CONV_ROWS + CONV_HALO), :]
        acc = jnp.zeros((CONV_ROWS, CONV_WIDTH), F32)
        for s in range(SUBLANES):
            offs = [o for o in range(first, first + CONV_TAPS) if o % SUBLANES == s]
            span = max(offs) - s + CONV_ROWS
            if s:
                phase_ref[s, 0:span, :] = win[s:s + span, :]
            for o in offs:
                rows = (pad_ref[pl.ds(pl.multiple_of(r0 + o, SUBLANES), CONV_ROWS), :] if s == 0
                        else phase_ref[s, o - s:o - s + CONV_ROWS, :])
                acc = acc + w_ref[o - first:o - first + 1, :] * rows
        acc = acc + b_ref[...]
        xc = acc - jnp.mean(acc, axis=-1, keepdims=True)
        ln = xc * lax.rsqrt(jnp.mean(xc * xc, axis=-1, keepdims=True) + EPS) * lng_ref[...] + lnb_ref[...]
        z = ln * jax.nn.sigmoid(ln)
        o_ref[pl.ds(r0, CONV_ROWS), :] = _rms(z, og_ref[...]).astype(BF16)
        return 0

    lax.fori_loop(0, seq // CONV_ROWS, body, 0)


def _conv_branch(y, w, b, lng, lnb, og, batch, seq):
    spec = pl.BlockSpec((seq, CONV_WIDTH), lambda bi: (bi, 0))
    return pl.pallas_call(
        _conv_kernel,
        grid=(batch,),
        in_specs=[spec, _full(w.shape), _full(b.shape), _full(lng.shape), _full(lnb.shape), _full(og.shape)],
        out_specs=spec,
        out_shape=jax.ShapeDtypeStruct(y.shape, BF16),
        scratch_shapes=[pltpu.VMEM((seq + CONV_HALO, CONV_WIDTH), F32),
                        pltpu.VMEM((SUBLANES, CONV_ROWS + CONV_HALO, CONV_WIDTH), F32)],
        compiler_params=_params("parallel"),
        name="conv_branch",
    )(y, w, b, lng, lnb, og)


def _mix_kernel(attn_ref, conv_ref, x_ref, ga_ref, woa_ref, woc_ref, gffn_ref, rwh_ref, rwl_ref, rb_ref,
                tri_ref, x1_ref, h2_ref, idx_ref, gate_ref, rank_ref, cnt_ref, base_ref):
    @pl.when(pl.program_id(0) == 0)
    def _():
        base_ref[...] = jnp.zeros(base_ref.shape, F32)

    a = attn_ref[...].astype(F32)
    attn_width = N_HEADS * V_DIM
    an = a * lax.rsqrt(jnp.sum(a * a, axis=-1, keepdims=True) * (1.0 / attn_width) + EPS) * ga_ref[...]
    x1 = (x_ref[...] + jnp.dot(an.astype(BF16), woa_ref[...], preferred_element_type=F32)
          + jnp.dot(conv_ref[...], woc_ref[...], preferred_element_type=F32))
    x1_ref[...] = x1
    h2 = _rms(x1, gffn_ref[...])
    _store_tile_rows(h2_ref, h2)

    hi = h2.astype(BF16)
    lo = (h2 - hi.astype(F32)).astype(BF16)
    dn = (((1,), (1,)), ((), ()))
    rwh, rwl = rwh_ref[...], rwl_ref[...]
    logits = (lax.dot_general(rwh, hi, dn, preferred_element_type=F32)
              + lax.dot_general(rwh, lo, dn, preferred_element_type=F32)
              + lax.dot_general(rwl, hi, dn, preferred_element_type=F32)) + rb_ref[...]

    rows = logits.shape[1]
    eidx = lax.broadcasted_iota(jnp.int32, (N_EXPERTS, rows), 0).astype(F32)
    work = logits
    sels, vals = [], []
    for k in range(TOP_K):
        mx = jnp.max(work, axis=0, keepdims=True)
        first = jnp.min(jnp.where(work == mx, eidx, float(N_EXPERTS)), axis=0, keepdims=True)
        sel = eidx == first
        work = jnp.where(sel, -jnp.inf, work)
        sels.append(sel)
        vals.append(mx)
        idx_ref[k:k + 1, :] = first.astype(jnp.int32)
    exps = [jnp.exp(v - vals[0]) for v in vals]
    denom = exps[0] + exps[1] + exps[2] + exps[3]
    for k in range(TOP_K):
        gate_ref[k:k + 1, :] = exps[k] / denom

    member = jnp.where(sels[0] | sels[1] | sels[2] | sels[3], 1.0, 0.0)
    before = jnp.dot(member.astype(BF16), tri_ref[...], preferred_element_type=F32)
    posn = base_ref[:, 0:1] + before
    for k in range(TOP_K):
        rank_ref[k:k + 1, :] = jnp.sum(jnp.where(sels[k], posn, 0.0), axis=0, keepdims=True).astype(jnp.int32)
    base_ref[...] = base_ref[...] + jnp.sum(member, axis=1, keepdims=True)
    cnt_ref[...] = base_ref[...].astype(jnp.int32)


def _mix_stage(attn, conv, x2, ga, woa, woc, gffn, rwh, rwl, rb, tri):
    t = x2.shape[0]
    row = lambda w: pl.BlockSpec((MIX_ROWS, w), lambda i: (i, 0))
    col = pl.BlockSpec((TOP_K, MIX_ROWS), lambda i: (0, i))
    return pl.pallas_call(
        _mix_kernel,
        grid=(t // MIX_ROWS,),
        in_specs=[row(attn.shape[1]), row(CONV_WIDTH), row(D_MODEL), _full(ga.shape), _full(woa.shape),
                  _full(woc.shape), _full(gffn.shape), _full(rwh.shape), _full(rwl.shape), _full(rb.shape),
                  _full(tri.shape)],
        out_specs=[row(D_MODEL), pl.BlockSpec((MIX_ROWS * ROW_TILES, LANES), lambda i: (i, 0)), col, col, col,
                   _full((N_EXPERTS, LANES))],
        out_shape=[jax.ShapeDtypeStruct((t, D_MODEL), F32), jax.ShapeDtypeStruct((t * ROW_TILES, LANES), F32),
                   jax.ShapeDtypeStruct((TOP_K, t), jnp.int32), jax.ShapeDtypeStruct((TOP_K, t), F32),
                   jax.ShapeDtypeStruct((TOP_K, t), jnp.int32),
                   jax.ShapeDtypeStruct((N_EXPERTS, LANES), jnp.int32)],
        scratch_shapes=[pltpu.VMEM((N_EXPERTS, LANES), F32)],
        compiler_params=_params("arbitrary"),
        name="mix_router",
    )(attn, conv, x2, ga, woa, woc, gffn, rwh, rwl, rb, tri)


def _dispatch_kernel(ends_ref, padded_ref, dest_ref, h_ref, xs_ref, zero_ref, sem, zsem):
    rows = h_ref.shape[0] // ROW_TILES
    block = MOE_ROWS * ROW_TILES
    n_blocks = xs_ref.shape[0] // block

    @pl.when(pl.program_id(0) == 0)
    def _():
        zero_ref[...] = jnp.zeros(zero_ref.shape, F32)

        def zero_copy(r0):
            return pltpu.make_async_copy(
                zero_ref, xs_ref.at[pl.ds(pl.multiple_of(r0 * ROW_TILES, block), block)], zsem)

        for start in (True, False):
            for e in range(N_EXPERTS):
                @pl.when(padded_ref[e] > 0)
                def _():
                    cp = zero_copy(ends_ref[e] - MOE_ROWS)
                    cp.start() if start else cp.wait()

            def tail(b, _):
                cp = zero_copy(b * MOE_ROWS)
                cp.start() if start else cp.wait()
                return 0

            lax.fori_loop(ends_ref[N_EXPERTS - 1] // MOE_ROWS, n_blocks, tail, 0)

    def issue(t, _):
        for k in range(TOP_K):
            pltpu.make_async_copy(_tile_row(h_ref, t), _tile_row(xs_ref, dest_ref[k, t]), sem).start(
                priority=k % DMA_PRIORITIES)
        return 0

    lax.fori_loop(0, rows, issue, 0, unroll=ISSUE_UNROLL)
    for _ in range(TOP_K):
        pltpu.make_async_copy(h_ref, xs_ref.at[pl.ds(0, rows * ROW_TILES)], sem).wait()


def _dispatch(ends, padded, dest, h2, n_blocks):
    t = h2.shape[0] // ROW_TILES
    grid_spec = pltpu.PrefetchScalarGridSpec(
        num_scalar_prefetch=2,
        grid=(t // DISPATCH_ROWS,),
        in_specs=[pl.BlockSpec((TOP_K, DISPATCH_ROWS), lambda i, *_: (0, i), memory_space=pltpu.SMEM),
                  pl.BlockSpec((DISPATCH_ROWS * ROW_TILES, LANES), lambda i, *_: (i, 0))],
        out_specs=pl.BlockSpec(memory_space=pl.ANY),
        scratch_shapes=[pltpu.VMEM((MOE_ROWS * ROW_TILES, LANES), F32), pltpu.SemaphoreType.DMA(()),
                        pltpu.SemaphoreType.DMA(())],
    )
    return pl.pallas_call(
        _dispatch_kernel,
        grid_spec=grid_spec,
        out_shape=jax.ShapeDtypeStruct((n_blocks * MOE_ROWS * ROW_TILES, LANES), F32),
        compiler_params=_params("arbitrary"),
        name="dispatch_rows",
    )(ends, padded, dest, h2)


def _moe_kernel(be_ref, nact_ref, xs_ref, wgu_ref, bgu_ref, wd_ref, bd_ref, ys_ref, wgu_bf, wd_bf):
    i = pl.program_id(0)
    active = i < nact_ref[0]

    @pl.when(jnp.logical_not(active))
    def _():
        ys_ref[...] = jnp.zeros(ys_ref.shape, F32)

    @pl.when(active & ((i == 0) | (be_ref[i] != be_ref[jnp.maximum(i - 1, 0)])))
    def _():
        wgu_bf[...] = wgu_ref[0].astype(BF16)
        wd_bf[...] = wd_ref[0].astype(BF16)

    @pl.when(active)
    def _():
        x = _load_tile_rows(xs_ref, MOE_ROWS).astype(BF16)
        gu = jnp.dot(x, wgu_bf[...], preferred_element_type=F32) + bgu_ref[0]
        gate = jnp.minimum(gu[:, :D_FF], SWIGLU_LIMIT)
        up = jnp.clip(gu[:, D_FF:], -SWIGLU_LIMIT, SWIGLU_LIMIT)
        mid = (up + 1.0) * (gate * jax.nn.sigmoid(gate * SWIGLU_ALPHA))
        _store_tile_rows(ys_ref, jnp.dot(mid.astype(BF16), wd_bf[...], preferred_element_type=F32) + bd_ref[0])


def _moe(block_expert, n_active, xs, wgu, bgu, wd, bd):
    block = MOE_ROWS * ROW_TILES
    n_blocks = xs.shape[0] // block
    grid_spec = pltpu.PrefetchScalarGridSpec(
        num_scalar_prefetch=2,
        grid=(n_blocks,),
        in_specs=[
            pl.BlockSpec((block, LANES), lambda i, be, na: (jnp.minimum(i, na[0] - 1), 0)),
            pl.BlockSpec((1, D_MODEL, 2 * D_FF), lambda i, be, na: (be[i], 0, 0)),
            pl.BlockSpec((1, 1, 2 * D_FF), lambda i, be, na: (be[i], 0, 0)),
            pl.BlockSpec((1, D_FF, D_MODEL), lambda i, be, na: (be[i], 0, 0)),
            pl.BlockSpec((1, 1, D_MODEL), lambda i, be, na: (be[i], 0, 0)),
        ],
        out_specs=pl.BlockSpec((block, LANES), lambda i, be, na: (i, 0)),
        scratch_shapes=[pltpu.VMEM((D_MODEL, 2 * D_FF), BF16), pltpu.VMEM((D_FF, D_MODEL), BF16)],
    )
    return pl.pallas_call(
        _moe_kernel,
        grid_spec=grid_spec,
        out_shape=jax.ShapeDtypeStruct(xs.shape, F32),
        compiler_params=_params("arbitrary"),
        name="expert_mlp",
    )(block_expert, n_active, xs, wgu, bgu, wd, bd)


def _combine_kernel(dest_ref, dest_next_ref, x1_ref, gate_ref, ys_ref, o_ref, buf_ref, sems):
    i, n = pl.program_id(0), pl.num_programs(0)
    rows = x1_ref.shape[0]

    def gather(idx_ref, slot):
        def issue(t, _):
            for k in range(TOP_K):
                pltpu.make_async_copy(_tile_row(ys_ref, idx_ref[k, t]), _tile_row(buf_ref, t, (slot, k)),
                                      sems.at[slot]).start(priority=k % DMA_PRIORITIES)
            return 0

        lax.fori_loop(0, rows, issue, 0, unroll=ISSUE_UNROLL)

    @pl.when(i == 0)
    def _():
        gather(dest_ref, 0)

    @pl.when(i + 1 < n)
    def _():
        gather(dest_next_ref, (i + 1) % 2)

    slot = i % 2
    for k in range(TOP_K):
        pltpu.make_async_copy(ys_ref.at[pl.ds(0, rows * ROW_TILES)], buf_ref.at[slot, k], sems.at[slot]).wait()
    acc = x1_ref[...]
    for k in range(TOP_K):
        acc = acc + gate_ref[:, k:k + 1] * _load_tile_rows(buf_ref, rows, (slot, k))
    o_ref[...] = acc


def _combine(dest, x1, gate_rows, ys):
    t = x1.shape[0]
    steps = t // COMBINE_ROWS
    row = lambda w: pl.BlockSpec((COMBINE_ROWS, w), lambda i: (i, 0))
    slots = lambda nxt: pl.BlockSpec((TOP_K, COMBINE_ROWS), lambda i: (0, jnp.minimum(i + nxt, steps - 1)),
                                     memory_space=pltpu.SMEM)
    return pl.pallas_call(
        _combine_kernel,
        grid=(steps,),
        in_specs=[slots(0), slots(1), row(D_MODEL), row(TOP_K), pl.BlockSpec(memory_space=pl.ANY)],
        out_specs=row(D_MODEL),
        out_shape=jax.ShapeDtypeStruct(x1.shape, F32),
        scratch_shapes=[pltpu.VMEM((2, TOP_K, COMBINE_ROWS * ROW_TILES, LANES), F32),
                        pltpu.SemaphoreType.DMA((2,))],
        compiler_params=_params("arbitrary"),
        name="combine_rows",
    )(dest, dest, x1, gate_rows, ys)


def _head_blocks(w, width):
    r = w.shape[0]
    w = w.reshape(r, N_HEADS, width)
    return jnp.pad(w, ((0, 0), (0, 0), (0, HEAD_PAD - width))).reshape(r, N_HEADS * HEAD_PAD)


def _lane_row(v, offset=0):
    return jnp.pad(v, (offset, HEAD_PAD - offset - v.shape[0])).reshape(1, HEAD_PAD)


def _swap_rope(w, sign):
    lo, hi = w[..., NOPE_DIM:NOPE_DIM + HALF_ROPE], w[..., NOPE_DIM + HALF_ROPE:QK_DIM]
    return jnp.concatenate([jnp.zeros_like(w[..., :NOPE_DIM]), sign * hi, lo], axis=-1)


def kernel(x, positions, norm_mix_g, w_in, q_latent_g, w_uq, kv_latent_g, w_ukv, q_head_g, k_head_g, conv_dw_w, conv_dw_b, conv_ln_g, conv_ln_b, attn_out_g, conv_out_g, w_out, norm_ffn_g, router_w, router_b, w_gate_up, b_gate_up, w_down, b_down):
    batch, seq, _ = x.shape
    t = batch * seq
    depth = norm_mix_g.shape[0]
    x2 = x.reshape(t, D_MODEL)
    cos, sin = _rope_tables(positions)
    ctab = jnp.concatenate([jnp.ones((t, NOPE_DIM), F32), cos, cos, jnp.zeros((t, HEAD_PAD - QK_DIM), F32)], axis=1)
    stab = jnp.concatenate([jnp.zeros((t, NOPE_DIM), F32), sin, sin, jnp.zeros((t, HEAD_PAD - QK_DIM), F32)], axis=1)
    tri = jnp.triu(jnp.ones((MIX_ROWS, MIX_ROWS), BF16), 1)
    o_kv = Q_LORA
    o_pe = o_kv + KV_LORA
    o_u = o_pe + ROPE_DIM

    for l in range(depth):
        wi = w_in[l]
        w_pe = wi[:, o_pe:o_u]
        w_pe_sw = jnp.concatenate([-w_pe[:, HALF_ROPE:], w_pe[:, :HALF_ROPE]], axis=1)
        pe_block = lambda w: jnp.pad(w, ((0, 0), (NOPE_DIM, HEAD_PAD - QK_DIM)))
        win = jnp.concatenate([wi[:, :o_pe], pe_block(w_pe), pe_block(w_pe_sw), wi[:, o_u:]], axis=1).astype(BF16)
        wq = w_uq[l].reshape(Q_LORA, N_HEADS, QK_DIM)
        wuq = _head_blocks(w_uq[l], QK_DIM).astype(BF16)
        wuqs = _head_blocks(_swap_rope(wq, -1.0).reshape(Q_LORA, -1), QK_DIM).astype(BF16)
        wkv = w_ukv[l].reshape(KV_LORA, N_HEADS, NOPE_DIM + V_DIM)
        wuk = _head_blocks(wkv[:, :, :NOPE_DIM].reshape(KV_LORA, -1), NOPE_DIM).astype(BF16)
        wuv = _head_blocks(wkv[:, :, NOPE_DIM:].reshape(KV_LORA, -1), V_DIM).astype(BF16)
        ga = _head_blocks(attn_out_g[l].reshape(1, -1), V_DIM)
        wo = w_out[l]
        woa = jnp.pad(wo[:N_HEADS * V_DIM].reshape(N_HEADS, V_DIM, D_MODEL),
                      ((0, 0), (0, HEAD_PAD - V_DIM), (0, 0))).reshape(N_HEADS * HEAD_PAD, D_MODEL).astype(BF16)
        woc = wo[N_HEADS * V_DIM:].astype(BF16)
        rwt = router_w[l].T
        rwh = rwt.astype(BF16)
        rwl = (rwt - rwh.astype(F32)).astype(BF16)

        q, k, v, y = _input_stage(
            x2, ctab, stab, norm_mix_g[l].reshape(1, -1), win, q_latent_g[l].reshape(1, -1), wuq, wuqs,
            kv_latent_g[l].reshape(1, -1), wuk, wuv, _lane_row(q_head_g[l]),
            _lane_row(_swap_rope(q_head_g[l], 1.0)), _lane_row(k_head_g[l]), _lane_row(_swap_rope(k_head_g[l], 1.0)))
        attn = _attention(q, k, v, batch, seq)
        conv = _conv_branch(y, conv_dw_w[l], conv_dw_b[l].reshape(1, -1), conv_ln_g[l].reshape(1, -1),
                            conv_ln_b[l].reshape(1, -1), conv_out_g[l].reshape(1, -1), batch, seq)
        x1, h2, idx, gate, rank, cnt = _mix_stage(
            attn, conv, x2, ga, woa, woc, norm_ffn_g[l].reshape(1, -1), rwh, rwl,
            router_b[l].reshape(-1, 1), tri)

        counts = cnt[:, 0]
        padded = (counts + MOE_ROWS - 1) // MOE_ROWS * MOE_ROWS
        ends = jnp.cumsum(padded)
        starts = ends - padded
        experts = jnp.arange(N_EXPERTS, dtype=jnp.int32)
        dest = rank + jnp.sum(jnp.where(idx[None] == experts[:, None, None], starts[:, None, None], 0), axis=0)
        n_blocks = (t * TOP_K + N_EXPERTS * (MOE_ROWS - 1)) // MOE_ROWS
        n_active = (ends[-1] // MOE_ROWS).astype(jnp.int32)
        blk = jnp.minimum(jnp.arange(n_blocks, dtype=jnp.int32), n_active - 1)
        be = jnp.minimum(jnp.sum((ends[None, :] <= (blk * MOE_ROWS)[:, None]).astype(jnp.int32), axis=1),
                         N_EXPERTS - 1)

        xs = _dispatch(ends.astype(jnp.int32), padded.astype(jnp.int32), dest, h2, n_blocks)
        ys = _moe(be, n_active.reshape(1), xs, w_gate_up[l], b_gate_up[l].reshape(N_EXPERTS, 1, -1),
                  w_down[l], b_down[l].reshape(N_EXPERTS, 1, -1))
        x2 = _combine(dest, x1, gate.T, ys)
    return x2.reshape(batch, seq, D_MODEL)
```

```python
import functools

import jax
import jax.numpy as jnp
from jax import lax
from jax.experimental import pallas as pl
from jax.experimental.pallas import tpu as pltpu

D_MODEL = 1024
N_HEADS = 8
NOPE_DIM = 64
ROPE_DIM = 32
QK_DIM = NOPE_DIM + ROPE_DIM
V_DIM = 64
Q_LORA = 384
KV_LORA = 128
CONV_WIDTH = 512
CONV_TAPS = 31
N_EXPERTS = 32
TOP_K = 4
D_FF = 1024
CHUNK = 64
ROPE_THETA = 10000.0
EPS = 1e-6
SWIGLU_ALPHA = 1.702
SWIGLU_LIMIT = 7.0
LOG2_E = 1.4426950408889634

LANES = 128
SUBLANES = 8
HEAD_PAD = LANES
HALF_ROPE = ROPE_DIM // 2
VMEM_LIMIT_BYTES = 56 * 1024 * 1024

IN_ROWS = 512
ATT_Q = 512
ATT_K = 512
ATT_AHEAD = 2
CONV_ROWS = 64
CONV_HALO = 32
MIX_ROWS = 512
DISPATCH_ROWS = 256
MOE_ROWS = 512
COMBINE_ROWS = 256
ISSUE_UNROLL = 4
DMA_PRIORITIES = 2

F32 = jnp.float32
BF16 = jnp.bfloat16


def _params(*semantics, flags=None):
    return pltpu.CompilerParams(dimension_semantics=semantics, vmem_limit_bytes=VMEM_LIMIT_BYTES, flags=flags)


def _rms(x, g):
    return x * lax.rsqrt(jnp.mean(x * x, axis=-1, keepdims=True) + EPS) * g


def _full(shape):
    return pl.BlockSpec(shape, lambda *_: (0,) * len(shape))


ROW_TILES = D_MODEL // LANES
assert ROW_TILES == SUBLANES


def _load_tile_rows(ref, rows, lead=()):
    return jnp.concatenate([ref[lead + (pl.ds(c, rows, stride=ROW_TILES), slice(None))] for c in range(ROW_TILES)],
                           axis=1)


def _store_tile_rows(ref, value):
    rows = value.shape[0]
    for c in range(ROW_TILES):
        ref[pl.ds(c, rows, stride=ROW_TILES), :] = value[:, c * LANES:(c + 1) * LANES]


def _tile_row(ref, r, lead=()):
    return ref.at[lead + (pl.ds(pl.multiple_of(r * ROW_TILES, ROW_TILES), ROW_TILES),)]


def _rope_table_kernel(pos_ref, invf_ref, cos_ref, sin_ref):
    ang = pos_ref[...].astype(F32) * invf_ref[...]
    cos_ref[...] = jnp.cos(ang)
    sin_ref[...] = jnp.sin(ang)


def _rope_tables(positions):
    t = positions.size
    per_row = LANES // HALF_ROPE
    inv_freq = 1.0 / (ROPE_THETA ** (jnp.arange(0, ROPE_DIM, 2, dtype=F32) / ROPE_DIM))
    pos = jnp.repeat(positions.reshape(t // per_row, per_row), HALF_ROPE, axis=1)
    invf = jnp.tile(inv_freq, per_row).reshape(1, LANES)
    shape = jax.ShapeDtypeStruct(pos.shape, F32)
    cos, sin = pl.pallas_call(
        _rope_table_kernel,
        in_specs=[_full(pos.shape), _full(invf.shape)],
        out_specs=[_full(pos.shape), _full(pos.shape)],
        out_shape=[shape, shape],
        compiler_params=pltpu.CompilerParams(vmem_limit_bytes=VMEM_LIMIT_BYTES),
        name="rope_tables",
    )(pos, invf)
    return cos.reshape(t, HALF_ROPE), sin.reshape(t, HALF_ROPE)


def _in_kernel(x_ref, ctab_ref, stab_ref, gmix_ref, win_ref, gql_ref, wuq_ref, wuqs_ref, gkvl_ref, wuk_ref,
               wuv_ref, gq_ref, gqs_ref, gk_ref, gks_ref, q_ref, k_ref, v_ref, y_ref):
    h = _rms(x_ref[...], gmix_ref[...]).astype(BF16)
    p = jnp.dot(h, win_ref[...], preferred_element_type=F32)
    o_kv = Q_LORA
    o_pe = o_kv + KV_LORA
    o_ps = o_pe + HEAD_PAD
    o_a = o_ps + HEAD_PAD
    o_g = o_a + CONV_WIDTH
    cq, ckv, kpe, kpe_sw = p[:, :o_kv], p[:, o_kv:o_pe], p[:, o_pe:o_ps], p[:, o_ps:o_a]
    y_ref[...] = p[:, o_a:o_g] * jax.nn.sigmoid(p[:, o_g:])

    cqn = _rms(cq, gql_ref[...]).astype(BF16)
    q = jnp.dot(cqn, wuq_ref[...], preferred_element_type=F32)
    q_sw = jnp.dot(cqn, wuqs_ref[...], preferred_element_type=F32)
    ckvn = _rms(ckv, gkvl_ref[...]).astype(BF16)
    kn = jnp.dot(ckvn, wuk_ref[...], preferred_element_type=F32)
    wide_row = lax.broadcasted_iota(jnp.int32, (N_HEADS * HEAD_PAD, 1), 0)
    ones_row = jnp.where(wide_row % HEAD_PAD == V_DIM, 1.0, 0.0)
    vt = lax.dot_general(wuv_ref[...], ckvn, (((1,), (1,)), ((), ())), preferred_element_type=F32)
    v_ref[...] = (vt + ones_row).astype(BF16)

    scale = QK_DIM ** -0.5 * LOG2_E
    cq_tab = ctab_ref[...] * (gq_ref[...] * scale)
    sq_tab = stab_ref[...] * (gqs_ref[...] * scale)
    ck_tab = ctab_ref[...] * gk_ref[...]
    sk_tab = stab_ref[...] * gks_ref[...]
    k_pe = kpe * ck_tab + kpe_sw * sk_tab
    ss_pe = jnp.sum(kpe * kpe, axis=-1, keepdims=True)
    for hd in range(N_HEADS):
        sl = slice(hd * HEAD_PAD, (hd + 1) * HEAD_PAD)
        qh = q[:, sl]
        rq = lax.rsqrt(jnp.sum(qh * qh, axis=-1, keepdims=True) * (1.0 / QK_DIM) + EPS)
        q_ref[:, sl] = ((qh * cq_tab + q_sw[:, sl] * sq_tab) * rq).astype(BF16)
        kh = kn[:, sl]
        rk = lax.rsqrt((jnp.sum(kh * kh, axis=-1, keepdims=True) + ss_pe) * (1.0 / QK_DIM) + EPS)
        k_ref[:, sl] = ((kh * ck_tab + k_pe) * rk).astype(BF16)


def _input_stage(x2, ctab, stab, gmix, win, gql, wuq, wuqs, gkvl, wuk, wuv, gq, gqs, gk, gks):
    t = x2.shape[0]
    wide = N_HEADS * HEAD_PAD
    row = lambda w: pl.BlockSpec((IN_ROWS, w), lambda i: (i, 0))
    consts = (gmix, win, gql, wuq, wuqs, gkvl, wuk, wuv, gq, gqs, gk, gks)
    return pl.pallas_call(
        _in_kernel,
        grid=(t // IN_ROWS,),
        in_specs=[row(D_MODEL), row(HEAD_PAD), row(HEAD_PAD)] + [_full(a.shape) for a in consts],
        out_specs=[row(wide), row(wide), pl.BlockSpec((wide, IN_ROWS), lambda i: (0, i)), row(CONV_WIDTH)],
        out_shape=[jax.ShapeDtypeStruct((t, wide), BF16), jax.ShapeDtypeStruct((t, wide), BF16),
                   jax.ShapeDtypeStruct((wide, t), BF16), jax.ShapeDtypeStruct((t, CONV_WIDTH), F32)],
        compiler_params=_params("parallel"),
        name="input_stage",
    )(x2, ctab, stab, *consts)


def _attn_kernel(q_ref, k_ref, vt_ref, o_ref, *scratch):
    m_refs, acc_refs = scratch[:N_HEADS], scratch[N_HEADS:]
    i = pl.program_id(1)
    dn = (((1,), (1,)), ((), ()))

    def step(r0, diagonal):
        old = None if diagonal else [(m_refs[hd][...], acc_refs[hd][...]) for hd in range(N_HEADS)]
        new = []

        def scores(hd):
            sl = slice(hd * HEAD_PAD, (hd + 1) * HEAD_PAD)
            return lax.dot_general(k_ref[pl.ds(r0, ATT_K), sl], q_ref[:, sl], dn, preferred_element_type=F32)

        ahead = [scores(hd) for hd in range(ATT_AHEAD)]
        for hd in range(N_HEADS):
            sl = slice(hd * HEAD_PAD, (hd + 1) * HEAD_PAD)
            s = ahead.pop(0)
            if hd + ATT_AHEAD < N_HEADS:
                ahead.append(scores(hd + ATT_AHEAD))
            if diagonal:
                kc = lax.broadcasted_iota(jnp.int32, (ATT_K, ATT_Q), 0) // CHUNK
                qc = lax.broadcasted_iota(jnp.int32, (ATT_K, ATT_Q), 1) // CHUNK
                s = jnp.where(kc <= qc, s, -jnp.inf)
            m_new = jnp.max(s, axis=0, keepdims=True)
            if not diagonal:
                m_new = jnp.maximum(old[hd][0], m_new)
            acc = jnp.dot(vt_ref[sl, pl.ds(r0, ATT_K)], jnp.exp2(s - m_new).astype(BF16),
                          preferred_element_type=F32)
            if not diagonal:
                acc = jnp.exp2(old[hd][0] - m_new) * old[hd][1] + acc
            new.append((m_new, acc))
        for hd in range(N_HEADS):
            m_refs[hd][...], acc_refs[hd][...] = new[hd]

    step(pl.multiple_of(i * ATT_Q, ATT_Q), True)

    def body(j, _):
        step(pl.multiple_of(j * ATT_K, ATT_K), False)
        return 0

    lax.fori_loop(0, i, body, 0)
    row = lax.broadcasted_iota(jnp.int32, (HEAD_PAD, 1), 0)
    for hd in range(N_HEADS):
        acc = acc_refs[hd][...]
        out_t = jnp.where(row < V_DIM, acc / acc[V_DIM:V_DIM + 1, :], 0.0)
        o_ref[:, hd * HEAD_PAD:(hd + 1) * HEAD_PAD] = out_t.T.astype(BF16)


def _attention(q, k, vt, batch, seq):
    assert ATT_Q == ATT_K and seq % ATT_Q == 0 and ATT_Q % CHUNK == 0
    nq = seq // ATT_Q
    wide = N_HEADS * HEAD_PAD
    qspec = pl.BlockSpec((ATT_Q, wide), lambda b, i: (b * nq + i, 0))
    return pl.pallas_call(
        _attn_kernel,
        grid=(batch, nq),
        in_specs=[qspec, pl.BlockSpec((seq, wide), lambda b, i: (b, 0)),
                  pl.BlockSpec((wide, seq), lambda b, i: (0, b))],
        out_specs=qspec,
        out_shape=jax.ShapeDtypeStruct(q.shape, BF16),
        scratch_shapes=([pltpu.VMEM((1, ATT_Q), F32)] * N_HEADS + [pltpu.VMEM((HEAD_PAD, ATT_Q), F32)] * N_HEADS),
        compiler_params=_params("parallel", "arbitrary"),
        name="attention",
    )(q, k, vt)


def _conv_kernel(y_ref, w_ref, b_ref, lng_ref, lnb_ref, og_ref, o_ref, pad_ref, phase_ref):
    seq = y_ref.shape[0]
    pad_ref[0:CONV_HALO, :] = jnp.zeros((CONV_HALO, CONV_WIDTH), F32)
    pad_ref[CONV_HALO:, :] = y_ref[...]
    first = CONV_HALO - (CONV_TAPS - 1)

    def body(i, _):
        r0 = pl.multiple_of(i * CONV_ROWS, CONV_ROWS)
        win = pad_ref[pl.ds(r0, CONV_ROWS + CONV_HALO), :]
        acc = jnp.zeros((CONV_ROWS, CONV_WIDTH), F32)
        for s in range(SUBLANES):
            offs = [o for o in range(first, first + CONV_TAPS) if o % SUBLANES == s]
            span = max(offs) - s + CONV_ROWS
            if s:
                phase_ref[s, 0:span, :] = win[s:s + span, :]
            for o in offs:
                rows = (pad_ref[pl.ds(pl.multiple_of(r0 + o, SUBLANES), CONV_ROWS), :] if s == 0
                        else phase_ref[s, o - s:o - s + CONV_ROWS, :])
                acc = acc + w_ref[o - first:o - first + 1, :] * rows
        acc = acc + b_ref[...]
        xc = acc - jnp.mean(acc, axis=-1, keepdims=True)
        ln = xc * lax.rsqrt(jnp.mean(xc * xc, axis=-1, keepdims=True) + EPS) * lng_ref[...] + lnb_ref[...]
        z = ln * jax.nn.sigmoid(ln)
        o_ref[pl.ds(r0, CONV_ROWS), :] = _rms(z, og_ref[...]).astype(BF16)
        return 0

    lax.fori_loop(0, seq // CONV_ROWS, body, 0)


def _conv_branch(y, w, b, lng, lnb, og, batch, seq):
    spec = pl.BlockSpec((seq, CONV_WIDTH), lambda bi: (bi, 0))
    return pl.pallas_call(
        _conv_kernel,
        grid=(batch,),
        in_specs=[spec, _full(w.shape), _full(b.shape), _full(lng.shape), _full(lnb.shape), _full(og.shape)],
        out_specs=spec,
        out_shape=jax.ShapeDtypeStruct(y.shape, BF16),
        scratch_shapes=[pltpu.VMEM((seq + CONV_HALO, CONV_WIDTH), F32),
                        pltpu.VMEM((SUBLANES, CONV_ROWS + CONV_HALO, CONV_WIDTH), F32)],
        compiler_params=_params("parallel"),
        name="conv_branch",
    )(y, w, b, lng, lnb, og)


def _mix_kernel(attn_ref, conv_ref, x_ref, ga_ref, woa_ref, woc_ref, gffn_ref, rwh_ref, rwl_ref, rb_ref,
                tri_ref, x1_ref, h2_ref, idx_ref, gate_ref, rank_ref, cnt_ref, base_ref):
    @pl.when(pl.program_id(0) == 0)
    def _():
        base_ref[...] = jnp.zeros(base_ref.shape, F32)

    a = attn_ref[...].astype(F32)
    attn_width = N_HEADS * V_DIM
    an = a * lax.rsqrt(jnp.sum(a * a, axis=-1, keepdims=True) * (1.0 / attn_width) + EPS) * ga_ref[...]
    x1 = (x_ref[...] + jnp.dot(an.astype(BF16), woa_ref[...], preferred_element_type=F32)
          + jnp.dot(conv_ref[...], woc_ref[...], preferred_element_type=F32))
    x1_ref[...] = x1
    h2 = _rms(x1, gffn_ref[...])
    _store_tile_rows(h2_ref, h2)

    hi = h2.astype(BF16)
    lo = (h2 - hi.astype(F32)).astype(BF16)
    dn = (((1,), (1,)), ((), ()))
    rwh, rwl = rwh_ref[...], rwl_ref[...]
    logits = (lax.dot_general(rwh, hi, dn, preferred_element_type=F32)
              + lax.dot_general(rwh, lo, dn, preferred_element_type=F32)
              + lax.dot_general(rwl, hi, dn, preferred_element_type=F32)) + rb_ref[...]

    rows = logits.shape[1]
    eidx = lax.broadcasted_iota(jnp.int32, (N_EXPERTS, rows), 0).astype(F32)
    work = logits
    sels, vals = [], []
    for k in range(TOP_K):
        mx = jnp.max(work, axis=0, keepdims=True)
        first = jnp.min(jnp.where(work == mx, eidx, float(N_EXPERTS)), axis=0, keepdims=True)
        sel = eidx == first
        work = jnp.where(sel, -jnp.inf, work)
        sels.append(sel)
        vals.append(mx)
        idx_ref[k:k + 1, :] = first.astype(jnp.int32)
    exps = [jnp.exp(v - vals[0]) for v in vals]
    denom = exps[0] + exps[1] + exps[2] + exps[3]
    for k in range(TOP_K):
        gate_ref[k:k + 1, :] = exps[k] / denom

    member = jnp.where(sels[0] | sels[1] | sels[2] | sels[3], 1.0, 0.0)
    before = jnp.dot(member.astype(BF16), tri_ref[...], preferred_element_type=F32)
    posn = base_ref[:, 0:1] + before
    for k in range(TOP_K):
        rank_ref[k:k + 1, :] = jnp.sum(jnp.where(sels[k], posn, 0.0), axis=0, keepdims=True).astype(jnp.int32)
    base_ref[...] = base_ref[...] + jnp.sum(member, axis=1, keepdims=True)
    cnt_ref[...] = base_ref[...].astype(jnp.int32)


def _mix_stage(attn, conv, x2, ga, woa, woc, gffn, rwh, rwl, rb, tri):
    t = x2.shape[0]
    row = lambda w: pl.BlockSpec((MIX_ROWS, w), lambda i: (i, 0))
    col = pl.BlockSpec((TOP_K, MIX_ROWS), lambda i: (0, i))
    return pl.pallas_call(
        _mix_kernel,
        grid=(t // MIX_ROWS,),
        in_specs=[row(attn.shape[1]), row(CONV_WIDTH), row(D_MODEL), _full(ga.shape), _full(woa.shape),
                  _full(woc.shape), _full(gffn.shape), _full(rwh.shape), _full(rwl.shape), _full(rb.shape),
                  _full(tri.shape)],
        out_specs=[row(D_MODEL), pl.BlockSpec((MIX_ROWS * ROW_TILES, LANES), lambda i: (i, 0)), col, col, col,
                   _full((N_EXPERTS, LANES))],
        out_shape=[jax.ShapeDtypeStruct((t, D_MODEL), F32), jax.ShapeDtypeStruct((t * ROW_TILES, LANES), F32),
                   jax.ShapeDtypeStruct((TOP_K, t), jnp.int32), jax.ShapeDtypeStruct((TOP_K, t), F32),
                   jax.ShapeDtypeStruct((TOP_K, t), jnp.int32),
                   jax.ShapeDtypeStruct((N_EXPERTS, LANES), jnp.int32)],
        scratch_shapes=[pltpu.VMEM((N_EXPERTS, LANES), F32)],
        compiler_params=_params("arbitrary"),
        name="mix_router",
    )(attn, conv, x2, ga, woa, woc, gffn, rwh, rwl, rb, tri)


def _dispatch_kernel(ends_ref, padded_ref, dest_ref, h_ref, xs_ref, zero_ref, sem, zsem):
    rows = h_ref.shape[0] // ROW_TILES
    block = MOE_ROWS * ROW_TILES
    n_blocks = xs_ref.shape[0] // block

    @pl.when(pl.program_id(0) == 0)
    def _():
        zero_ref[...] = jnp.zeros(zero_ref.shape, F32)

        def zero_copy(r0):
            return pltpu.make_async_copy(
                zero_ref, xs_ref.at[pl.ds(pl.multiple_of(r0 * ROW_TILES, block), block)], zsem)

        for start in (True, False):
            for e in range(N_EXPERTS):
                @pl.when(padded_ref[e] > 0)
                def _():
                    cp = zero_copy(ends_ref[e] - MOE_ROWS)
                    cp.start() if start else cp.wait()

            def tail(b, _):
                cp = zero_copy(b * MOE_ROWS)
                cp.start() if start else cp.wait()
                return 0

            lax.fori_loop(ends_ref[N_EXPERTS - 1] // MOE_ROWS, n_blocks, tail, 0)

    def issue(t, _):
        for k in range(TOP_K):
            pltpu.make_async_copy(_tile_row(h_ref, t), _tile_row(xs_ref, dest_ref[k, t]), sem).start(
                priority=k % DMA_PRIORITIES)
        return 0

    lax.fori_loop(0, rows, issue, 0, unroll=ISSUE_UNROLL)
    for _ in range(TOP_K):
        pltpu.make_async_copy(h_ref, xs_ref.at[pl.ds(0, rows * ROW_TILES)], sem).wait()


def _dispatch(ends, padded, dest, h2, n_blocks):
    t = h2.shape[0] // ROW_TILES
    grid_spec = pltpu.PrefetchScalarGridSpec(
        num_scalar_prefetch=2,
        grid=(t // DISPATCH_ROWS,),
        in_specs=[pl.BlockSpec((TOP_K, DISPATCH_ROWS), lambda i, *_: (0, i), memory_space=pltpu.SMEM),
                  pl.BlockSpec((DISPATCH_ROWS * ROW_TILES, LANES), lambda i, *_: (i, 0))],
        out_specs=pl.BlockSpec(memory_space=pl.ANY),
        scratch_shapes=[pltpu.VMEM((MOE_ROWS * ROW_TILES, LANES), F32), pltpu.SemaphoreType.DMA(()),
                        pltpu.SemaphoreType.DMA(())],
    )
    return pl.pallas_call(
        _dispatch_kernel,
        grid_spec=grid_spec,
        out_shape=jax.ShapeDtypeStruct((n_blocks * MOE_ROWS * ROW_TILES, LANES), F32),
        compiler_params=_params("arbitrary"),
        name="dispatch_rows",
    )(ends, padded, dest, h2)


def _moe_kernel(be_ref, nact_ref, xs_ref, wgu_ref, bgu_ref, wd_ref, bd_ref, ys_ref, wgu_bf, wd_bf):
    i = pl.program_id(0)
    active = i < nact_ref[0]

    @pl.when(jnp.logical_not(active))
    def _():
        ys_ref[...] = jnp.zeros(ys_ref.shape, F32)

    @pl.when(active & ((i == 0) | (be_ref[i] != be_ref[jnp.maximum(i - 1, 0)])))
    def _():
        wgu_bf[...] = wgu_ref[0].astype(BF16)
        wd_bf[...] = wd_ref[0].astype(BF16)

    @pl.when(active)
    def _():
        x = _load_tile_rows(xs_ref, MOE_ROWS).astype(BF16)
        gu = jnp.dot(x, wgu_bf[...], preferred_element_type=F32) + bgu_ref[0]
        gate = jnp.minimum(gu[:, :D_FF], SWIGLU_LIMIT)
        up = jnp.clip(gu[:, D_FF:], -SWIGLU_LIMIT, SWIGLU_LIMIT)
        mid = (up + 1.0) * (gate * jax.nn.sigmoid(gate * SWIGLU_ALPHA))
        _store_tile_rows(ys_ref, jnp.dot(mid.astype(BF16), wd_bf[...], preferred_element_type=F32) + bd_ref[0])


def _moe(block_expert, n_active, xs, wgu, bgu, wd, bd):
    block = MOE_ROWS * ROW_TILES
    n_blocks = xs.shape[0] // block
    grid_spec = pltpu.PrefetchScalarGridSpec(
        num_scalar_prefetch=2,
        grid=(n_blocks,),
        in_specs=[
            pl.BlockSpec((block, LANES), lambda i, be, na: (jnp.minimum(i, na[0] - 1), 0)),
            pl.BlockSpec((1, D_MODEL, 2 * D_FF), lambda i, be, na: (be[i], 0, 0)),
            pl.BlockSpec((1, 1, 2 * D_FF), lambda i, be, na: (be[i], 0, 0)),
            pl.BlockSpec((1, D_FF, D_MODEL), lambda i, be, na: (be[i], 0, 0)),
            pl.BlockSpec((1, 1, D_MODEL), lambda i, be, na: (be[i], 0, 0)),
        ],
        out_specs=pl.BlockSpec((block, LANES), lambda i, be, na: (i, 0)),
        scratch_shapes=[pltpu.VMEM((D_MODEL, 2 * D_FF), BF16), pltpu.VMEM((D_FF, D_MODEL), BF16)],
    )
    return pl.pallas_call(
        _moe_kernel,
        grid_spec=grid_spec,
        out_shape=jax.ShapeDtypeStruct(xs.shape, F32),
        compiler_params=_params("arbitrary"),
        name="expert_mlp",
    )(block_expert, n_active, xs, wgu, bgu, wd, bd)


def _combine_kernel(dest_ref, dest_next_ref, x1_ref, gate_ref, ys_ref, o_ref, buf_ref, sems):
    i, n = pl.program_id(0), pl.num_programs(0)
    rows = x1_ref.shape[0]

    def gather(idx_ref, slot):
        def issue(t, _):
            for k in range(TOP_K):
                pltpu.make_async_copy(_tile_row(ys_ref, idx_ref[k, t]), _tile_row(buf_ref, t, (slot, k)),
                                      sems.at[slot]).start(priority=k % DMA_PRIORITIES)
            return 0

        lax.fori_loop(0, rows, issue, 0, unroll=ISSUE_UNROLL)

    @pl.when(i == 0)
    def _():
        gather(dest_ref, 0)

    @pl.when(i + 1 < n)
    def _():
        gather(dest_next_ref, (i + 1) % 2)

    slot = i % 2
    for k in range(TOP_K):
        pltpu.make_async_copy(ys_ref.at[pl.ds(0, rows * ROW_TILES)], buf_ref.at[slot, k], sems.at[slot]).wait()
    acc = x1_ref[...]
    for k in range(TOP_K):
        acc = acc + gate_ref[:, k:k + 1] * _load_tile_rows(buf_ref, rows, (slot, k))
    o_ref[...] = acc


def _combine(dest, x1, gate_rows, ys):
    t = x1.shape[0]
    steps = t // COMBINE_ROWS
    row = lambda w: pl.BlockSpec((COMBINE_ROWS, w), lambda i: (i, 0))
    slots = lambda nxt: pl.BlockSpec((TOP_K, COMBINE_ROWS), lambda i: (0, jnp.minimum(i + nxt, steps - 1)),
                                     memory_space=pltpu.SMEM)
    return pl.pallas_call(
        _combine_kernel,
        grid=(steps,),
        in_specs=[slots(0), slots(1), row(D_MODEL), row(TOP_K), pl.BlockSpec(memory_space=pl.ANY)],
        out_specs=row(D_MODEL),
        out_shape=jax.ShapeDtypeStruct(x1.shape, F32),
        scratch_shapes=[pltpu.VMEM((2, TOP_K, COMBINE_ROWS * ROW_TILES, LANES), F32),
                        pltpu.SemaphoreType.DMA((2,))],
        compiler_params=_params("arbitrary"),
        name="combine_rows",
    )(dest, dest, x1, gate_rows, ys)


def _head_blocks(w, width):
    r = w.shape[0]
    w = w.reshape(r, N_HEADS, width)
    return jnp.pad(w, ((0, 0), (0, 0), (0, HEAD_PAD - width))).reshape(r, N_HEADS * HEAD_PAD)


def _lane_row(v, offset=0):
    return jnp.pad(v, (offset, HEAD_PAD - offset - v.shape[0])).reshape(1, HEAD_PAD)


def _swap_rope(w, sign):
    lo, hi = w[..., NOPE_DIM:NOPE_DIM + HALF_ROPE], w[..., NOPE_DIM + HALF_ROPE:QK_DIM]
    return jnp.concatenate([jnp.zeros_like(w[..., :NOPE_DIM]), sign * hi, lo], axis=-1)


def kernel(x, positions, norm_mix_g, w_in, q_latent_g, w_uq, kv_latent_g, w_ukv, q_head_g, k_head_g, conv_dw_w, conv_dw_b, conv_ln_g, conv_ln_b, attn_out_g, conv_out_g, w_out, norm_ffn_g, router_w, router_b, w_gate_up, b_gate_up, w_down, b_down):
    batch, seq, _ = x.shape
    t = batch * seq
    depth = norm_mix_g.shape[0]
    x2 = x.reshape(t, D_MODEL)
    cos, sin = _rope_tables(positions)
    ctab = jnp.concatenate([jnp.ones((t, NOPE_DIM), F32), cos, cos, jnp.zeros((t, HEAD_PAD - QK_DIM), F32)], axis=1)
    stab = jnp.concatenate([jnp.zeros((t, NOPE_DIM), F32), sin, sin, jnp.zeros((t, HEAD_PAD - QK_DIM), F32)], axis=1)
    tri = jnp.triu(jnp.ones((MIX_ROWS, MIX_ROWS), BF16), 1)
    o_kv = Q_LORA
    o_pe = o_kv + KV_LORA
    o_u = o_pe + ROPE_DIM

    for l in range(depth):
        wi = w_in[l]
        w_pe = wi[:, o_pe:o_u]
        w_pe_sw = jnp.concatenate([-w_pe[:, HALF_ROPE:], w_pe[:, :HALF_ROPE]], axis=1)
        pe_block = lambda w: jnp.pad(w, ((0, 0), (NOPE_DIM, HEAD_PAD - QK_DIM)))
        win = jnp.concatenate([wi[:, :o_pe], pe_block(w_pe), pe_block(w_pe_sw), wi[:, o_u:]], axis=1).astype(BF16)
        wq = w_uq[l].reshape(Q_LORA, N_HEADS, QK_DIM)
        wuq = _head_blocks(w_uq[l], QK_DIM).astype(BF16)
        wuqs = _head_blocks(_swap_rope(wq, -1.0).reshape(Q_LORA, -1), QK_DIM).astype(BF16)
        wkv = w_ukv[l].reshape(KV_LORA, N_HEADS, NOPE_DIM + V_DIM)
        wuk = _head_blocks(wkv[:, :, :NOPE_DIM].reshape(KV_LORA, -1), NOPE_DIM).astype(BF16)
        wuv = _head_blocks(wkv[:, :, NOPE_DIM:].reshape(KV_LORA, -1), V_DIM).T.astype(BF16)
        ga = _head_blocks(attn_out_g[l].reshape(1, -1), V_DIM)
        wo = w_out[l]
        woa = jnp.pad(wo[:N_HEADS * V_DIM].reshape(N_HEADS, V_DIM, D_MODEL),
                      ((0, 0), (0, HEAD_PAD - V_DIM), (0, 0))).reshape(N_HEADS * HEAD_PAD, D_MODEL).astype(BF16)
        woc = wo[N_HEADS * V_DIM:].astype(BF16)
        rwt = router_w[l].T
        rwh = rwt.astype(BF16)
        rwl = (rwt - rwh.astype(F32)).astype(BF16)

        q, k, v, y = _input_stage(
            x2, ctab, stab, norm_mix_g[l].reshape(1, -1), win, q_latent_g[l].reshape(1, -1), wuq, wuqs,
            kv_latent_g[l].reshape(1, -1), wuk, wuv, _lane_row(q_head_g[l]),
            _lane_row(_swap_rope(q_head_g[l], 1.0)), _lane_row(k_head_g[l]), _lane_row(_swap_rope(k_head_g[l], 1.0)))
        attn = _attention(q, k, v, batch, seq)
        conv = _conv_branch(y, conv_dw_w[l], conv_dw_b[l].reshape(1, -1), conv_ln_g[l].reshape(1, -1),
                            conv_ln_b[l].reshape(1, -1), conv_out_g[l].reshape(1, -1), batch, seq)
        x1, h2, idx, gate, rank, cnt = _mix_stage(
            attn, conv, x2, ga, woa, woc, norm_ffn_g[l].reshape(1, -1), rwh, rwl,
            router_b[l].reshape(-1, 1), tri)

        counts = cnt[:, 0]
        padded = (counts + MOE_ROWS - 1) // MOE_ROWS * MOE_ROWS
        ends = jnp.cumsum(padded)
        starts = ends - padded
        experts = jnp.arange(N_EXPERTS, dtype=jnp.int32)
        dest = rank + jnp.sum(jnp.where(idx[None] == experts[:, None, None], starts[:, None, None], 0), axis=0)
        n_blocks = (t * TOP_K + N_EXPERTS * (MOE_ROWS - 1)) // MOE_ROWS
        n_active = (ends[-1] // MOE_ROWS).astype(jnp.int32)
        blk = jnp.minimum(jnp.arange(n_blocks, dtype=jnp.int32), n_active - 1)
        be = jnp.minimum(jnp.sum((ends[None, :] <= (blk * MOE_ROWS)[:, None]).astype(jnp.int32), axis=1),
                         N_EXPERTS - 1)

        xs = _dispatch(ends.astype(jnp.int32), padded.astype(jnp.int32), dest, h2, n_blocks)
        ys = _moe(be, n_active.reshape(1), xs, w_gate_up[l], b_gate_up[l].reshape(N_EXPERTS, 1, -1),
                  w_down[l], b_down[l].reshape(N_EXPERTS, 1, -1))
        x2 = _combine(dest, x1, gate.T, ys)
    return x2.reshape(batch, seq, D_MODEL)
```

```python
import functools

import jax
import jax.numpy as jnp
from jax import lax
from jax.experimental import pallas as pl
from jax.experimental.pallas import tpu as pltpu

D_MODEL = 1024
N_HEADS = 8
NOPE_DIM = 64
ROPE_DIM = 32
QK_DIM = NOPE_DIM + ROPE_DIM
V_DIM = 64
Q_LORA = 384
KV_LORA = 128
CONV_WIDTH = 512
CONV_TAPS = 31
N_EXPERTS = 32
TOP_K = 4
D_FF = 1024
CHUNK = 64
ROPE_THETA = 10000.0
EPS = 1e-6
SWIGLU_ALPHA = 1.702
SWIGLU_LIMIT = 7.0
LOG2_E = 1.4426950408889634

LANES = 128
SUBLANES = 8
HEAD_PAD = LANES
HALF_ROPE = ROPE_DIM // 2
VMEM_LIMIT_BYTES = 56 * 1024 * 1024

IN_ROWS = 512
ATT_Q = 512
ATT_K = 512
ATT_AHEAD = 2
CONV_ROWS = 64
CONV_HALO = 32
MIX_ROWS = 512
DISPATCH_ROWS = 256
MOE_ROWS = 512
COMBINE_ROWS = 256
ISSUE_UNROLL = 4
DMA_PRIORITIES = 2

F32 = jnp.float32
BF16 = jnp.bfloat16


def _params(*semantics, flags=None):
    return pltpu.CompilerParams(dimension_semantics=semantics, vmem_limit_bytes=VMEM_LIMIT_BYTES, flags=flags)


def _rms(x, g):
    return x * lax.rsqrt(jnp.mean(x * x, axis=-1, keepdims=True) + EPS) * g


def _full(shape):
    return pl.BlockSpec(shape, lambda *_: (0,) * len(shape))


ROW_TILES = D_MODEL // LANES
PACKED_TILES = ROW_TILES // 2
assert ROW_TILES == SUBLANES


def _load_tile_rows(ref, rows, lead=(), tiles=ROW_TILES):
    return jnp.concatenate([ref[lead + (pl.ds(c, rows, stride=tiles), slice(None))] for c in range(tiles)], axis=1)


def _store_tile_rows(ref, value, tiles=ROW_TILES):
    rows = value.shape[0]
    for c in range(tiles):
        ref[pl.ds(c, rows, stride=tiles), :] = value[:, c * LANES:(c + 1) * LANES]


def _tile_row(ref, r, lead=(), tiles=ROW_TILES):
    return ref.at[lead + (pl.ds(pl.multiple_of(r * tiles, tiles), tiles),)]


def _pack_bf16_pairs(x):
    half = x.shape[1] // 2
    bits = lax.bitcast_convert_type(x.astype(BF16).astype(F32), jnp.uint32)
    return (bits[:, :half] >> 16) | bits[:, half:]


def _unpack_bf16_pairs(p):
    left = lax.bitcast_convert_type(p << 16, F32)
    right = lax.bitcast_convert_type(p & jnp.uint32(0xFFFF0000), F32)
    return jnp.concatenate([left, right], axis=1).astype(BF16)


def _rope_table_kernel(pos_ref, invf_ref, cos_ref, sin_ref):
    ang = pos_ref[...].astype(F32) * invf_ref[...]
    cos_ref[...] = jnp.cos(ang)
    sin_ref[...] = jnp.sin(ang)


def _rope_tables(positions):
    t = positions.size
    per_row = LANES // HALF_ROPE
    inv_freq = 1.0 / (ROPE_THETA ** (jnp.arange(0, ROPE_DIM, 2, dtype=F32) / ROPE_DIM))
    pos = jnp.repeat(positions.reshape(t // per_row, per_row), HALF_ROPE, axis=1)
    invf = jnp.tile(inv_freq, per_row).reshape(1, LANES)
    shape = jax.ShapeDtypeStruct(pos.shape, F32)
    cos, sin = pl.pallas_call(
        _rope_table_kernel,
        in_specs=[_full(pos.shape), _full(invf.shape)],
        out_specs=[_full(pos.shape), _full(pos.shape)],
        out_shape=[shape, shape],
        compiler_params=pltpu.CompilerParams(vmem_limit_bytes=VMEM_LIMIT_BYTES),
        name="rope_tables",
    )(pos, invf)
    return cos.reshape(t, HALF_ROPE), sin.reshape(t, HALF_ROPE)


def _in_kernel(x_ref, ctab_ref, stab_ref, gmix_ref, win_ref, gql_ref, wuq_ref, wuqs_ref, gkvl_ref, wuk_ref,
               wuv_ref, gq_ref, gqs_ref, gk_ref, gks_ref, q_ref, k_ref, v_ref, y_ref):
    h = _rms(x_ref[...], gmix_ref[...]).astype(BF16)
    p = jnp.dot(h, win_ref[...], preferred_element_type=F32)
    o_kv = Q_LORA
    o_pe = o_kv + KV_LORA
    o_ps = o_pe + HEAD_PAD
    o_a = o_ps + HEAD_PAD
    o_g = o_a + CONV_WIDTH
    cq, ckv, kpe, kpe_sw = p[:, :o_kv], p[:, o_kv:o_pe], p[:, o_pe:o_ps], p[:, o_ps:o_a]
    y_ref[...] = p[:, o_a:o_g] * jax.nn.sigmoid(p[:, o_g:])

    cqn = _rms(cq, gql_ref[...]).astype(BF16)
    q = jnp.dot(cqn, wuq_ref[...], preferred_element_type=F32)
    q_sw = jnp.dot(cqn, wuqs_ref[...], preferred_element_type=F32)
    ckvn = _rms(ckv, gkvl_ref[...]).astype(BF16)
    kn = jnp.dot(ckvn, wuk_ref[...], preferred_element_type=F32)
    wide_row = lax.broadcasted_iota(jnp.int32, (N_HEADS * HEAD_PAD, 1), 0)
    ones_row = jnp.where(wide_row % HEAD_PAD == V_DIM, 1.0, 0.0)
    vt = lax.dot_general(wuv_ref[...], ckvn, (((1,), (1,)), ((), ())), preferred_element_type=F32)
    v_ref[...] = (vt + ones_row).astype(BF16)

    scale = QK_DIM ** -0.5 * LOG2_E
    cq_tab = ctab_ref[...] * (gq_ref[...] * scale)
    sq_tab = stab_ref[...] * (gqs_ref[...] * scale)
    ck_tab = ctab_ref[...] * gk_ref[...]
    sk_tab = stab_ref[...] * gks_ref[...]
    k_pe = kpe * ck_tab + kpe_sw * sk_tab
    ss_pe = jnp.sum(kpe * kpe, axis=-1, keepdims=True)
    for hd in range(N_HEADS):
        sl = slice(hd * HEAD_PAD, (hd + 1) * HEAD_PAD)
        qh = q[:, sl]
        rq = lax.rsqrt(jnp.sum(qh * qh, axis=-1, keepdims=True) * (1.0 / QK_DIM) + EPS)
        q_ref[:, sl] = ((qh * cq_tab + q_sw[:, sl] * sq_tab) * rq).astype(BF16)
        kh = kn[:, sl]
        rk = lax.rsqrt((jnp.sum(kh * kh, axis=-1, keepdims=True) + ss_pe) * (1.0 / QK_DIM) + EPS)
        k_ref[:, sl] = ((kh * ck_tab + k_pe) * rk).astype(BF16)


def _input_stage(x2, ctab, stab, gmix, win, gql, wuq, wuqs, gkvl, wuk, wuv, gq, gqs, gk, gks):
    t = x2.shape[0]
    wide = N_HEADS * HEAD_PAD
    row = lambda w: pl.BlockSpec((IN_ROWS, w), lambda i: (i, 0))
    consts = (gmix, win, gql, wuq, wuqs, gkvl, wuk, wuv, gq, gqs, gk, gks)
    return pl.pallas_call(
        _in_kernel,
        grid=(t // IN_ROWS,),
        in_specs=[row(D_MODEL), row(HEAD_PAD), row(HEAD_PAD)] + [_full(a.shape) for a in consts],
        out_specs=[row(wide), row(wide), pl.BlockSpec((wide, IN_ROWS), lambda i: (0, i)), row(CONV_WIDTH)],
        out_shape=[jax.ShapeDtypeStruct((t, wide), BF16), jax.ShapeDtypeStruct((t, wide), BF16),
                   jax.ShapeDtypeStruct((wide, t), BF16), jax.ShapeDtypeStruct((t, CONV_WIDTH), F32)],
        compiler_params=_params("parallel"),
        name="input_stage",
    )(x2, ctab, stab, *consts)


def _attn_kernel(q_ref, k_ref, vt_ref, o_ref, *scratch):
    m_refs, acc_refs = scratch[:N_HEADS], scratch[N_HEADS:]
    i = pl.program_id(1)
    dn = (((1,), (1,)), ((), ()))

    def step(r0, diagonal):
        old = None if diagonal else [(m_refs[hd][...], acc_refs[hd][...]) for hd in range(N_HEADS)]
        new = []

        def scores(hd):
            sl = slice(hd * HEAD_PAD, (hd + 1) * HEAD_PAD)
            return lax.dot_general(k_ref[pl.ds(r0, ATT_K), sl], q_ref[:, sl], dn, preferred_element_type=F32)

        ahead = [scores(hd) for hd in range(ATT_AHEAD)]
        for hd in range(N_HEADS):
            sl = slice(hd * HEAD_PAD, (hd + 1) * HEAD_PAD)
            s = ahead.pop(0)
            if hd + ATT_AHEAD < N_HEADS:
                ahead.append(scores(hd + ATT_AHEAD))
            if diagonal:
                kc = lax.broadcasted_iota(jnp.int32, (ATT_K, ATT_Q), 0) // CHUNK
                qc = lax.broadcasted_iota(jnp.int32, (ATT_K, ATT_Q), 1) // CHUNK
                s = jnp.where(kc <= qc, s, -jnp.inf)
            m_new = jnp.max(s, axis=0, keepdims=True)
            if not diagonal:
                m_new = jnp.maximum(old[hd][0], m_new)
            acc = jnp.dot(vt_ref[sl, pl.ds(r0, ATT_K)], jnp.exp2(s - m_new).astype(BF16),
                          preferred_element_type=F32)
            if not diagonal:
                acc = jnp.exp2(old[hd][0] - m_new) * old[hd][1] + acc
            new.append((m_new, acc))
        for hd in range(N_HEADS):
            m_refs[hd][...], acc_refs[hd][...] = new[hd]

    step(pl.multiple_of(i * ATT_Q, ATT_Q), True)

    def body(j, _):
        step(pl.multiple_of(j * ATT_K, ATT_K), False)
        return 0

    lax.fori_loop(0, i, body, 0)
    row = lax.broadcasted_iota(jnp.int32, (HEAD_PAD, 1), 0)
    for hd in range(N_HEADS):
        acc = acc_refs[hd][...]
        out_t = jnp.where(row < V_DIM, acc / acc[V_DIM:V_DIM + 1, :], 0.0)
        o_ref[:, hd * HEAD_PAD:(hd + 1) * HEAD_PAD] = out_t.T.astype(BF16)


def _attention(q, k, vt, batch, seq):
    assert ATT_Q == ATT_K and seq % ATT_Q == 0 and ATT_Q % CHUNK == 0
    nq = seq // ATT_Q
    wide = N_HEADS * HEAD_PAD
    qspec = pl.BlockSpec((ATT_Q, wide), lambda b, i: (b * nq + i, 0))
    return pl.pallas_call(
        _attn_kernel,
        grid=(batch, nq),
        in_specs=[qspec, pl.BlockSpec((seq, wide), lambda b, i: (b, 0)),
                  pl.BlockSpec((wide, seq), lambda b, i: (0, b))],
        out_specs=qspec,
        out_shape=jax.ShapeDtypeStruct(q.shape, BF16),
        scratch_shapes=([pltpu.VMEM((1, ATT_Q), F32)] * N_HEADS + [pltpu.VMEM((HEAD_PAD, ATT_Q), F32)] * N_HEADS),
        compiler_params=_params("parallel", "arbitrary"),
        name="attention",
    )(q, k, vt)


def _conv_kernel(y_ref, w_ref, b_ref, lng_ref, lnb_ref, og_ref, o_ref, pad_ref, phase_ref):
    seq = y_ref.shape[0]
    pad_ref[0:CONV_HALO, :] = jnp.zeros((CONV_HALO, CONV_WIDTH), F32)
    pad_ref[CONV_HALO:, :] = y_ref[...]
    first = CONV_HALO - (CONV_TAPS - 1)

    def body(i, _):
        r0 = pl.multiple_of(i * CONV_ROWS, CONV_ROWS)
        win = pad_ref[pl.ds(r0, CONV_ROWS + CONV_HALO), :]
        acc = jnp.zeros((CONV_ROWS, CONV_WIDTH), F32)
        for s in range(SUBLANES):
            offs = [o for o in range(first, first + CONV_TAPS) if o % SUBLANES == s]
            span = max(offs) - s + CONV_ROWS
            if s:
                phase_ref[s, 0:span, :] = win[s:s + span, :]
            for o in offs:
                rows = (pad_ref[pl.ds(pl.multiple_of(r0 + o, SUBLANES), CONV_ROWS), :] if s == 0
                        else phase_ref[s, o - s:o - s + CONV_ROWS, :])
                acc = acc + w_ref[o - first:o - first + 1, :] * rows
        acc = acc + b_ref[...]
        xc = acc - jnp.mean(acc, axis=-1, keepdims=True)
        ln = xc * lax.rsqrt(jnp.mean(xc * xc, axis=-1, keepdims=True) + EPS) * lng_ref[...] + lnb_ref[...]
        z = ln * jax.nn.sigmoid(ln)
        o_ref[pl.ds(r0, CONV_ROWS), :] = _rms(z, og_ref[...]).astype(BF16)
        return 0

    lax.fori_loop(0, seq // CONV_ROWS, body, 0)


def _conv_branch(y, w, b, lng, lnb, og, batch, seq):
    spec = pl.BlockSpec((seq, CONV_WIDTH), lambda bi: (bi, 0))
    return pl.pallas_call(
        _conv_kernel,
        grid=(batch,),
        in_specs=[spec, _full(w.shape), _full(b.shape), _full(lng.shape), _full(lnb.shape), _full(og.shape)],
        out_specs=spec,
        out_shape=jax.ShapeDtypeStruct(y.shape, BF16),
        scratch_shapes=[pltpu.VMEM((seq + CONV_HALO, CONV_WIDTH), F32),
                        pltpu.VMEM((SUBLANES, CONV_ROWS + CONV_HALO, CONV_WIDTH), F32)],
        compiler_params=_params("parallel"),
        name="conv_branch",
    )(y, w, b, lng, lnb, og)


def _mix_kernel(attn_ref, conv_ref, x_ref, ga_ref, woa_ref, woc_ref, gffn_ref, rwh_ref, rwl_ref, rb_ref,
                tri_ref, x1_ref, h2_ref, idx_ref, gate_ref, rank_ref, cnt_ref, base_ref):
    @pl.when(pl.program_id(0) == 0)
    def _():
        base_ref[...] = jnp.zeros(base_ref.shape, F32)

    a = attn_ref[...].astype(F32)
    attn_width = N_HEADS * V_DIM
    an = a * lax.rsqrt(jnp.sum(a * a, axis=-1, keepdims=True) * (1.0 / attn_width) + EPS) * ga_ref[...]
    x1 = (x_ref[...] + jnp.dot(an.astype(BF16), woa_ref[...], preferred_element_type=F32)
          + jnp.dot(conv_ref[...], woc_ref[...], preferred_element_type=F32))
    x1_ref[...] = x1
    h2 = _rms(x1, gffn_ref[...])
    _store_tile_rows(h2_ref, _pack_bf16_pairs(h2), PACKED_TILES)

    hi = h2.astype(BF16)
    lo = (h2 - hi.astype(F32)).astype(BF16)
    dn = (((1,), (1,)), ((), ()))
    rwh, rwl = rwh_ref[...], rwl_ref[...]
    logits = (lax.dot_general(rwh, hi, dn, preferred_element_type=F32)
              + lax.dot_general(rwh, lo, dn, preferred_element_type=F32)
              + lax.dot_general(rwl, hi, dn, preferred_element_type=F32)) + rb_ref[...]

    rows = logits.shape[1]
    eidx = lax.broadcasted_iota(jnp.int32, (N_EXPERTS, rows), 0).astype(F32)
    work = logits
    sels, vals = [], []
    for k in range(TOP_K):
        mx = jnp.max(work, axis=0, keepdims=True)
        first = jnp.min(jnp.where(work == mx, eidx, float(N_EXPERTS)), axis=0, keepdims=True)
        sel = eidx == first
        work = jnp.where(sel, -jnp.inf, work)
        sels.append(sel)
        vals.append(mx)
        idx_ref[k:k + 1, :] = first.astype(jnp.int32)
    exps = [jnp.exp(v - vals[0]) for v in vals]
    denom = exps[0] + exps[1] + exps[2] + exps[3]
    for k in range(TOP_K):
        gate_ref[k:k + 1, :] = exps[k] / denom

    member = jnp.where(sels[0] | sels[1] | sels[2] | sels[3], 1.0, 0.0)
    before = jnp.dot(member.astype(BF16), tri_ref[...], preferred_element_type=F32)
    posn = base_ref[:, 0:1] + before
    for k in range(TOP_K):
        rank_ref[k:k + 1, :] = jnp.sum(jnp.where(sels[k], posn, 0.0), axis=0, keepdims=True).astype(jnp.int32)
    base_ref[...] = base_ref[...] + jnp.sum(member, axis=1, keepdims=True)
    cnt_ref[...] = base_ref[...].astype(jnp.int32)


def _mix_stage(attn, conv, x2, ga, woa, woc, gffn, rwh, rwl, rb, tri):
    t = x2.shape[0]
    proj = lambda i: (i, 0)
    row = lambda w: pl.BlockSpec((MIX_ROWS, w), proj)
    col = pl.BlockSpec((TOP_K, MIX_ROWS), lambda i: (0, i))
    return pl.pallas_call(
        _mix_kernel,
        grid=(t // MIX_ROWS,),
        in_specs=[row(attn.shape[1]), row(CONV_WIDTH), row(D_MODEL), _full(ga.shape), _full(woa.shape),
                  _full(woc.shape), _full(gffn.shape), _full(rwh.shape), _full(rwl.shape), _full(rb.shape),
                  _full(tri.shape)],
        out_specs=[row(D_MODEL), pl.BlockSpec((MIX_ROWS * PACKED_TILES, LANES), proj), col, col, col,
                   _full((N_EXPERTS, LANES))],
        out_shape=[jax.ShapeDtypeStruct((t, D_MODEL), F32),
                   jax.ShapeDtypeStruct((t * PACKED_TILES, LANES), jnp.uint32),
                   jax.ShapeDtypeStruct((TOP_K, t), jnp.int32), jax.ShapeDtypeStruct((TOP_K, t), F32),
                   jax.ShapeDtypeStruct((TOP_K, t), jnp.int32),
                   jax.ShapeDtypeStruct((N_EXPERTS, LANES), jnp.int32)],
        scratch_shapes=[pltpu.VMEM((N_EXPERTS, LANES), F32)],
        compiler_params=_params("arbitrary"),
        name="mix_router",
    )(attn, conv, x2, ga, woa, woc, gffn, rwh, rwl, rb, tri)


def _dispatch_kernel(ends_ref, padded_ref, dest_ref, h_ref, xs_ref, zero_ref, sem, zsem):
    tiles = PACKED_TILES
    rows = h_ref.shape[0] // tiles
    block = MOE_ROWS * tiles
    n_blocks = xs_ref.shape[0] // block

    @pl.when(pl.program_id(0) == 0)
    def _():
        zero_ref[...] = jnp.zeros(zero_ref.shape, zero_ref.dtype)

        def zero_copy(r0):
            return pltpu.make_async_copy(
                zero_ref, xs_ref.at[pl.ds(pl.multiple_of(r0 * tiles, block), block)], zsem)

        for start in (True, False):
            for e in range(N_EXPERTS):
                @pl.when(padded_ref[e] > 0)
                def _():
                    cp = zero_copy(ends_ref[e] - MOE_ROWS)
                    cp.start() if start else cp.wait()

            def tail(b, _):
                cp = zero_copy(b * MOE_ROWS)
                cp.start() if start else cp.wait()
                return 0

            lax.fori_loop(ends_ref[N_EXPERTS - 1] // MOE_ROWS, n_blocks, tail, 0)

    def issue(t, _):
        for k in range(TOP_K):
            pltpu.make_async_copy(_tile_row(h_ref, t, tiles=tiles), _tile_row(xs_ref, dest_ref[k, t], tiles=tiles),
                                  sem).start(priority=k % DMA_PRIORITIES)
        return 0

    lax.fori_loop(0, rows, issue, 0, unroll=ISSUE_UNROLL)
    for _ in range(TOP_K):
        pltpu.make_async_copy(h_ref, xs_ref.at[pl.ds(0, rows * tiles)], sem).wait()


def _dispatch(ends, padded, dest, h2, n_blocks):
    t = h2.shape[0] // PACKED_TILES
    grid_spec = pltpu.PrefetchScalarGridSpec(
        num_scalar_prefetch=2,
        grid=(t // DISPATCH_ROWS,),
        in_specs=[pl.BlockSpec((TOP_K, DISPATCH_ROWS), lambda i, *_: (0, i), memory_space=pltpu.SMEM),
                  pl.BlockSpec((DISPATCH_ROWS * PACKED_TILES, LANES), lambda i, *_: (i, 0))],
        out_specs=pl.BlockSpec(memory_space=pl.ANY),
        scratch_shapes=[pltpu.VMEM((MOE_ROWS * PACKED_TILES, LANES), h2.dtype), pltpu.SemaphoreType.DMA(()),
                        pltpu.SemaphoreType.DMA(())],
    )
    return pl.pallas_call(
        _dispatch_kernel,
        grid_spec=grid_spec,
        out_shape=jax.ShapeDtypeStruct((n_blocks * MOE_ROWS * PACKED_TILES, LANES), h2.dtype),
        compiler_params=_params("arbitrary"),
        name="dispatch_rows",
    )(ends, padded, dest, h2)


def _moe_kernel(be_ref, nact_ref, xs_ref, wgu_ref, bgu_ref, wd_ref, bd_ref, ys_ref, wgu_bf, wd_bf):
    i = pl.program_id(0)
    active = i < nact_ref[0]

    @pl.when(jnp.logical_not(active))
    def _():
        ys_ref[...] = jnp.zeros(ys_ref.shape, F32)

    @pl.when(active & ((i == 0) | (be_ref[i] != be_ref[jnp.maximum(i - 1, 0)])))
    def _():
        wgu_bf[...] = wgu_ref[0].astype(BF16)
        wd_bf[...] = wd_ref[0].astype(BF16)

    @pl.when(active)
    def _():
        x = _unpack_bf16_pairs(_load_tile_rows(xs_ref, MOE_ROWS, tiles=PACKED_TILES))
        gu = jnp.dot(x, wgu_bf[...], preferred_element_type=F32) + bgu_ref[0]
        gate = jnp.minimum(gu[:, :D_FF], SWIGLU_LIMIT)
        up = jnp.clip(gu[:, D_FF:], -SWIGLU_LIMIT, SWIGLU_LIMIT)
        mid = (up + 1.0) * (gate * jax.nn.sigmoid(gate * SWIGLU_ALPHA))
        _store_tile_rows(ys_ref, jnp.dot(mid.astype(BF16), wd_bf[...], preferred_element_type=F32) + bd_ref[0])


def _moe(block_expert, n_active, xs, wgu, bgu, wd, bd):
    block = MOE_ROWS * ROW_TILES
    n_blocks = xs.shape[0] // (MOE_ROWS * PACKED_TILES)
    grid_spec = pltpu.PrefetchScalarGridSpec(
        num_scalar_prefetch=2,
        grid=(n_blocks,),
        in_specs=[
            pl.BlockSpec((MOE_ROWS * PACKED_TILES, LANES), lambda i, be, na: (jnp.minimum(i, na[0] - 1), 0)),
            pl.BlockSpec((1, D_MODEL, 2 * D_FF), lambda i, be, na: (be[i], 0, 0)),
            pl.BlockSpec((1, 1, 2 * D_FF), lambda i, be, na: (be[i], 0, 0)),
            pl.BlockSpec((1, D_FF, D_MODEL), lambda i, be, na: (be[i], 0, 0)),
            pl.BlockSpec((1, 1, D_MODEL), lambda i, be, na: (be[i], 0, 0)),
        ],
        out_specs=pl.BlockSpec((block, LANES), lambda i, be, na: (i, 0)),
        scratch_shapes=[pltpu.VMEM((D_MODEL, 2 * D_FF), BF16), pltpu.VMEM((D_FF, D_MODEL), BF16)],
    )
    return pl.pallas_call(
        _moe_kernel,
        grid_spec=grid_spec,
        out_shape=jax.ShapeDtypeStruct((n_blocks * block, LANES), F32),
        compiler_params=_params("arbitrary"),
        name="expert_mlp",
    )(block_expert, n_active, xs, wgu, bgu, wd, bd)


def _combine_kernel(dest_ref, dest_next_ref, x1_ref, gate_ref, ys_ref, o_ref, buf_ref, sems):
    i, n = pl.program_id(0), pl.num_programs(0)
    rows = x1_ref.shape[0]

    def gather(idx_ref, slot):
        def issue(t, _):
            for k in range(TOP_K):
                pltpu.make_async_copy(_tile_row(ys_ref, idx_ref[k, t]), _tile_row(buf_ref, t, (slot, k)),
                                      sems.at[slot]).start(priority=k % DMA_PRIORITIES)
            return 0

        lax.fori_loop(0, rows, issue, 0, unroll=ISSUE_UNROLL)

    @pl.when(i == 0)
    def _():
        gather(dest_ref, 0)

    @pl.when(i + 1 < n)
    def _():
        gather(dest_next_ref, (i + 1) % 2)

    slot = i % 2
    for k in range(TOP_K):
        pltpu.make_async_copy(ys_ref.at[pl.ds(0, rows * ROW_TILES)], buf_ref.at[slot, k], sems.at[slot]).wait()
    acc = x1_ref[...]
    for k in range(TOP_K):
        acc = acc + gate_ref[:, k:k + 1] * _load_tile_rows(buf_ref, rows, (slot, k))
    o_ref[...] = acc


def _combine(dest, x1, gate_rows, ys):
    t = x1.shape[0]
    steps = t // COMBINE_ROWS
    row = lambda w: pl.BlockSpec((COMBINE_ROWS, w), lambda i: (i, 0))
    slots = lambda nxt: pl.BlockSpec((TOP_K, COMBINE_ROWS), lambda i: (0, jnp.minimum(i + nxt, steps - 1)),
                                     memory_space=pltpu.SMEM)
    return pl.pallas_call(
        _combine_kernel,
        grid=(steps,),
        in_specs=[slots(0), slots(1), row(D_MODEL), row(TOP_K), pl.BlockSpec(memory_space=pl.ANY)],
        out_specs=row(D_MODEL),
        out_shape=jax.ShapeDtypeStruct(x1.shape, F32),
        scratch_shapes=[pltpu.VMEM((2, TOP_K, COMBINE_ROWS * ROW_TILES, LANES), F32),
                        pltpu.SemaphoreType.DMA((2,))],
        compiler_params=_params("arbitrary"),
        name="combine_rows",
    )(dest, dest, x1, gate_rows, ys)


def _head_blocks(w, width):
    r = w.shape[0]
    w = w.reshape(r, N_HEADS, width)
    return jnp.pad(w, ((0, 0), (0, 0), (0, HEAD_PAD - width))).reshape(r, N_HEADS * HEAD_PAD)


def _lane_row(v, offset=0):
    return jnp.pad(v, (offset, HEAD_PAD - offset - v.shape[0])).reshape(1, HEAD_PAD)


def _swap_rope(w, sign):
    lo, hi = w[..., NOPE_DIM:NOPE_DIM + HALF_ROPE], w[..., NOPE_DIM + HALF_ROPE:QK_DIM]
    return jnp.concatenate([jnp.zeros_like(w[..., :NOPE_DIM]), sign * hi, lo], axis=-1)


def kernel(x, positions, norm_mix_g, w_in, q_latent_g, w_uq, kv_latent_g, w_ukv, q_head_g, k_head_g, conv_dw_w, conv_dw_b, conv_ln_g, conv_ln_b, attn_out_g, conv_out_g, w_out, norm_ffn_g, router_w, router_b, w_gate_up, b_gate_up, w_down, b_down):
    batch, seq, _ = x.shape
    t = batch * seq
    depth = norm_mix_g.shape[0]
    x2 = x.reshape(t, D_MODEL)
    cos, sin = _rope_tables(positions)
    ctab = jnp.concatenate([jnp.ones((t, NOPE_DIM), F32), cos, cos, jnp.zeros((t, HEAD_PAD - QK_DIM), F32)], axis=1)
    stab = jnp.concatenate([jnp.zeros((t, NOPE_DIM), F32), sin, sin, jnp.zeros((t, HEAD_PAD - QK_DIM), F32)], axis=1)
    tri = jnp.triu(jnp.ones((MIX_ROWS, MIX_ROWS), BF16), 1)
    o_kv = Q_LORA
    o_pe = o_kv + KV_LORA
    o_u = o_pe + ROPE_DIM

    for l in range(depth):
        wi = w_in[l]
        w_pe = wi[:, o_pe:o_u]
        w_pe_sw = jnp.concatenate([-w_pe[:, HALF_ROPE:], w_pe[:, :HALF_ROPE]], axis=1)
        pe_block = lambda w: jnp.pad(w, ((0, 0), (NOPE_DIM, HEAD_PAD - QK_DIM)))
        win = jnp.concatenate([wi[:, :o_pe], pe_block(w_pe), pe_block(w_pe_sw), wi[:, o_u:]], axis=1).astype(BF16)
        wq = w_uq[l].reshape(Q_LORA, N_HEADS, QK_DIM)
        wuq = _head_blocks(w_uq[l], QK_DIM).astype(BF16)
        wuqs = _head_blocks(_swap_rope(wq, -1.0).reshape(Q_LORA, -1), QK_DIM).astype(BF16)
        wkv = w_ukv[l].reshape(KV_LORA, N_HEADS, NOPE_DIM + V_DIM)
        wuk = _head_blocks(wkv[:, :, :NOPE_DIM].reshape(KV_LORA, -1), NOPE_DIM).astype(BF16)
        wuv = _head_blocks(wkv[:, :, NOPE_DIM:].reshape(KV_LORA, -1), V_DIM).T.astype(BF16)
        ga = _head_blocks(attn_out_g[l].reshape(1, -1), V_DIM)
        wo = w_out[l]
        woa = jnp.pad(wo[:N_HEADS * V_DIM].reshape(N_HEADS, V_DIM, D_MODEL),
                      ((0, 0), (0, HEAD_PAD - V_DIM), (0, 0))).reshape(N_HEADS * HEAD_PAD, D_MODEL).astype(BF16)
        woc = wo[N_HEADS * V_DIM:].astype(BF16)
        rwt = router_w[l].T
        rwh = rwt.astype(BF16)
        rwl = (rwt - rwh.astype(F32)).astype(BF16)

        q, k, v, y = _input_stage(
            x2, ctab, stab, norm_mix_g[l].reshape(1, -1), win, q_latent_g[l].reshape(1, -1), wuq, wuqs,
            kv_latent_g[l].reshape(1, -1), wuk, wuv, _lane_row(q_head_g[l]),
            _lane_row(_swap_rope(q_head_g[l], 1.0)), _lane_row(k_head_g[l]), _lane_row(_swap_rope(k_head_g[l], 1.0)))
        attn = _attention(q, k, v, batch, seq)
        conv = _conv_branch(y, conv_dw_w[l], conv_dw_b[l].reshape(1, -1), conv_ln_g[l].reshape(1, -1),
                            conv_ln_b[l].reshape(1, -1), conv_out_g[l].reshape(1, -1), batch, seq)
        x1, h2, idx, gate, rank, cnt = _mix_stage(
            attn, conv, x2, ga, woa, woc, norm_ffn_g[l].reshape(1, -1), rwh, rwl,
            router_b[l].reshape(-1, 1), tri)

        counts = cnt[:, 0]
        padded = (counts + MOE_ROWS - 1) // MOE_ROWS * MOE_ROWS
        ends = jnp.cumsum(padded)
        starts = ends - padded
        experts = jnp.arange(N_EXPERTS, dtype=jnp.int32)
        dest = rank + jnp.sum(jnp.where(idx[None] == experts[:, None, None], starts[:, None, None], 0), axis=0)
        n_blocks = (t * TOP_K + N_EXPERTS * (MOE_ROWS - 1)) // MOE_ROWS
        n_active = (ends[-1] // MOE_ROWS).astype(jnp.int32)
        blk = jnp.minimum(jnp.arange(n_blocks, dtype=jnp.int32), n_active - 1)
        be = jnp.minimum(jnp.sum((ends[None, :] <= (blk * MOE_ROWS)[:, None]).astype(jnp.int32), axis=1),
                         N_EXPERTS - 1)

        xs = _dispatch(ends.astype(jnp.int32), padded.astype(jnp.int32), dest, h2, n_blocks)
        ys = _moe(be, n_active.reshape(1), xs, w_gate_up[l], b_gate_up[l].reshape(N_EXPERTS, 1, -1),
                  w_down[l], b_down[l].reshape(N_EXPERTS, 1, -1))
        x2 = _combine(dest, x1, gate.T, ys)
    return x2.reshape(batch, seq, D_MODEL)
```

```python
import functools

import jax
import jax.numpy as jnp
from jax import lax
from jax.experimental import pallas as pl
from jax.experimental.pallas import tpu as pltpu

D_MODEL = 1024
N_HEADS = 8
NOPE_DIM = 64
ROPE_DIM = 32
QK_DIM = NOPE_DIM + ROPE_DIM
V_DIM = 64
Q_LORA = 384
KV_LORA = 128
CONV_WIDTH = 512
CONV_TAPS = 31
N_EXPERTS = 32
TOP_K = 4
D_FF = 1024
CHUNK = 64
ROPE_THETA = 10000.0
EPS = 1e-6
SWIGLU_ALPHA = 1.702
SWIGLU_LIMIT = 7.0
LOG2_E = 1.4426950408889634

LANES = 128
SUBLANES = 8
HEAD_PAD = LANES
HALF_ROPE = ROPE_DIM // 2
VMEM_LIMIT_BYTES = 56 * 1024 * 1024

IN_ROWS = 512
ATT_Q = 512
ATT_K = 512
ATT_QPART = 256
ATT_AHEAD = 4
CONV_ROWS = 256
CONV_HALO = 32
MIX_ROWS = 512
DISPATCH_ROWS = 256
MOE_ROWS = 512
COMBINE_ROWS = 256
ISSUE_UNROLL = 4
DMA_PRIORITIES = 2

F32 = jnp.float32
BF16 = jnp.bfloat16


def _params(*semantics, flags=None):
    return pltpu.CompilerParams(dimension_semantics=semantics, vmem_limit_bytes=VMEM_LIMIT_BYTES, flags=flags)


def _rms(x, g):
    return x * lax.rsqrt(jnp.mean(x * x, axis=-1, keepdims=True) + EPS) * g


def _full(shape):
    return pl.BlockSpec(shape, lambda *_: (0,) * len(shape))


ROW_TILES = D_MODEL // LANES
PACKED_TILES = ROW_TILES // 2
assert ROW_TILES == SUBLANES


def _load_tile_rows(ref, rows, lead=(), tiles=ROW_TILES):
    return jnp.concatenate([ref[lead + (pl.ds(c, rows, stride=tiles), slice(None))] for c in range(tiles)], axis=1)


def _store_tile_rows(ref, value, tiles=ROW_TILES):
    rows = value.shape[0]
    for c in range(tiles):
        ref[pl.ds(c, rows, stride=tiles), :] = value[:, c * LANES:(c + 1) * LANES]


def _tile_row(ref, r, lead=(), tiles=ROW_TILES):
    return ref.at[lead + (pl.ds(pl.multiple_of(r * tiles, tiles), tiles),)]


def _pack_bf16_pairs(x):
    half = x.shape[1] // 2
    bits = lax.bitcast_convert_type(x.astype(BF16).astype(F32), jnp.uint32)
    return (bits[:, :half] >> 16) | bits[:, half:]


def _unpack_bf16_pairs(p):
    left = lax.bitcast_convert_type(p << 16, F32)
    right = lax.bitcast_convert_type(p & jnp.uint32(0xFFFF0000), F32)
    return jnp.concatenate([left, right], axis=1).astype(BF16)


def _rope_table_kernel(pos_ref, invf_ref, cos_ref, sin_ref):
    ang = pos_ref[...].astype(F32) * invf_ref[...]
    cos_ref[...] = jnp.cos(ang)
    sin_ref[...] = jnp.sin(ang)


def _rope_tables(positions):
    t = positions.size
    per_row = LANES // HALF_ROPE
    inv_freq = 1.0 / (ROPE_THETA ** (jnp.arange(0, ROPE_DIM, 2, dtype=F32) / ROPE_DIM))
    pos = jnp.broadcast_to(positions.reshape(t // per_row, per_row, 1),
                           (t // per_row, per_row, HALF_ROPE)).reshape(t // per_row, LANES)
    invf = jnp.tile(inv_freq, per_row).reshape(1, LANES)
    shape = jax.ShapeDtypeStruct(pos.shape, F32)
    cos, sin = pl.pallas_call(
        _rope_table_kernel,
        in_specs=[_full(pos.shape), _full(invf.shape)],
        out_specs=[_full(pos.shape), _full(pos.shape)],
        out_shape=[shape, shape],
        compiler_params=pltpu.CompilerParams(vmem_limit_bytes=VMEM_LIMIT_BYTES),
        name="rope_tables",
    )(pos, invf)
    return cos.reshape(t, HALF_ROPE), sin.reshape(t, HALF_ROPE)


def _in_kernel(x_ref, ctab_ref, stab_ref, gmix_ref, win_ref, gql_ref, wuq_ref, wuqs_ref, gkvl_ref, wuk_ref,
               wuv_ref, gq_ref, gqs_ref, gk_ref, gks_ref, q_ref, k_ref, v_ref, y_ref):
    h = _rms(x_ref[...], gmix_ref[...]).astype(BF16)
    p = jnp.dot(h, win_ref[...], preferred_element_type=F32)
    o_kv = Q_LORA
    o_pe = o_kv + KV_LORA
    o_ps = o_pe + HEAD_PAD
    o_a = o_ps + HEAD_PAD
    o_g = o_a + CONV_WIDTH
    cq, ckv, kpe, kpe_sw = p[:, :o_kv], p[:, o_kv:o_pe], p[:, o_pe:o_ps], p[:, o_ps:o_a]
    y_ref[...] = p[:, o_a:o_g] * jax.nn.sigmoid(p[:, o_g:])

    cqn = _rms(cq, gql_ref[...]).astype(BF16)
    q = jnp.dot(cqn, wuq_ref[...], preferred_element_type=F32)
    q_sw = jnp.dot(cqn, wuqs_ref[...], preferred_element_type=F32)
    ckvn = _rms(ckv, gkvl_ref[...]).astype(BF16)
    kn = jnp.dot(ckvn, wuk_ref[...], preferred_element_type=F32)
    wide_row = lax.broadcasted_iota(jnp.int32, (N_HEADS * HEAD_PAD, 1), 0)
    ones_row = jnp.where(wide_row % HEAD_PAD == V_DIM, 1.0, 0.0)
    vt = lax.dot_general(wuv_ref[...], ckvn, (((1,), (1,)), ((), ())), preferred_element_type=F32)
    v_ref[...] = (vt + ones_row).astype(BF16)

    scale = QK_DIM ** -0.5 * LOG2_E
    cq_tab = ctab_ref[...] * (gq_ref[...] * scale)
    sq_tab = stab_ref[...] * (gqs_ref[...] * scale)
    ck_tab = ctab_ref[...] * gk_ref[...]
    sk_tab = stab_ref[...] * gks_ref[...]
    k_pe = kpe * ck_tab + kpe_sw * sk_tab
    ss_pe = jnp.sum(kpe * kpe, axis=-1, keepdims=True)
    for hd in range(N_HEADS):
        sl = slice(hd * HEAD_PAD, (hd + 1) * HEAD_PAD)
        qh = q[:, sl]
        rq = lax.rsqrt(jnp.sum(qh * qh, axis=-1, keepdims=True) * (1.0 / QK_DIM) + EPS)
        q_ref[:, sl] = ((qh * cq_tab + q_sw[:, sl] * sq_tab) * rq).astype(BF16)
        kh = kn[:, sl]
        rk = lax.rsqrt((jnp.sum(kh * kh, axis=-1, keepdims=True) + ss_pe) * (1.0 / QK_DIM) + EPS)
        k_ref[:, sl] = ((kh * ck_tab + k_pe) * rk).astype(BF16)


def _input_stage(x2, ctab, stab, gmix, win, gql, wuq, wuqs, gkvl, wuk, wuv, gq, gqs, gk, gks):
    t = x2.shape[0]
    wide = N_HEADS * HEAD_PAD
    row = lambda w: pl.BlockSpec((IN_ROWS, w), lambda i: (i, 0))
    consts = (gmix, win, gql, wuq, wuqs, gkvl, wuk, wuv, gq, gqs, gk, gks)
    return pl.pallas_call(
        _in_kernel,
        grid=(t // IN_ROWS,),
        in_specs=[row(D_MODEL), row(HEAD_PAD), row(HEAD_PAD)] + [_full(a.shape) for a in consts],
        out_specs=[row(wide), row(wide), pl.BlockSpec((wide, IN_ROWS), lambda i: (0, i)), row(CONV_WIDTH)],
        out_shape=[jax.ShapeDtypeStruct((t, wide), BF16), jax.ShapeDtypeStruct((t, wide), BF16),
                   jax.ShapeDtypeStruct((wide, t), BF16), jax.ShapeDtypeStruct((t, CONV_WIDTH), F32)],
        compiler_params=_params("parallel"),
        name="input_stage",
    )(x2, ctab, stab, *consts)


def _attn_kernel(q_ref, k_ref, vt_ref, o_ref, *scratch):
    units = [(hd, part) for hd in range(N_HEADS) for part in range(ATT_Q // ATT_QPART)]
    m_refs, acc_refs = scratch[:len(units)], scratch[len(units):]
    i = pl.program_id(1)
    dn = (((1,), (1,)), ((), ()))

    def step(r0, diagonal):
        old = None if diagonal else [(m_refs[u][...], acc_refs[u][...]) for u in range(len(units))]
        new = []

        def keys(part):
            return (part + 1) * ATT_QPART if diagonal else ATT_K

        def scores(u):
            hd, part = units[u]
            sl = slice(hd * HEAD_PAD, (hd + 1) * HEAD_PAD)
            return lax.dot_general(k_ref[pl.ds(r0, keys(part)), sl],
                                   q_ref[part * ATT_QPART:(part + 1) * ATT_QPART, sl], dn,
                                   preferred_element_type=F32)

        ahead = [scores(u) for u in range(ATT_AHEAD)]
        for u, (hd, part) in enumerate(units):
            sl = slice(hd * HEAD_PAD, (hd + 1) * HEAD_PAD)
            s = ahead.pop(0)
            if u + ATT_AHEAD < len(units):
                ahead.append(scores(u + ATT_AHEAD))
            if diagonal:
                kc = lax.broadcasted_iota(jnp.int32, s.shape, 0) // CHUNK
                qc = (lax.broadcasted_iota(jnp.int32, s.shape, 1) + part * ATT_QPART) // CHUNK
                s = jnp.where(kc <= qc, s, -jnp.inf)
            m_new = jnp.max(s, axis=0, keepdims=True)
            if not diagonal:
                m_new = jnp.maximum(old[u][0], m_new)
            acc = jnp.dot(vt_ref[sl, pl.ds(r0, keys(part))], jnp.exp2(s - m_new).astype(BF16),
                          preferred_element_type=F32)
            if not diagonal:
                acc = jnp.exp2(old[u][0] - m_new) * old[u][1] + acc
            new.append((m_new, acc))
        for u in range(len(units)):
            m_refs[u][...], acc_refs[u][...] = new[u]

    step(pl.multiple_of(i * ATT_Q, ATT_Q), True)

    def body(j, _):
        step(pl.multiple_of(j * ATT_K, ATT_K), False)
        return 0

    lax.fori_loop(0, i, body, 0)
    row = lax.broadcasted_iota(jnp.int32, (HEAD_PAD, 1), 0)
    for u, (hd, part) in enumerate(units):
        acc = acc_refs[u][...]
        out_t = jnp.where(row < V_DIM, acc / acc[V_DIM:V_DIM + 1, :], 0.0)
        o_ref[part * ATT_QPART:(part + 1) * ATT_QPART, hd * HEAD_PAD:(hd + 1) * HEAD_PAD] = out_t.T.astype(BF16)


def _attention(q, k, vt, batch, seq):
    assert ATT_Q == ATT_K and seq % ATT_Q == 0 and ATT_Q % ATT_QPART == 0 and ATT_QPART % CHUNK == 0
    nq = seq // ATT_Q
    chains = N_HEADS * (ATT_Q // ATT_QPART)
    wide = N_HEADS * HEAD_PAD
    qspec = pl.BlockSpec((ATT_Q, wide), lambda b, i: (b * nq + i, 0))
    return pl.pallas_call(
        _attn_kernel,
        grid=(batch, nq),
        in_specs=[qspec, pl.BlockSpec((seq, wide), lambda b, i: (b, 0)),
                  pl.BlockSpec((wide, seq), lambda b, i: (0, b))],
        out_specs=qspec,
        out_shape=jax.ShapeDtypeStruct(q.shape, BF16),
        scratch_shapes=([pltpu.VMEM((1, ATT_QPART), F32)] * chains + [pltpu.VMEM((HEAD_PAD, ATT_QPART), F32)] * chains),
        compiler_params=_params("parallel", "arbitrary"),
        name="attention",
    )(q, k, vt)


def _conv_kernel(y_ref, w_ref, b_ref, lng_ref, lnb_ref, og_ref, o_ref, pad_ref, phase_ref):
    seq = y_ref.shape[0]
    pad_ref[0:CONV_HALO, :] = jnp.zeros((CONV_HALO, CONV_WIDTH), F32)
    pad_ref[CONV_HALO:, :] = y_ref[...]
    first = CONV_HALO - (CONV_TAPS - 1)

    def body(i, _):
        r0 = pl.multiple_of(i * CONV_ROWS, CONV_ROWS)
        win = pad_ref[pl.ds(r0, CONV_ROWS + CONV_HALO), :]
        acc = jnp.zeros((CONV_ROWS, CONV_WIDTH), F32)
        for s in range(SUBLANES):
            offs = [o for o in range(first, first + CONV_TAPS) if o % SUBLANES == s]
            span = max(offs) - s + CONV_ROWS
            if s:
                phase_ref[s, 0:span, :] = win[s:s + span, :]
            for o in offs:
                rows = (pad_ref[pl.ds(pl.multiple_of(r0 + o, SUBLANES), CONV_ROWS), :] if s == 0
                        else phase_ref[s, o - s:o - s + CONV_ROWS, :])
                acc = acc + w_ref[o - first:o - first + 1, :] * rows
        acc = acc + b_ref[...]
        xc = acc - jnp.mean(acc, axis=-1, keepdims=True)
        ln = xc * lax.rsqrt(jnp.mean(xc * xc, axis=-1, keepdims=True) + EPS) * lng_ref[...] + lnb_ref[...]
        z = ln * jax.nn.sigmoid(ln)
        o_ref[pl.ds(r0, CONV_ROWS), :] = _rms(z, og_ref[...]).astype(BF16)
        return 0

    lax.fori_loop(0, seq // CONV_ROWS, body, 0)


def _conv_branch(y, w, b, lng, lnb, og, batch, seq):
    spec = pl.BlockSpec((seq, CONV_WIDTH), lambda bi: (bi, 0))
    return pl.pallas_call(
        _conv_kernel,
        grid=(batch,),
        in_specs=[spec, _full(w.shape), _full(b.shape), _full(lng.shape), _full(lnb.shape), _full(og.shape)],
        out_specs=spec,
        out_shape=jax.ShapeDtypeStruct(y.shape, BF16),
        scratch_shapes=[pltpu.VMEM((seq + CONV_HALO, CONV_WIDTH), F32),
                        pltpu.VMEM((SUBLANES, CONV_ROWS + CONV_HALO, CONV_WIDTH), F32)],
        compiler_params=_params("parallel"),
        name="conv_branch",
    )(y, w, b, lng, lnb, og)


def _mix_kernel(attn_ref, conv_ref, x_ref, ga_ref, woa_ref, woc_ref, gffn_ref, rwh_ref, rwl_ref, rb_ref,
                tri_ref, x1_ref, h2_ref, idx_ref, gate_ref, rank_ref, cnt_ref, base_ref):
    @pl.when(pl.program_id(0) == 0)
    def _():
        base_ref[...] = jnp.zeros(base_ref.shape, F32)

    a = attn_ref[...].astype(F32)
    attn_width = N_HEADS * V_DIM
    an = a * lax.rsqrt(jnp.sum(a * a, axis=-1, keepdims=True) * (1.0 / attn_width) + EPS) * ga_ref[...]
    x1 = (x_ref[...] + jnp.dot(an.astype(BF16), woa_ref[...], preferred_element_type=F32)
          + jnp.dot(conv_ref[...], woc_ref[...], preferred_element_type=F32))
    x1_ref[...] = x1
    h2 = _rms(x1, gffn_ref[...])
    _store_tile_rows(h2_ref, _pack_bf16_pairs(h2), PACKED_TILES)

    hi = h2.astype(BF16)
    lo = (h2 - hi.astype(F32)).astype(BF16)
    dn = (((1,), (1,)), ((), ()))
    rwh, rwl = rwh_ref[...], rwl_ref[...]
    logits = (lax.dot_general(rwh, hi, dn, preferred_element_type=F32)
              + lax.dot_general(rwh, lo, dn, preferred_element_type=F32)
              + lax.dot_general(rwl, hi, dn, preferred_element_type=F32)) + rb_ref[...]

    rows = logits.shape[1]
    eidx = lax.broadcasted_iota(jnp.int32, (N_EXPERTS, rows), 0).astype(F32)
    work = logits
    sels, vals = [], []
    for k in range(TOP_K):
        mx = jnp.max(work, axis=0, keepdims=True)
        first = jnp.min(jnp.where(work == mx, eidx, float(N_EXPERTS)), axis=0, keepdims=True)
        sel = eidx == first
        work = jnp.where(sel, -jnp.inf, work)
        sels.append(sel)
        vals.append(mx)
        idx_ref[k:k + 1, :] = first.astype(jnp.int32)
    exps = [jnp.exp(v - vals[0]) for v in vals]
    denom = exps[0] + exps[1] + exps[2] + exps[3]
    for k in range(TOP_K):
        gate_ref[k:k + 1, :] = exps[k] / denom

    member = jnp.where(sels[0] | sels[1] | sels[2] | sels[3], 1.0, 0.0)
    before = jnp.dot(member.astype(BF16), tri_ref[...], preferred_element_type=F32)
    posn = base_ref[:, 0:1] + before
    for k in range(TOP_K):
        rank_ref[k:k + 1, :] = jnp.sum(jnp.where(sels[k], posn, 0.0), axis=0, keepdims=True).astype(jnp.int32)
    base_ref[...] = base_ref[...] + jnp.sum(member, axis=1, keepdims=True)
    cnt_ref[...] = base_ref[...].astype(jnp.int32)


def _mix_stage(attn, conv, x2, ga, woa, woc, gffn, rwh, rwl, rb, tri):
    t = x2.shape[0]
    proj = lambda i: (i, 0)
    row = lambda w: pl.BlockSpec((MIX_ROWS, w), proj)
    col = pl.BlockSpec((TOP_K, MIX_ROWS), lambda i: (0, i))
    return pl.pallas_call(
        _mix_kernel,
        grid=(t // MIX_ROWS,),
        in_specs=[row(attn.shape[1]), row(CONV_WIDTH), row(D_MODEL), _full(ga.shape), _full(woa.shape),
                  _full(woc.shape), _full(gffn.shape), _full(rwh.shape), _full(rwl.shape), _full(rb.shape),
                  _full(tri.shape)],
        out_specs=[row(D_MODEL), pl.BlockSpec((MIX_ROWS * PACKED_TILES, LANES), proj), col, col, col,
                   _full((N_EXPERTS, LANES))],
        out_shape=[jax.ShapeDtypeStruct((t, D_MODEL), F32),
                   jax.ShapeDtypeStruct((t * PACKED_TILES, LANES), jnp.uint32),
                   jax.ShapeDtypeStruct((TOP_K, t), jnp.int32), jax.ShapeDtypeStruct((TOP_K, t), F32),
                   jax.ShapeDtypeStruct((TOP_K, t), jnp.int32),
                   jax.ShapeDtypeStruct((N_EXPERTS, LANES), jnp.int32)],
        scratch_shapes=[pltpu.VMEM((N_EXPERTS, LANES), F32)],
        compiler_params=_params("arbitrary"),
        name="mix_router",
    )(attn, conv, x2, ga, woa, woc, gffn, rwh, rwl, rb, tri)


def _dispatch_kernel(ends_ref, padded_ref, dest_ref, h_ref, xs_ref, zero_ref, sem, zsem):
    tiles = PACKED_TILES
    rows = h_ref.shape[0] // tiles
    block = MOE_ROWS * tiles
    n_blocks = xs_ref.shape[0] // block

    @pl.when(pl.program_id(0) == 0)
    def _():
        zero_ref[...] = jnp.zeros(zero_ref.shape, zero_ref.dtype)

        def zero_copy(r0):
            return pltpu.make_async_copy(
                zero_ref, xs_ref.at[pl.ds(pl.multiple_of(r0 * tiles, block), block)], zsem)

        for start in (True, False):
            for e in range(N_EXPERTS):
                @pl.when(padded_ref[e] > 0)
                def _():
                    cp = zero_copy(ends_ref[e] - MOE_ROWS)
                    cp.start() if start else cp.wait()

            def tail(b, _):
                cp = zero_copy(b * MOE_ROWS)
                cp.start() if start else cp.wait()
                return 0

            lax.fori_loop(ends_ref[N_EXPERTS - 1] // MOE_ROWS, n_blocks, tail, 0)

    def issue(t, _):
        for k in range(TOP_K):
            pltpu.make_async_copy(_tile_row(h_ref, t, tiles=tiles), _tile_row(xs_ref, dest_ref[k, t], tiles=tiles),
                                  sem).start(priority=k % DMA_PRIORITIES)
        return 0

    lax.fori_loop(0, rows, issue, 0, unroll=ISSUE_UNROLL)
    for _ in range(TOP_K):
        pltpu.make_async_copy(h_ref, xs_ref.at[pl.ds(0, rows * tiles)], sem).wait()


def _dispatch(ends, padded, dest, h2, n_blocks):
    t = h2.shape[0] // PACKED_TILES
    grid_spec = pltpu.PrefetchScalarGridSpec(
        num_scalar_prefetch=2,
        grid=(t // DISPATCH_ROWS,),
        in_specs=[pl.BlockSpec((TOP_K, DISPATCH_ROWS), lambda i, *_: (0, i), memory_space=pltpu.SMEM),
                  pl.BlockSpec((DISPATCH_ROWS * PACKED_TILES, LANES), lambda i, *_: (i, 0))],
        out_specs=pl.BlockSpec(memory_space=pl.ANY),
        scratch_shapes=[pltpu.VMEM((MOE_ROWS * PACKED_TILES, LANES), h2.dtype), pltpu.SemaphoreType.DMA(()),
                        pltpu.SemaphoreType.DMA(())],
    )
    return pl.pallas_call(
        _dispatch_kernel,
        grid_spec=grid_spec,
        out_shape=jax.ShapeDtypeStruct((n_blocks * MOE_ROWS * PACKED_TILES, LANES), h2.dtype),
        compiler_params=_params("arbitrary"),
        name="dispatch_rows",
    )(ends, padded, dest, h2)


def _moe_kernel(be_ref, nact_ref, xs_ref, wgu_ref, bgu_ref, wd_ref, bd_ref, ys_ref, wgu_bf, wd_bf):
    i = pl.program_id(0)
    active = i < nact_ref[0]

    @pl.when(jnp.logical_not(active))
    def _():
        ys_ref[...] = jnp.zeros(ys_ref.shape, F32)

    @pl.when(active & ((i == 0) | (be_ref[i] != be_ref[jnp.maximum(i - 1, 0)])))
    def _():
        wgu_bf[...] = wgu_ref[0].astype(BF16)
        wd_bf[...] = wd_ref[0].astype(BF16)

    @pl.when(active)
    def _():
        x = _unpack_bf16_pairs(_load_tile_rows(xs_ref, MOE_ROWS, tiles=PACKED_TILES))
        gu = jnp.dot(x, wgu_bf[...], preferred_element_type=F32) + bgu_ref[0]
        gate = jnp.minimum(gu[:, :D_FF], SWIGLU_LIMIT)
        up = jnp.clip(gu[:, D_FF:], -SWIGLU_LIMIT, SWIGLU_LIMIT)
        mid = (up + 1.0) * (gate * jax.nn.sigmoid(gate * SWIGLU_ALPHA))
        _store_tile_rows(ys_ref, jnp.dot(mid.astype(BF16), wd_bf[...], preferred_element_type=F32) + bd_ref[0])


def _moe(block_expert, n_active, xs, wgu, bgu, wd, bd):
    block = MOE_ROWS * ROW_TILES
    n_blocks = xs.shape[0] // (MOE_ROWS * PACKED_TILES)
    grid_spec = pltpu.PrefetchScalarGridSpec(
        num_scalar_prefetch=2,
        grid=(n_blocks,),
        in_specs=[
            pl.BlockSpec((MOE_ROWS * PACKED_TILES, LANES), lambda i, be, na: (jnp.minimum(i, na[0] - 1), 0)),
            pl.BlockSpec((1, D_MODEL, 2 * D_FF), lambda i, be, na: (be[i], 0, 0)),
            pl.BlockSpec((1, 1, 2 * D_FF), lambda i, be, na: (be[i], 0, 0)),
            pl.BlockSpec((1, D_FF, D_MODEL), lambda i, be, na: (be[i], 0, 0)),
            pl.BlockSpec((1, 1, D_MODEL), lambda i, be, na: (be[i], 0, 0)),
        ],
        out_specs=pl.BlockSpec((block, LANES), lambda i, be, na: (i, 0)),
        scratch_shapes=[pltpu.VMEM((D_MODEL, 2 * D_FF), BF16), pltpu.VMEM((D_FF, D_MODEL), BF16)],
    )
    return pl.pallas_call(
        _moe_kernel,
        grid_spec=grid_spec,
        out_shape=jax.ShapeDtypeStruct((n_blocks * block, LANES), F32),
        compiler_params=_params("arbitrary"),
        name="expert_mlp",
    )(block_expert, n_active, xs, wgu, bgu, wd, bd)


def _combine_kernel(dest_ref, dest_next_ref, x1_ref, gate_ref, ys_ref, o_ref, buf_ref, sems):
    i, n = pl.program_id(0), pl.num_programs(0)
    rows = x1_ref.shape[0]

    def gather(idx_ref, slot):
        def issue(t, _):
            for k in range(TOP_K):
                pltpu.make_async_copy(_tile_row(ys_ref, idx_ref[k, t]), _tile_row(buf_ref, t, (slot, k)),
                                      sems.at[slot]).start(priority=k % DMA_PRIORITIES)
            return 0

        lax.fori_loop(0, rows, issue, 0, unroll=ISSUE_UNROLL)

    @pl.when(i == 0)
    def _():
        gather(dest_ref, 0)

    @pl.when(i + 1 < n)
    def _():
        gather(dest_next_ref, (i + 1) % 2)

    slot = i % 2
    for k in range(TOP_K):
        pltpu.make_async_copy(ys_ref.at[pl.ds(0, rows * ROW_TILES)], buf_ref.at[slot, k], sems.at[slot]).wait()
    acc = x1_ref[...]
    for k in range(TOP_K):
        acc = acc + gate_ref[:, k:k + 1] * _load_tile_rows(buf_ref, rows, (slot, k))
    o_ref[...] = acc


def _combine(dest, x1, gate_rows, ys):
    t = x1.shape[0]
    steps = t // COMBINE_ROWS
    row = lambda w: pl.BlockSpec((COMBINE_ROWS, w), lambda i: (i, 0))
    slots = lambda nxt: pl.BlockSpec((TOP_K, COMBINE_ROWS), lambda i: (0, jnp.minimum(i + nxt, steps - 1)),
                                     memory_space=pltpu.SMEM)
    return pl.pallas_call(
        _combine_kernel,
        grid=(steps,),
        in_specs=[slots(0), slots(1), row(D_MODEL), row(TOP_K), pl.BlockSpec(memory_space=pl.ANY)],
        out_specs=row(D_MODEL),
        out_shape=jax.ShapeDtypeStruct(x1.shape, F32),
        scratch_shapes=[pltpu.VMEM((2, TOP_K, COMBINE_ROWS * ROW_TILES, LANES), F32),
                        pltpu.SemaphoreType.DMA((2,))],
        compiler_params=_params("arbitrary"),
        name="combine_rows",
    )(dest, dest, x1, gate_rows, ys)


def _head_blocks(w, width):
    r = w.shape[0]
    w = w.reshape(r, N_HEADS, width)
    return jnp.pad(w, ((0, 0), (0, 0), (0, HEAD_PAD - width))).reshape(r, N_HEADS * HEAD_PAD)


def _lane_row(v, offset=0):
    return jnp.pad(v, (offset, HEAD_PAD - offset - v.shape[0])).reshape(1, HEAD_PAD)


def _swap_rope(w, sign):
    lo, hi = w[..., NOPE_DIM:NOPE_DIM + HALF_ROPE], w[..., NOPE_DIM + HALF_ROPE:QK_DIM]
    return jnp.concatenate([jnp.zeros_like(w[..., :NOPE_DIM]), sign * hi, lo], axis=-1)


def kernel(x, positions, norm_mix_g, w_in, q_latent_g, w_uq, kv_latent_g, w_ukv, q_head_g, k_head_g, conv_dw_w, conv_dw_b, conv_ln_g, conv_ln_b, attn_out_g, conv_out_g, w_out, norm_ffn_g, router_w, router_b, w_gate_up, b_gate_up, w_down, b_down):
    batch, seq, _ = x.shape
    t = batch * seq
    depth = norm_mix_g.shape[0]
    x2 = x.reshape(t, D_MODEL)
    cos, sin = _rope_tables(positions)
    ctab = jnp.concatenate([jnp.ones((t, NOPE_DIM), F32), cos, cos, jnp.zeros((t, HEAD_PAD - QK_DIM), F32)], axis=1)
    stab = jnp.concatenate([jnp.zeros((t, NOPE_DIM), F32), sin, sin, jnp.zeros((t, HEAD_PAD - QK_DIM), F32)], axis=1)
    tri = jnp.triu(jnp.ones((MIX_ROWS, MIX_ROWS), BF16), 1)
    o_kv = Q_LORA
    o_pe = o_kv + KV_LORA
    o_u = o_pe + ROPE_DIM

    for l in range(depth):
        wi = w_in[l]
        w_pe = wi[:, o_pe:o_u]
        w_pe_sw = jnp.concatenate([-w_pe[:, HALF_ROPE:], w_pe[:, :HALF_ROPE]], axis=1)
        pe_block = lambda w: jnp.pad(w, ((0, 0), (NOPE_DIM, HEAD_PAD - QK_DIM)))
        win = jnp.concatenate([wi[:, :o_pe], pe_block(w_pe), pe_block(w_pe_sw), wi[:, o_u:]], axis=1).astype(BF16)
        wq = w_uq[l].reshape(Q_LORA, N_HEADS, QK_DIM)
        wuq = _head_blocks(w_uq[l], QK_DIM).astype(BF16)
        wuqs = _head_blocks(_swap_rope(wq, -1.0).reshape(Q_LORA, -1), QK_DIM).astype(BF16)
        wkv = w_ukv[l].reshape(KV_LORA, N_HEADS, NOPE_DIM + V_DIM)
        wuk = _head_blocks(wkv[:, :, :NOPE_DIM].reshape(KV_LORA, -1), NOPE_DIM).astype(BF16)
        wuv = _head_blocks(wkv[:, :, NOPE_DIM:].reshape(KV_LORA, -1), V_DIM).T.astype(BF16)
        ga = _head_blocks(attn_out_g[l].reshape(1, -1), V_DIM)
        wo = w_out[l]
        woa = jnp.pad(wo[:N_HEADS * V_DIM].reshape(N_HEADS, V_DIM, D_MODEL),
                      ((0, 0), (0, HEAD_PAD - V_DIM), (0, 0))).reshape(N_HEADS * HEAD_PAD, D_MODEL).astype(BF16)
        woc = wo[N_HEADS * V_DIM:].astype(BF16)
        rwt = router_w[l].T
        rwh = rwt.astype(BF16)
        rwl = (rwt - rwh.astype(F32)).astype(BF16)

        q, k, v, y = _input_stage(
            x2, ctab, stab, norm_mix_g[l].reshape(1, -1), win, q_latent_g[l].reshape(1, -1), wuq, wuqs,
            kv_latent_g[l].reshape(1, -1), wuk, wuv, _lane_row(q_head_g[l]),
            _lane_row(_swap_rope(q_head_g[l], 1.0)), _lane_row(k_head_g[l]), _lane_row(_swap_rope(k_head_g[l], 1.0)))
        attn = _attention(q, k, v, batch, seq)
        conv = _conv_branch(y, conv_dw_w[l], conv_dw_b[l].reshape(1, -1), conv_ln_g[l].reshape(1, -1),
                            conv_ln_b[l].reshape(1, -1), conv_out_g[l].reshape(1, -1), batch, seq)
        x1, h2, idx, gate, rank, cnt = _mix_stage(
            attn, conv, x2, ga, woa, woc, norm_ffn_g[l].reshape(1, -1), rwh, rwl,
            router_b[l].reshape(-1, 1), tri)

        counts = cnt[:, 0]
        padded = (counts + MOE_ROWS - 1) // MOE_ROWS * MOE_ROWS
        ends = jnp.cumsum(padded)
        starts = ends - padded
        experts = jnp.arange(N_EXPERTS, dtype=jnp.int32)
        dest = rank + jnp.sum(jnp.where(idx[None] == experts[:, None, None], starts[:, None, None], 0), axis=0)
        n_blocks = (t * TOP_K + N_EXPERTS * (MOE_ROWS - 1)) // MOE_ROWS
        n_active = (ends[-1] // MOE_ROWS).astype(jnp.int32)
        blk = jnp.minimum(jnp.arange(n_blocks, dtype=jnp.int32), n_active - 1)
        be = jnp.minimum(jnp.sum((ends[None, :] <= (blk * MOE_ROWS)[:, None]).astype(jnp.int32), axis=1),
                         N_EXPERTS - 1)

        xs = _dispatch(ends.astype(jnp.int32), padded.astype(jnp.int32), dest, h2, n_blocks)
        ys = _moe(be, n_active.reshape(1), xs, w_gate_up[l], b_gate_up[l].reshape(N_EXPERTS, 1, -1),
                  w_down[l], b_down[l].reshape(N_EXPERTS, 1, -1))
        x2 = _combine(dest, x1, gate.T, ys)
    return x2.reshape(batch, seq, D_MODEL)
```

```python
import functools

import jax
import jax.numpy as jnp
from jax import lax
from jax.experimental import pallas as pl
from jax.experimental.pallas import tpu as pltpu

D_MODEL = 1024
N_HEADS = 8
NOPE_DIM = 64
ROPE_DIM = 32
QK_DIM = NOPE_DIM + ROPE_DIM
V_DIM = 64
Q_LORA = 384
KV_LORA = 128
CONV_WIDTH = 512
CONV_TAPS = 31
N_EXPERTS = 32
TOP_K = 4
D_FF = 1024
CHUNK = 64
ROPE_THETA = 10000.0
EPS = 1e-6
SWIGLU_ALPHA = 1.702
SWIGLU_LIMIT = 7.0
LOG2_E = 1.4426950408889634

LANES = 128
SUBLANES = 8
HEAD_PAD = LANES
HALF_ROPE = ROPE_DIM // 2
VMEM_LIMIT_BYTES = 56 * 1024 * 1024

IN_ROWS = 512
ATT_Q = 512
ATT_K = 512
ATT_QPART = 256
ATT_AHEAD = 4
CONV_ROWS = 256
CONV_HALO = 32
MIX_ROWS = 512
DISPATCH_ROWS = 512
MOE_ROWS = 512
COMBINE_ROWS = 512
ISSUE_UNROLL = 4
DMA_PRIORITIES = 2

F32 = jnp.float32
BF16 = jnp.bfloat16


def _params(*semantics, flags=None):
    return pltpu.CompilerParams(dimension_semantics=semantics, vmem_limit_bytes=VMEM_LIMIT_BYTES, flags=flags)


def _rms(x, g):
    return x * lax.rsqrt(jnp.mean(x * x, axis=-1, keepdims=True) + EPS) * g


def _full(shape):
    return pl.BlockSpec(shape, lambda *_: (0,) * len(shape))


ROW_TILES = D_MODEL // LANES
PACKED_TILES = ROW_TILES // 2
assert ROW_TILES == SUBLANES


def _load_tile_rows(ref, rows, lead=(), tiles=ROW_TILES):
    return jnp.concatenate([ref[lead + (pl.ds(c, rows, stride=tiles), slice(None))] for c in range(tiles)], axis=1)


def _store_tile_rows(ref, value, tiles=ROW_TILES):
    rows = value.shape[0]
    for c in range(tiles):
        ref[pl.ds(c, rows, stride=tiles), :] = value[:, c * LANES:(c + 1) * LANES]


def _tile_row(ref, r, lead=(), tiles=ROW_TILES):
    return ref.at[lead + (pl.ds(pl.multiple_of(r * tiles, tiles), tiles),)]


def _pack_bf16_pairs(x):
    half = x.shape[1] // 2
    bits = lax.bitcast_convert_type(x.astype(BF16).astype(F32), jnp.uint32)
    return (bits[:, :half] >> 16) | bits[:, half:]


def _unpack_bf16_pairs(p):
    left = lax.bitcast_convert_type(p << 16, F32)
    right = lax.bitcast_convert_type(p & jnp.uint32(0xFFFF0000), F32)
    return jnp.concatenate([left, right], axis=1).astype(BF16)


def _rope_table_kernel(pos_ref, invf_ref, cos_ref, sin_ref):
    ang = pos_ref[...].astype(F32) * invf_ref[...]
    cos_ref[...] = jnp.cos(ang)
    sin_ref[...] = jnp.sin(ang)


def _rope_tables(positions):
    t = positions.size
    per_row = LANES // HALF_ROPE
    inv_freq = 1.0 / (ROPE_THETA ** (jnp.arange(0, ROPE_DIM, 2, dtype=F32) / ROPE_DIM))
    pos = jnp.broadcast_to(positions.reshape(t // per_row, per_row, 1),
                           (t // per_row, per_row, HALF_ROPE)).reshape(t // per_row, LANES)
    invf = jnp.tile(inv_freq, per_row).reshape(1, LANES)
    shape = jax.ShapeDtypeStruct(pos.shape, F32)
    cos, sin = pl.pallas_call(
        _rope_table_kernel,
        in_specs=[_full(pos.shape), _full(invf.shape)],
        out_specs=[_full(pos.shape), _full(pos.shape)],
        out_shape=[shape, shape],
        compiler_params=pltpu.CompilerParams(vmem_limit_bytes=VMEM_LIMIT_BYTES),
        name="rope_tables",
    )(pos, invf)
    return cos.reshape(t, HALF_ROPE), sin.reshape(t, HALF_ROPE)


def _in_kernel(x_ref, tab_ref, gmix_ref, win_ref, gql_ref, wuq_ref, wuqs_ref, gkvl_ref, wuk_ref,
               wuv_ref, gq_ref, gqs_ref, gk_ref, gks_ref, q_ref, k_ref, v_ref, y_ref):
    h = _rms(x_ref[...], gmix_ref[...]).astype(BF16)
    p = jnp.dot(h, win_ref[...], preferred_element_type=F32)
    o_kv = Q_LORA
    o_pe = o_kv + KV_LORA
    o_ps = o_pe + HEAD_PAD
    o_a = o_ps + HEAD_PAD
    o_g = o_a + CONV_WIDTH
    cq, ckv, kpe, kpe_sw = p[:, :o_kv], p[:, o_kv:o_pe], p[:, o_pe:o_ps], p[:, o_ps:o_a]
    y_ref[...] = p[:, o_a:o_g] * jax.nn.sigmoid(p[:, o_g:])

    cqn = _rms(cq, gql_ref[...]).astype(BF16)
    q = jnp.dot(cqn, wuq_ref[...], preferred_element_type=F32)
    q_sw = jnp.dot(cqn, wuqs_ref[...], preferred_element_type=F32)
    ckvn = _rms(ckv, gkvl_ref[...]).astype(BF16)
    kn = jnp.dot(ckvn, wuk_ref[...], preferred_element_type=F32)
    wide_row = lax.broadcasted_iota(jnp.int32, (N_HEADS * HEAD_PAD, 1), 0)
    ones_row = jnp.where(wide_row % HEAD_PAD == V_DIM, 1.0, 0.0)
    vt = lax.dot_general(wuv_ref[...], ckvn, (((1,), (1,)), ((), ())), preferred_element_type=F32)
    v_ref[...] = (vt + ones_row).astype(BF16)

    tab = tab_ref[...]
    lane = lax.broadcasted_iota(jnp.int32, (1, HEAD_PAD), 1)
    ctab = jnp.where(lane < QK_DIM, tab, 0.0)
    stab = jnp.where((lane >= NOPE_DIM) & (lane < QK_DIM), pltpu.roll(tab, QK_DIM, 1), 0.0)
    scale = QK_DIM ** -0.5 * LOG2_E
    cq_tab = ctab * (gq_ref[...] * scale)
    sq_tab = stab * (gqs_ref[...] * scale)
    ck_tab = ctab * gk_ref[...]
    sk_tab = stab * gks_ref[...]
    k_pe = kpe * ck_tab + kpe_sw * sk_tab
    ss_pe = jnp.sum(kpe * kpe, axis=-1, keepdims=True)
    for hd in range(N_HEADS):
        sl = slice(hd * HEAD_PAD, (hd + 1) * HEAD_PAD)
        qh = q[:, sl]
        rq = lax.rsqrt(jnp.sum(qh * qh, axis=-1, keepdims=True) * (1.0 / QK_DIM) + EPS)
        q_ref[:, sl] = ((qh * cq_tab + q_sw[:, sl] * sq_tab) * rq).astype(BF16)
        kh = kn[:, sl]
        rk = lax.rsqrt((jnp.sum(kh * kh, axis=-1, keepdims=True) + ss_pe) * (1.0 / QK_DIM) + EPS)
        k_ref[:, sl] = ((kh * ck_tab + k_pe) * rk).astype(BF16)


def _input_stage(x2, tab, gmix, win, gql, wuq, wuqs, gkvl, wuk, wuv, gq, gqs, gk, gks):
    t = x2.shape[0]
    wide = N_HEADS * HEAD_PAD
    row = lambda w: pl.BlockSpec((IN_ROWS, w), lambda i: (i, 0))
    consts = (gmix, win, gql, wuq, wuqs, gkvl, wuk, wuv, gq, gqs, gk, gks)
    return pl.pallas_call(
        _in_kernel,
        grid=(t // IN_ROWS,),
        in_specs=[row(D_MODEL), row(HEAD_PAD)] + [_full(a.shape) for a in consts],
        out_specs=[row(wide), row(wide), pl.BlockSpec((wide, IN_ROWS), lambda i: (0, i)), row(CONV_WIDTH)],
        out_shape=[jax.ShapeDtypeStruct((t, wide), BF16), jax.ShapeDtypeStruct((t, wide), BF16),
                   jax.ShapeDtypeStruct((wide, t), BF16), jax.ShapeDtypeStruct((t, CONV_WIDTH), F32)],
        compiler_params=_params("parallel"),
        name="input_stage",
    )(x2, tab, *consts)


def _attn_kernel(q_ref, k_ref, vt_ref, o_ref, *scratch):
    units = [(hd, part) for hd in range(N_HEADS) for part in range(ATT_Q // ATT_QPART)]
    m_refs, acc_refs = scratch[:len(units)], scratch[len(units):]
    i = pl.program_id(1)
    dn = (((1,), (1,)), ((), ()))

    def step(r0, diagonal):
        old = None if diagonal else [(m_refs[u][...], acc_refs[u][...]) for u in range(len(units))]
        new = []

        def keys(part):
            return (part + 1) * ATT_QPART if diagonal else ATT_K

        def scores(u):
            hd, part = units[u]
            sl = slice(hd * HEAD_PAD, (hd + 1) * HEAD_PAD)
            return lax.dot_general(k_ref[pl.ds(r0, keys(part)), sl],
                                   q_ref[part * ATT_QPART:(part + 1) * ATT_QPART, sl], dn,
                                   preferred_element_type=F32)

        ahead = [scores(u) for u in range(ATT_AHEAD)]
        for u, (hd, part) in enumerate(units):
            sl = slice(hd * HEAD_PAD, (hd + 1) * HEAD_PAD)
            s = ahead.pop(0)
            if u + ATT_AHEAD < len(units):
                ahead.append(scores(u + ATT_AHEAD))
            if diagonal:
                kc = lax.broadcasted_iota(jnp.int32, s.shape, 0) // CHUNK
                qc = (lax.broadcasted_iota(jnp.int32, s.shape, 1) + part * ATT_QPART) // CHUNK
                s = jnp.where(kc <= qc, s, -jnp.inf)
            m_new = jnp.max(s, axis=0, keepdims=True)
            if not diagonal:
                m_new = jnp.maximum(old[u][0], m_new)
            acc = jnp.dot(vt_ref[sl, pl.ds(r0, keys(part))], jnp.exp2(s - m_new).astype(BF16),
                          preferred_element_type=F32)
            if not diagonal:
                acc = jnp.exp2(old[u][0] - m_new) * old[u][1] + acc
            new.append((m_new, acc))
        for u in range(len(units)):
            m_refs[u][...], acc_refs[u][...] = new[u]

    step(pl.multiple_of(i * ATT_Q, ATT_Q), True)

    def body(j, _):
        step(pl.multiple_of(j * ATT_K, ATT_K), False)
        return 0

    lax.fori_loop(0, i, body, 0)
    parts = ATT_Q // ATT_QPART
    for pair in range(N_HEADS // 2):
        for part in range(parts):
            halves = []
            for hd in (2 * pair, 2 * pair + 1):
                acc = acc_refs[hd * parts + part][...]
                halves.append(acc[:V_DIM, :] / acc[V_DIM:V_DIM + 1, :])
            o_ref[part * ATT_QPART:(part + 1) * ATT_QPART, pair * LANES:(pair + 1) * LANES] = (
                jnp.concatenate(halves, axis=0).T.astype(BF16))


def _attention(q, k, vt, batch, seq):
    assert ATT_Q == ATT_K and seq % ATT_Q == 0 and ATT_Q % ATT_QPART == 0 and ATT_QPART % CHUNK == 0
    nq = seq // ATT_Q
    chains = N_HEADS * (ATT_Q // ATT_QPART)
    wide = N_HEADS * HEAD_PAD
    qspec = pl.BlockSpec((ATT_Q, wide), lambda b, i: (b * nq + i, 0))
    return pl.pallas_call(
        _attn_kernel,
        grid=(batch, nq),
        in_specs=[qspec, pl.BlockSpec((seq, wide), lambda b, i: (b, 0)),
                  pl.BlockSpec((wide, seq), lambda b, i: (0, b))],
        out_specs=pl.BlockSpec((ATT_Q, N_HEADS * V_DIM), lambda b, i: (b * nq + i, 0)),
        out_shape=jax.ShapeDtypeStruct((q.shape[0], N_HEADS * V_DIM), BF16),
        scratch_shapes=([pltpu.VMEM((1, ATT_QPART), F32)] * chains + [pltpu.VMEM((HEAD_PAD, ATT_QPART), F32)] * chains),
        compiler_params=_params("parallel", "arbitrary"),
        name="attention",
    )(q, k, vt)


def _conv_kernel(y_ref, w_ref, b_ref, lng_ref, lnb_ref, og_ref, o_ref, pad_ref, phase_ref):
    seq = y_ref.shape[0]
    pad_ref[0:CONV_HALO, :] = jnp.zeros((CONV_HALO, CONV_WIDTH), F32)
    pad_ref[CONV_HALO:, :] = y_ref[...]
    first = CONV_HALO - (CONV_TAPS - 1)

    def body(i, _):
        r0 = pl.multiple_of(i * CONV_ROWS, CONV_ROWS)
        win = pad_ref[pl.ds(r0, CONV_ROWS + CONV_HALO), :]
        acc = jnp.zeros((CONV_ROWS, CONV_WIDTH), F32)
        for s in range(SUBLANES):
            offs = [o for o in range(first, first + CONV_TAPS) if o % SUBLANES == s]
            span = max(offs) - s + CONV_ROWS
            if s:
                phase_ref[s, 0:span, :] = win[s:s + span, :]
            for o in offs:
                rows = (pad_ref[pl.ds(pl.multiple_of(r0 + o, SUBLANES), CONV_ROWS), :] if s == 0
                        else phase_ref[s, o - s:o - s + CONV_ROWS, :])
                acc = acc + w_ref[o - first:o - first + 1, :] * rows
        acc = acc + b_ref[...]
        xc = acc - jnp.mean(acc, axis=-1, keepdims=True)
        ln = xc * lax.rsqrt(jnp.mean(xc * xc, axis=-1, keepdims=True) + EPS) * lng_ref[...] + lnb_ref[...]
        z = ln * jax.nn.sigmoid(ln)
        o_ref[pl.ds(r0, CONV_ROWS), :] = _rms(z, og_ref[...]).astype(BF16)
        return 0

    lax.fori_loop(0, seq // CONV_ROWS, body, 0)


def _conv_branch(y, w, b, lng, lnb, og, batch, seq):
    spec = pl.BlockSpec((seq, CONV_WIDTH), lambda bi: (bi, 0))
    return pl.pallas_call(
        _conv_kernel,
        grid=(batch,),
        in_specs=[spec, _full(w.shape), _full(b.shape), _full(lng.shape), _full(lnb.shape), _full(og.shape)],
        out_specs=spec,
        out_shape=jax.ShapeDtypeStruct(y.shape, BF16),
        scratch_shapes=[pltpu.VMEM((seq + CONV_HALO, CONV_WIDTH), F32),
                        pltpu.VMEM((SUBLANES, CONV_ROWS + CONV_HALO, CONV_WIDTH), F32)],
        compiler_params=_params("parallel"),
        name="conv_branch",
    )(y, w, b, lng, lnb, og)


def _mix_kernel(attn_ref, conv_ref, x_ref, ga_ref, woa_ref, woc_ref, gffn_ref, rwh_ref, rwl_ref, rb_ref,
                tri_ref, x1_ref, h2_ref, idx_ref, gate_ref, rank_ref, cnt_ref, base_ref):
    @pl.when(pl.program_id(0) == 0)
    def _():
        base_ref[...] = jnp.zeros(base_ref.shape, F32)

    a = attn_ref[...].astype(F32)
    attn_width = N_HEADS * V_DIM
    an = a * lax.rsqrt(jnp.sum(a * a, axis=-1, keepdims=True) * (1.0 / attn_width) + EPS) * ga_ref[...]
    x1 = (x_ref[...] + jnp.dot(an.astype(BF16), woa_ref[...], preferred_element_type=F32)
          + jnp.dot(conv_ref[...], woc_ref[...], preferred_element_type=F32))
    x1_ref[...] = x1
    h2 = _rms(x1, gffn_ref[...])
    _store_tile_rows(h2_ref, _pack_bf16_pairs(h2), PACKED_TILES)

    hi = h2.astype(BF16)
    lo = (h2 - hi.astype(F32)).astype(BF16)
    dn = (((1,), (1,)), ((), ()))
    rwh, rwl = rwh_ref[...], rwl_ref[...]
    logits = (lax.dot_general(rwh, hi, dn, preferred_element_type=F32)
              + lax.dot_general(rwh, lo, dn, preferred_element_type=F32)
              + lax.dot_general(rwl, hi, dn, preferred_element_type=F32)) + rb_ref[...]

    rows = logits.shape[1]
    eidx = lax.broadcasted_iota(jnp.int32, (N_EXPERTS, rows), 0).astype(F32)
    work = logits
    sels, vals = [], []
    for k in range(TOP_K):
        mx = jnp.max(work, axis=0, keepdims=True)
        first = jnp.min(jnp.where(work == mx, eidx, float(N_EXPERTS)), axis=0, keepdims=True)
        sel = eidx == first
        work = jnp.where(sel, -jnp.inf, work)
        sels.append(sel)
        vals.append(mx)
        idx_ref[k:k + 1, :] = first.astype(jnp.int32)
    exps = [jnp.exp(v - vals[0]) for v in vals]
    denom = exps[0] + exps[1] + exps[2] + exps[3]
    for k in range(TOP_K):
        gate_ref[k:k + 1, :] = exps[k] / denom

    member = jnp.where(sels[0] | sels[1] | sels[2] | sels[3], 1.0, 0.0)
    before = jnp.dot(member.astype(BF16), tri_ref[...], preferred_element_type=F32)
    posn = base_ref[:, 0:1] + before
    for k in range(TOP_K):
        rank_ref[k:k + 1, :] = jnp.sum(jnp.where(sels[k], posn, 0.0), axis=0, keepdims=True).astype(jnp.int32)
    base_ref[...] = base_ref[...] + jnp.sum(member, axis=1, keepdims=True)
    cnt_ref[...] = base_ref[...].astype(jnp.int32)


def _mix_stage(attn, conv, x2, ga, woa, woc, gffn, rwh, rwl, rb, tri):
    t = x2.shape[0]
    proj = lambda i: (i, 0)
    row = lambda w: pl.BlockSpec((MIX_ROWS, w), proj)
    col = pl.BlockSpec((TOP_K, MIX_ROWS), lambda i: (0, i))
    return pl.pallas_call(
        _mix_kernel,
        grid=(t // MIX_ROWS,),
        in_specs=[row(attn.shape[1]), row(CONV_WIDTH), row(D_MODEL), _full(ga.shape), _full(woa.shape),
                  _full(woc.shape), _full(gffn.shape), _full(rwh.shape), _full(rwl.shape), _full(rb.shape),
                  _full(tri.shape)],
        out_specs=[row(D_MODEL), pl.BlockSpec((MIX_ROWS * PACKED_TILES, LANES), proj), col, col, col,
                   _full((N_EXPERTS, LANES))],
        out_shape=[jax.ShapeDtypeStruct((t, D_MODEL), F32),
                   jax.ShapeDtypeStruct((t * PACKED_TILES, LANES), jnp.uint32),
                   jax.ShapeDtypeStruct((TOP_K, t), jnp.int32), jax.ShapeDtypeStruct((TOP_K, t), F32),
                   jax.ShapeDtypeStruct((TOP_K, t), jnp.int32),
                   jax.ShapeDtypeStruct((N_EXPERTS, LANES), jnp.int32)],
        scratch_shapes=[pltpu.VMEM((N_EXPERTS, LANES), F32)],
        compiler_params=_params("arbitrary"),
        name="mix_router",
    )(attn, conv, x2, ga, woa, woc, gffn, rwh, rwl, rb, tri)


def _dispatch_kernel(ends_ref, padded_ref, dest_ref, h_ref, xs_ref, zero_ref, sem, zsem):
    tiles = PACKED_TILES
    rows = h_ref.shape[0] // tiles
    block = MOE_ROWS * tiles
    n_blocks = xs_ref.shape[0] // block

    @pl.when(pl.program_id(0) == 0)
    def _():
        zero_ref[...] = jnp.zeros(zero_ref.shape, zero_ref.dtype)

        def zero_copy(r0):
            return pltpu.make_async_copy(
                zero_ref, xs_ref.at[pl.ds(pl.multiple_of(r0 * tiles, block), block)], zsem)

        for start in (True, False):
            for e in range(N_EXPERTS):
                @pl.when(padded_ref[e] > 0)
                def _():
                    cp = zero_copy(ends_ref[e] - MOE_ROWS)
                    cp.start() if start else cp.wait()

            def tail(b, _):
                cp = zero_copy(b * MOE_ROWS)
                cp.start() if start else cp.wait()
                return 0

            lax.fori_loop(ends_ref[N_EXPERTS - 1] // MOE_ROWS, n_blocks, tail, 0)

    def issue(t, _):
        for k in range(TOP_K):
            pltpu.make_async_copy(_tile_row(h_ref, t, tiles=tiles), _tile_row(xs_ref, dest_ref[k, t], tiles=tiles),
                                  sem).start(priority=k % DMA_PRIORITIES)
        return 0

    lax.fori_loop(0, rows, issue, 0, unroll=ISSUE_UNROLL)
    for _ in range(TOP_K):
        pltpu.make_async_copy(h_ref, xs_ref.at[pl.ds(0, rows * tiles)], sem).wait()


def _dispatch(ends, padded, dest, h2, n_blocks):
    t = h2.shape[0] // PACKED_TILES
    grid_spec = pltpu.PrefetchScalarGridSpec(
        num_scalar_prefetch=2,
        grid=(t // DISPATCH_ROWS,),
        in_specs=[pl.BlockSpec((TOP_K, DISPATCH_ROWS), lambda i, *_: (0, i), memory_space=pltpu.SMEM),
                  pl.BlockSpec((DISPATCH_ROWS * PACKED_TILES, LANES), lambda i, *_: (i, 0))],
        out_specs=pl.BlockSpec(memory_space=pl.ANY),
        scratch_shapes=[pltpu.VMEM((MOE_ROWS * PACKED_TILES, LANES), h2.dtype), pltpu.SemaphoreType.DMA(()),
                        pltpu.SemaphoreType.DMA(())],
    )
    return pl.pallas_call(
        _dispatch_kernel,
        grid_spec=grid_spec,
        out_shape=jax.ShapeDtypeStruct((n_blocks * MOE_ROWS * PACKED_TILES, LANES), h2.dtype),
        compiler_params=_params("arbitrary"),
        name="dispatch_rows",
    )(ends, padded, dest, h2)


def _moe_kernel(be_ref, nact_ref, xs_ref, wgu_ref, bgu_ref, wd_ref, bd_ref, ys_ref, wgu_bf, wd_bf):
    i = pl.program_id(0)
    active = i < nact_ref[0]

    @pl.when(jnp.logical_not(active))
    def _():
        ys_ref[...] = jnp.zeros(ys_ref.shape, F32)

    @pl.when(active & ((i == 0) | (be_ref[i] != be_ref[jnp.maximum(i - 1, 0)])))
    def _():
        wgu_bf[...] = wgu_ref[0].astype(BF16)
        wd_bf[...] = wd_ref[0].astype(BF16)

    @pl.when(active)
    def _():
        x = _unpack_bf16_pairs(_load_tile_rows(xs_ref, MOE_ROWS, tiles=PACKED_TILES))
        gu = jnp.dot(x, wgu_bf[...], preferred_element_type=F32) + bgu_ref[0]
        gate = jnp.minimum(gu[:, :D_FF], SWIGLU_LIMIT)
        up = jnp.clip(gu[:, D_FF:], -SWIGLU_LIMIT, SWIGLU_LIMIT)
        mid = (up + 1.0) * (gate * jax.nn.sigmoid(gate * SWIGLU_ALPHA))
        _store_tile_rows(ys_ref, jnp.dot(mid.astype(BF16), wd_bf[...], preferred_element_type=F32) + bd_ref[0])


def _moe(block_expert, n_active, xs, wgu, bgu, wd, bd):
    block = MOE_ROWS * ROW_TILES
    n_blocks = xs.shape[0] // (MOE_ROWS * PACKED_TILES)
    grid_spec = pltpu.PrefetchScalarGridSpec(
        num_scalar_prefetch=2,
        grid=(n_blocks,),
        in_specs=[
            pl.BlockSpec((MOE_ROWS * PACKED_TILES, LANES), lambda i, be, na: (jnp.minimum(i, na[0] - 1), 0)),
            pl.BlockSpec((1, D_MODEL, 2 * D_FF), lambda i, be, na: (be[i], 0, 0)),
            pl.BlockSpec((1, 1, 2 * D_FF), lambda i, be, na: (be[i], 0, 0)),
            pl.BlockSpec((1, D_FF, D_MODEL), lambda i, be, na: (be[i], 0, 0)),
            pl.BlockSpec((1, 1, D_MODEL), lambda i, be, na: (be[i], 0, 0)),
        ],
        out_specs=pl.BlockSpec((block, LANES), lambda i, be, na: (i, 0)),
        scratch_shapes=[pltpu.VMEM((D_MODEL, 2 * D_FF), BF16), pltpu.VMEM((D_FF, D_MODEL), BF16)],
    )
    return pl.pallas_call(
        _moe_kernel,
        grid_spec=grid_spec,
        out_shape=jax.ShapeDtypeStruct((n_blocks * block, LANES), F32),
        compiler_params=_params("arbitrary"),
        name="expert_mlp",
    )(block_expert, n_active, xs, wgu, bgu, wd, bd)


def _combine_kernel(dest_ref, dest_next_ref, x1_ref, gate_ref, ys_ref, o_ref, buf_ref, sems):
    i, n = pl.program_id(0), pl.num_programs(0)
    rows = x1_ref.shape[0]

    def gather(idx_ref, slot):
        def issue(t, _):
            for k in range(TOP_K):
                pltpu.make_async_copy(_tile_row(ys_ref, idx_ref[k, t]), _tile_row(buf_ref, t, (slot, k)),
                                      sems.at[slot]).start(priority=k % DMA_PRIORITIES)
            return 0

        lax.fori_loop(0, rows, issue, 0, unroll=ISSUE_UNROLL)

    @pl.when(i == 0)
    def _():
        gather(dest_ref, 0)

    @pl.when(i + 1 < n)
    def _():
        gather(dest_next_ref, (i + 1) % 2)

    slot = i % 2
    for k in range(TOP_K):
        pltpu.make_async_copy(ys_ref.at[pl.ds(0, rows * ROW_TILES)], buf_ref.at[slot, k], sems.at[slot]).wait()
    acc = x1_ref[...]
    for k in range(TOP_K):
        acc = acc + gate_ref[:, k:k + 1] * _load_tile_rows(buf_ref, rows, (slot, k))
    o_ref[...] = acc


def _combine(dest, x1, gate_rows, ys):
    t = x1.shape[0]
    steps = t // COMBINE_ROWS
    row = lambda w: pl.BlockSpec((COMBINE_ROWS, w), lambda i: (i, 0))
    slots = lambda nxt: pl.BlockSpec((TOP_K, COMBINE_ROWS), lambda i: (0, jnp.minimum(i + nxt, steps - 1)),
                                     memory_space=pltpu.SMEM)
    return pl.pallas_call(
        _combine_kernel,
        grid=(steps,),
        in_specs=[slots(0), slots(1), row(D_MODEL), row(TOP_K), pl.BlockSpec(memory_space=pl.ANY)],
        out_specs=row(D_MODEL),
        out_shape=jax.ShapeDtypeStruct(x1.shape, F32),
        scratch_shapes=[pltpu.VMEM((2, TOP_K, COMBINE_ROWS * ROW_TILES, LANES), F32),
                        pltpu.SemaphoreType.DMA((2,))],
        compiler_params=_params("arbitrary"),
        name="combine_rows",
    )(dest, dest, x1, gate_rows, ys)


def _head_blocks(w, width):
    r = w.shape[0]
    w = w.reshape(r, N_HEADS, width)
    return jnp.pad(w, ((0, 0), (0, 0), (0, HEAD_PAD - width))).reshape(r, N_HEADS * HEAD_PAD)


def _lane_row(v, offset=0):
    return jnp.pad(v, (offset, HEAD_PAD - offset - v.shape[0])).reshape(1, HEAD_PAD)


def _swap_rope(w, sign):
    lo, hi = w[..., NOPE_DIM:NOPE_DIM + HALF_ROPE], w[..., NOPE_DIM + HALF_ROPE:QK_DIM]
    return jnp.concatenate([jnp.zeros_like(w[..., :NOPE_DIM]), sign * hi, lo], axis=-1)


def kernel(x, positions, norm_mix_g, w_in, q_latent_g, w_uq, kv_latent_g, w_ukv, q_head_g, k_head_g, conv_dw_w, conv_dw_b, conv_ln_g, conv_ln_b, attn_out_g, conv_out_g, w_out, norm_ffn_g, router_w, router_b, w_gate_up, b_gate_up, w_down, b_down):
    batch, seq, _ = x.shape
    t = batch * seq
    depth = norm_mix_g.shape[0]
    x2 = x.reshape(t, D_MODEL)
    cos, sin = _rope_tables(positions)
    tab = jnp.concatenate([jnp.ones((t, NOPE_DIM), F32), cos, cos, sin, sin], axis=1)
    tri = jnp.triu(jnp.ones((MIX_ROWS, MIX_ROWS), BF16), 1)
    o_kv = Q_LORA
    o_pe = o_kv + KV_LORA
    o_u = o_pe + ROPE_DIM

    for l in range(depth):
        wi = w_in[l]
        w_pe = wi[:, o_pe:o_u]
        w_pe_sw = jnp.concatenate([-w_pe[:, HALF_ROPE:], w_pe[:, :HALF_ROPE]], axis=1)
        pe_block = lambda w: jnp.pad(w, ((0, 0), (NOPE_DIM, HEAD_PAD - QK_DIM)))
        win = jnp.concatenate([wi[:, :o_pe], pe_block(w_pe), pe_block(w_pe_sw), wi[:, o_u:]], axis=1).astype(BF16)
        wq = w_uq[l].reshape(Q_LORA, N_HEADS, QK_DIM)
        wuq = _head_blocks(w_uq[l], QK_DIM).astype(BF16)
        wuqs = _head_blocks(_swap_rope(wq, -1.0).reshape(Q_LORA, -1), QK_DIM).astype(BF16)
        wkv = w_ukv[l].reshape(KV_LORA, N_HEADS, NOPE_DIM + V_DIM)
        wuk = _head_blocks(wkv[:, :, :NOPE_DIM].reshape(KV_LORA, -1), NOPE_DIM).astype(BF16)
        wuv = _head_blocks(wkv[:, :, NOPE_DIM:].reshape(KV_LORA, -1), V_DIM).T.astype(BF16)
        ga = attn_out_g[l].reshape(1, -1)
        wo = w_out[l]
        woa = wo[:N_HEADS * V_DIM].astype(BF16)
        woc = wo[N_HEADS * V_DIM:].astype(BF16)
        rwt = router_w[l].T
        rwh = rwt.astype(BF16)
        rwl = (rwt - rwh.astype(F32)).astype(BF16)

        q, k, v, y = _input_stage(
            x2, tab, norm_mix_g[l].reshape(1, -1), win, q_latent_g[l].reshape(1, -1), wuq, wuqs,
            kv_latent_g[l].reshape(1, -1), wuk, wuv, _lane_row(q_head_g[l]),
            _lane_row(_swap_rope(q_head_g[l], 1.0)), _lane_row(k_head_g[l]), _lane_row(_swap_rope(k_head_g[l], 1.0)))
        attn = _attention(q, k, v, batch, seq)
        conv = _conv_branch(y, conv_dw_w[l], conv_dw_b[l].reshape(1, -1), conv_ln_g[l].reshape(1, -1),
                            conv_ln_b[l].reshape(1, -1), conv_out_g[l].reshape(1, -1), batch, seq)
        x1, h2, idx, gate, rank, cnt = _mix_stage(
            attn, conv, x2, ga, woa, woc, norm_ffn_g[l].reshape(1, -1), rwh, rwl,
            router_b[l].reshape(-1, 1), tri)

        counts = cnt[:, 0]
        padded = (counts + MOE_ROWS - 1) // MOE_ROWS * MOE_ROWS
        ends = jnp.cumsum(padded)
        starts = ends - padded
        experts = jnp.arange(N_EXPERTS, dtype=jnp.int32)
        dest = rank + jnp.sum(jnp.where(idx[None] == experts[:, None, None], starts[:, None, None], 0), axis=0)
        n_blocks = (t * TOP_K + N_EXPERTS * (MOE_ROWS - 1)) // MOE_ROWS
        n_active = (ends[-1] // MOE_ROWS).astype(jnp.int32)
        blk = jnp.minimum(jnp.arange(n_blocks, dtype=jnp.int32), n_active - 1)
        be = jnp.minimum(jnp.sum((ends[None, :] <= (blk * MOE_ROWS)[:, None]).astype(jnp.int32), axis=1),
                         N_EXPERTS - 1)

        xs = _dispatch(ends.astype(jnp.int32), padded.astype(jnp.int32), dest, h2, n_blocks)
        ys = _moe(be, n_active.reshape(1), xs, w_gate_up[l], b_gate_up[l].reshape(N_EXPERTS, 1, -1),
                  w_down[l], b_down[l].reshape(N_EXPERTS, 1, -1))
        x2 = _combine(dest, x1, gate.T, ys)
    return x2.reshape(batch, seq, D_MODEL)
```

```python
import jax
import jax.numpy as jnp
from jax import lax
from jax.experimental import pallas as pl
from jax.experimental.pallas import tpu as pltpu

D_MODEL = 1024
N_HEADS = 8
NOPE_DIM = 64
ROPE_DIM = 32
QK_DIM = NOPE_DIM + ROPE_DIM
V_DIM = 64
Q_LORA = 384
KV_LORA = 128
CONV_WIDTH = 512
CONV_TAPS = 31
N_EXPERTS = 32
TOP_K = 4
D_FF = 1024
CHUNK = 64
ROPE_THETA = 10000.0
EPS = 1e-6
SWIGLU_ALPHA = 1.702
SWIGLU_LIMIT = 7.0
LOG2_E = 1.4426950408889634

LANES = 128
SUBLANES = 8
HEAD_PAD = LANES
HALF_ROPE = ROPE_DIM // 2
VMEM_LIMIT_BYTES = 56 * 1024 * 1024

IN_ROWS = 512
ATT_Q = 512
ATT_K = 512
ATT_QPART = 256
ATT_AHEAD = 4
CONV_ROWS = 256
CONV_HALO = 32
MIX_ROWS = 512
DISPATCH_ROWS = 512
MOE_ROWS = 512
COMBINE_ROWS = 512
ISSUE_UNROLL = 4
DMA_PRIORITIES = 2

F32 = jnp.float32
BF16 = jnp.bfloat16


def _params(*semantics, flags=None):
    return pltpu.CompilerParams(dimension_semantics=semantics, vmem_limit_bytes=VMEM_LIMIT_BYTES, flags=flags)


def _rms(x, g):
    return x * lax.rsqrt(jnp.mean(x * x, axis=-1, keepdims=True) + EPS) * g


def _full(shape):
    return pl.BlockSpec(shape, lambda *_: (0,) * len(shape))


ROW_TILES = D_MODEL // LANES
PACKED_TILES = ROW_TILES // 2
assert ROW_TILES == SUBLANES


def _load_tile_rows(ref, rows, lead=(), tiles=ROW_TILES):
    return jnp.concatenate([ref[lead + (pl.ds(c, rows, stride=tiles), slice(None))] for c in range(tiles)], axis=1)


def _store_tile_rows(ref, value, tiles=ROW_TILES):
    rows = value.shape[0]
    for c in range(tiles):
        ref[pl.ds(c, rows, stride=tiles), :] = value[:, c * LANES:(c + 1) * LANES]


def _tile_row(ref, r, lead=(), tiles=ROW_TILES):
    return ref.at[lead + (pl.ds(pl.multiple_of(r * tiles, tiles), tiles),)]


def _pack_bf16_pairs(x):
    half = x.shape[1] // 2
    bits = lax.bitcast_convert_type(x.astype(BF16).astype(F32), jnp.uint32)
    return (bits[:, :half] >> 16) | bits[:, half:]


def _unpack_bf16_pairs(p):
    left = lax.bitcast_convert_type(p << 16, F32)
    right = lax.bitcast_convert_type(p & jnp.uint32(0xFFFF0000), F32)
    return jnp.concatenate([left, right], axis=1).astype(BF16)


def _in_kernel(x_ref, pos_ref, invf_ref, gmix_ref, win_ref, gql_ref, wuq_ref, wuqs_ref, gkvl_ref, wuk_ref,
               wuv_ref, gq_ref, gqs_ref, gk_ref, gks_ref, q_ref, k_ref, v_ref, y_ref):
    h = _rms(x_ref[...], gmix_ref[...]).astype(BF16)
    p = jnp.dot(h, win_ref[...], preferred_element_type=F32)
    o_kv = Q_LORA
    o_pe = o_kv + KV_LORA
    o_ps = o_pe + HEAD_PAD
    o_a = o_ps + HEAD_PAD
    o_g = o_a + CONV_WIDTH
    cq, ckv, kpe, kpe_sw = p[:, :o_kv], p[:, o_kv:o_pe], p[:, o_pe:o_ps], p[:, o_ps:o_a]
    y_ref[...] = p[:, o_a:o_g] * jax.nn.sigmoid(p[:, o_g:])

    cqn = _rms(cq, gql_ref[...]).astype(BF16)
    q = jnp.dot(cqn, wuq_ref[...], preferred_element_type=F32)
    q_sw = jnp.dot(cqn, wuqs_ref[...], preferred_element_type=F32)
    ckvn = _rms(ckv, gkvl_ref[...]).astype(BF16)
    kn = jnp.dot(ckvn, wuk_ref[...], preferred_element_type=F32)
    wide_row = lax.broadcasted_iota(jnp.int32, (N_HEADS * HEAD_PAD, 1), 0)
    ones_row = jnp.where(wide_row % HEAD_PAD == V_DIM, 1.0, 0.0)
    vt = lax.dot_general(wuv_ref[...], ckvn, (((1,), (1,)), ((), ())), preferred_element_type=F32)
    v_ref[...] = (vt + ones_row).astype(BF16)

    ang = invf_ref[...] * pos_ref[...].astype(F32)
    cos_t, sin_t = jnp.cos(ang), jnp.sin(ang)
    tab = jnp.concatenate([jnp.ones((NOPE_DIM, ang.shape[1]), F32), cos_t, cos_t, sin_t, sin_t], axis=0).T
    lane = lax.broadcasted_iota(jnp.int32, (1, HEAD_PAD), 1)
    ctab = jnp.where(lane < QK_DIM, tab, 0.0)
    stab = jnp.where((lane >= NOPE_DIM) & (lane < QK_DIM), pltpu.roll(tab, QK_DIM, 1), 0.0)
    scale = QK_DIM ** -0.5 * LOG2_E
    cq_tab = ctab * (gq_ref[...] * scale)
    sq_tab = stab * (gqs_ref[...] * scale)
    ck_tab = ctab * gk_ref[...]
    sk_tab = stab * gks_ref[...]
    k_pe = kpe * ck_tab + kpe_sw * sk_tab
    ss_pe = jnp.sum(kpe * kpe, axis=-1, keepdims=True)
    for hd in range(N_HEADS):
        sl = slice(hd * HEAD_PAD, (hd + 1) * HEAD_PAD)
        qh = q[:, sl]
        rq = lax.rsqrt(jnp.sum(qh * qh, axis=-1, keepdims=True) * (1.0 / QK_DIM) + EPS)
        q_ref[:, sl] = ((qh * cq_tab + q_sw[:, sl] * sq_tab) * rq).astype(BF16)
        kh = kn[:, sl]
        rk = lax.rsqrt((jnp.sum(kh * kh, axis=-1, keepdims=True) + ss_pe) * (1.0 / QK_DIM) + EPS)
        k_ref[:, sl] = ((kh * ck_tab + k_pe) * rk).astype(BF16)


def _input_stage(x2, pos, invf, gmix, win, gql, wuq, wuqs, gkvl, wuk, wuv, gq, gqs, gk, gks):
    t = x2.shape[0]
    wide = N_HEADS * HEAD_PAD
    row = lambda w: pl.BlockSpec((IN_ROWS, w), lambda i: (i, 0))
    consts = (invf, gmix, win, gql, wuq, wuqs, gkvl, wuk, wuv, gq, gqs, gk, gks)
    return pl.pallas_call(
        _in_kernel,
        grid=(t // IN_ROWS,),
        in_specs=[row(D_MODEL), pl.BlockSpec((1, IN_ROWS), lambda i: (0, i))] + [_full(a.shape) for a in consts],
        out_specs=[row(wide), row(wide), pl.BlockSpec((wide, IN_ROWS), lambda i: (0, i)), row(CONV_WIDTH)],
        out_shape=[jax.ShapeDtypeStruct((t, wide), BF16), jax.ShapeDtypeStruct((t, wide), BF16),
                   jax.ShapeDtypeStruct((wide, t), BF16), jax.ShapeDtypeStruct((t, CONV_WIDTH), F32)],
        compiler_params=_params("parallel"),
        name="input_stage",
    )(x2, pos, *consts)


def _attn_kernel(q_ref, k_ref, vt_ref, o_ref, *scratch):
    units = [(hd, part) for hd in range(N_HEADS) for part in range(ATT_Q // ATT_QPART)]
    m_refs, acc_refs = scratch[:len(units)], scratch[len(units):]
    i = pl.program_id(1)
    dn = (((1,), (1,)), ((), ()))

    def step(r0, diagonal):
        old = None if diagonal else [(m_refs[u][...], acc_refs[u][...]) for u in range(len(units))]
        new = []

        def keys(part):
            return (part + 1) * ATT_QPART if diagonal else ATT_K

        def scores(u):
            hd, part = units[u]
            sl = slice(hd * HEAD_PAD, (hd + 1) * HEAD_PAD)
            return lax.dot_general(k_ref[pl.ds(r0, keys(part)), sl],
                                   q_ref[part * ATT_QPART:(part + 1) * ATT_QPART, sl], dn,
                                   preferred_element_type=F32)

        ahead = [scores(u) for u in range(ATT_AHEAD)]
        for u, (hd, part) in enumerate(units):
            sl = slice(hd * HEAD_PAD, (hd + 1) * HEAD_PAD)
            s = ahead.pop(0)
            if u + ATT_AHEAD < len(units):
                ahead.append(scores(u + ATT_AHEAD))
            if diagonal:
                kc = lax.broadcasted_iota(jnp.int32, s.shape, 0) // CHUNK
                qc = (lax.broadcasted_iota(jnp.int32, s.shape, 1) + part * ATT_QPART) // CHUNK
                s = jnp.where(kc <= qc, s, -jnp.inf)
            m_new = jnp.max(s, axis=0, keepdims=True)
            if not diagonal:
                m_new = jnp.maximum(old[u][0], m_new)
            acc = jnp.dot(vt_ref[sl, pl.ds(r0, keys(part))], jnp.exp2(s - m_new).astype(BF16),
                          preferred_element_type=F32)
            if not diagonal:
                acc = jnp.exp2(old[u][0] - m_new) * old[u][1] + acc
            new.append((m_new, acc))
        for u in range(len(units)):
            m_refs[u][...], acc_refs[u][...] = new[u]

    step(pl.multiple_of(i * ATT_Q, ATT_Q), True)

    def body(j, _):
        step(pl.multiple_of(j * ATT_K, ATT_K), False)
        return 0

    lax.fori_loop(0, i, body, 0)
    parts = ATT_Q // ATT_QPART
    for pair in range(N_HEADS // 2):
        for part in range(parts):
            halves = []
            for hd in (2 * pair, 2 * pair + 1):
                acc = acc_refs[hd * parts + part][...]
                halves.append(acc[:V_DIM, :] / acc[V_DIM:V_DIM + 1, :])
            o_ref[part * ATT_QPART:(part + 1) * ATT_QPART, pair * LANES:(pair + 1) * LANES] = (
                jnp.concatenate(halves, axis=0).T.astype(BF16))


def _attention(q, k, vt, batch, seq):
    assert ATT_Q == ATT_K and seq % ATT_Q == 0 and ATT_Q % ATT_QPART == 0 and ATT_QPART % CHUNK == 0
    nq = seq // ATT_Q
    chains = N_HEADS * (ATT_Q // ATT_QPART)
    wide = N_HEADS * HEAD_PAD
    qspec = pl.BlockSpec((ATT_Q, wide), lambda b, i: (b * nq + i, 0))
    return pl.pallas_call(
        _attn_kernel,
        grid=(batch, nq),
        in_specs=[qspec, pl.BlockSpec((seq, wide), lambda b, i: (b, 0)),
                  pl.BlockSpec((wide, seq), lambda b, i: (0, b))],
        out_specs=pl.BlockSpec((ATT_Q, N_HEADS * V_DIM), lambda b, i: (b * nq + i, 0)),
        out_shape=jax.ShapeDtypeStruct((q.shape[0], N_HEADS * V_DIM), BF16),
        scratch_shapes=([pltpu.VMEM((1, ATT_QPART), F32)] * chains + [pltpu.VMEM((HEAD_PAD, ATT_QPART), F32)] * chains),
        compiler_params=_params("parallel", "arbitrary"),
        name="attention",
    )(q, k, vt)


def _conv_kernel(y_ref, w_ref, b_ref, lng_ref, lnb_ref, og_ref, o_ref, pad_ref, phase_ref):
    seq = y_ref.shape[0]
    pad_ref[0:CONV_HALO, :] = jnp.zeros((CONV_HALO, CONV_WIDTH), F32)
    pad_ref[CONV_HALO:, :] = y_ref[...]
    first = CONV_HALO - (CONV_TAPS - 1)

    def body(i, _):
        r0 = pl.multiple_of(i * CONV_ROWS, CONV_ROWS)
        win = pad_ref[pl.ds(r0, CONV_ROWS + CONV_HALO), :]
        acc = jnp.zeros((CONV_ROWS, CONV_WIDTH), F32)
        for s in range(SUBLANES):
            offs = [o for o in range(first, first + CONV_TAPS) if o % SUBLANES == s]
            span = max(offs) - s + CONV_ROWS
            if s:
                phase_ref[s, 0:span, :] = win[s:s + span, :]
            for o in offs:
                rows = (pad_ref[pl.ds(pl.multiple_of(r0 + o, SUBLANES), CONV_ROWS), :] if s == 0
                        else phase_ref[s, o - s:o - s + CONV_ROWS, :])
                acc = acc + w_ref[o - first:o - first + 1, :] * rows
        acc = acc + b_ref[...]
        xc = acc - jnp.mean(acc, axis=-1, keepdims=True)
        ln = xc * lax.rsqrt(jnp.mean(xc * xc, axis=-1, keepdims=True) + EPS) * lng_ref[...] + lnb_ref[...]
        z = ln * jax.nn.sigmoid(ln)
        o_ref[pl.ds(r0, CONV_ROWS), :] = _rms(z, og_ref[...]).astype(BF16)
        return 0

    lax.fori_loop(0, seq // CONV_ROWS, body, 0)


def _conv_branch(y, w, b, lng, lnb, og, batch, seq):
    spec = pl.BlockSpec((seq, CONV_WIDTH), lambda bi: (bi, 0))
    return pl.pallas_call(
        _conv_kernel,
        grid=(batch,),
        in_specs=[spec, _full(w.shape), _full(b.shape), _full(lng.shape), _full(lnb.shape), _full(og.shape)],
        out_specs=spec,
        out_shape=jax.ShapeDtypeStruct(y.shape, BF16),
        scratch_shapes=[pltpu.VMEM((seq + CONV_HALO, CONV_WIDTH), F32),
                        pltpu.VMEM((SUBLANES, CONV_ROWS + CONV_HALO, CONV_WIDTH), F32)],
        compiler_params=_params("parallel"),
        name="conv_branch",
    )(y, w, b, lng, lnb, og)


def _mix_kernel(attn_ref, conv_ref, x_ref, ga_ref, woa_ref, woc_ref, gffn_ref, rwh_ref, rwl_ref, rb_ref,
                tri_ref, x1_ref, h2_ref, idx_ref, gate_ref, rank_ref, cnt_ref, base_ref):
    @pl.when(pl.program_id(0) == 0)
    def _():
        base_ref[...] = jnp.zeros(base_ref.shape, F32)

    a = attn_ref[...].astype(F32)
    attn_width = N_HEADS * V_DIM
    an = a * lax.rsqrt(jnp.sum(a * a, axis=-1, keepdims=True) * (1.0 / attn_width) + EPS) * ga_ref[...]
    x1 = (x_ref[...] + jnp.dot(an.astype(BF16), woa_ref[...], preferred_element_type=F32)
          + jnp.dot(conv_ref[...], woc_ref[...], preferred_element_type=F32))
    x1_ref[...] = x1
    h2 = _rms(x1, gffn_ref[...])
    _store_tile_rows(h2_ref, _pack_bf16_pairs(h2), PACKED_TILES)

    hi = h2.astype(BF16)
    lo = (h2 - hi.astype(F32)).astype(BF16)
    dn = (((1,), (1,)), ((), ()))
    rwh, rwl = rwh_ref[...], rwl_ref[...]
    logits = (lax.dot_general(rwh, hi, dn, preferred_element_type=F32)
              + lax.dot_general(rwh, lo, dn, preferred_element_type=F32)
              + lax.dot_general(rwl, hi, dn, preferred_element_type=F32)) + rb_ref[...]

    rows = logits.shape[1]
    eidx = lax.broadcasted_iota(jnp.int32, (N_EXPERTS, rows), 0).astype(F32)
    work = logits
    sels, vals = [], []
    for k in range(TOP_K):
        mx = jnp.max(work, axis=0, keepdims=True)
        first = jnp.min(jnp.where(work == mx, eidx, float(N_EXPERTS)), axis=0, keepdims=True)
        sel = eidx == first
        work = jnp.where(sel, -jnp.inf, work)
        sels.append(sel)
        vals.append(mx)
        idx_ref[k:k + 1, :] = first.astype(jnp.int32)
    exps = [jnp.exp(v - vals[0]) for v in vals]
    denom = exps[0] + exps[1] + exps[2] + exps[3]
    for k in range(TOP_K):
        gate_ref[k:k + 1, :] = exps[k] / denom

    member = jnp.where(sels[0] | sels[1] | sels[2] | sels[3], 1.0, 0.0)
    before = jnp.dot(member.astype(BF16), tri_ref[...], preferred_element_type=F32)
    posn = base_ref[:, 0:1] + before
    for k in range(TOP_K):
        rank_ref[k:k + 1, :] = jnp.sum(jnp.where(sels[k], posn, 0.0), axis=0, keepdims=True).astype(jnp.int32)
    base_ref[...] = base_ref[...] + jnp.sum(member, axis=1, keepdims=True)
    cnt_ref[...] = base_ref[...].astype(jnp.int32)


def _mix_stage(attn, conv, x2, ga, woa, woc, gffn, rwh, rwl, rb, tri):
    t = x2.shape[0]
    proj = lambda i: (i, 0)
    row = lambda w: pl.BlockSpec((MIX_ROWS, w), proj)
    col = pl.BlockSpec((TOP_K, MIX_ROWS), lambda i: (0, i))
    return pl.pallas_call(
        _mix_kernel,
        grid=(t // MIX_ROWS,),
        in_specs=[row(attn.shape[1]), row(CONV_WIDTH), row(D_MODEL), _full(ga.shape), _full(woa.shape),
                  _full(woc.shape), _full(gffn.shape), _full(rwh.shape), _full(rwl.shape), _full(rb.shape),
                  _full(tri.shape)],
        out_specs=[row(D_MODEL), pl.BlockSpec((MIX_ROWS * PACKED_TILES, LANES), proj), col, col, col,
                   _full((N_EXPERTS, LANES))],
        out_shape=[jax.ShapeDtypeStruct((t, D_MODEL), F32),
                   jax.ShapeDtypeStruct((t * PACKED_TILES, LANES), jnp.uint32),
                   jax.ShapeDtypeStruct((TOP_K, t), jnp.int32), jax.ShapeDtypeStruct((TOP_K, t), F32),
                   jax.ShapeDtypeStruct((TOP_K, t), jnp.int32),
                   jax.ShapeDtypeStruct((N_EXPERTS, LANES), jnp.int32)],
        scratch_shapes=[pltpu.VMEM((N_EXPERTS, LANES), F32)],
        compiler_params=_params("arbitrary"),
        name="mix_router",
    )(attn, conv, x2, ga, woa, woc, gffn, rwh, rwl, rb, tri)


def _dispatch_kernel(ends_ref, padded_ref, dest_ref, h_ref, xs_ref, zero_ref, sem, zsem):
    tiles = PACKED_TILES
    rows = h_ref.shape[0] // tiles
    block = MOE_ROWS * tiles
    n_blocks = xs_ref.shape[0] // block

    @pl.when(pl.program_id(0) == 0)
    def _():
        zero_ref[...] = jnp.zeros(zero_ref.shape, zero_ref.dtype)

        def zero_copy(r0):
            return pltpu.make_async_copy(
                zero_ref, xs_ref.at[pl.ds(pl.multiple_of(r0 * tiles, block), block)], zsem)

        for start in (True, False):
            for e in range(N_EXPERTS):
                @pl.when(padded_ref[e] > 0)
                def _():
                    cp = zero_copy(ends_ref[e] - MOE_ROWS)
                    cp.start() if start else cp.wait()

            def tail(b, _):
                cp = zero_copy(b * MOE_ROWS)
                cp.start() if start else cp.wait()
                return 0

            lax.fori_loop(ends_ref[N_EXPERTS - 1] // MOE_ROWS, n_blocks, tail, 0)

    def issue(t, _):
        for k in range(TOP_K):
            pltpu.make_async_copy(_tile_row(h_ref, t, tiles=tiles), _tile_row(xs_ref, dest_ref[k, t], tiles=tiles),
                                  sem).start(priority=k % DMA_PRIORITIES)
        return 0

    lax.fori_loop(0, rows, issue, 0, unroll=ISSUE_UNROLL)
    for _ in range(TOP_K):
        pltpu.make_async_copy(h_ref, xs_ref.at[pl.ds(0, rows * tiles)], sem).wait()


def _dispatch(ends, padded, dest, h2, n_blocks):
    t = h2.shape[0] // PACKED_TILES
    grid_spec = pltpu.PrefetchScalarGridSpec(
        num_scalar_prefetch=2,
        grid=(t // DISPATCH_ROWS,),
        in_specs=[pl.BlockSpec((TOP_K, DISPATCH_ROWS), lambda i, *_: (0, i), memory_space=pltpu.SMEM),
                  pl.BlockSpec((DISPATCH_ROWS * PACKED_TILES, LANES), lambda i, *_: (i, 0))],
        out_specs=pl.BlockSpec(memory_space=pl.ANY),
        scratch_shapes=[pltpu.VMEM((MOE_ROWS * PACKED_TILES, LANES), h2.dtype), pltpu.SemaphoreType.DMA(()),
                        pltpu.SemaphoreType.DMA(())],
    )
    return pl.pallas_call(
        _dispatch_kernel,
        grid_spec=grid_spec,
        out_shape=jax.ShapeDtypeStruct((n_blocks * MOE_ROWS * PACKED_TILES, LANES), h2.dtype),
        compiler_params=_params("arbitrary"),
        name="dispatch_rows",
    )(ends, padded, dest, h2)


def _moe_kernel(be_ref, nact_ref, xs_ref, wgu_ref, bgu_ref, wd_ref, bd_ref, ys_ref, wgu_bf, wd_bf):
    i = pl.program_id(0)
    active = i < nact_ref[0]

    @pl.when(jnp.logical_not(active))
    def _():
        ys_ref[...] = jnp.zeros(ys_ref.shape, F32)

    @pl.when(active & ((i == 0) | (be_ref[i] != be_ref[jnp.maximum(i - 1, 0)])))
    def _():
        wgu_bf[...] = wgu_ref[0].astype(BF16)
        wd_bf[...] = wd_ref[0].astype(BF16)

    @pl.when(active)
    def _():
        x = _unpack_bf16_pairs(_load_tile_rows(xs_ref, MOE_ROWS, tiles=PACKED_TILES))
        gu = jnp.dot(x, wgu_bf[...], preferred_element_type=F32) + bgu_ref[0]
        gate = jnp.minimum(gu[:, :D_FF], SWIGLU_LIMIT)
        up = jnp.clip(gu[:, D_FF:], -SWIGLU_LIMIT, SWIGLU_LIMIT)
        mid = (up + 1.0) * (gate * jax.nn.sigmoid(gate * SWIGLU_ALPHA))
        _store_tile_rows(ys_ref, jnp.dot(mid.astype(BF16), wd_bf[...], preferred_element_type=F32) + bd_ref[0])


def _moe(block_expert, n_active, xs, wgu, bgu, wd, bd):
    block = MOE_ROWS * ROW_TILES
    n_blocks = xs.shape[0] // (MOE_ROWS * PACKED_TILES)
    grid_spec = pltpu.PrefetchScalarGridSpec(
        num_scalar_prefetch=2,
        grid=(n_blocks,),
        in_specs=[
            pl.BlockSpec((MOE_ROWS * PACKED_TILES, LANES), lambda i, be, na: (jnp.minimum(i, na[0] - 1), 0)),
            pl.BlockSpec((1, D_MODEL, 2 * D_FF), lambda i, be, na: (be[i], 0, 0)),
            pl.BlockSpec((1, 1, 2 * D_FF), lambda i, be, na: (be[i], 0, 0)),
            pl.BlockSpec((1, D_FF, D_MODEL), lambda i, be, na: (be[i], 0, 0)),
            pl.BlockSpec((1, 1, D_MODEL), lambda i, be, na: (be[i], 0, 0)),
        ],
        out_specs=pl.BlockSpec((block, LANES), lambda i, be, na: (i, 0)),
        scratch_shapes=[pltpu.VMEM((D_MODEL, 2 * D_FF), BF16), pltpu.VMEM((D_FF, D_MODEL), BF16)],
    )
    return pl.pallas_call(
        _moe_kernel,
        grid_spec=grid_spec,
        out_shape=jax.ShapeDtypeStruct((n_blocks * block, LANES), F32),
        compiler_params=_params("arbitrary"),
        name="expert_mlp",
    )(block_expert, n_active, xs, wgu, bgu, wd, bd)


def _combine_kernel(dest_ref, dest_next_ref, x1_ref, gate_ref, ys_ref, o_ref, buf_ref, sems):
    i, n = pl.program_id(0), pl.num_programs(0)
    rows = x1_ref.shape[0]

    def gather(idx_ref, slot):
        def issue(t, _):
            for k in range(TOP_K):
                pltpu.make_async_copy(_tile_row(ys_ref, idx_ref[k, t]), _tile_row(buf_ref, t, (slot, k)),
                                      sems.at[slot]).start(priority=k % DMA_PRIORITIES)
            return 0

        lax.fori_loop(0, rows, issue, 0, unroll=ISSUE_UNROLL)

    @pl.when(i == 0)
    def _():
        gather(dest_ref, 0)

    @pl.when(i + 1 < n)
    def _():
        gather(dest_next_ref, (i + 1) % 2)

    slot = i % 2
    for k in range(TOP_K):
        pltpu.make_async_copy(ys_ref.at[pl.ds(0, rows * ROW_TILES)], buf_ref.at[slot, k], sems.at[slot]).wait()
    acc = x1_ref[...]
    for k in range(TOP_K):
        acc = acc + gate_ref[:, k:k + 1] * _load_tile_rows(buf_ref, rows, (slot, k))
    o_ref[...] = acc


def _combine(dest, x1, gate_rows, ys):
    t = x1.shape[0]
    steps = t // COMBINE_ROWS
    row = lambda w: pl.BlockSpec((COMBINE_ROWS, w), lambda i: (i, 0))
    slots = lambda nxt: pl.BlockSpec((TOP_K, COMBINE_ROWS), lambda i: (0, jnp.minimum(i + nxt, steps - 1)),
                                     memory_space=pltpu.SMEM)
    return pl.pallas_call(
        _combine_kernel,
        grid=(steps,),
        in_specs=[slots(0), slots(1), row(D_MODEL), row(TOP_K), pl.BlockSpec(memory_space=pl.ANY)],
        out_specs=row(D_MODEL),
        out_shape=jax.ShapeDtypeStruct(x1.shape, F32),
        scratch_shapes=[pltpu.VMEM((2, TOP_K, COMBINE_ROWS * ROW_TILES, LANES), F32),
                        pltpu.SemaphoreType.DMA((2,))],
        compiler_params=_params("arbitrary"),
        name="combine_rows",
    )(dest, dest, x1, gate_rows, ys)


def _head_blocks(w, width):
    r = w.shape[0]
    w = w.reshape(r, N_HEADS, width)
    return jnp.pad(w, ((0, 0), (0, 0), (0, HEAD_PAD - width))).reshape(r, N_HEADS * HEAD_PAD)


def _lane_row(v, offset=0):
    return jnp.pad(v, (offset, HEAD_PAD - offset - v.shape[0])).reshape(1, HEAD_PAD)


def _swap_rope(w, sign):
    lo, hi = w[..., NOPE_DIM:NOPE_DIM + HALF_ROPE], w[..., NOPE_DIM + HALF_ROPE:QK_DIM]
    return jnp.concatenate([jnp.zeros_like(w[..., :NOPE_DIM]), sign * hi, lo], axis=-1)


def kernel(x, positions, norm_mix_g, w_in, q_latent_g, w_uq, kv_latent_g, w_ukv, q_head_g, k_head_g, conv_dw_w, conv_dw_b, conv_ln_g, conv_ln_b, attn_out_g, conv_out_g, w_out, norm_ffn_g, router_w, router_b, w_gate_up, b_gate_up, w_down, b_down):
    batch, seq, _ = x.shape
    t = batch * seq
    depth = norm_mix_g.shape[0]
    x2 = x.reshape(t, D_MODEL)
    pos = positions.reshape(1, t)
    invf = (1.0 / (ROPE_THETA ** (jnp.arange(0, ROPE_DIM, 2, dtype=F32) / ROPE_DIM))).reshape(HALF_ROPE, 1)
    tri = jnp.triu(jnp.ones((MIX_ROWS, MIX_ROWS), BF16), 1)
    o_kv = Q_LORA
    o_pe = o_kv + KV_LORA
    o_u = o_pe + ROPE_DIM

    for l in range(depth):
        wi = w_in[l]
        w_pe = wi[:, o_pe:o_u]
        w_pe_sw = jnp.concatenate([-w_pe[:, HALF_ROPE:], w_pe[:, :HALF_ROPE]], axis=1)
        pe_block = lambda w: jnp.pad(w, ((0, 0), (NOPE_DIM, HEAD_PAD - QK_DIM)))
        win = jnp.concatenate([wi[:, :o_pe], pe_block(w_pe), pe_block(w_pe_sw), wi[:, o_u:]], axis=1).astype(BF16)
        wq = w_uq[l].reshape(Q_LORA, N_HEADS, QK_DIM)
        wuq = _head_blocks(w_uq[l], QK_DIM).astype(BF16)
        wuqs = _head_blocks(_swap_rope(wq, -1.0).reshape(Q_LORA, -1), QK_DIM).astype(BF16)
        wkv = w_ukv[l].reshape(KV_LORA, N_HEADS, NOPE_DIM + V_DIM)
        wuk = _head_blocks(wkv[:, :, :NOPE_DIM].reshape(KV_LORA, -1), NOPE_DIM).astype(BF16)
        wuv = _head_blocks(wkv[:, :, NOPE_DIM:].reshape(KV_LORA, -1), V_DIM).T.astype(BF16)
        ga = attn_out_g[l].reshape(1, -1)
        wo = w_out[l]
        woa = wo[:N_HEADS * V_DIM].astype(BF16)
        woc = wo[N_HEADS * V_DIM:].astype(BF16)
        rwt = router_w[l].T
        rwh = rwt.astype(BF16)
        rwl = (rwt - rwh.astype(F32)).astype(BF16)

        q, k, v, y = _input_stage(
            x2, pos, invf, norm_mix_g[l].reshape(1, -1), win, q_latent_g[l].reshape(1, -1), wuq, wuqs,
            kv_latent_g[l].reshape(1, -1), wuk, wuv, _lane_row(q_head_g[l]),
            _lane_row(_swap_rope(q_head_g[l], 1.0)), _lane_row(k_head_g[l]), _lane_row(_swap_rope(k_head_g[l], 1.0)))
        attn = _attention(q, k, v, batch, seq)
        conv = _conv_branch(y, conv_dw_w[l], conv_dw_b[l].reshape(1, -1), conv_ln_g[l].reshape(1, -1),
                            conv_ln_b[l].reshape(1, -1), conv_out_g[l].reshape(1, -1), batch, seq)
        x1, h2, idx, gate, rank, cnt = _mix_stage(
            attn, conv, x2, ga, woa, woc, norm_ffn_g[l].reshape(1, -1), rwh, rwl,
            router_b[l].reshape(-1, 1), tri)

        counts = cnt[:, 0]
        padded = (counts + MOE_ROWS - 1) // MOE_ROWS * MOE_ROWS
        ends = jnp.cumsum(padded)
        starts = ends - padded
        experts = jnp.arange(N_EXPERTS, dtype=jnp.int32)
        dest = rank + jnp.sum(jnp.where(idx[None] == experts[:, None, None], starts[:, None, None], 0), axis=0)
        n_blocks = (t * TOP_K + N_EXPERTS * (MOE_ROWS - 1)) // MOE_ROWS
        n_active = (ends[-1] // MOE_ROWS).astype(jnp.int32)
        blk = jnp.minimum(jnp.arange(n_blocks, dtype=jnp.int32), n_active - 1)
        be = jnp.minimum(jnp.sum((ends[None, :] <= (blk * MOE_ROWS)[:, None]).astype(jnp.int32), axis=1),
                         N_EXPERTS - 1)

        xs = _dispatch(ends.astype(jnp.int32), padded.astype(jnp.int32), dest, h2, n_blocks)
        ys = _moe(be, n_active.reshape(1), xs, w_gate_up[l], b_gate_up[l].reshape(N_EXPERTS, 1, -1),
                  w_down[l], b_down[l].reshape(N_EXPERTS, 1, -1))
        x2 = _combine(dest, x1, gate.T, ys)
    return x2.reshape(batch, seq, D_MODEL)
```

```python
import jax
import jax.numpy as jnp
from jax import lax
from jax.experimental import pallas as pl
from jax.experimental.pallas import tpu as pltpu

D_MODEL = 1024
N_HEADS = 8
NOPE_DIM = 64
ROPE_DIM = 32
QK_DIM = NOPE_DIM + ROPE_DIM
V_DIM = 64
Q_LORA = 384
KV_LORA = 128
CONV_WIDTH = 512
CONV_TAPS = 31
N_EXPERTS = 32
TOP_K = 4
D_FF = 1024
CHUNK = 64
ROPE_THETA = 10000.0
EPS = 1e-6
SWIGLU_ALPHA = 1.702
SWIGLU_LIMIT = 7.0
LOG2_E = 1.4426950408889634

LANES = 128
SUBLANES = 8
HEAD_PAD = LANES
HALF_ROPE = ROPE_DIM // 2
VMEM_LIMIT_BYTES = 56 * 1024 * 1024

IN_ROWS = 512
ATT_Q = 512
ATT_K = 512
ATT_QPART = 256
ATT_AHEAD = 4
CONV_ROWS = 256
CONV_HALO = 32
MIX_ROWS = 512
DISPATCH_ROWS = 512
MOE_ROWS = 512
COMBINE_ROWS = 512
ISSUE_UNROLL = 8
DMA_PRIORITIES = 2

F32 = jnp.float32
BF16 = jnp.bfloat16


def _params(*semantics):
    return pltpu.CompilerParams(dimension_semantics=semantics, vmem_limit_bytes=VMEM_LIMIT_BYTES)


def _rms(x, g):
    return x * lax.rsqrt(jnp.mean(x * x, axis=-1, keepdims=True) + EPS) * g


def _full(shape):
    return pl.BlockSpec(shape, lambda *_: (0,) * len(shape))


ROW_TILES = D_MODEL // LANES
PACKED_TILES = ROW_TILES // 2
assert ROW_TILES == SUBLANES


def _load_tile_rows(ref, rows, lead=(), tiles=ROW_TILES):
    return jnp.concatenate([ref[lead + (pl.ds(c, rows, stride=tiles), slice(None))] for c in range(tiles)], axis=1)


def _store_tile_rows(ref, value, tiles=ROW_TILES):
    rows = value.shape[0]
    for c in range(tiles):
        ref[pl.ds(c, rows, stride=tiles), :] = value[:, c * LANES:(c + 1) * LANES]


def _tile_row(ref, r, lead=(), tiles=ROW_TILES):
    return ref.at[lead + (pl.ds(pl.multiple_of(r * tiles, tiles), tiles),)]


def _pack_bf16_pairs(x):
    half = x.shape[1] // 2
    bits = lax.bitcast_convert_type(x.astype(BF16).astype(F32), jnp.uint32)
    return (bits[:, :half] >> 16) | bits[:, half:]


def _unpack_bf16_pairs(p):
    left = lax.bitcast_convert_type(p << 16, F32)
    right = lax.bitcast_convert_type(p & jnp.uint32(0xFFFF0000), F32)
    return jnp.concatenate([left, right], axis=1).astype(BF16)


def _in_kernel(x_ref, pos_ref, invf_ref, gmix_ref, win_ref, gql_ref, wuq_ref, wuqs_ref, gkvl_ref, wuk_ref,
               wuv_ref, gq_ref, gqs_ref, gk_ref, gks_ref, q_ref, k_ref, v_ref, y_ref):
    h = _rms(x_ref[...], gmix_ref[...]).astype(BF16)
    p = jnp.dot(h, win_ref[...], preferred_element_type=F32)
    o_kv = Q_LORA
    o_pe = o_kv + KV_LORA
    o_ps = o_pe + HEAD_PAD
    o_a = o_ps + HEAD_PAD
    o_g = o_a + CONV_WIDTH
    cq, ckv, kpe, kpe_sw = p[:, :o_kv], p[:, o_kv:o_pe], p[:, o_pe:o_ps], p[:, o_ps:o_a]
    y_ref[...] = p[:, o_a:o_g] * jax.nn.sigmoid(p[:, o_g:])

    cqn = _rms(cq, gql_ref[...]).astype(BF16)
    q = jnp.dot(cqn, wuq_ref[...], preferred_element_type=F32)
    q_sw = jnp.dot(cqn, wuqs_ref[...], preferred_element_type=F32)
    ckvn = _rms(ckv, gkvl_ref[...]).astype(BF16)
    kn = jnp.dot(ckvn, wuk_ref[...], preferred_element_type=F32)
    wide_row = lax.broadcasted_iota(jnp.int32, (N_HEADS * HEAD_PAD, 1), 0)
    ones_row = jnp.where(wide_row % HEAD_PAD == V_DIM, 1.0, 0.0)
    vt = lax.dot_general(wuv_ref[...], ckvn, (((1,), (1,)), ((), ())), preferred_element_type=F32)
    v_ref[...] = (vt + ones_row).astype(BF16)

    ang = invf_ref[...] * pos_ref[...].astype(F32)
    cos_t, sin_t = jnp.cos(ang), jnp.sin(ang)
    tab = jnp.concatenate([jnp.ones((NOPE_DIM, ang.shape[1]), F32), cos_t, cos_t, sin_t, sin_t], axis=0).T
    lane = lax.broadcasted_iota(jnp.int32, (1, HEAD_PAD), 1)
    ctab = jnp.where(lane < QK_DIM, tab, 0.0)
    stab = jnp.where((lane >= NOPE_DIM) & (lane < QK_DIM), pltpu.roll(tab, QK_DIM, 1), 0.0)
    scale = QK_DIM ** -0.5 * LOG2_E
    cq_tab = ctab * (gq_ref[...] * scale)
    sq_tab = stab * (gqs_ref[...] * scale)
    ck_tab = ctab * gk_ref[...]
    sk_tab = stab * gks_ref[...]
    k_pe = kpe * ck_tab + kpe_sw * sk_tab
    ss_pe = jnp.sum(kpe * kpe, axis=-1, keepdims=True)
    for hd in range(N_HEADS):
        sl = slice(hd * HEAD_PAD, (hd + 1) * HEAD_PAD)
        qh = q[:, sl]
        rq = lax.rsqrt(jnp.sum(qh * qh, axis=-1, keepdims=True) * (1.0 / QK_DIM) + EPS)
        q_ref[:, sl] = ((qh * cq_tab + q_sw[:, sl] * sq_tab) * rq).astype(BF16)
        kh = kn[:, sl]
        rk = lax.rsqrt((jnp.sum(kh * kh, axis=-1, keepdims=True) + ss_pe) * (1.0 / QK_DIM) + EPS)
        k_ref[:, sl] = ((kh * ck_tab + k_pe) * rk).astype(BF16)


def _input_stage(x2, pos, invf, gmix, win, gql, wuq, wuqs, gkvl, wuk, wuv, gq, gqs, gk, gks):
    t = x2.shape[0]
    wide = N_HEADS * HEAD_PAD
    row = lambda w: pl.BlockSpec((IN_ROWS, w), lambda i: (i, 0))
    consts = (invf, gmix, win, gql, wuq, wuqs, gkvl, wuk, wuv, gq, gqs, gk, gks)
    return pl.pallas_call(
        _in_kernel,
        grid=(t // IN_ROWS,),
        in_specs=[row(D_MODEL), pl.BlockSpec((1, IN_ROWS), lambda i: (0, i))] + [_full(a.shape) for a in consts],
        out_specs=[row(wide), row(wide), pl.BlockSpec((wide, IN_ROWS), lambda i: (0, i)), row(CONV_WIDTH)],
        out_shape=[jax.ShapeDtypeStruct((t, wide), BF16), jax.ShapeDtypeStruct((t, wide), BF16),
                   jax.ShapeDtypeStruct((wide, t), BF16), jax.ShapeDtypeStruct((t, CONV_WIDTH), F32)],
        compiler_params=_params("parallel"),
        name="input_stage",
    )(x2, pos, *consts)


def _attn_kernel(q_ref, k_ref, vt_ref, o_ref, *scratch):
    units = [(hd, part) for hd in range(N_HEADS) for part in range(ATT_Q // ATT_QPART)]
    m_refs, acc_refs = scratch[:len(units)], scratch[len(units):]
    i = pl.program_id(1)
    dn = (((1,), (1,)), ((), ()))

    def step(r0, diagonal):
        old = None if diagonal else [(m_refs[u][...], acc_refs[u][...]) for u in range(len(units))]
        new = []

        def keys(part):
            return (part + 1) * ATT_QPART if diagonal else ATT_K

        def scores(u):
            hd, part = units[u]
            sl = slice(hd * HEAD_PAD, (hd + 1) * HEAD_PAD)
            return lax.dot_general(k_ref[pl.ds(r0, keys(part)), sl],
                                   q_ref[part * ATT_QPART:(part + 1) * ATT_QPART, sl], dn,
                                   preferred_element_type=F32)

        ahead = [scores(u) for u in range(ATT_AHEAD)]
        for u, (hd, part) in enumerate(units):
            sl = slice(hd * HEAD_PAD, (hd + 1) * HEAD_PAD)
            s = ahead.pop(0)
            if u + ATT_AHEAD < len(units):
                ahead.append(scores(u + ATT_AHEAD))
            if diagonal:
                kc = lax.broadcasted_iota(jnp.int32, s.shape, 0) // CHUNK
                qc = (lax.broadcasted_iota(jnp.int32, s.shape, 1) + part * ATT_QPART) // CHUNK
                s = jnp.where(kc <= qc, s, -jnp.inf)
            m_new = jnp.max(s, axis=0, keepdims=True)
            if not diagonal:
                m_new = jnp.maximum(old[u][0], m_new)
            acc = jnp.dot(vt_ref[sl, pl.ds(r0, keys(part))], jnp.exp2(s - m_new).astype(BF16),
                          preferred_element_type=F32)
            if not diagonal:
                acc = jnp.exp2(old[u][0] - m_new) * old[u][1] + acc
            new.append((m_new, acc))
        for u in range(len(units)):
            m_refs[u][...], acc_refs[u][...] = new[u]

    step(pl.multiple_of(i * ATT_Q, ATT_Q), True)

    def body(j, _):
        step(pl.multiple_of(j * ATT_K, ATT_K), False)
        return 0

    lax.fori_loop(0, i, body, 0)
    parts = ATT_Q // ATT_QPART
    for pair in range(N_HEADS // 2):
        for part in range(parts):
            halves = []
            for hd in (2 * pair, 2 * pair + 1):
                acc = acc_refs[hd * parts + part][...]
                halves.append(acc[:V_DIM, :] / acc[V_DIM:V_DIM + 1, :])
            o_ref[part * ATT_QPART:(part + 1) * ATT_QPART, pair * LANES:(pair + 1) * LANES] = (
                jnp.concatenate(halves, axis=0).T.astype(BF16))


def _attention(q, k, vt, batch, seq):
    assert ATT_Q == ATT_K and seq % ATT_Q == 0 and ATT_Q % ATT_QPART == 0 and ATT_QPART % CHUNK == 0
    nq = seq // ATT_Q
    chains = N_HEADS * (ATT_Q // ATT_QPART)
    wide = N_HEADS * HEAD_PAD
    qspec = pl.BlockSpec((ATT_Q, wide), lambda b, i: (b * nq + i, 0))
    return pl.pallas_call(
        _attn_kernel,
        grid=(batch, nq),
        in_specs=[qspec, pl.BlockSpec((seq, wide), lambda b, i: (b, 0)),
                  pl.BlockSpec((wide, seq), lambda b, i: (0, b))],
        out_specs=pl.BlockSpec((ATT_Q, N_HEADS * V_DIM), lambda b, i: (b * nq + i, 0)),
        out_shape=jax.ShapeDtypeStruct((q.shape[0], N_HEADS * V_DIM), BF16),
        scratch_shapes=([pltpu.VMEM((1, ATT_QPART), F32)] * chains + [pltpu.VMEM((HEAD_PAD, ATT_QPART), F32)] * chains),
        compiler_params=_params("parallel", "arbitrary"),
        name="attention",
    )(q, k, vt)


def _conv_kernel(y_ref, w_ref, b_ref, lng_ref, lnb_ref, og_ref, o_ref, pad_ref, phase_ref):
    seq = y_ref.shape[0]
    pad_ref[0:CONV_HALO, :] = jnp.zeros((CONV_HALO, CONV_WIDTH), F32)
    pad_ref[CONV_HALO:, :] = y_ref[...]
    first = CONV_HALO - (CONV_TAPS - 1)

    def body(i, _):
        r0 = pl.multiple_of(i * CONV_ROWS, CONV_ROWS)
        win = pad_ref[pl.ds(r0, CONV_ROWS + CONV_HALO), :]
        acc = jnp.zeros((CONV_ROWS, CONV_WIDTH), F32)
        for s in range(SUBLANES):
            offs = [o for o in range(first, first + CONV_TAPS) if o % SUBLANES == s]
            span = max(offs) - s + CONV_ROWS
            if s:
                phase_ref[s, 0:span, :] = win[s:s + span, :]
            for o in offs:
                rows = (pad_ref[pl.ds(pl.multiple_of(r0 + o, SUBLANES), CONV_ROWS), :] if s == 0
                        else phase_ref[s, o - s:o - s + CONV_ROWS, :])
                acc = acc + w_ref[o - first:o - first + 1, :] * rows
        acc = acc + b_ref[...]
        xc = acc - jnp.mean(acc, axis=-1, keepdims=True)
        ln = xc * lax.rsqrt(jnp.mean(xc * xc, axis=-1, keepdims=True) + EPS) * lng_ref[...] + lnb_ref[...]
        z = ln * jax.nn.sigmoid(ln)
        o_ref[pl.ds(r0, CONV_ROWS), :] = _rms(z, og_ref[...]).astype(BF16)
        return 0

    lax.fori_loop(0, seq // CONV_ROWS, body, 0)


def _conv_branch(y, w, b, lng, lnb, og, batch, seq):
    spec = pl.BlockSpec((seq, CONV_WIDTH), lambda bi: (bi, 0))
    return pl.pallas_call(
        _conv_kernel,
        grid=(batch,),
        in_specs=[spec, _full(w.shape), _full(b.shape), _full(lng.shape), _full(lnb.shape), _full(og.shape)],
        out_specs=spec,
        out_shape=jax.ShapeDtypeStruct(y.shape, BF16),
        scratch_shapes=[pltpu.VMEM((seq + CONV_HALO, CONV_WIDTH), F32),
                        pltpu.VMEM((SUBLANES, CONV_ROWS + CONV_HALO, CONV_WIDTH), F32)],
        compiler_params=_params("parallel"),
        name="conv_branch",
    )(y, w, b, lng, lnb, og)


def _mix_kernel(attn_ref, conv_ref, x_ref, ga_ref, woa_ref, woc_ref, gffn_ref, rwh_ref, rwl_ref, rb_ref,
                tri_ref, x1_ref, h2_ref, idx_ref, gate_ref, rank_ref, cnt_ref, base_ref):
    @pl.when(pl.program_id(0) == 0)
    def _():
        base_ref[...] = jnp.zeros(base_ref.shape, F32)

    an = _rms(attn_ref[...].astype(F32), ga_ref[...])
    x1 = (x_ref[...] + jnp.dot(an.astype(BF16), woa_ref[...], preferred_element_type=F32)
          + jnp.dot(conv_ref[...], woc_ref[...], preferred_element_type=F32))
    x1_ref[...] = x1
    h2 = _rms(x1, gffn_ref[...])
    _store_tile_rows(h2_ref, _pack_bf16_pairs(h2), PACKED_TILES)

    hi = h2.astype(BF16)
    lo = (h2 - hi.astype(F32)).astype(BF16)
    dn = (((1,), (1,)), ((), ()))
    rwh, rwl = rwh_ref[...], rwl_ref[...]
    logits = (lax.dot_general(rwh, hi, dn, preferred_element_type=F32)
              + lax.dot_general(rwh, lo, dn, preferred_element_type=F32)
              + lax.dot_general(rwl, hi, dn, preferred_element_type=F32)) + rb_ref[...]

    rows = logits.shape[1]
    eidx = lax.broadcasted_iota(jnp.int32, (N_EXPERTS, rows), 0).astype(F32)
    work = logits
    sels, vals = [], []
    for k in range(TOP_K):
        mx = jnp.max(work, axis=0, keepdims=True)
        first = jnp.min(jnp.where(work == mx, eidx, float(N_EXPERTS)), axis=0, keepdims=True)
        sel = eidx == first
        work = jnp.where(sel, -jnp.inf, work)
        sels.append(sel)
        vals.append(mx)
        idx_ref[k:k + 1, :] = first.astype(jnp.int32)
    exps = [jnp.exp(v - vals[0]) for v in vals]
    denom = exps[0] + exps[1] + exps[2] + exps[3]
    for k in range(TOP_K):
        gate_ref[k:k + 1, :] = exps[k] / denom

    member = jnp.where(sels[0] | sels[1] | sels[2] | sels[3], 1.0, 0.0)
    before = jnp.dot(member.astype(BF16), tri_ref[...], preferred_element_type=F32)
    posn = base_ref[:, 0:1] + before
    for k in range(TOP_K):
        rank_ref[k:k + 1, :] = jnp.sum(jnp.where(sels[k], posn, 0.0), axis=0, keepdims=True).astype(jnp.int32)
    base_ref[...] = base_ref[...] + jnp.sum(member, axis=1, keepdims=True)
    cnt_ref[...] = base_ref[...].astype(jnp.int32)


def _mix_stage(attn, conv, x2, ga, woa, woc, gffn, rwh, rwl, rb, tri):
    t = x2.shape[0]
    proj = lambda i: (i, 0)
    row = lambda w: pl.BlockSpec((MIX_ROWS, w), proj)
    col = pl.BlockSpec((TOP_K, MIX_ROWS), lambda i: (0, i))
    return pl.pallas_call(
        _mix_kernel,
        grid=(t // MIX_ROWS,),
        in_specs=[row(attn.shape[1]), row(CONV_WIDTH), row(D_MODEL), _full(ga.shape), _full(woa.shape),
                  _full(woc.shape), _full(gffn.shape), _full(rwh.shape), _full(rwl.shape), _full(rb.shape),
                  _full(tri.shape)],
        out_specs=[row(D_MODEL), pl.BlockSpec((MIX_ROWS * PACKED_TILES, LANES), proj), col, col, col,
                   _full((N_EXPERTS, LANES))],
        out_shape=[jax.ShapeDtypeStruct((t, D_MODEL), F32),
                   jax.ShapeDtypeStruct((t * PACKED_TILES, LANES), jnp.uint32),
                   jax.ShapeDtypeStruct((TOP_K, t), jnp.int32), jax.ShapeDtypeStruct((TOP_K, t), F32),
                   jax.ShapeDtypeStruct((TOP_K, t), jnp.int32),
                   jax.ShapeDtypeStruct((N_EXPERTS, LANES), jnp.int32)],
        scratch_shapes=[pltpu.VMEM((N_EXPERTS, LANES), F32)],
        compiler_params=_params("arbitrary"),
        name="mix_router",
    )(attn, conv, x2, ga, woa, woc, gffn, rwh, rwl, rb, tri)


def _dispatch_kernel(ends_ref, padded_ref, dest_ref, h_ref, xs_ref, zero_ref, sem, zsem):
    tiles = PACKED_TILES
    rows = h_ref.shape[0] // tiles
    block = MOE_ROWS * tiles
    n_blocks = xs_ref.shape[0] // block

    @pl.when(pl.program_id(0) == 0)
    def _():
        zero_ref[...] = jnp.zeros(zero_ref.shape, zero_ref.dtype)

        def zero_copy(r0):
            return pltpu.make_async_copy(
                zero_ref, xs_ref.at[pl.ds(pl.multiple_of(r0 * tiles, block), block)], zsem)

        for start in (True, False):
            for e in range(N_EXPERTS):
                @pl.when(padded_ref[e] > 0)
                def _():
                    cp = zero_copy(ends_ref[e] - MOE_ROWS)
                    cp.start() if start else cp.wait()

            def tail(b, _):
                cp = zero_copy(b * MOE_ROWS)
                cp.start() if start else cp.wait()
                return 0

            lax.fori_loop(ends_ref[N_EXPERTS - 1] // MOE_ROWS, n_blocks, tail, 0)

    def issue(t, _):
        for k in range(TOP_K):
            pltpu.make_async_copy(_tile_row(h_ref, t, tiles=tiles), _tile_row(xs_ref, dest_ref[k, t], tiles=tiles),
                                  sem).start(priority=k % DMA_PRIORITIES)
        return 0

    lax.fori_loop(0, rows, issue, 0, unroll=ISSUE_UNROLL)
    for _ in range(TOP_K):
        pltpu.make_async_copy(h_ref, xs_ref.at[pl.ds(0, rows * tiles)], sem).wait()


def _dispatch(ends, padded, dest, h2, n_blocks):
    t = h2.shape[0] // PACKED_TILES
    grid_spec = pltpu.PrefetchScalarGridSpec(
        num_scalar_prefetch=2,
        grid=(t // DISPATCH_ROWS,),
        in_specs=[pl.BlockSpec((TOP_K, DISPATCH_ROWS), lambda i, *_: (0, i), memory_space=pltpu.SMEM),
                  pl.BlockSpec((DISPATCH_ROWS * PACKED_TILES, LANES), lambda i, *_: (i, 0))],
        out_specs=pl.BlockSpec(memory_space=pl.ANY),
        scratch_shapes=[pltpu.VMEM((MOE_ROWS * PACKED_TILES, LANES), h2.dtype), pltpu.SemaphoreType.DMA(()),
                        pltpu.SemaphoreType.DMA(())],
    )
    return pl.pallas_call(
        _dispatch_kernel,
        grid_spec=grid_spec,
        out_shape=jax.ShapeDtypeStruct((n_blocks * MOE_ROWS * PACKED_TILES, LANES), h2.dtype),
        compiler_params=_params("arbitrary"),
        name="dispatch_rows",
    )(ends, padded, dest, h2)


def _moe_kernel(be_ref, nact_ref, xs_ref, wgu_ref, bgu_ref, wd_ref, bd_ref, ys_ref, wgu_bf, wd_bf):
    i = pl.program_id(0)
    active = i < nact_ref[0]

    @pl.when(jnp.logical_not(active))
    def _():
        ys_ref[...] = jnp.zeros(ys_ref.shape, F32)

    @pl.when(active & ((i == 0) | (be_ref[i] != be_ref[jnp.maximum(i - 1, 0)])))
    def _():
        wgu_bf[...] = wgu_ref[0].astype(BF16)
        wd_bf[...] = wd_ref[0].astype(BF16)

    @pl.when(active)
    def _():
        x = _unpack_bf16_pairs(_load_tile_rows(xs_ref, MOE_ROWS, tiles=PACKED_TILES))
        gu = jnp.dot(x, wgu_bf[...], preferred_element_type=F32) + bgu_ref[0]
        gate = jnp.minimum(gu[:, :D_FF], SWIGLU_LIMIT)
        up = jnp.clip(gu[:, D_FF:], -SWIGLU_LIMIT, SWIGLU_LIMIT)
        mid = (up + 1.0) * (gate * jax.nn.sigmoid(gate * SWIGLU_ALPHA))
        _store_tile_rows(ys_ref, jnp.dot(mid.astype(BF16), wd_bf[...], preferred_element_type=F32) + bd_ref[0])


def _moe(block_expert, n_active, xs, wgu, bgu, wd, bd):
    block = MOE_ROWS * ROW_TILES
    n_blocks = xs.shape[0] // (MOE_ROWS * PACKED_TILES)
    grid_spec = pltpu.PrefetchScalarGridSpec(
        num_scalar_prefetch=2,
        grid=(n_blocks,),
        in_specs=[
            pl.BlockSpec((MOE_ROWS * PACKED_TILES, LANES), lambda i, be, na: (jnp.minimum(i, na[0] - 1), 0)),
            pl.BlockSpec((1, D_MODEL, 2 * D_FF), lambda i, be, na: (be[i], 0, 0)),
            pl.BlockSpec((1, 1, 2 * D_FF), lambda i, be, na: (be[i], 0, 0)),
            pl.BlockSpec((1, D_FF, D_MODEL), lambda i, be, na: (be[i], 0, 0)),
            pl.BlockSpec((1, 1, D_MODEL), lambda i, be, na: (be[i], 0, 0)),
        ],
        out_specs=pl.BlockSpec((block, LANES), lambda i, be, na: (i, 0)),
        scratch_shapes=[pltpu.VMEM((D_MODEL, 2 * D_FF), BF16), pltpu.VMEM((D_FF, D_MODEL), BF16)],
    )
    return pl.pallas_call(
        _moe_kernel,
        grid_spec=grid_spec,
        out_shape=jax.ShapeDtypeStruct((n_blocks * block, LANES), F32),
        compiler_params=_params("arbitrary"),
        name="expert_mlp",
    )(block_expert, n_active, xs, wgu, bgu, wd, bd)


def _combine_kernel(dest_ref, dest_next_ref, x1_ref, gate_ref, ys_ref, o_ref, buf_ref, sems):
    i, n = pl.program_id(0), pl.num_programs(0)
    rows = x1_ref.shape[0]

    def gather(idx_ref, slot):
        def issue(t, _):
            for k in range(TOP_K):
                pltpu.make_async_copy(_tile_row(ys_ref, idx_ref[k * rows + t]), _tile_row(buf_ref, t, (slot, k)),
                                      sems.at[slot]).start(priority=k % DMA_PRIORITIES)
            return 0

        lax.fori_loop(0, rows, issue, 0, unroll=ISSUE_UNROLL)

    @pl.when(i == 0)
    def _():
        gather(dest_ref, 0)

    @pl.when(i + 1 < n)
    def _():
        gather(dest_next_ref, (i + 1) % 2)

    slot = i % 2
    for k in range(TOP_K):
        pltpu.make_async_copy(ys_ref.at[pl.ds(0, rows * ROW_TILES)], buf_ref.at[slot, k], sems.at[slot]).wait()
    acc = x1_ref[...]
    for k in range(TOP_K):
        acc = acc + gate_ref[:, k:k + 1] * _load_tile_rows(buf_ref, rows, (slot, k))
    o_ref[...] = acc


def _combine(dest, x1, gate_rows, ys):
    t = x1.shape[0]
    steps = t // COMBINE_ROWS
    row = lambda w: pl.BlockSpec((COMBINE_ROWS, w), lambda i: (i, 0))
    dest = dest.reshape(TOP_K, steps, COMBINE_ROWS).transpose(1, 0, 2).reshape(-1)
    slots = lambda nxt: pl.BlockSpec((TOP_K * COMBINE_ROWS,), lambda i: (jnp.minimum(i + nxt, steps - 1),),
                                     memory_space=pltpu.SMEM)
    return pl.pallas_call(
        _combine_kernel,
        grid=(steps,),
        in_specs=[slots(0), slots(1), row(D_MODEL), row(TOP_K), pl.BlockSpec(memory_space=pl.ANY)],
        out_specs=row(D_MODEL),
        out_shape=jax.ShapeDtypeStruct(x1.shape, F32),
        scratch_shapes=[pltpu.VMEM((2, TOP_K, COMBINE_ROWS * ROW_TILES, LANES), F32),
                        pltpu.SemaphoreType.DMA((2,))],
        compiler_params=_params("arbitrary"),
        name="combine_rows",
    )(dest, dest, x1, gate_rows, ys)


def _head_blocks(w, width):
    r = w.shape[0]
    w = w.reshape(r, N_HEADS, width)
    return jnp.pad(w, ((0, 0), (0, 0), (0, HEAD_PAD - width))).reshape(r, N_HEADS * HEAD_PAD)


def _lane_row(v, offset=0):
    return jnp.pad(v, (offset, HEAD_PAD - offset - v.shape[0])).reshape(1, HEAD_PAD)


def _swap_rope(w, sign):
    lo, hi = w[..., NOPE_DIM:NOPE_DIM + HALF_ROPE], w[..., NOPE_DIM + HALF_ROPE:QK_DIM]
    return jnp.concatenate([jnp.zeros_like(w[..., :NOPE_DIM]), sign * hi, lo], axis=-1)


def kernel(x, positions, norm_mix_g, w_in, q_latent_g, w_uq, kv_latent_g, w_ukv, q_head_g, k_head_g, conv_dw_w, conv_dw_b, conv_ln_g, conv_ln_b, attn_out_g, conv_out_g, w_out, norm_ffn_g, router_w, router_b, w_gate_up, b_gate_up, w_down, b_down):
    batch, seq, d_model = x.shape
    t = batch * seq
    assert d_model == D_MODEL and seq % max(ATT_Q, CONV_ROWS) == 0
    assert all(t % rows == 0 for rows in (IN_ROWS, MIX_ROWS, DISPATCH_ROWS, COMBINE_ROWS))
    depth = norm_mix_g.shape[0]
    x2 = x.reshape(t, D_MODEL)
    pos = positions.reshape(1, t)
    invf = (1.0 / (ROPE_THETA ** (jnp.arange(0, ROPE_DIM, 2, dtype=F32) / ROPE_DIM))).reshape(HALF_ROPE, 1)
    tri = jnp.triu(jnp.ones((MIX_ROWS, MIX_ROWS), BF16), 1)
    o_kv = Q_LORA
    o_pe = o_kv + KV_LORA
    o_u = o_pe + ROPE_DIM

    for l in range(depth):
        wi = w_in[l]
        w_pe = wi[:, o_pe:o_u]
        w_pe_sw = jnp.concatenate([-w_pe[:, HALF_ROPE:], w_pe[:, :HALF_ROPE]], axis=1)
        pe_block = lambda w: jnp.pad(w, ((0, 0), (NOPE_DIM, HEAD_PAD - QK_DIM)))
        win = jnp.concatenate([wi[:, :o_pe], pe_block(w_pe), pe_block(w_pe_sw), wi[:, o_u:]], axis=1).astype(BF16)
        wq = w_uq[l].reshape(Q_LORA, N_HEADS, QK_DIM)
        wuq = _head_blocks(w_uq[l], QK_DIM).astype(BF16)
        wuqs = _head_blocks(_swap_rope(wq, -1.0).reshape(Q_LORA, -1), QK_DIM).astype(BF16)
        wkv = w_ukv[l].reshape(KV_LORA, N_HEADS, NOPE_DIM + V_DIM)
        wuk = _head_blocks(wkv[:, :, :NOPE_DIM].reshape(KV_LORA, -1), NOPE_DIM).astype(BF16)
        wuv = _head_blocks(wkv[:, :, NOPE_DIM:].reshape(KV_LORA, -1), V_DIM).T.astype(BF16)
        ga = attn_out_g[l].reshape(1, -1)
        wo = w_out[l]
        woa = wo[:N_HEADS * V_DIM].astype(BF16)
        woc = wo[N_HEADS * V_DIM:].astype(BF16)
        rwt = router_w[l].T
        rwh = rwt.astype(BF16)
        rwl = (rwt - rwh.astype(F32)).astype(BF16)

        q, k, v, y = _input_stage(
            x2, pos, invf, norm_mix_g[l].reshape(1, -1), win, q_latent_g[l].reshape(1, -1), wuq, wuqs,
            kv_latent_g[l].reshape(1, -1), wuk, wuv, _lane_row(q_head_g[l]),
            _lane_row(_swap_rope(q_head_g[l], 1.0)), _lane_row(k_head_g[l]), _lane_row(_swap_rope(k_head_g[l], 1.0)))
        attn = _attention(q, k, v, batch, seq)
        conv = _conv_branch(y, conv_dw_w[l], conv_dw_b[l].reshape(1, -1), conv_ln_g[l].reshape(1, -1),
                            conv_ln_b[l].reshape(1, -1), conv_out_g[l].reshape(1, -1), batch, seq)
        x1, h2, idx, gate, rank, cnt = _mix_stage(
            attn, conv, x2, ga, woa, woc, norm_ffn_g[l].reshape(1, -1), rwh, rwl,
            router_b[l].reshape(-1, 1), tri)

        counts = cnt[:, 0]
        padded = (counts + MOE_ROWS - 1) // MOE_ROWS * MOE_ROWS
        ends = jnp.cumsum(padded)
        starts = ends - padded
        experts = jnp.arange(N_EXPERTS, dtype=jnp.int32)
        dest = rank + jnp.sum(jnp.where(idx[None] == experts[:, None, None], starts[:, None, None], 0), axis=0)
        n_blocks = (t * TOP_K + N_EXPERTS * (MOE_ROWS - 1)) // MOE_ROWS
        n_active = (ends[-1] // MOE_ROWS).astype(jnp.int32)
        blk = jnp.minimum(jnp.arange(n_blocks, dtype=jnp.int32), n_active - 1)
        be = jnp.minimum(jnp.sum((ends[None, :] <= (blk * MOE_ROWS)[:, None]).astype(jnp.int32), axis=1),
                         N_EXPERTS - 1)

        xs = _dispatch(ends.astype(jnp.int32), padded.astype(jnp.int32), dest, h2, n_blocks)
        ys = _moe(be, n_active.reshape(1), xs, w_gate_up[l], b_gate_up[l].reshape(N_EXPERTS, 1, -1),
                  w_down[l], b_down[l].reshape(N_EXPERTS, 1, -1))
        x2 = _combine(dest, x1, gate.T, ys)
    return x2.reshape(batch, seq, D_MODEL)
```

```python
import jax
import jax.numpy as jnp
from jax import lax
from jax.experimental import pallas as pl
from jax.experimental.pallas import tpu as pltpu

D_MODEL = 1024
N_HEADS = 8
NOPE_DIM = 64
ROPE_DIM = 32
QK_DIM = NOPE_DIM + ROPE_DIM
V_DIM = 64
Q_LORA = 384
KV_LORA = 128
CONV_WIDTH = 512
CONV_TAPS = 31
N_EXPERTS = 32
TOP_K = 4
D_FF = 1024
CHUNK = 64
ROPE_THETA = 10000.0
EPS = 1e-6
SWIGLU_ALPHA = 1.702
SWIGLU_LIMIT = 7.0
LOG2_E = 1.4426950408889634

LANES = 128
SUBLANES = 8
HEAD_PAD = LANES
HALF_ROPE = ROPE_DIM // 2
VMEM_LIMIT_BYTES = 56 * 1024 * 1024

IN_ROWS = 1024
ATT_Q = 512
ATT_K = 512
ATT_QPART = 256
ATT_AHEAD = 4
CONV_ROWS = 256
CONV_HALO = 32
MIX_ROWS = 1024
DISPATCH_ROWS = 512
MOE_ROWS = 512
MOE_STEP = 2 * MOE_ROWS
COMBINE_ROWS = 512
ISSUE_UNROLL = 8
DMA_PRIORITIES = 2

F32 = jnp.float32
BF16 = jnp.bfloat16


def _params(*semantics):
    return pltpu.CompilerParams(dimension_semantics=semantics, vmem_limit_bytes=VMEM_LIMIT_BYTES)


def _rms(x, g):
    return x * lax.rsqrt(jnp.mean(x * x, axis=-1, keepdims=True) + EPS) * g


def _full(shape):
    return pl.BlockSpec(shape, lambda *_: (0,) * len(shape))


ROW_TILES = D_MODEL // LANES
PACKED_TILES = ROW_TILES // 2
assert ROW_TILES == SUBLANES


def _load_tile_rows(ref, rows, lead=(), tiles=ROW_TILES, first=0):
    return jnp.concatenate(
        [ref[lead + (pl.ds(first * tiles + c, rows, stride=tiles), slice(None))] for c in range(tiles)], axis=1)


def _store_tile_rows(ref, value, tiles=ROW_TILES, first=0):
    rows = value.shape[0]
    for c in range(tiles):
        ref[pl.ds(first * tiles + c, rows, stride=tiles), :] = value[:, c * LANES:(c + 1) * LANES]


def _tile_row(ref, r, lead=(), tiles=ROW_TILES):
    return ref.at[lead + (pl.ds(pl.multiple_of(r * tiles, tiles), tiles),)]


def _pack_bf16_pairs(x):
    half = x.shape[1] // 2
    bits = lax.bitcast_convert_type(x.astype(BF16).astype(F32), jnp.uint32)
    return (bits[:, :half] >> 16) | bits[:, half:]


def _unpack_bf16_pairs(p):
    left = lax.bitcast_convert_type(p << 16, F32)
    right = lax.bitcast_convert_type(p & jnp.uint32(0xFFFF0000), F32)
    return jnp.concatenate([left, right], axis=1).astype(BF16)


def _in_kernel(x_ref, pos_ref, invf_ref, gmix_ref, win_ref, gql_ref, wuq_ref, wuqs_ref, gkvl_ref, wuk_ref,
               wuv_ref, gq_ref, gqs_ref, gk_ref, gks_ref, q_ref, k_ref, v_ref, y_ref):
    h = _rms(x_ref[...], gmix_ref[...]).astype(BF16)
    p = jnp.dot(h, win_ref[...], preferred_element_type=F32)
    o_kv = Q_LORA
    o_pe = o_kv + KV_LORA
    o_ps = o_pe + HEAD_PAD
    o_a = o_ps + HEAD_PAD
    o_g = o_a + CONV_WIDTH
    cq, ckv, kpe, kpe_sw = p[:, :o_kv], p[:, o_kv:o_pe], p[:, o_pe:o_ps], p[:, o_ps:o_a]
    y_ref[...] = p[:, o_a:o_g] * jax.nn.sigmoid(p[:, o_g:])

    cqn = _rms(cq, gql_ref[...]).astype(BF16)
    q = jnp.dot(cqn, wuq_ref[...], preferred_element_type=F32)
    q_sw = jnp.dot(cqn, wuqs_ref[...], preferred_element_type=F32)
    ckvn = _rms(ckv, gkvl_ref[...]).astype(BF16)
    kn = jnp.dot(ckvn, wuk_ref[...], preferred_element_type=F32)
    wide_row = lax.broadcasted_iota(jnp.int32, (N_HEADS * HEAD_PAD, 1), 0)
    ones_row = jnp.where(wide_row % HEAD_PAD == V_DIM, 1.0, 0.0)
    vt = lax.dot_general(wuv_ref[...], ckvn, (((1,), (1,)), ((), ())), preferred_element_type=F32)
    v_ref[...] = (vt + ones_row).astype(BF16)

    ang = invf_ref[...] * pos_ref[...].astype(F32)
    cos_t, sin_t = jnp.cos(ang), jnp.sin(ang)
    tab = jnp.concatenate([jnp.ones((NOPE_DIM, ang.shape[1]), F32), cos_t, cos_t, sin_t, sin_t], axis=0).T
    lane = lax.broadcasted_iota(jnp.int32, (1, HEAD_PAD), 1)
    ctab = jnp.where(lane < QK_DIM, tab, 0.0)
    stab = jnp.where((lane >= NOPE_DIM) & (lane < QK_DIM), pltpu.roll(tab, QK_DIM, 1), 0.0)
    scale = QK_DIM ** -0.5 * LOG2_E
    cq_tab = ctab * (gq_ref[...] * scale)
    sq_tab = stab * (gqs_ref[...] * scale)
    ck_tab = ctab * gk_ref[...]
    sk_tab = stab * gks_ref[...]
    k_pe = kpe * ck_tab + kpe_sw * sk_tab
    ss_pe = jnp.sum(kpe * kpe, axis=-1, keepdims=True)
    for hd in range(N_HEADS):
        sl = slice(hd * HEAD_PAD, (hd + 1) * HEAD_PAD)
        qh = q[:, sl]
        rq = lax.rsqrt(jnp.sum(qh * qh, axis=-1, keepdims=True) * (1.0 / QK_DIM) + EPS)
        q_ref[:, sl] = ((qh * cq_tab + q_sw[:, sl] * sq_tab) * rq).astype(BF16)
        kh = kn[:, sl]
        rk = lax.rsqrt((jnp.sum(kh * kh, axis=-1, keepdims=True) + ss_pe) * (1.0 / QK_DIM) + EPS)
        k_ref[:, sl] = ((kh * ck_tab + k_pe) * rk).astype(BF16)


def _input_stage(x2, pos, invf, gmix, win, gql, wuq, wuqs, gkvl, wuk, wuv, gq, gqs, gk, gks):
    t = x2.shape[0]
    wide = N_HEADS * HEAD_PAD
    row = lambda w: pl.BlockSpec((IN_ROWS, w), lambda i: (i, 0))
    consts = (invf, gmix, win, gql, wuq, wuqs, gkvl, wuk, wuv, gq, gqs, gk, gks)
    return pl.pallas_call(
        _in_kernel,
        grid=(t // IN_ROWS,),
        in_specs=[row(D_MODEL), pl.BlockSpec((1, IN_ROWS), lambda i: (0, i))] + [_full(a.shape) for a in consts],
        out_specs=[row(wide), row(wide), pl.BlockSpec((wide, IN_ROWS), lambda i: (0, i)), row(CONV_WIDTH)],
        out_shape=[jax.ShapeDtypeStruct((t, wide), BF16), jax.ShapeDtypeStruct((t, wide), BF16),
                   jax.ShapeDtypeStruct((wide, t), BF16), jax.ShapeDtypeStruct((t, CONV_WIDTH), F32)],
        compiler_params=_params("parallel"),
        name="input_stage",
    )(x2, pos, *consts)


def _attn_kernel(q_ref, k_ref, vt_ref, o_ref, *scratch):
    units = [(hd, part) for hd in range(N_HEADS) for part in range(ATT_Q // ATT_QPART)]
    m_refs, acc_refs = scratch[:len(units)], scratch[len(units):]
    i = pl.program_id(1)
    dn = (((1,), (1,)), ((), ()))

    def step(r0, diagonal):
        old = None if diagonal else [(m_refs[u][...], acc_refs[u][...]) for u in range(len(units))]
        new = []

        def keys(part):
            return (part + 1) * ATT_QPART if diagonal else ATT_K

        def scores(u):
            hd, part = units[u]
            sl = slice(hd * HEAD_PAD, (hd + 1) * HEAD_PAD)
            return lax.dot_general(k_ref[pl.ds(r0, keys(part)), sl],
                                   q_ref[part * ATT_QPART:(part + 1) * ATT_QPART, sl], dn,
                                   preferred_element_type=F32)

        ahead = [scores(u) for u in range(ATT_AHEAD)]
        for u, (hd, part) in enumerate(units):
            sl = slice(hd * HEAD_PAD, (hd + 1) * HEAD_PAD)
            s = ahead.pop(0)
            if u + ATT_AHEAD < len(units):
                ahead.append(scores(u + ATT_AHEAD))
            if diagonal:
                kc = lax.broadcasted_iota(jnp.int32, s.shape, 0) // CHUNK
                qc = (lax.broadcasted_iota(jnp.int32, s.shape, 1) + part * ATT_QPART) // CHUNK
                s = jnp.where(kc <= qc, s, -jnp.inf)
            m_new = jnp.max(s, axis=0, keepdims=True)
            if not diagonal:
                m_new = jnp.maximum(old[u][0], m_new)
            acc = jnp.dot(vt_ref[sl, pl.ds(r0, keys(part))], jnp.exp2(s - m_new).astype(BF16),
                          preferred_element_type=F32)
            if not diagonal:
                acc = jnp.exp2(old[u][0] - m_new) * old[u][1] + acc
            new.append((m_new, acc))
        for u in range(len(units)):
            m_refs[u][...], acc_refs[u][...] = new[u]

    step(pl.multiple_of(i * ATT_Q, ATT_Q), True)

    def body(j, _):
        step(pl.multiple_of(j * ATT_K, ATT_K), False)
        return 0

    lax.fori_loop(0, i, body, 0)
    parts = ATT_Q // ATT_QPART
    for pair in range(N_HEADS // 2):
        for part in range(parts):
            halves = []
            for hd in (2 * pair, 2 * pair + 1):
                acc = acc_refs[hd * parts + part][...]
                halves.append(acc[:V_DIM, :] / acc[V_DIM:V_DIM + 1, :])
            o_ref[part * ATT_QPART:(part + 1) * ATT_QPART, pair * LANES:(pair + 1) * LANES] = (
                jnp.concatenate(halves, axis=0).T.astype(BF16))


def _attention(q, k, vt, batch, seq):
    assert ATT_Q == ATT_K and seq % ATT_Q == 0 and ATT_Q % ATT_QPART == 0 and ATT_QPART % CHUNK == 0
    nq = seq // ATT_Q
    chains = N_HEADS * (ATT_Q // ATT_QPART)
    wide = N_HEADS * HEAD_PAD
    qspec = pl.BlockSpec((ATT_Q, wide), lambda b, i: (b * nq + i, 0))
    return pl.pallas_call(
        _attn_kernel,
        grid=(batch, nq),
        in_specs=[qspec, pl.BlockSpec((seq, wide), lambda b, i: (b, 0)),
                  pl.BlockSpec((wide, seq), lambda b, i: (0, b))],
        out_specs=pl.BlockSpec((ATT_Q, N_HEADS * V_DIM), lambda b, i: (b * nq + i, 0)),
        out_shape=jax.ShapeDtypeStruct((q.shape[0], N_HEADS * V_DIM), BF16),
        scratch_shapes=([pltpu.VMEM((1, ATT_QPART), F32)] * chains + [pltpu.VMEM((HEAD_PAD, ATT_QPART), F32)] * chains),
        compiler_params=_params("parallel", "arbitrary"),
        name="attention",
    )(q, k, vt)


def _conv_kernel(y_ref, w_ref, b_ref, lng_ref, lnb_ref, og_ref, o_ref, pad_ref, phase_ref):
    seq = y_ref.shape[0]
    pad_ref[0:CONV_HALO, :] = jnp.zeros((CONV_HALO, CONV_WIDTH), F32)
    pad_ref[CONV_HALO:, :] = y_ref[...]
    first = CONV_HALO - (CONV_TAPS - 1)

    def body(i, _):
        r0 = pl.multiple_of(i * CONV_ROWS, CONV_ROWS)
        win = pad_ref[pl.ds(r0, CONV_ROWS + CONV_HALO), :]
        acc = jnp.zeros((CONV_ROWS, CONV_WIDTH), F32)
        for s in range(SUBLANES):
            offs = [o for o in range(first, first + CONV_TAPS) if o % SUBLANES == s]
            span = max(offs) - s + CONV_ROWS
            if s:
                phase_ref[s, 0:span, :] = win[s:s + span, :]
            for o in offs:
                rows = (pad_ref[pl.ds(pl.multiple_of(r0 + o, SUBLANES), CONV_ROWS), :] if s == 0
                        else phase_ref[s, o - s:o - s + CONV_ROWS, :])
                acc = acc + w_ref[o - first:o - first + 1, :] * rows
        acc = acc + b_ref[...]
        xc = acc - jnp.mean(acc, axis=-1, keepdims=True)
        ln = xc * lax.rsqrt(jnp.mean(xc * xc, axis=-1, keepdims=True) + EPS) * lng_ref[...] + lnb_ref[...]
        z = ln * jax.nn.sigmoid(ln)
        o_ref[pl.ds(r0, CONV_ROWS), :] = _rms(z, og_ref[...]).astype(BF16)
        return 0

    lax.fori_loop(0, seq // CONV_ROWS, body, 0)


def _conv_branch(y, w, b, lng, lnb, og, batch, seq):
    spec = pl.BlockSpec((seq, CONV_WIDTH), lambda bi: (bi, 0))
    return pl.pallas_call(
        _conv_kernel,
        grid=(batch,),
        in_specs=[spec, _full(w.shape), _full(b.shape), _full(lng.shape), _full(lnb.shape), _full(og.shape)],
        out_specs=spec,
        out_shape=jax.ShapeDtypeStruct(y.shape, BF16),
        scratch_shapes=[pltpu.VMEM((seq + CONV_HALO, CONV_WIDTH), F32),
                        pltpu.VMEM((SUBLANES, CONV_ROWS + CONV_HALO, CONV_WIDTH), F32)],
        compiler_params=_params("parallel"),
        name="conv_branch",
    )(y, w, b, lng, lnb, og)


def _mix_kernel(attn_ref, conv_ref, x_ref, ga_ref, woa_ref, woc_ref, gffn_ref, rwh_ref, rwl_ref, rb_ref,
                tri_ref, x1_ref, h2_ref, idx_ref, gate_ref, rank_ref, cnt_ref, base_ref):
    @pl.when(pl.program_id(0) == 0)
    def _():
        base_ref[...] = jnp.zeros(base_ref.shape, F32)

    an = _rms(attn_ref[...].astype(F32), ga_ref[...])
    x1 = (x_ref[...] + jnp.dot(an.astype(BF16), woa_ref[...], preferred_element_type=F32)
          + jnp.dot(conv_ref[...], woc_ref[...], preferred_element_type=F32))
    x1_ref[...] = x1
    h2 = _rms(x1, gffn_ref[...])
    _store_tile_rows(h2_ref, _pack_bf16_pairs(h2), PACKED_TILES)

    hi = h2.astype(BF16)
    lo = (h2 - hi.astype(F32)).astype(BF16)
    dn = (((1,), (1,)), ((), ()))
    rwh, rwl = rwh_ref[...], rwl_ref[...]
    logits = (lax.dot_general(rwh, hi, dn, preferred_element_type=F32)
              + lax.dot_general(rwh, lo, dn, preferred_element_type=F32)
              + lax.dot_general(rwl, hi, dn, preferred_element_type=F32)) + rb_ref[...]

    rows = logits.shape[1]
    eidx = lax.broadcasted_iota(jnp.int32, (N_EXPERTS, rows), 0).astype(F32)
    work = logits
    sels, vals = [], []
    for k in range(TOP_K):
        mx = jnp.max(work, axis=0, keepdims=True)
        first = jnp.min(jnp.where(work == mx, eidx, float(N_EXPERTS)), axis=0, keepdims=True)
        sel = eidx == first
        work = jnp.where(sel, -jnp.inf, work)
        sels.append(sel)
        vals.append(mx)
        idx_ref[k:k + 1, :] = first.astype(jnp.int32)
    exps = [jnp.exp(v - vals[0]) for v in vals]
    denom = exps[0] + exps[1] + exps[2] + exps[3]
    for k in range(TOP_K):
        gate_ref[k:k + 1, :] = exps[k] / denom

    member = jnp.where(sels[0] | sels[1] | sels[2] | sels[3], 1.0, 0.0)
    before = jnp.dot(member.astype(BF16), tri_ref[...], preferred_element_type=F32)
    posn = base_ref[:, 0:1] + before
    for k in range(TOP_K):
        rank_ref[k:k + 1, :] = jnp.sum(jnp.where(sels[k], posn, 0.0), axis=0, keepdims=True).astype(jnp.int32)
    base_ref[...] = base_ref[...] + jnp.sum(member, axis=1, keepdims=True)
    cnt_ref[...] = base_ref[...].astype(jnp.int32)


def _mix_stage(attn, conv, x2, ga, woa, woc, gffn, rwh, rwl, rb, tri):
    t = x2.shape[0]
    proj = lambda i: (i, 0)
    row = lambda w: pl.BlockSpec((MIX_ROWS, w), proj)
    col = pl.BlockSpec((TOP_K, MIX_ROWS), lambda i: (0, i))
    return pl.pallas_call(
        _mix_kernel,
        grid=(t // MIX_ROWS,),
        in_specs=[row(attn.shape[1]), row(CONV_WIDTH), row(D_MODEL), _full(ga.shape), _full(woa.shape),
                  _full(woc.shape), _full(gffn.shape), _full(rwh.shape), _full(rwl.shape), _full(rb.shape),
                  _full(tri.shape)],
        out_specs=[row(D_MODEL), pl.BlockSpec((MIX_ROWS * PACKED_TILES, LANES), proj), col, col, col,
                   _full((N_EXPERTS, LANES))],
        out_shape=[jax.ShapeDtypeStruct((t, D_MODEL), F32),
                   jax.ShapeDtypeStruct((t * PACKED_TILES, LANES), jnp.uint32),
                   jax.ShapeDtypeStruct((TOP_K, t), jnp.int32), jax.ShapeDtypeStruct((TOP_K, t), F32),
                   jax.ShapeDtypeStruct((TOP_K, t), jnp.int32),
                   jax.ShapeDtypeStruct((N_EXPERTS, LANES), jnp.int32)],
        scratch_shapes=[pltpu.VMEM((N_EXPERTS, LANES), F32)],
        compiler_params=_params("arbitrary"),
        name="mix_router",
    )(attn, conv, x2, ga, woa, woc, gffn, rwh, rwl, rb, tri)


def _dispatch_kernel(ends_ref, padded_ref, dest_ref, h_ref, xs_ref, zero_ref, sem, zsem):
    tiles = PACKED_TILES
    rows = h_ref.shape[0] // tiles
    block = MOE_STEP * tiles
    n_blocks = xs_ref.shape[0] // block

    @pl.when(pl.program_id(0) == 0)
    def _():
        zero_ref[...] = jnp.zeros(zero_ref.shape, zero_ref.dtype)

        def zero_copy(r0):
            return pltpu.make_async_copy(
                zero_ref, xs_ref.at[pl.ds(pl.multiple_of(r0 * tiles, block), block)], zsem)

        for start in (True, False):
            for e in range(N_EXPERTS):
                @pl.when(padded_ref[e] > 0)
                def _():
                    cp = zero_copy(ends_ref[e] - MOE_STEP)
                    cp.start() if start else cp.wait()

            def tail(b, _):
                cp = zero_copy(b * MOE_STEP)
                cp.start() if start else cp.wait()
                return 0

            lax.fori_loop(ends_ref[N_EXPERTS - 1] // MOE_STEP, n_blocks, tail, 0)

    def issue(t, _):
        for k in range(TOP_K):
            pltpu.make_async_copy(_tile_row(h_ref, t, tiles=tiles), _tile_row(xs_ref, dest_ref[k, t], tiles=tiles),
                                  sem).start(priority=k % DMA_PRIORITIES)
        return 0

    lax.fori_loop(0, rows, issue, 0, unroll=ISSUE_UNROLL)
    for _ in range(TOP_K):
        pltpu.make_async_copy(h_ref, xs_ref.at[pl.ds(0, rows * tiles)], sem).wait()


def _dispatch(ends, padded, dest, h2, n_blocks):
    t = h2.shape[0] // PACKED_TILES
    grid_spec = pltpu.PrefetchScalarGridSpec(
        num_scalar_prefetch=2,
        grid=(t // DISPATCH_ROWS,),
        in_specs=[pl.BlockSpec((TOP_K, DISPATCH_ROWS), lambda i, *_: (0, i), memory_space=pltpu.SMEM),
                  pl.BlockSpec((DISPATCH_ROWS * PACKED_TILES, LANES), lambda i, *_: (i, 0))],
        out_specs=pl.BlockSpec(memory_space=pl.ANY),
        scratch_shapes=[pltpu.VMEM((MOE_STEP * PACKED_TILES, LANES), h2.dtype), pltpu.SemaphoreType.DMA(()),
                        pltpu.SemaphoreType.DMA(())],
    )
    return pl.pallas_call(
        _dispatch_kernel,
        grid_spec=grid_spec,
        out_shape=jax.ShapeDtypeStruct((n_blocks * MOE_STEP * PACKED_TILES, LANES), h2.dtype),
        compiler_params=_params("arbitrary"),
        name="dispatch_rows",
    )(ends, padded, dest, h2)


def _moe_kernel(be_ref, nact_ref, passes_ref, xs_ref, wgu_ref, bgu_ref, wd_ref, bd_ref, ys_ref, wgu_bf, wd_bf):
    i = pl.program_id(0)

    @pl.when((passes_ref[i] > 0) & ((i == 0) | (be_ref[i] != be_ref[jnp.maximum(i - 1, 0)])))
    def _():
        wgu_bf[...] = wgu_ref[0].astype(BF16)
        wd_bf[...] = wd_ref[0].astype(BF16)

    for p in range(MOE_STEP // MOE_ROWS):
        first = p * MOE_ROWS

        @pl.when(passes_ref[i] <= p)
        def _():
            ys_ref[first * ROW_TILES:(first + MOE_ROWS) * ROW_TILES, :] = jnp.zeros((MOE_ROWS * ROW_TILES, LANES), F32)

        @pl.when(passes_ref[i] > p)
        def _():
            x = _unpack_bf16_pairs(_load_tile_rows(xs_ref, MOE_ROWS, tiles=PACKED_TILES, first=first))
            gu = jnp.dot(x, wgu_bf[...], preferred_element_type=F32) + bgu_ref[0]
            gate = jnp.minimum(gu[:, :D_FF], SWIGLU_LIMIT)
            up = jnp.clip(gu[:, D_FF:], -SWIGLU_LIMIT, SWIGLU_LIMIT)
            mid = (up + 1.0) * (gate * jax.nn.sigmoid(gate * SWIGLU_ALPHA))
            y = jnp.dot(mid.astype(BF16), wd_bf[...], preferred_element_type=F32) + bd_ref[0]
            _store_tile_rows(ys_ref, y, first=first)


def _moe(block_expert, n_active, passes, xs, wgu, bgu, wd, bd):
    block = MOE_STEP * ROW_TILES
    n_blocks = xs.shape[0] // (MOE_STEP * PACKED_TILES)
    grid_spec = pltpu.PrefetchScalarGridSpec(
        num_scalar_prefetch=3,
        grid=(n_blocks,),
        in_specs=[
            pl.BlockSpec((MOE_STEP * PACKED_TILES, LANES), lambda i, be, na, ps: (jnp.minimum(i, na[0] - 1), 0)),
            pl.BlockSpec((1, D_MODEL, 2 * D_FF), lambda i, be, na, ps: (be[i], 0, 0)),
            pl.BlockSpec((1, 1, 2 * D_FF), lambda i, be, na, ps: (be[i], 0, 0)),
            pl.BlockSpec((1, D_FF, D_MODEL), lambda i, be, na, ps: (be[i], 0, 0)),
            pl.BlockSpec((1, 1, D_MODEL), lambda i, be, na, ps: (be[i], 0, 0)),
        ],
        out_specs=pl.BlockSpec((block, LANES), lambda i, be, na, ps: (i, 0)),
        scratch_shapes=[pltpu.VMEM((D_MODEL, 2 * D_FF), BF16), pltpu.VMEM((D_FF, D_MODEL), BF16)],
    )
    return pl.pallas_call(
        _moe_kernel,
        grid_spec=grid_spec,
        out_shape=jax.ShapeDtypeStruct((n_blocks * block, LANES), F32),
        compiler_params=_params("arbitrary"),
        name="expert_mlp",
    )(block_expert, n_active, passes, xs, wgu, bgu, wd, bd)


def _combine_kernel(dest_ref, dest_next_ref, x1_ref, gate_ref, ys_ref, o_ref, buf_ref, sems):
    i, n = pl.program_id(0), pl.num_programs(0)
    rows = x1_ref.shape[0]

    def gather(idx_ref, slot):
        def issue(t, _):
            for k in range(TOP_K):
                pltpu.make_async_copy(_tile_row(ys_ref, idx_ref[k * rows + t]), _tile_row(buf_ref, t, (slot, k)),
                                      sems.at[slot]).start(priority=k % DMA_PRIORITIES)
            return 0

        lax.fori_loop(0, rows, issue, 0, unroll=ISSUE_UNROLL)

    @pl.when(i == 0)
    def _():
        gather(dest_ref, 0)

    @pl.when(i + 1 < n)
    def _():
        gather(dest_next_ref, (i + 1) % 2)

    slot = i % 2
    for k in range(TOP_K):
        pltpu.make_async_copy(ys_ref.at[pl.ds(0, rows * ROW_TILES)], buf_ref.at[slot, k], sems.at[slot]).wait()
    acc = x1_ref[...]
    for k in range(TOP_K):
        acc = acc + gate_ref[:, k:k + 1] * _load_tile_rows(buf_ref, rows, (slot, k))
    o_ref[...] = acc


def _combine(dest, x1, gate_rows, ys):
    t = x1.shape[0]
    steps = t // COMBINE_ROWS
    row = lambda w: pl.BlockSpec((COMBINE_ROWS, w), lambda i: (i, 0))
    dest = dest.reshape(TOP_K, steps, COMBINE_ROWS).transpose(1, 0, 2).reshape(-1)
    slots = lambda nxt: pl.BlockSpec((TOP_K * COMBINE_ROWS,), lambda i: (jnp.minimum(i + nxt, steps - 1),),
                                     memory_space=pltpu.SMEM)
    return pl.pallas_call(
        _combine_kernel,
        grid=(steps,),
        in_specs=[slots(0), slots(1), row(D_MODEL), row(TOP_K), pl.BlockSpec(memory_space=pl.ANY)],
        out_specs=row(D_MODEL),
        out_shape=jax.ShapeDtypeStruct(x1.shape, F32),
        scratch_shapes=[pltpu.VMEM((2, TOP_K, COMBINE_ROWS * ROW_TILES, LANES), F32),
                        pltpu.SemaphoreType.DMA((2,))],
        compiler_params=_params("arbitrary"),
        name="combine_rows",
    )(dest, dest, x1, gate_rows, ys)


def _head_blocks(w, width):
    r = w.shape[0]
    w = w.reshape(r, N_HEADS, width)
    return jnp.pad(w, ((0, 0), (0, 0), (0, HEAD_PAD - width))).reshape(r, N_HEADS * HEAD_PAD)


def _lane_row(v, offset=0):
    return jnp.pad(v, (offset, HEAD_PAD - offset - v.shape[0])).reshape(1, HEAD_PAD)


def _swap_rope(w, sign):
    lo, hi = w[..., NOPE_DIM:NOPE_DIM + HALF_ROPE], w[..., NOPE_DIM + HALF_ROPE:QK_DIM]
    return jnp.concatenate([jnp.zeros_like(w[..., :NOPE_DIM]), sign * hi, lo], axis=-1)


def kernel(x, positions, norm_mix_g, w_in, q_latent_g, w_uq, kv_latent_g, w_ukv, q_head_g, k_head_g, conv_dw_w, conv_dw_b, conv_ln_g, conv_ln_b, attn_out_g, conv_out_g, w_out, norm_ffn_g, router_w, router_b, w_gate_up, b_gate_up, w_down, b_down):
    batch, seq, d_model = x.shape
    t = batch * seq
    assert d_model == D_MODEL and seq % max(ATT_Q, CONV_ROWS) == 0
    assert all(t % rows == 0 for rows in (IN_ROWS, MIX_ROWS, DISPATCH_ROWS, COMBINE_ROWS))
    depth = norm_mix_g.shape[0]
    x2 = x.reshape(t, D_MODEL)
    pos = positions.reshape(1, t)
    invf = (1.0 / (ROPE_THETA ** (jnp.arange(0, ROPE_DIM, 2, dtype=F32) / ROPE_DIM))).reshape(HALF_ROPE, 1)
    tri = jnp.triu(jnp.ones((MIX_ROWS, MIX_ROWS), BF16), 1)
    o_kv = Q_LORA
    o_pe = o_kv + KV_LORA
    o_u = o_pe + ROPE_DIM

    for l in range(depth):
        wi = w_in[l]
        w_pe = wi[:, o_pe:o_u]
        w_pe_sw = jnp.concatenate([-w_pe[:, HALF_ROPE:], w_pe[:, :HALF_ROPE]], axis=1)
        pe_block = lambda w: jnp.pad(w, ((0, 0), (NOPE_DIM, HEAD_PAD - QK_DIM)))
        win = jnp.concatenate([wi[:, :o_pe], pe_block(w_pe), pe_block(w_pe_sw), wi[:, o_u:]], axis=1).astype(BF16)
        wq = w_uq[l].reshape(Q_LORA, N_HEADS, QK_DIM)
        wuq = _head_blocks(w_uq[l], QK_DIM).astype(BF16)
        wuqs = _head_blocks(_swap_rope(wq, -1.0).reshape(Q_LORA, -1), QK_DIM).astype(BF16)
        wkv = w_ukv[l].reshape(KV_LORA, N_HEADS, NOPE_DIM + V_DIM)
        wuk = _head_blocks(wkv[:, :, :NOPE_DIM].reshape(KV_LORA, -1), NOPE_DIM).astype(BF16)
        wuv = _head_blocks(wkv[:, :, NOPE_DIM:].reshape(KV_LORA, -1), V_DIM).T.astype(BF16)
        ga = attn_out_g[l].reshape(1, -1)
        wo = w_out[l]
        woa = wo[:N_HEADS * V_DIM].astype(BF16)
        woc = wo[N_HEADS * V_DIM:].astype(BF16)
        rwt = router_w[l].T
        rwh = rwt.astype(BF16)
        rwl = (rwt - rwh.astype(F32)).astype(BF16)

        q, k, v, y = _input_stage(
            x2, pos, invf, norm_mix_g[l].reshape(1, -1), win, q_latent_g[l].reshape(1, -1), wuq, wuqs,
            kv_latent_g[l].reshape(1, -1), wuk, wuv, _lane_row(q_head_g[l]),
            _lane_row(_swap_rope(q_head_g[l], 1.0)), _lane_row(k_head_g[l]), _lane_row(_swap_rope(k_head_g[l], 1.0)))
        attn = _attention(q, k, v, batch, seq)
        conv = _conv_branch(y, conv_dw_w[l], conv_dw_b[l].reshape(1, -1), conv_ln_g[l].reshape(1, -1),
                            conv_ln_b[l].reshape(1, -1), conv_out_g[l].reshape(1, -1), batch, seq)
        x1, h2, idx, gate, rank, cnt = _mix_stage(
            attn, conv, x2, ga, woa, woc, norm_ffn_g[l].reshape(1, -1), rwh, rwl,
            router_b[l].reshape(-1, 1), tri)

        counts = cnt[:, 0]
        padded = (counts + MOE_STEP - 1) // MOE_STEP * MOE_STEP
        ends = jnp.cumsum(padded)
        starts = ends - padded
        experts = jnp.arange(N_EXPERTS, dtype=jnp.int32)
        dest = rank + jnp.sum(jnp.where(idx[None] == experts[:, None, None], starts[:, None, None], 0), axis=0)
        n_blocks = (t * TOP_K + N_EXPERTS * (MOE_STEP - 1)) // MOE_STEP
        n_active = (ends[-1] // MOE_STEP).astype(jnp.int32)
        step = jnp.arange(n_blocks, dtype=jnp.int32)
        blk = jnp.minimum(step, n_active - 1)
        be = jnp.minimum(jnp.sum((ends[None, :] <= (blk * MOE_STEP)[:, None]).astype(jnp.int32), axis=1),
                         N_EXPERTS - 1)
        real = jnp.clip((starts + counts)[be] - step * MOE_STEP, 0, MOE_STEP)
        passes = jnp.where(step < n_active, (real + MOE_ROWS - 1) // MOE_ROWS, 0).astype(jnp.int32)

        xs = _dispatch(ends.astype(jnp.int32), padded.astype(jnp.int32), dest, h2, n_blocks)
        ys = _moe(be, n_active.reshape(1), passes, xs, w_gate_up[l], b_gate_up[l].reshape(N_EXPERTS, 1, -1),
                  w_down[l], b_down[l].reshape(N_EXPERTS, 1, -1))
        x2 = _combine(dest, x1, gate.T, ys)
    return x2.reshape(batch, seq, D_MODEL)
```

```python
import jax
import jax.numpy as jnp
from jax import lax
from jax.experimental import pallas as pl
from jax.experimental.pallas import tpu as pltpu

D_MODEL = 1024
N_HEADS = 8
NOPE_DIM = 64
ROPE_DIM = 32
QK_DIM = NOPE_DIM + ROPE_DIM
V_DIM = 64
Q_LORA = 384
KV_LORA = 128
CONV_WIDTH = 512
CONV_TAPS = 31
N_EXPERTS = 32
TOP_K = 4
D_FF = 1024
CHUNK = 64
ROPE_THETA = 10000.0
EPS = 1e-6
SWIGLU_ALPHA = 1.702
SWIGLU_LIMIT = 7.0
LOG2_E = 1.4426950408889634

LANES = 128
SUBLANES = 8
HEAD_PAD = LANES
HALF_ROPE = ROPE_DIM // 2
VMEM_LIMIT_BYTES = 56 * 1024 * 1024

IN_ROWS = 1024
ATT_Q = 512
ATT_K = 512
ATT_QPART = 256
ATT_AHEAD = 4
CONV_ROWS = 256
CONV_HALO = 32
MIX_ROWS = 1024
MIX_GROUPS = 2
DISPATCH_ROWS = 512
MOE_ROWS = 512
MOE_STEP = 2 * MOE_ROWS
COMBINE_ROWS = 512
ISSUE_UNROLL = 8
DMA_PRIORITIES = 2

F32 = jnp.float32
BF16 = jnp.bfloat16


def _params(*semantics):
    return pltpu.CompilerParams(dimension_semantics=semantics, vmem_limit_bytes=VMEM_LIMIT_BYTES)


def _rms(x, g):
    return x * lax.rsqrt(jnp.mean(x * x, axis=-1, keepdims=True) + EPS) * g


def _full(shape):
    return pl.BlockSpec(shape, lambda *_: (0,) * len(shape))


ROW_TILES = D_MODEL // LANES
PACKED_TILES = ROW_TILES // 2
assert ROW_TILES == SUBLANES


def _load_tile_rows(ref, rows, lead=(), tiles=ROW_TILES, first=0):
    return jnp.concatenate(
        [ref[lead + (pl.ds(first * tiles + c, rows, stride=tiles), slice(None))] for c in range(tiles)], axis=1)


def _store_tile_rows(ref, value, tiles=ROW_TILES, first=0):
    rows = value.shape[0]
    for c in range(tiles):
        ref[pl.ds(first * tiles + c, rows, stride=tiles), :] = value[:, c * LANES:(c + 1) * LANES]


def _tile_row(ref, r, lead=(), tiles=ROW_TILES):
    return ref.at[lead + (pl.ds(pl.multiple_of(r * tiles, tiles), tiles),)]


def _pack_bf16_pairs(x):
    half = x.shape[1] // 2
    bits = lax.bitcast_convert_type(x.astype(BF16).astype(F32), jnp.uint32)
    return (bits[:, :half] >> 16) | bits[:, half:]


def _unpack_bf16_pairs(p):
    left = lax.bitcast_convert_type(p << 16, F32)
    right = lax.bitcast_convert_type(p & jnp.uint32(0xFFFF0000), F32)
    return jnp.concatenate([left, right], axis=1).astype(BF16)


def _in_kernel(x_ref, pos_ref, invf_ref, gmix_ref, win_ref, gql_ref, wuq_ref, wuqs_ref, gkvl_ref, wuk_ref,
               wuv_ref, gq_ref, gqs_ref, gk_ref, gks_ref, q_ref, k_ref, v_ref, y_ref):
    h = _rms(x_ref[...], gmix_ref[...]).astype(BF16)
    p = jnp.dot(h, win_ref[...], preferred_element_type=F32)
    o_kv = Q_LORA
    o_pe = o_kv + KV_LORA
    o_ps = o_pe + HEAD_PAD
    o_a = o_ps + HEAD_PAD
    o_g = o_a + CONV_WIDTH
    cq, ckv, kpe, kpe_sw = p[:, :o_kv], p[:, o_kv:o_pe], p[:, o_pe:o_ps], p[:, o_ps:o_a]
    y_ref[...] = p[:, o_a:o_g] * jax.nn.sigmoid(p[:, o_g:])

    cqn = _rms(cq, gql_ref[...]).astype(BF16)
    q = jnp.dot(cqn, wuq_ref[...], preferred_element_type=F32)
    q_sw = jnp.dot(cqn, wuqs_ref[...], preferred_element_type=F32)
    ckvn = _rms(ckv, gkvl_ref[...]).astype(BF16)
    kn = jnp.dot(ckvn, wuk_ref[...], preferred_element_type=F32)
    wide_row = lax.broadcasted_iota(jnp.int32, (N_HEADS * HEAD_PAD, 1), 0)
    ones_row = jnp.where(wide_row % HEAD_PAD == V_DIM, 1.0, 0.0)
    vt = lax.dot_general(wuv_ref[...], ckvn, (((1,), (1,)), ((), ())), preferred_element_type=F32)
    v_ref[...] = (vt + ones_row).astype(BF16)

    ang = invf_ref[...] * pos_ref[...].astype(F32)
    cos_t, sin_t = jnp.cos(ang), jnp.sin(ang)
    tab = jnp.concatenate([jnp.ones((NOPE_DIM, ang.shape[1]), F32), cos_t, cos_t, sin_t, sin_t], axis=0).T
    lane = lax.broadcasted_iota(jnp.int32, (1, HEAD_PAD), 1)
    ctab = jnp.where(lane < QK_DIM, tab, 0.0)
    stab = jnp.where((lane >= NOPE_DIM) & (lane < QK_DIM), pltpu.roll(tab, QK_DIM, 1), 0.0)
    scale = QK_DIM ** -0.5 * LOG2_E
    cq_tab = ctab * (gq_ref[...] * scale)
    sq_tab = stab * (gqs_ref[...] * scale)
    ck_tab = ctab * gk_ref[...]
    sk_tab = stab * gks_ref[...]
    k_pe = kpe * ck_tab + kpe_sw * sk_tab
    ss_pe = jnp.sum(kpe * kpe, axis=-1, keepdims=True)
    for hd in range(N_HEADS):
        sl = slice(hd * HEAD_PAD, (hd + 1) * HEAD_PAD)
        qh = q[:, sl]
        rq = lax.rsqrt(jnp.sum(qh * qh, axis=-1, keepdims=True) * (1.0 / QK_DIM) + EPS)
        q_ref[:, sl] = ((qh * cq_tab + q_sw[:, sl] * sq_tab) * rq).astype(BF16)
        kh = kn[:, sl]
        rk = lax.rsqrt((jnp.sum(kh * kh, axis=-1, keepdims=True) + ss_pe) * (1.0 / QK_DIM) + EPS)
        k_ref[:, sl] = ((kh * ck_tab + k_pe) * rk).astype(BF16)


def _input_stage(x2, pos, invf, gmix, win, gql, wuq, wuqs, gkvl, wuk, wuv, gq, gqs, gk, gks):
    t = x2.shape[0]
    wide = N_HEADS * HEAD_PAD
    row = lambda w: pl.BlockSpec((IN_ROWS, w), lambda i: (i, 0))
    consts = (invf, gmix, win, gql, wuq, wuqs, gkvl, wuk, wuv, gq, gqs, gk, gks)
    return pl.pallas_call(
        _in_kernel,
        grid=(t // IN_ROWS,),
        in_specs=[row(D_MODEL), pl.BlockSpec((1, IN_ROWS), lambda i: (0, i))] + [_full(a.shape) for a in consts],
        out_specs=[row(wide), row(wide), pl.BlockSpec((wide, IN_ROWS), lambda i: (0, i)), row(CONV_WIDTH)],
        out_shape=[jax.ShapeDtypeStruct((t, wide), BF16), jax.ShapeDtypeStruct((t, wide), BF16),
                   jax.ShapeDtypeStruct((wide, t), BF16), jax.ShapeDtypeStruct((t, CONV_WIDTH), F32)],
        compiler_params=_params("parallel"),
        name="input_stage",
    )(x2, pos, *consts)


def _attn_kernel(q_ref, k_ref, vt_ref, o_ref, *scratch):
    units = [(hd, part) for hd in range(N_HEADS) for part in range(ATT_Q // ATT_QPART)]
    m_refs, acc_refs = scratch[:len(units)], scratch[len(units):]
    i = pl.program_id(1)
    dn = (((1,), (1,)), ((), ()))

    def step(r0, diagonal):
        old = None if diagonal else [(m_refs[u][...], acc_refs[u][...]) for u in range(len(units))]
        new = []

        def keys(part):
            return (part + 1) * ATT_QPART if diagonal else ATT_K

        def scores(u):
            hd, part = units[u]
            sl = slice(hd * HEAD_PAD, (hd + 1) * HEAD_PAD)
            return lax.dot_general(k_ref[pl.ds(r0, keys(part)), sl],
                                   q_ref[part * ATT_QPART:(part + 1) * ATT_QPART, sl], dn,
                                   preferred_element_type=F32)

        ahead = [scores(u) for u in range(ATT_AHEAD)]
        for u, (hd, part) in enumerate(units):
            sl = slice(hd * HEAD_PAD, (hd + 1) * HEAD_PAD)
            s = ahead.pop(0)
            if u + ATT_AHEAD < len(units):
                ahead.append(scores(u + ATT_AHEAD))
            if diagonal:
                kc = lax.broadcasted_iota(jnp.int32, s.shape, 0) // CHUNK
                qc = (lax.broadcasted_iota(jnp.int32, s.shape, 1) + part * ATT_QPART) // CHUNK
                s = jnp.where(kc <= qc, s, -jnp.inf)
            m_new = jnp.max(s, axis=0, keepdims=True)
            if not diagonal:
                m_new = jnp.maximum(old[u][0], m_new)
            acc = jnp.dot(vt_ref[sl, pl.ds(r0, keys(part))], jnp.exp2(s - m_new).astype(BF16),
                          preferred_element_type=F32)
            if not diagonal:
                acc = jnp.exp2(old[u][0] - m_new) * old[u][1] + acc
            new.append((m_new, acc))
        for u in range(len(units)):
            m_refs[u][...], acc_refs[u][...] = new[u]

    step(pl.multiple_of(i * ATT_Q, ATT_Q), True)

    def body(j, _):
        step(pl.multiple_of(j * ATT_K, ATT_K), False)
        return 0

    lax.fori_loop(0, i, body, 0)
    parts = ATT_Q // ATT_QPART
    for pair in range(N_HEADS // 2):
        for part in range(parts):
            halves = []
            for hd in (2 * pair, 2 * pair + 1):
                acc = acc_refs[hd * parts + part][...]
                halves.append(acc[:V_DIM, :] / acc[V_DIM:V_DIM + 1, :])
            o_ref[part * ATT_QPART:(part + 1) * ATT_QPART, pair * LANES:(pair + 1) * LANES] = (
                jnp.concatenate(halves, axis=0).T.astype(BF16))


def _attention(q, k, vt, batch, seq):
    assert ATT_Q == ATT_K and seq % ATT_Q == 0 and ATT_Q % ATT_QPART == 0 and ATT_QPART % CHUNK == 0
    nq = seq // ATT_Q
    chains = N_HEADS * (ATT_Q // ATT_QPART)
    wide = N_HEADS * HEAD_PAD
    qspec = pl.BlockSpec((ATT_Q, wide), lambda b, i: (b * nq + i, 0))
    return pl.pallas_call(
        _attn_kernel,
        grid=(batch, nq),
        in_specs=[qspec, pl.BlockSpec((seq, wide), lambda b, i: (b, 0)),
                  pl.BlockSpec((wide, seq), lambda b, i: (0, b))],
        out_specs=pl.BlockSpec((ATT_Q, N_HEADS * V_DIM), lambda b, i: (b * nq + i, 0)),
        out_shape=jax.ShapeDtypeStruct((q.shape[0], N_HEADS * V_DIM), BF16),
        scratch_shapes=([pltpu.VMEM((1, ATT_QPART), F32)] * chains + [pltpu.VMEM((HEAD_PAD, ATT_QPART), F32)] * chains),
        compiler_params=_params("parallel", "arbitrary"),
        name="attention",
    )(q, k, vt)


def _conv_kernel(y_ref, w_ref, b_ref, lng_ref, lnb_ref, og_ref, o_ref, pad_ref, phase_ref):
    seq = y_ref.shape[0]
    pad_ref[0:CONV_HALO, :] = jnp.zeros((CONV_HALO, CONV_WIDTH), F32)
    pad_ref[CONV_HALO:, :] = y_ref[...]
    first = CONV_HALO - (CONV_TAPS - 1)

    def body(i, _):
        r0 = pl.multiple_of(i * CONV_ROWS, CONV_ROWS)
        win = pad_ref[pl.ds(r0, CONV_ROWS + CONV_HALO), :]
        acc = jnp.zeros((CONV_ROWS, CONV_WIDTH), F32)
        for s in range(SUBLANES):
            offs = [o for o in range(first, first + CONV_TAPS) if o % SUBLANES == s]
            span = max(offs) - s + CONV_ROWS
            if s:
                phase_ref[s, 0:span, :] = win[s:s + span, :]
            for o in offs:
                rows = (pad_ref[pl.ds(pl.multiple_of(r0 + o, SUBLANES), CONV_ROWS), :] if s == 0
                        else phase_ref[s, o - s:o - s + CONV_ROWS, :])
                acc = acc + w_ref[o - first:o - first + 1, :] * rows
        acc = acc + b_ref[...]
        xc = acc - jnp.mean(acc, axis=-1, keepdims=True)
        ln = xc * lax.rsqrt(jnp.mean(xc * xc, axis=-1, keepdims=True) + EPS) * lng_ref[...] + lnb_ref[...]
        z = ln * jax.nn.sigmoid(ln)
        o_ref[pl.ds(r0, CONV_ROWS), :] = _rms(z, og_ref[...]).astype(BF16)
        return 0

    lax.fori_loop(0, seq // CONV_ROWS, body, 0)


def _conv_branch(y, w, b, lng, lnb, og, batch, seq):
    spec = pl.BlockSpec((seq, CONV_WIDTH), lambda bi: (bi, 0))
    return pl.pallas_call(
        _conv_kernel,
        grid=(batch,),
        in_specs=[spec, _full(w.shape), _full(b.shape), _full(lng.shape), _full(lnb.shape), _full(og.shape)],
        out_specs=spec,
        out_shape=jax.ShapeDtypeStruct(y.shape, BF16),
        scratch_shapes=[pltpu.VMEM((seq + CONV_HALO, CONV_WIDTH), F32),
                        pltpu.VMEM((SUBLANES, CONV_ROWS + CONV_HALO, CONV_WIDTH), F32)],
        compiler_params=_params("parallel"),
        name="conv_branch",
    )(y, w, b, lng, lnb, og)


def _mix_kernel(attn_ref, conv_ref, x_ref, ga_ref, wo_ref, gffn_ref, rw2_ref, rb_ref,
                tri_ref, x1_ref, h2_ref, idx_ref, gate_ref, rank_ref, cnt_ref, base_ref):
    @pl.when(pl.program_id(0) == 0)
    def _():
        base_ref[...] = jnp.zeros(base_ref.shape, F32)

    size = MIX_ROWS // MIX_GROUPS
    groups = [slice(g * size, (g + 1) * size) for g in range(MIX_GROUPS)]
    dn = (((1,), (1,)), ((), ()))

    def project(rs):
        an = _rms(attn_ref[rs, :].astype(F32), ga_ref[...])
        mixed = jnp.concatenate([an.astype(BF16), conv_ref[rs, :]], axis=1)
        return x_ref[rs, :] + jnp.dot(mixed, wo_ref[...], preferred_element_type=F32)

    def norm_and_logits(g, rs, x1):
        x1_ref[rs, :] = x1
        h2 = _rms(x1, gffn_ref[...])
        _store_tile_rows(h2_ref, _pack_bf16_pairs(h2), PACKED_TILES, first=g * size)
        hi = h2.astype(BF16)
        lo = (h2 - hi.astype(F32)).astype(BF16)
        both = lax.dot_general(rw2_ref[...], hi, dn, preferred_element_type=F32)
        return (both[:N_EXPERTS] + both[N_EXPERTS:]
                + lax.dot_general(rw2_ref[:N_EXPERTS, :], lo, dn, preferred_element_type=F32)) + rb_ref[...]

    def top_k(rs, logits):
        eidx = lax.broadcasted_iota(jnp.int32, logits.shape, 0).astype(F32)
        work = logits
        sels, vals = [], []
        for k in range(TOP_K):
            mx = jnp.max(work, axis=0, keepdims=True)
            first = jnp.min(jnp.where(work == mx, eidx, float(N_EXPERTS)), axis=0, keepdims=True)
            sel = eidx == first
            work = jnp.where(sel, -jnp.inf, work)
            sels.append(sel)
            vals.append(mx)
            idx_ref[k:k + 1, rs] = first.astype(jnp.int32)
        exps = [jnp.exp(v - vals[0]) for v in vals]
        denom = exps[0] + exps[1] + exps[2] + exps[3]
        for k in range(TOP_K):
            gate_ref[k:k + 1, rs] = exps[k] / denom
        return sels

    x1s, logits, sels = {}, {}, {}
    for s in range(MIX_GROUPS + 2):
        if s < MIX_GROUPS:
            x1s[s] = project(groups[s])
        if 0 <= s - 1 < MIX_GROUPS:
            logits[s - 1] = norm_and_logits(s - 1, groups[s - 1], x1s.pop(s - 1))
        if 0 <= s - 2 < MIX_GROUPS:
            sels[s - 2] = top_k(groups[s - 2], logits.pop(s - 2))
    sels = [sels[g] for g in range(MIX_GROUPS)]

    sels = [jnp.concatenate([s[k] for s in sels], axis=1) for k in range(TOP_K)]
    member = jnp.where(sels[0] | sels[1] | sels[2] | sels[3], 1.0, 0.0)
    blocks = [member[:, b * LANES:(b + 1) * LANES] for b in range(MIX_ROWS // LANES)]
    inner = jnp.dot(jnp.concatenate(blocks, axis=0).astype(BF16), tri_ref[...], preferred_element_type=F32)
    offset = base_ref[:, 0:1]
    pieces = []
    for b, blk in enumerate(blocks):
        pieces.append(inner[b * N_EXPERTS:(b + 1) * N_EXPERTS, :] + offset)
        offset = offset + jnp.sum(blk, axis=1, keepdims=True)
    posn = jnp.concatenate(pieces, axis=1)
    for k in range(TOP_K):
        rank_ref[k:k + 1, :] = jnp.sum(jnp.where(sels[k], posn, 0.0), axis=0, keepdims=True).astype(jnp.int32)
    base_ref[...] = jnp.broadcast_to(offset, base_ref.shape)
    cnt_ref[...] = base_ref[...].astype(jnp.int32)


def _mix_stage(attn, conv, x2, ga, wo, gffn, rw2, rb, tri):
    t = x2.shape[0]
    proj = lambda i: (i, 0)
    row = lambda w: pl.BlockSpec((MIX_ROWS, w), proj)
    col = pl.BlockSpec((TOP_K, MIX_ROWS), lambda i: (0, i))
    return pl.pallas_call(
        _mix_kernel,
        grid=(t // MIX_ROWS,),
        in_specs=[row(attn.shape[1]), row(CONV_WIDTH), row(D_MODEL), _full(ga.shape), _full(wo.shape),
                  _full(gffn.shape), _full(rw2.shape), _full(rb.shape), _full(tri.shape)],
        out_specs=[row(D_MODEL), pl.BlockSpec((MIX_ROWS * PACKED_TILES, LANES), proj), col, col, col,
                   _full((N_EXPERTS, LANES))],
        out_shape=[jax.ShapeDtypeStruct((t, D_MODEL), F32),
                   jax.ShapeDtypeStruct((t * PACKED_TILES, LANES), jnp.uint32),
                   jax.ShapeDtypeStruct((TOP_K, t), jnp.int32), jax.ShapeDtypeStruct((TOP_K, t), F32),
                   jax.ShapeDtypeStruct((TOP_K, t), jnp.int32),
                   jax.ShapeDtypeStruct((N_EXPERTS, LANES), jnp.int32)],
        scratch_shapes=[pltpu.VMEM((N_EXPERTS, LANES), F32)],
        compiler_params=_params("arbitrary"),
        name="mix_router",
    )(attn, conv, x2, ga, wo, gffn, rw2, rb, tri)


def _dispatch_kernel(ends_ref, padded_ref, dest_ref, h_ref, xs_ref, zero_ref, sem, zsem):
    tiles = PACKED_TILES
    rows = h_ref.shape[0] // tiles
    block = MOE_STEP * tiles
    n_blocks = xs_ref.shape[0] // block

    @pl.when(pl.program_id(0) == 0)
    def _():
        zero_ref[...] = jnp.zeros(zero_ref.shape, zero_ref.dtype)

        def zero_copy(r0):
            return pltpu.make_async_copy(
                zero_ref, xs_ref.at[pl.ds(pl.multiple_of(r0 * tiles, block), block)], zsem)

        for start in (True, False):
            for e in range(N_EXPERTS):
                @pl.when(padded_ref[e] > 0)
                def _():
                    cp = zero_copy(ends_ref[e] - MOE_STEP)
                    cp.start() if start else cp.wait()

            def tail(b, _):
                cp = zero_copy(b * MOE_STEP)
                cp.start() if start else cp.wait()
                return 0

            lax.fori_loop(ends_ref[N_EXPERTS - 1] // MOE_STEP, n_blocks, tail, 0)

    def issue(t, _):
        for k in range(TOP_K):
            pltpu.make_async_copy(_tile_row(h_ref, t, tiles=tiles), _tile_row(xs_ref, dest_ref[k, t], tiles=tiles),
                                  sem).start(priority=k % DMA_PRIORITIES)
        return 0

    lax.fori_loop(0, rows, issue, 0, unroll=ISSUE_UNROLL)
    for _ in range(TOP_K):
        pltpu.make_async_copy(h_ref, xs_ref.at[pl.ds(0, rows * tiles)], sem).wait()


def _dispatch(ends, padded, dest, h2, n_blocks):
    t = h2.shape[0] // PACKED_TILES
    grid_spec = pltpu.PrefetchScalarGridSpec(
        num_scalar_prefetch=2,
        grid=(t // DISPATCH_ROWS,),
        in_specs=[pl.BlockSpec((TOP_K, DISPATCH_ROWS), lambda i, *_: (0, i), memory_space=pltpu.SMEM),
                  pl.BlockSpec((DISPATCH_ROWS * PACKED_TILES, LANES), lambda i, *_: (i, 0))],
        out_specs=pl.BlockSpec(memory_space=pl.ANY),
        scratch_shapes=[pltpu.VMEM((MOE_STEP * PACKED_TILES, LANES), h2.dtype), pltpu.SemaphoreType.DMA(()),
                        pltpu.SemaphoreType.DMA(())],
    )
    return pl.pallas_call(
        _dispatch_kernel,
        grid_spec=grid_spec,
        out_shape=jax.ShapeDtypeStruct((n_blocks * MOE_STEP * PACKED_TILES, LANES), h2.dtype),
        compiler_params=_params("arbitrary"),
        name="dispatch_rows",
    )(ends, padded, dest, h2)


def _moe_kernel(be_ref, nact_ref, passes_ref, xs_ref, wgu_ref, bgu_ref, wd_ref, bd_ref, ys_ref, wgu_bf, wd_bf):
    i = pl.program_id(0)

    @pl.when((passes_ref[i] > 0) & ((i == 0) | (be_ref[i] != be_ref[jnp.maximum(i - 1, 0)])))
    def _():
        wgu_bf[...] = wgu_ref[0].astype(BF16)
        wd_bf[...] = wd_ref[0].astype(BF16)

    for p in range(MOE_STEP // MOE_ROWS):
        first = p * MOE_ROWS

        @pl.when(passes_ref[i] <= p)
        def _():
            ys_ref[first * ROW_TILES:(first + MOE_ROWS) * ROW_TILES, :] = jnp.zeros((MOE_ROWS * ROW_TILES, LANES), F32)

        @pl.when(passes_ref[i] > p)
        def _():
            x = _unpack_bf16_pairs(_load_tile_rows(xs_ref, MOE_ROWS, tiles=PACKED_TILES, first=first))
            gu = jnp.dot(x, wgu_bf[...], preferred_element_type=F32) + bgu_ref[0]
            gate = jnp.minimum(gu[:, :D_FF], SWIGLU_LIMIT)
            up = jnp.clip(gu[:, D_FF:], -SWIGLU_LIMIT, SWIGLU_LIMIT)
            mid = (up + 1.0) * (gate * jax.nn.sigmoid(gate * SWIGLU_ALPHA))
            y = jnp.dot(mid.astype(BF16), wd_bf[...], preferred_element_type=F32) + bd_ref[0]
            _store_tile_rows(ys_ref, y, first=first)


def _moe(block_expert, n_active, passes, xs, wgu, bgu, wd, bd):
    block = MOE_STEP * ROW_TILES
    n_blocks = xs.shape[0] // (MOE_STEP * PACKED_TILES)
    grid_spec = pltpu.PrefetchScalarGridSpec(
        num_scalar_prefetch=3,
        grid=(n_blocks,),
        in_specs=[
            pl.BlockSpec((MOE_STEP * PACKED_TILES, LANES), lambda i, be, na, ps: (jnp.minimum(i, na[0] - 1), 0)),
            pl.BlockSpec((1, D_MODEL, 2 * D_FF), lambda i, be, na, ps: (be[i], 0, 0)),
            pl.BlockSpec((1, 1, 2 * D_FF), lambda i, be, na, ps: (be[i], 0, 0)),
            pl.BlockSpec((1, D_FF, D_MODEL), lambda i, be, na, ps: (be[i], 0, 0)),
            pl.BlockSpec((1, 1, D_MODEL), lambda i, be, na, ps: (be[i], 0, 0)),
        ],
        out_specs=pl.BlockSpec((block, LANES), lambda i, be, na, ps: (i, 0)),
        scratch_shapes=[pltpu.VMEM((D_MODEL, 2 * D_FF), BF16), pltpu.VMEM((D_FF, D_MODEL), BF16)],
    )
    return pl.pallas_call(
        _moe_kernel,
        grid_spec=grid_spec,
        out_shape=jax.ShapeDtypeStruct((n_blocks * block, LANES), F32),
        compiler_params=_params("arbitrary"),
        name="expert_mlp",
    )(block_expert, n_active, passes, xs, wgu, bgu, wd, bd)


def _combine_kernel(dest_ref, dest_next_ref, x1_ref, gate_ref, ys_ref, o_ref, buf_ref, sems):
    i, n = pl.program_id(0), pl.num_programs(0)
    rows = x1_ref.shape[0]

    def gather(idx_ref, slot):
        def issue(t, _):
            for k in range(TOP_K):
                pltpu.make_async_copy(_tile_row(ys_ref, idx_ref[k * rows + t]), _tile_row(buf_ref, t, (slot, k)),
                                      sems.at[slot]).start(priority=k % DMA_PRIORITIES)
            return 0

        lax.fori_loop(0, rows, issue, 0, unroll=ISSUE_UNROLL)

    @pl.when(i == 0)
    def _():
        gather(dest_ref, 0)

    @pl.when(i + 1 < n)
    def _():
        gather(dest_next_ref, (i + 1) % 2)

    slot = i % 2
    for k in range(TOP_K):
        pltpu.make_async_copy(ys_ref.at[pl.ds(0, rows * ROW_TILES)], buf_ref.at[slot, k], sems.at[slot]).wait()
    acc = x1_ref[...]
    for k in range(TOP_K):
        acc = acc + gate_ref[:, k:k + 1] * _load_tile_rows(buf_ref, rows, (slot, k))
    o_ref[...] = acc


def _combine(dest, x1, gate_rows, ys):
    t = x1.shape[0]
    steps = t // COMBINE_ROWS
    row = lambda w: pl.BlockSpec((COMBINE_ROWS, w), lambda i: (i, 0))
    dest = dest.reshape(TOP_K, steps, COMBINE_ROWS).transpose(1, 0, 2).reshape(-1)
    slots = lambda nxt: pl.BlockSpec((TOP_K * COMBINE_ROWS,), lambda i: (jnp.minimum(i + nxt, steps - 1),),
                                     memory_space=pltpu.SMEM)
    return pl.pallas_call(
        _combine_kernel,
        grid=(steps,),
        in_specs=[slots(0), slots(1), row(D_MODEL), row(TOP_K), pl.BlockSpec(memory_space=pl.ANY)],
        out_specs=row(D_MODEL),
        out_shape=jax.ShapeDtypeStruct(x1.shape, F32),
        scratch_shapes=[pltpu.VMEM((2, TOP_K, COMBINE_ROWS * ROW_TILES, LANES), F32),
                        pltpu.SemaphoreType.DMA((2,))],
        compiler_params=_params("arbitrary"),
        name="combine_rows",
    )(dest, dest, x1, gate_rows, ys)


def _head_blocks(w, width):
    r = w.shape[0]
    w = w.reshape(r, N_HEADS, width)
    return jnp.pad(w, ((0, 0), (0, 0), (0, HEAD_PAD - width))).reshape(r, N_HEADS * HEAD_PAD)


def _lane_row(v, offset=0):
    return jnp.pad(v, (offset, HEAD_PAD - offset - v.shape[0])).reshape(1, HEAD_PAD)


def _swap_rope(w, sign):
    lo, hi = w[..., NOPE_DIM:NOPE_DIM + HALF_ROPE], w[..., NOPE_DIM + HALF_ROPE:QK_DIM]
    return jnp.concatenate([jnp.zeros_like(w[..., :NOPE_DIM]), sign * hi, lo], axis=-1)


def kernel(x, positions, norm_mix_g, w_in, q_latent_g, w_uq, kv_latent_g, w_ukv, q_head_g, k_head_g, conv_dw_w, conv_dw_b, conv_ln_g, conv_ln_b, attn_out_g, conv_out_g, w_out, norm_ffn_g, router_w, router_b, w_gate_up, b_gate_up, w_down, b_down):
    batch, seq, d_model = x.shape
    t = batch * seq
    assert d_model == D_MODEL and seq % max(ATT_Q, CONV_ROWS) == 0
    assert all(t % rows == 0 for rows in (IN_ROWS, MIX_ROWS, DISPATCH_ROWS, COMBINE_ROWS))
    depth = norm_mix_g.shape[0]
    x2 = x.reshape(t, D_MODEL)
    pos = positions.reshape(1, t)
    invf = (1.0 / (ROPE_THETA ** (jnp.arange(0, ROPE_DIM, 2, dtype=F32) / ROPE_DIM))).reshape(HALF_ROPE, 1)
    tri = jnp.triu(jnp.ones((LANES, LANES), BF16), 1)
    o_kv = Q_LORA
    o_pe = o_kv + KV_LORA
    o_u = o_pe + ROPE_DIM

    for l in range(depth):
        wi = w_in[l]
        w_pe = wi[:, o_pe:o_u]
        w_pe_sw = jnp.concatenate([-w_pe[:, HALF_ROPE:], w_pe[:, :HALF_ROPE]], axis=1)
        pe_block = lambda w: jnp.pad(w, ((0, 0), (NOPE_DIM, HEAD_PAD - QK_DIM)))
        win = jnp.concatenate([wi[:, :o_pe], pe_block(w_pe), pe_block(w_pe_sw), wi[:, o_u:]], axis=1).astype(BF16)
        wq = w_uq[l].reshape(Q_LORA, N_HEADS, QK_DIM)
        wuq = _head_blocks(w_uq[l], QK_DIM).astype(BF16)
        wuqs = _head_blocks(_swap_rope(wq, -1.0).reshape(Q_LORA, -1), QK_DIM).astype(BF16)
        wkv = w_ukv[l].reshape(KV_LORA, N_HEADS, NOPE_DIM + V_DIM)
        wuk = _head_blocks(wkv[:, :, :NOPE_DIM].reshape(KV_LORA, -1), NOPE_DIM).astype(BF16)
        wuv = _head_blocks(wkv[:, :, NOPE_DIM:].reshape(KV_LORA, -1), V_DIM).T.astype(BF16)
        ga = attn_out_g[l].reshape(1, -1)
        wo = w_out[l].astype(BF16)
        rwt = router_w[l].T
        rwh = rwt.astype(BF16)
        rw2 = jnp.concatenate([rwh, (rwt - rwh.astype(F32)).astype(BF16)], axis=0)

        q, k, v, y = _input_stage(
            x2, pos, invf, norm_mix_g[l].reshape(1, -1), win, q_latent_g[l].reshape(1, -1), wuq, wuqs,
            kv_latent_g[l].reshape(1, -1), wuk, wuv, _lane_row(q_head_g[l]),
            _lane_row(_swap_rope(q_head_g[l], 1.0)), _lane_row(k_head_g[l]), _lane_row(_swap_rope(k_head_g[l], 1.0)))
        attn = _attention(q, k, v, batch, seq)
        conv = _conv_branch(y, conv_dw_w[l], conv_dw_b[l].reshape(1, -1), conv_ln_g[l].reshape(1, -1),
                            conv_ln_b[l].reshape(1, -1), conv_out_g[l].reshape(1, -1), batch, seq)
        x1, h2, idx, gate, rank, cnt = _mix_stage(
            attn, conv, x2, ga, wo, norm_ffn_g[l].reshape(1, -1), rw2, router_b[l].reshape(-1, 1), tri)

        counts = cnt[:, 0]
        padded = (counts + MOE_STEP - 1) // MOE_STEP * MOE_STEP
        ends = jnp.cumsum(padded)
        starts = ends - padded
        experts = jnp.arange(N_EXPERTS, dtype=jnp.int32)
        dest = rank + jnp.sum(jnp.where(idx[None] == experts[:, None, None], starts[:, None, None], 0), axis=0)
        n_blocks = (t * TOP_K + N_EXPERTS * (MOE_STEP - 1)) // MOE_STEP
        n_active = (ends[-1] // MOE_STEP).astype(jnp.int32)
        step = jnp.arange(n_blocks, dtype=jnp.int32)
        blk = jnp.minimum(step, n_active - 1)
        be = jnp.minimum(jnp.sum((ends[None, :] <= (blk * MOE_STEP)[:, None]).astype(jnp.int32), axis=1),
                         N_EXPERTS - 1)
        real = jnp.clip((starts + counts)[be] - step * MOE_STEP, 0, MOE_STEP)
        passes = jnp.where(step < n_active, (real + MOE_ROWS - 1) // MOE_ROWS, 0).astype(jnp.int32)

        xs = _dispatch(ends.astype(jnp.int32), padded.astype(jnp.int32), dest, h2, n_blocks)
        ys = _moe(be, n_active.reshape(1), passes, xs, w_gate_up[l], b_gate_up[l].reshape(N_EXPERTS, 1, -1),
                  w_down[l], b_down[l].reshape(N_EXPERTS, 1, -1))
        x2 = _combine(dest, x1, gate.T, ys)
    return x2.reshape(batch, seq, D_MODEL)
```

```python
import jax
import jax.numpy as jnp
from jax import lax
from jax.experimental import pallas as pl
from jax.experimental.pallas import tpu as pltpu

D_MODEL = 1024
N_HEADS = 8
NOPE_DIM = 64
ROPE_DIM = 32
QK_DIM = NOPE_DIM + ROPE_DIM
V_DIM = 64
Q_LORA = 384
KV_LORA = 128
CONV_WIDTH = 512
CONV_TAPS = 31
N_EXPERTS = 32
TOP_K = 4
D_FF = 1024
CHUNK = 64
ROPE_THETA = 10000.0
EPS = 1e-6
SWIGLU_ALPHA = 1.702
SWIGLU_LIMIT = 7.0
LOG2_E = 1.4426950408889634

LANES = 128
SUBLANES = 8
HEAD_PAD = LANES
HALF_ROPE = ROPE_DIM // 2
VMEM_LIMIT_BYTES = 56 * 1024 * 1024

IN_ROWS = 1024
ATT_Q = 1024
ATT_K = 512
ATT_QPART = 256
ATT_AHEAD = 4
CONV_ROWS = 256
CONV_HALO = 32
MIX_ROWS = 1024
MIX_GROUPS = 2
DISPATCH_ROWS = 512
MOE_ROWS = 512
MOE_STEP = 2 * MOE_ROWS
COMBINE_ROWS = 512
ISSUE_UNROLL = 8
DMA_PRIORITIES = 2

F32 = jnp.float32
BF16 = jnp.bfloat16


def _params(*semantics):
    return pltpu.CompilerParams(dimension_semantics=semantics, vmem_limit_bytes=VMEM_LIMIT_BYTES)


def _rms(x, g):
    return x * lax.rsqrt(jnp.mean(x * x, axis=-1, keepdims=True) + EPS) * g


def _full(shape):
    return pl.BlockSpec(shape, lambda *_: (0,) * len(shape))


ROW_TILES = D_MODEL // LANES
PACKED_TILES = ROW_TILES // 2
assert ROW_TILES == SUBLANES


def _load_tile_rows(ref, rows, lead=(), tiles=ROW_TILES, first=0):
    return jnp.concatenate(
        [ref[lead + (pl.ds(first * tiles + c, rows, stride=tiles), slice(None))] for c in range(tiles)], axis=1)


def _store_tile_rows(ref, value, tiles=ROW_TILES, first=0):
    rows = value.shape[0]
    for c in range(tiles):
        ref[pl.ds(first * tiles + c, rows, stride=tiles), :] = value[:, c * LANES:(c + 1) * LANES]


def _tile_row(ref, r, lead=(), tiles=ROW_TILES):
    return ref.at[lead + (pl.ds(pl.multiple_of(r * tiles, tiles), tiles),)]


def _pack_bf16_pairs(x):
    half = x.shape[1] // 2
    bits = lax.bitcast_convert_type(x.astype(BF16).astype(F32), jnp.uint32)
    return (bits[:, :half] >> 16) | bits[:, half:]


def _unpack_bf16_pairs(p):
    left = lax.bitcast_convert_type(p << 16, F32)
    right = lax.bitcast_convert_type(p & jnp.uint32(0xFFFF0000), F32)
    return jnp.concatenate([left, right], axis=1).astype(BF16)


def _in_kernel(x_ref, pos_ref, invf_ref, gmix_ref, win_ref, gql_ref, wuq_ref, wuqs_ref, gkvl_ref, wuk_ref,
               wuv_ref, gq_ref, gqs_ref, gk_ref, gks_ref, q_ref, k_ref, v_ref, y_ref):
    h = _rms(x_ref[...], gmix_ref[...]).astype(BF16)
    p = jnp.dot(h, win_ref[...], preferred_element_type=F32)
    o_kv = Q_LORA
    o_pe = o_kv + KV_LORA
    o_ps = o_pe + HEAD_PAD
    o_a = o_ps + HEAD_PAD
    o_g = o_a + CONV_WIDTH
    cq, ckv, kpe, kpe_sw = p[:, :o_kv], p[:, o_kv:o_pe], p[:, o_pe:o_ps], p[:, o_ps:o_a]
    y_ref[...] = p[:, o_a:o_g] * jax.nn.sigmoid(p[:, o_g:])

    cqn = _rms(cq, gql_ref[...]).astype(BF16)
    q = jnp.dot(cqn, wuq_ref[...], preferred_element_type=F32)
    q_sw = jnp.dot(cqn, wuqs_ref[...], preferred_element_type=F32)
    ckvn = _rms(ckv, gkvl_ref[...]).astype(BF16)
    kn = jnp.dot(ckvn, wuk_ref[...], preferred_element_type=F32)
    wide_row = lax.broadcasted_iota(jnp.int32, (N_HEADS * HEAD_PAD, 1), 0)
    ones_row = jnp.where(wide_row % HEAD_PAD == V_DIM, 1.0, 0.0)
    vt = lax.dot_general(wuv_ref[...], ckvn, (((1,), (1,)), ((), ())), preferred_element_type=F32)
    v_ref[...] = (vt + ones_row).astype(BF16)

    ang = invf_ref[...] * pos_ref[...].astype(F32)
    cos_t, sin_t = jnp.cos(ang), jnp.sin(ang)
    tab = jnp.concatenate([jnp.ones((NOPE_DIM, ang.shape[1]), F32), cos_t, cos_t, sin_t, sin_t], axis=0).T
    lane = lax.broadcasted_iota(jnp.int32, (1, HEAD_PAD), 1)
    ctab = jnp.where(lane < QK_DIM, tab, 0.0)
    stab = jnp.where((lane >= NOPE_DIM) & (lane < QK_DIM), pltpu.roll(tab, QK_DIM, 1), 0.0)
    scale = QK_DIM ** -0.5 * LOG2_E
    cq_tab = ctab * (gq_ref[...] * scale)
    sq_tab = stab * (gqs_ref[...] * scale)
    ck_tab = ctab * gk_ref[...]
    sk_tab = stab * gks_ref[...]
    k_pe = kpe * ck_tab + kpe_sw * sk_tab
    ss_pe = jnp.sum(kpe * kpe, axis=-1, keepdims=True)
    for hd in range(N_HEADS):
        sl = slice(hd * HEAD_PAD, (hd + 1) * HEAD_PAD)
        qh = q[:, sl]
        rq = lax.rsqrt(jnp.sum(qh * qh, axis=-1, keepdims=True) * (1.0 / QK_DIM) + EPS)
        q_ref[:, sl] = ((qh * cq_tab + q_sw[:, sl] * sq_tab) * rq).astype(BF16)
        kh = kn[:, sl]
        rk = lax.rsqrt((jnp.sum(kh * kh, axis=-1, keepdims=True) + ss_pe) * (1.0 / QK_DIM) + EPS)
        k_ref[:, sl] = ((kh * ck_tab + k_pe) * rk).astype(BF16)


def _input_stage(x2, pos, invf, gmix, win, gql, wuq, wuqs, gkvl, wuk, wuv, gq, gqs, gk, gks):
    t = x2.shape[0]
    wide = N_HEADS * HEAD_PAD
    row = lambda w: pl.BlockSpec((IN_ROWS, w), lambda i: (i, 0))
    consts = (invf, gmix, win, gql, wuq, wuqs, gkvl, wuk, wuv, gq, gqs, gk, gks)
    return pl.pallas_call(
        _in_kernel,
        grid=(t // IN_ROWS,),
        in_specs=[row(D_MODEL), pl.BlockSpec((1, IN_ROWS), lambda i: (0, i))] + [_full(a.shape) for a in consts],
        out_specs=[row(wide), row(wide), pl.BlockSpec((wide, IN_ROWS), lambda i: (0, i)), row(CONV_WIDTH)],
        out_shape=[jax.ShapeDtypeStruct((t, wide), BF16), jax.ShapeDtypeStruct((t, wide), BF16),
                   jax.ShapeDtypeStruct((wide, t), BF16), jax.ShapeDtypeStruct((t, CONV_WIDTH), F32)],
        compiler_params=_params("parallel"),
        name="input_stage",
    )(x2, pos, *consts)


def _attn_kernel(q_ref, k_ref, vt_ref, o_ref, *scratch):
    parts, block_parts, blocks = ATT_Q // ATT_QPART, ATT_K // ATT_QPART, ATT_Q // ATT_K
    units = [(hd, part) for hd in range(N_HEADS) for part in range(parts)]
    m_refs, acc_refs = scratch[:len(units)], scratch[len(units):]
    i = pl.program_id(1)
    dn = (((1,), (1,)), ((), ()))

    def step(r0, block, diagonal):
        todo = [u for u, (_, part) in enumerate(units) if block is None or part // block_parts == block]
        old = None if diagonal else {u: (m_refs[u][...], acc_refs[u][...]) for u in todo}
        new = {}

        def keys(part):
            return (part % block_parts + 1) * ATT_QPART if diagonal else ATT_K

        def scores(u):
            hd, part = units[u]
            sl = slice(hd * HEAD_PAD, (hd + 1) * HEAD_PAD)
            return lax.dot_general(k_ref[pl.ds(r0, keys(part)), sl],
                                   q_ref[part * ATT_QPART:(part + 1) * ATT_QPART, sl], dn,
                                   preferred_element_type=F32)

        ahead = [scores(u) for u in todo[:ATT_AHEAD]]
        for n, u in enumerate(todo):
            hd, part = units[u]
            sl = slice(hd * HEAD_PAD, (hd + 1) * HEAD_PAD)
            s = ahead.pop(0)
            if n + ATT_AHEAD < len(todo):
                ahead.append(scores(todo[n + ATT_AHEAD]))
            if diagonal:
                kc = lax.broadcasted_iota(jnp.int32, s.shape, 0) // CHUNK
                qc = (lax.broadcasted_iota(jnp.int32, s.shape, 1) + part % block_parts * ATT_QPART) // CHUNK
                s = jnp.where(kc <= qc, s, -jnp.inf)
            m_new = jnp.max(s, axis=0, keepdims=True)
            if not diagonal:
                m_new = jnp.maximum(old[u][0], m_new)
            acc = jnp.dot(vt_ref[sl, pl.ds(r0, keys(part))], jnp.exp2(s - m_new).astype(BF16),
                          preferred_element_type=F32)
            if not diagonal:
                acc = jnp.exp2(old[u][0] - m_new) * old[u][1] + acc
            new[u] = (m_new, acc)
        for u in todo:
            m_refs[u][...], acc_refs[u][...] = new[u]

    tile0 = pl.multiple_of(i * ATT_Q, ATT_Q)
    for block in range(blocks):
        step(tile0 + block * ATT_K, block, True)
        for earlier in range(block):
            step(tile0 + earlier * ATT_K, block, False)

    def body(j, _):
        step(pl.multiple_of(j * ATT_K, ATT_K), None, False)
        return 0

    lax.fori_loop(0, i * blocks, body, 0)
    for pair in range(N_HEADS // 2):
        for part in range(parts):
            halves = []
            for hd in (2 * pair, 2 * pair + 1):
                acc = acc_refs[hd * parts + part][...]
                halves.append(acc[:V_DIM, :] / acc[V_DIM:V_DIM + 1, :])
            o_ref[part * ATT_QPART:(part + 1) * ATT_QPART, pair * LANES:(pair + 1) * LANES] = (
                jnp.concatenate(halves, axis=0).T.astype(BF16))


def _attention(q, k, vt, batch, seq):
    assert seq % ATT_Q == 0 and ATT_Q % ATT_K == 0 and ATT_K % ATT_QPART == 0 and ATT_QPART % CHUNK == 0
    nq = seq // ATT_Q
    chains = N_HEADS * (ATT_Q // ATT_QPART)
    wide = N_HEADS * HEAD_PAD
    qspec = pl.BlockSpec((ATT_Q, wide), lambda b, i: (b * nq + i, 0))
    return pl.pallas_call(
        _attn_kernel,
        grid=(batch, nq),
        in_specs=[qspec, pl.BlockSpec((seq, wide), lambda b, i: (b, 0)),
                  pl.BlockSpec((wide, seq), lambda b, i: (0, b))],
        out_specs=pl.BlockSpec((ATT_Q, N_HEADS * V_DIM), lambda b, i: (b * nq + i, 0)),
        out_shape=jax.ShapeDtypeStruct((q.shape[0], N_HEADS * V_DIM), BF16),
        scratch_shapes=([pltpu.VMEM((1, ATT_QPART), F32)] * chains + [pltpu.VMEM((HEAD_PAD, ATT_QPART), F32)] * chains),
        compiler_params=_params("parallel", "arbitrary"),
        name="attention",
    )(q, k, vt)


def _conv_kernel(y_ref, w_ref, b_ref, lng_ref, lnb_ref, og_ref, o_ref, pad_ref, phase_ref):
    seq = y_ref.shape[0]
    pad_ref[0:CONV_HALO, :] = jnp.zeros((CONV_HALO, CONV_WIDTH), F32)
    pad_ref[CONV_HALO:, :] = y_ref[...]
    first = CONV_HALO - (CONV_TAPS - 1)

    def body(i, _):
        r0 = pl.multiple_of(i * CONV_ROWS, CONV_ROWS)
        win = pad_ref[pl.ds(r0, CONV_ROWS + CONV_HALO), :]
        acc = jnp.zeros((CONV_ROWS, CONV_WIDTH), F32)
        for s in range(SUBLANES):
            offs = [o for o in range(first, first + CONV_TAPS) if o % SUBLANES == s]
            span = max(offs) - s + CONV_ROWS
            if s:
                phase_ref[s, 0:span, :] = win[s:s + span, :]
            for o in offs:
                rows = (pad_ref[pl.ds(pl.multiple_of(r0 + o, SUBLANES), CONV_ROWS), :] if s == 0
                        else phase_ref[s, o - s:o - s + CONV_ROWS, :])
                acc = acc + w_ref[o - first:o - first + 1, :] * rows
        acc = acc + b_ref[...]
        xc = acc - jnp.mean(acc, axis=-1, keepdims=True)
        ln = xc * lax.rsqrt(jnp.mean(xc * xc, axis=-1, keepdims=True) + EPS) * lng_ref[...] + lnb_ref[...]
        z = ln * jax.nn.sigmoid(ln)
        o_ref[pl.ds(r0, CONV_ROWS), :] = _rms(z, og_ref[...]).astype(BF16)
        return 0

    lax.fori_loop(0, seq // CONV_ROWS, body, 0)


def _conv_branch(y, w, b, lng, lnb, og, batch, seq):
    spec = pl.BlockSpec((seq, CONV_WIDTH), lambda bi: (bi, 0))
    return pl.pallas_call(
        _conv_kernel,
        grid=(batch,),
        in_specs=[spec, _full(w.shape), _full(b.shape), _full(lng.shape), _full(lnb.shape), _full(og.shape)],
        out_specs=spec,
        out_shape=jax.ShapeDtypeStruct(y.shape, BF16),
        scratch_shapes=[pltpu.VMEM((seq + CONV_HALO, CONV_WIDTH), F32),
                        pltpu.VMEM((SUBLANES, CONV_ROWS + CONV_HALO, CONV_WIDTH), F32)],
        compiler_params=_params("parallel"),
        name="conv_branch",
    )(y, w, b, lng, lnb, og)


def _mix_kernel(attn_ref, conv_ref, x_ref, ga_ref, wo_ref, gffn_ref, rw2_ref, rb_ref,
                tri_ref, x1_ref, h2_ref, idx_ref, gate_ref, rank_ref, cnt_ref, base_ref):
    @pl.when(pl.program_id(0) == 0)
    def _():
        base_ref[...] = jnp.zeros(base_ref.shape, F32)

    size = MIX_ROWS // MIX_GROUPS
    groups = [slice(g * size, (g + 1) * size) for g in range(MIX_GROUPS)]
    dn = (((1,), (1,)), ((), ()))

    def project(rs):
        an = _rms(attn_ref[rs, :].astype(F32), ga_ref[...])
        mixed = jnp.concatenate([an.astype(BF16), conv_ref[rs, :]], axis=1)
        return x_ref[rs, :] + jnp.dot(mixed, wo_ref[...], preferred_element_type=F32)

    def norm_and_logits(g, rs, x1):
        x1_ref[rs, :] = x1
        h2 = _rms(x1, gffn_ref[...])
        _store_tile_rows(h2_ref, _pack_bf16_pairs(h2), PACKED_TILES, first=g * size)
        hi = h2.astype(BF16)
        lo = (h2 - hi.astype(F32)).astype(BF16)
        both = lax.dot_general(rw2_ref[...], hi, dn, preferred_element_type=F32)
        return (both[:N_EXPERTS] + both[N_EXPERTS:]
                + lax.dot_general(rw2_ref[:N_EXPERTS, :], lo, dn, preferred_element_type=F32)) + rb_ref[...]

    def top_k(rs, logits):
        eidx = lax.broadcasted_iota(jnp.int32, logits.shape, 0).astype(F32)
        work = logits
        sels, vals = [], []
        for k in range(TOP_K):
            mx = jnp.max(work, axis=0, keepdims=True)
            first = jnp.min(jnp.where(work == mx, eidx, float(N_EXPERTS)), axis=0, keepdims=True)
            sel = eidx == first
            work = jnp.where(sel, -jnp.inf, work)
            sels.append(sel)
            vals.append(mx)
            idx_ref[k:k + 1, rs] = first.astype(jnp.int32)
        exps = [jnp.exp(v - vals[0]) for v in vals]
        denom = exps[0] + exps[1] + exps[2] + exps[3]
        for k in range(TOP_K):
            gate_ref[k:k + 1, rs] = exps[k] / denom
        return sels

    x1s, logits, sels = {}, {}, {}
    for s in range(MIX_GROUPS + 2):
        if s < MIX_GROUPS:
            x1s[s] = project(groups[s])
        if 0 <= s - 1 < MIX_GROUPS:
            logits[s - 1] = norm_and_logits(s - 1, groups[s - 1], x1s.pop(s - 1))
        if 0 <= s - 2 < MIX_GROUPS:
            sels[s - 2] = top_k(groups[s - 2], logits.pop(s - 2))
    sels = [sels[g] for g in range(MIX_GROUPS)]

    sels = [jnp.concatenate([s[k] for s in sels], axis=1) for k in range(TOP_K)]
    member = jnp.where(sels[0] | sels[1] | sels[2] | sels[3], 1.0, 0.0)
    blocks = [member[:, b * LANES:(b + 1) * LANES] for b in range(MIX_ROWS // LANES)]
    inner = jnp.dot(jnp.concatenate(blocks, axis=0).astype(BF16), tri_ref[...], preferred_element_type=F32)
    offset = base_ref[:, 0:1]
    pieces = []
    for b, blk in enumerate(blocks):
        pieces.append(inner[b * N_EXPERTS:(b + 1) * N_EXPERTS, :] + offset)
        offset = offset + jnp.sum(blk, axis=1, keepdims=True)
    posn = jnp.concatenate(pieces, axis=1)
    for k in range(TOP_K):
        rank_ref[k:k + 1, :] = jnp.sum(jnp.where(sels[k], posn, 0.0), axis=0, keepdims=True).astype(jnp.int32)
    base_ref[...] = jnp.broadcast_to(offset, base_ref.shape)
    cnt_ref[...] = base_ref[...].astype(jnp.int32)


def _mix_stage(attn, conv, x2, ga, wo, gffn, rw2, rb, tri):
    t = x2.shape[0]
    proj = lambda i: (i, 0)
    row = lambda w: pl.BlockSpec((MIX_ROWS, w), proj)
    col = pl.BlockSpec((TOP_K, MIX_ROWS), lambda i: (0, i))
    return pl.pallas_call(
        _mix_kernel,
        grid=(t // MIX_ROWS,),
        in_specs=[row(attn.shape[1]), row(CONV_WIDTH), row(D_MODEL), _full(ga.shape), _full(wo.shape),
                  _full(gffn.shape), _full(rw2.shape), _full(rb.shape), _full(tri.shape)],
        out_specs=[row(D_MODEL), pl.BlockSpec((MIX_ROWS * PACKED_TILES, LANES), proj), col, col, col,
                   _full((N_EXPERTS, LANES))],
        out_shape=[jax.ShapeDtypeStruct((t, D_MODEL), F32),
                   jax.ShapeDtypeStruct((t * PACKED_TILES, LANES), jnp.uint32),
                   jax.ShapeDtypeStruct((TOP_K, t), jnp.int32), jax.ShapeDtypeStruct((TOP_K, t), F32),
                   jax.ShapeDtypeStruct((TOP_K, t), jnp.int32),
                   jax.ShapeDtypeStruct((N_EXPERTS, LANES), jnp.int32)],
        scratch_shapes=[pltpu.VMEM((N_EXPERTS, LANES), F32)],
        compiler_params=_params("arbitrary"),
        name="mix_router",
    )(attn, conv, x2, ga, wo, gffn, rw2, rb, tri)


def _dispatch_kernel(ends_ref, padded_ref, dest_ref, h_ref, xs_ref, zero_ref, sem, zsem):
    tiles = PACKED_TILES
    rows = h_ref.shape[0] // tiles
    block = MOE_STEP * tiles
    n_blocks = xs_ref.shape[0] // block

    @pl.when(pl.program_id(0) == 0)
    def _():
        zero_ref[...] = jnp.zeros(zero_ref.shape, zero_ref.dtype)

        def zero_copy(r0):
            return pltpu.make_async_copy(
                zero_ref, xs_ref.at[pl.ds(pl.multiple_of(r0 * tiles, block), block)], zsem)

        for start in (True, False):
            for e in range(N_EXPERTS):
                @pl.when(padded_ref[e] > 0)
                def _():
                    cp = zero_copy(ends_ref[e] - MOE_STEP)
                    cp.start() if start else cp.wait()

            def tail(b, _):
                cp = zero_copy(b * MOE_STEP)
                cp.start() if start else cp.wait()
                return 0

            lax.fori_loop(ends_ref[N_EXPERTS - 1] // MOE_STEP, n_blocks, tail, 0)

    def issue(t, _):
        for k in range(TOP_K):
            pltpu.make_async_copy(_tile_row(h_ref, t, tiles=tiles), _tile_row(xs_ref, dest_ref[k, t], tiles=tiles),
                                  sem).start(priority=k % DMA_PRIORITIES)
        return 0

    lax.fori_loop(0, rows, issue, 0, unroll=ISSUE_UNROLL)
    for _ in range(TOP_K):
        pltpu.make_async_copy(h_ref, xs_ref.at[pl.ds(0, rows * tiles)], sem).wait()


def _dispatch(ends, padded, dest, h2, n_blocks):
    t = h2.shape[0] // PACKED_TILES
    grid_spec = pltpu.PrefetchScalarGridSpec(
        num_scalar_prefetch=2,
        grid=(t // DISPATCH_ROWS,),
        in_specs=[pl.BlockSpec((TOP_K, DISPATCH_ROWS), lambda i, *_: (0, i), memory_space=pltpu.SMEM),
                  pl.BlockSpec((DISPATCH_ROWS * PACKED_TILES, LANES), lambda i, *_: (i, 0))],
        out_specs=pl.BlockSpec(memory_space=pl.ANY),
        scratch_shapes=[pltpu.VMEM((MOE_STEP * PACKED_TILES, LANES), h2.dtype), pltpu.SemaphoreType.DMA(()),
                        pltpu.SemaphoreType.DMA(())],
    )
    return pl.pallas_call(
        _dispatch_kernel,
        grid_spec=grid_spec,
        out_shape=jax.ShapeDtypeStruct((n_blocks * MOE_STEP * PACKED_TILES, LANES), h2.dtype),
        compiler_params=_params("arbitrary"),
        name="dispatch_rows",
    )(ends, padded, dest, h2)


def _moe_kernel(be_ref, nact_ref, passes_ref, xs_ref, wgu_ref, bgu_ref, wd_ref, bd_ref, ys_ref, wgu_bf, wd_bf):
    i = pl.program_id(0)

    @pl.when((passes_ref[i] > 0) & ((i == 0) | (be_ref[i] != be_ref[jnp.maximum(i - 1, 0)])))
    def _():
        wgu_bf[...] = wgu_ref[0].astype(BF16)
        wd_bf[...] = wd_ref[0].astype(BF16)

    for p in range(MOE_STEP // MOE_ROWS):
        first = p * MOE_ROWS

        @pl.when(passes_ref[i] <= p)
        def _():
            ys_ref[first * ROW_TILES:(first + MOE_ROWS) * ROW_TILES, :] = jnp.zeros((MOE_ROWS * ROW_TILES, LANES), F32)

        @pl.when(passes_ref[i] > p)
        def _():
            x = _unpack_bf16_pairs(_load_tile_rows(xs_ref, MOE_ROWS, tiles=PACKED_TILES, first=first))
            gu = jnp.dot(x, wgu_bf[...], preferred_element_type=F32) + bgu_ref[0]
            gate = jnp.minimum(gu[:, :D_FF], SWIGLU_LIMIT)
            up = jnp.clip(gu[:, D_FF:], -SWIGLU_LIMIT, SWIGLU_LIMIT)
            mid = (up + 1.0) * (gate * jax.nn.sigmoid(gate * SWIGLU_ALPHA))
            y = jnp.dot(mid.astype(BF16), wd_bf[...], preferred_element_type=F32) + bd_ref[0]
            _store_tile_rows(ys_ref, y, first=first)


def _moe(block_expert, n_active, passes, xs, wgu, bgu, wd, bd):
    block = MOE_STEP * ROW_TILES
    n_blocks = xs.shape[0] // (MOE_STEP * PACKED_TILES)
    grid_spec = pltpu.PrefetchScalarGridSpec(
        num_scalar_prefetch=3,
        grid=(n_blocks,),
        in_specs=[
            pl.BlockSpec((MOE_STEP * PACKED_TILES, LANES), lambda i, be, na, ps: (jnp.minimum(i, na[0] - 1), 0)),
            pl.BlockSpec((1, D_MODEL, 2 * D_FF), lambda i, be, na, ps: (be[i], 0, 0)),
            pl.BlockSpec((1, 1, 2 * D_FF), lambda i, be, na, ps: (be[i], 0, 0)),
            pl.BlockSpec((1, D_FF, D_MODEL), lambda i, be, na, ps: (be[i], 0, 0)),
            pl.BlockSpec((1, 1, D_MODEL), lambda i, be, na, ps: (be[i], 0, 0)),
        ],
        out_specs=pl.BlockSpec((block, LANES), lambda i, be, na, ps: (i, 0)),
        scratch_shapes=[pltpu.VMEM((D_MODEL, 2 * D_FF), BF16), pltpu.VMEM((D_FF, D_MODEL), BF16)],
    )
    return pl.pallas_call(
        _moe_kernel,
        grid_spec=grid_spec,
        out_shape=jax.ShapeDtypeStruct((n_blocks * block, LANES), F32),
        compiler_params=_params("arbitrary"),
        name="expert_mlp",
    )(block_expert, n_active, passes, xs, wgu, bgu, wd, bd)


def _combine_kernel(dest_ref, dest_next_ref, x1_ref, gate_ref, ys_ref, o_ref, buf_ref, sems):
    i, n = pl.program_id(0), pl.num_programs(0)
    rows = x1_ref.shape[0]

    def gather(idx_ref, slot):
        def issue(t, _):
            for k in range(TOP_K):
                pltpu.make_async_copy(_tile_row(ys_ref, idx_ref[k * rows + t]), _tile_row(buf_ref, t, (slot, k)),
                                      sems.at[slot]).start(priority=k % DMA_PRIORITIES)
            return 0

        lax.fori_loop(0, rows, issue, 0, unroll=ISSUE_UNROLL)

    @pl.when(i == 0)
    def _():
        gather(dest_ref, 0)

    @pl.when(i + 1 < n)
    def _():
        gather(dest_next_ref, (i + 1) % 2)

    slot = i % 2
    for k in range(TOP_K):
        pltpu.make_async_copy(ys_ref.at[pl.ds(0, rows * ROW_TILES)], buf_ref.at[slot, k], sems.at[slot]).wait()
    acc = x1_ref[...]
    for k in range(TOP_K):
        acc = acc + gate_ref[:, k:k + 1] * _load_tile_rows(buf_ref, rows, (slot, k))
    o_ref[...] = acc


def _combine(dest, x1, gate_rows, ys):
    t = x1.shape[0]
    steps = t // COMBINE_ROWS
    row = lambda w: pl.BlockSpec((COMBINE_ROWS, w), lambda i: (i, 0))
    dest = dest.reshape(TOP_K, steps, COMBINE_ROWS).transpose(1, 0, 2).reshape(-1)
    slots = lambda nxt: pl.BlockSpec((TOP_K * COMBINE_ROWS,), lambda i: (jnp.minimum(i + nxt, steps - 1),),
                                     memory_space=pltpu.SMEM)
    return pl.pallas_call(
        _combine_kernel,
        grid=(steps,),
        in_specs=[slots(0), slots(1), row(D_MODEL), row(TOP_K), pl.BlockSpec(memory_space=pl.ANY)],
        out_specs=row(D_MODEL),
        out_shape=jax.ShapeDtypeStruct(x1.shape, F32),
        scratch_shapes=[pltpu.VMEM((2, TOP_K, COMBINE_ROWS * ROW_TILES, LANES), F32),
                        pltpu.SemaphoreType.DMA((2,))],
        compiler_params=_params("arbitrary"),
        name="combine_rows",
    )(dest, dest, x1, gate_rows, ys)


def _head_blocks(w, width):
    r = w.shape[0]
    w = w.reshape(r, N_HEADS, width)
    return jnp.pad(w, ((0, 0), (0, 0), (0, HEAD_PAD - width))).reshape(r, N_HEADS * HEAD_PAD)


def _lane_row(v, offset=0):
    return jnp.pad(v, (offset, HEAD_PAD - offset - v.shape[0])).reshape(1, HEAD_PAD)


def _swap_rope(w, sign):
    lo, hi = w[..., NOPE_DIM:NOPE_DIM + HALF_ROPE], w[..., NOPE_DIM + HALF_ROPE:QK_DIM]
    return jnp.concatenate([jnp.zeros_like(w[..., :NOPE_DIM]), sign * hi, lo], axis=-1)


def kernel(x, positions, norm_mix_g, w_in, q_latent_g, w_uq, kv_latent_g, w_ukv, q_head_g, k_head_g, conv_dw_w, conv_dw_b, conv_ln_g, conv_ln_b, attn_out_g, conv_out_g, w_out, norm_ffn_g, router_w, router_b, w_gate_up, b_gate_up, w_down, b_down):
    batch, seq, d_model = x.shape
    t = batch * seq
    assert d_model == D_MODEL and seq % max(ATT_Q, CONV_ROWS) == 0
    assert all(t % rows == 0 for rows in (IN_ROWS, MIX_ROWS, DISPATCH_ROWS, COMBINE_ROWS))
    depth = norm_mix_g.shape[0]
    x2 = x.reshape(t, D_MODEL)
    pos = positions.reshape(1, t)
    invf = (1.0 / (ROPE_THETA ** (jnp.arange(0, ROPE_DIM, 2, dtype=F32) / ROPE_DIM))).reshape(HALF_ROPE, 1)
    tri = jnp.triu(jnp.ones((LANES, LANES), BF16), 1)
    o_kv = Q_LORA
    o_pe = o_kv + KV_LORA
    o_u = o_pe + ROPE_DIM

    for l in range(depth):
        wi = w_in[l]
        w_pe = wi[:, o_pe:o_u]
        w_pe_sw = jnp.concatenate([-w_pe[:, HALF_ROPE:], w_pe[:, :HALF_ROPE]], axis=1)
        pe_block = lambda w: jnp.pad(w, ((0, 0), (NOPE_DIM, HEAD_PAD - QK_DIM)))
        win = jnp.concatenate([wi[:, :o_pe], pe_block(w_pe), pe_block(w_pe_sw), wi[:, o_u:]], axis=1).astype(BF16)
        wq = w_uq[l].reshape(Q_LORA, N_HEADS, QK_DIM)
        wuq = _head_blocks(w_uq[l], QK_DIM).astype(BF16)
        wuqs = _head_blocks(_swap_rope(wq, -1.0).reshape(Q_LORA, -1), QK_DIM).astype(BF16)
        wkv = w_ukv[l].reshape(KV_LORA, N_HEADS, NOPE_DIM + V_DIM)
        wuk = _head_blocks(wkv[:, :, :NOPE_DIM].reshape(KV_LORA, -1), NOPE_DIM).astype(BF16)
        wuv = _head_blocks(wkv[:, :, NOPE_DIM:].reshape(KV_LORA, -1), V_DIM).T.astype(BF16)
        ga = attn_out_g[l].reshape(1, -1)
        wo = w_out[l].astype(BF16)
        rwt = router_w[l].T
        rwh = rwt.astype(BF16)
        rw2 = jnp.concatenate([rwh, (rwt - rwh.astype(F32)).astype(BF16)], axis=0)

        q, k, v, y = _input_stage(
            x2, pos, invf, norm_mix_g[l].reshape(1, -1), win, q_latent_g[l].reshape(1, -1), wuq, wuqs,
            kv_latent_g[l].reshape(1, -1), wuk, wuv, _lane_row(q_head_g[l]),
            _lane_row(_swap_rope(q_head_g[l], 1.0)), _lane_row(k_head_g[l]), _lane_row(_swap_rope(k_head_g[l], 1.0)))
        attn = _attention(q, k, v, batch, seq)
        conv = _conv_branch(y, conv_dw_w[l], conv_dw_b[l].reshape(1, -1), conv_ln_g[l].reshape(1, -1),
                            conv_ln_b[l].reshape(1, -1), conv_out_g[l].reshape(1, -1), batch, seq)
        x1, h2, idx, gate, rank, cnt = _mix_stage(
            attn, conv, x2, ga, wo, norm_ffn_g[l].reshape(1, -1), rw2, router_b[l].reshape(-1, 1), tri)

        counts = cnt[:, 0]
        padded = (counts + MOE_STEP - 1) // MOE_STEP * MOE_STEP
        ends = jnp.cumsum(padded)
        starts = ends - padded
        experts = jnp.arange(N_EXPERTS, dtype=jnp.int32)
        dest = rank + jnp.sum(jnp.where(idx[None] == experts[:, None, None], starts[:, None, None], 0), axis=0)
        n_blocks = (t * TOP_K + N_EXPERTS * (MOE_STEP - 1)) // MOE_STEP
        n_active = (ends[-1] // MOE_STEP).astype(jnp.int32)
        step = jnp.arange(n_blocks, dtype=jnp.int32)
        blk = jnp.minimum(step, n_active - 1)
        be = jnp.minimum(jnp.sum((ends[None, :] <= (blk * MOE_STEP)[:, None]).astype(jnp.int32), axis=1),
                         N_EXPERTS - 1)
        real = jnp.clip((starts + counts)[be] - step * MOE_STEP, 0, MOE_STEP)
        passes = jnp.where(step < n_active, (real + MOE_ROWS - 1) // MOE_ROWS, 0).astype(jnp.int32)

        xs = _dispatch(ends.astype(jnp.int32), padded.astype(jnp.int32), dest, h2, n_blocks)
        ys = _moe(be, n_active.reshape(1), passes, xs, w_gate_up[l], b_gate_up[l].reshape(N_EXPERTS, 1, -1),
                  w_down[l], b_down[l].reshape(N_EXPERTS, 1, -1))
        x2 = _combine(dest, x1, gate.T, ys)
    return x2.reshape(batch, seq, D_MODEL)
```

```python
import jax
import jax.numpy as jnp
from jax import lax
from jax.experimental import pallas as pl
from jax.experimental.pallas import tpu as pltpu

D_MODEL = 1024
N_HEADS = 8
NOPE_DIM = 64
ROPE_DIM = 32
QK_DIM = NOPE_DIM + ROPE_DIM
V_DIM = 64
Q_LORA = 384
KV_LORA = 128
CONV_WIDTH = 512
CONV_TAPS = 31
N_EXPERTS = 32
TOP_K = 4
D_FF = 1024
CHUNK = 64
ROPE_THETA = 10000.0
EPS = 1e-6
SWIGLU_ALPHA = 1.702
SWIGLU_LIMIT = 7.0
LOG2_E = 1.4426950408889634

LANES = 128
SUBLANES = 8
HEAD_PAD = LANES
HALF_ROPE = ROPE_DIM // 2
V_ROWS = V_DIM + 2 * SUBLANES
VMEM_LIMIT_BYTES = 56 * 1024 * 1024

IN_ROWS = 1024
ATT_Q = 1024
ATT_K = 512
ATT_QPART = 256
ATT_AHEAD = 4
CONV_ROWS = 256
CONV_HALO = 32
MIX_ROWS = 1024
MIX_GROUPS = 2
DISPATCH_ROWS = 512
MOE_ROWS = 512
MOE_STEP = 2 * MOE_ROWS
COMBINE_ROWS = 512
ISSUE_UNROLL = 8
DMA_PRIORITIES = 2

F32 = jnp.float32
BF16 = jnp.bfloat16


def _params(*semantics):
    return pltpu.CompilerParams(dimension_semantics=semantics, vmem_limit_bytes=VMEM_LIMIT_BYTES)


def _rms(x, g):
    return x * lax.rsqrt(jnp.mean(x * x, axis=-1, keepdims=True) + EPS) * g


def _full(shape):
    return pl.BlockSpec(shape, lambda *_: (0,) * len(shape))


ROW_TILES = D_MODEL // LANES
PACKED_TILES = ROW_TILES // 2
assert ROW_TILES == SUBLANES


def _load_tile_rows(ref, rows, lead=(), tiles=ROW_TILES, first=0):
    return jnp.concatenate(
        [ref[lead + (pl.ds(first * tiles + c, rows, stride=tiles), slice(None))] for c in range(tiles)], axis=1)


def _store_tile_rows(ref, value, tiles=ROW_TILES, first=0):
    rows = value.shape[0]
    for c in range(tiles):
        ref[pl.ds(first * tiles + c, rows, stride=tiles), :] = value[:, c * LANES:(c + 1) * LANES]


def _tile_row(ref, r, lead=(), tiles=ROW_TILES):
    return ref.at[lead + (pl.ds(pl.multiple_of(r * tiles, tiles), tiles),)]


def _pack_bf16_pairs(x):
    half = x.shape[1] // 2
    bits = lax.bitcast_convert_type(x.astype(BF16).astype(F32), jnp.uint32)
    return (bits[:, :half] >> 16) | bits[:, half:]


def _unpack_bf16_pairs(p):
    left = lax.bitcast_convert_type(p << 16, F32)
    right = lax.bitcast_convert_type(p & jnp.uint32(0xFFFF0000), F32)
    return jnp.concatenate([left, right], axis=1).astype(BF16)


def _in_kernel(x_ref, pos_ref, invf_ref, gmix_ref, win_ref, gql_ref, wuq_ref, wuqs_ref, gkvl_ref, wuk_ref,
               wuv_ref, gq_ref, gqs_ref, gk_ref, gks_ref, q_ref, k_ref, v_ref, y_ref):
    h = _rms(x_ref[...], gmix_ref[...]).astype(BF16)
    p = jnp.dot(h, win_ref[...], preferred_element_type=F32)
    o_kv = Q_LORA
    o_pe = o_kv + KV_LORA
    o_ps = o_pe + HEAD_PAD
    o_a = o_ps + HEAD_PAD
    o_g = o_a + CONV_WIDTH
    cq, ckv, kpe, kpe_sw = p[:, :o_kv], p[:, o_kv:o_pe], p[:, o_pe:o_ps], p[:, o_ps:o_a]
    y_ref[...] = p[:, o_a:o_g] * jax.nn.sigmoid(p[:, o_g:])

    cqn = _rms(cq, gql_ref[...]).astype(BF16)
    q = jnp.dot(cqn, wuq_ref[...], preferred_element_type=F32)
    q_sw = jnp.dot(cqn, wuqs_ref[...], preferred_element_type=F32)
    ckvn = _rms(ckv, gkvl_ref[...]).astype(BF16)
    kn = jnp.dot(ckvn, wuk_ref[...], preferred_element_type=F32)
    wide_row = lax.broadcasted_iota(jnp.int32, (N_HEADS * HEAD_PAD, 1), 0)
    ones_row = jnp.where(wide_row % HEAD_PAD == V_DIM, 1.0, 0.0)
    vt = lax.dot_general(wuv_ref[...], ckvn, (((1,), (1,)), ((), ())), preferred_element_type=F32)
    v_ref[...] = (vt + ones_row).astype(BF16)

    ang = invf_ref[...] * pos_ref[...].astype(F32)
    cos_t, sin_t = jnp.cos(ang), jnp.sin(ang)
    tab = jnp.concatenate([jnp.ones((NOPE_DIM, ang.shape[1]), F32), cos_t, cos_t, sin_t, sin_t], axis=0).T
    lane = lax.broadcasted_iota(jnp.int32, (1, HEAD_PAD), 1)
    ctab = jnp.where(lane < QK_DIM, tab, 0.0)
    stab = jnp.where((lane >= NOPE_DIM) & (lane < QK_DIM), pltpu.roll(tab, QK_DIM, 1), 0.0)
    scale = QK_DIM ** -0.5 * LOG2_E
    cq_tab = ctab * (gq_ref[...] * scale)
    sq_tab = stab * (gqs_ref[...] * scale)
    ck_tab = ctab * gk_ref[...]
    sk_tab = stab * gks_ref[...]
    k_pe = kpe * ck_tab + kpe_sw * sk_tab
    ss_pe = jnp.sum(kpe * kpe, axis=-1, keepdims=True)
    for hd in range(N_HEADS):
        sl = slice(hd * HEAD_PAD, (hd + 1) * HEAD_PAD)
        qh = q[:, sl]
        rq = lax.rsqrt(jnp.sum(qh * qh, axis=-1, keepdims=True) * (1.0 / QK_DIM) + EPS)
        q_ref[:, sl] = ((qh * cq_tab + q_sw[:, sl] * sq_tab) * rq).astype(BF16)
        kh = kn[:, sl]
        rk = lax.rsqrt((jnp.sum(kh * kh, axis=-1, keepdims=True) + ss_pe) * (1.0 / QK_DIM) + EPS)
        k_ref[:, sl] = ((kh * ck_tab + k_pe) * rk).astype(BF16)


def _input_stage(x2, pos, invf, gmix, win, gql, wuq, wuqs, gkvl, wuk, wuv, gq, gqs, gk, gks):
    t = x2.shape[0]
    wide = N_HEADS * HEAD_PAD
    row = lambda w: pl.BlockSpec((IN_ROWS, w), lambda i: (i, 0))
    consts = (invf, gmix, win, gql, wuq, wuqs, gkvl, wuk, wuv, gq, gqs, gk, gks)
    return pl.pallas_call(
        _in_kernel,
        grid=(t // IN_ROWS,),
        in_specs=[row(D_MODEL), pl.BlockSpec((1, IN_ROWS), lambda i: (0, i))] + [_full(a.shape) for a in consts],
        out_specs=[row(wide), row(wide), pl.BlockSpec((wide, IN_ROWS), lambda i: (0, i)), row(CONV_WIDTH)],
        out_shape=[jax.ShapeDtypeStruct((t, wide), BF16), jax.ShapeDtypeStruct((t, wide), BF16),
                   jax.ShapeDtypeStruct((wide, t), BF16), jax.ShapeDtypeStruct((t, CONV_WIDTH), F32)],
        compiler_params=_params("parallel"),
        name="input_stage",
    )(x2, pos, *consts)


def _attn_kernel(q_ref, k_ref, vt_ref, o_ref, *scratch):
    parts, block_parts, blocks = ATT_Q // ATT_QPART, ATT_K // ATT_QPART, ATT_Q // ATT_K
    units = [(hd, part) for hd in range(N_HEADS) for part in range(parts)]
    m_refs, acc_refs = scratch[:len(units)], scratch[len(units):]
    i = pl.program_id(1)
    dn = (((1,), (1,)), ((), ()))

    def step(r0, block, diagonal):
        todo = [u for u, (_, part) in enumerate(units) if block is None or part // block_parts == block]
        old = None if diagonal else {u: (m_refs[u][...], acc_refs[u][...]) for u in todo}
        new = {}

        def keys(part):
            return (part % block_parts + 1) * ATT_QPART if diagonal else ATT_K

        def scores(u):
            hd, part = units[u]
            sl = slice(hd * HEAD_PAD, (hd + 1) * HEAD_PAD)
            return lax.dot_general(k_ref[pl.ds(r0, keys(part)), sl],
                                   q_ref[part * ATT_QPART:(part + 1) * ATT_QPART, sl], dn,
                                   preferred_element_type=F32)

        ahead = [scores(u) for u in todo[:ATT_AHEAD]]
        for n, u in enumerate(todo):
            hd, part = units[u]
            s = ahead.pop(0)
            if n + ATT_AHEAD < len(todo):
                ahead.append(scores(todo[n + ATT_AHEAD]))
            if diagonal:
                kc = lax.broadcasted_iota(jnp.int32, s.shape, 0) // CHUNK
                qc = (lax.broadcasted_iota(jnp.int32, s.shape, 1) + part % block_parts * ATT_QPART) // CHUNK
                s = jnp.where(kc <= qc, s, -jnp.inf)
            m_new = jnp.max(s, axis=0, keepdims=True)
            if not diagonal:
                m_new = jnp.maximum(old[u][0], m_new)
            acc = jnp.dot(vt_ref[hd * HEAD_PAD:hd * HEAD_PAD + V_ROWS, pl.ds(r0, keys(part))],
                          jnp.exp2(s - m_new).astype(BF16), preferred_element_type=F32)
            if not diagonal:
                acc = jnp.exp2(old[u][0] - m_new) * old[u][1] + acc
            new[u] = (m_new, acc)
        for u in todo:
            m_refs[u][...], acc_refs[u][...] = new[u]

    tile0 = pl.multiple_of(i * ATT_Q, ATT_Q)
    for block in range(blocks):
        step(tile0 + block * ATT_K, block, True)
        for earlier in range(block):
            step(tile0 + earlier * ATT_K, block, False)

    def body(j, _):
        step(pl.multiple_of(j * ATT_K, ATT_K), None, False)
        return 0

    lax.fori_loop(0, i * blocks, body, 0)
    for pair in range(N_HEADS // 2):
        for part in range(parts):
            halves = []
            for hd in (2 * pair, 2 * pair + 1):
                acc = acc_refs[hd * parts + part][...]
                halves.append(acc[:V_DIM, :] / acc[V_DIM:V_DIM + 1, :])
            o_ref[part * ATT_QPART:(part + 1) * ATT_QPART, pair * LANES:(pair + 1) * LANES] = (
                jnp.concatenate(halves, axis=0).T.astype(BF16))


def _attention(q, k, vt, batch, seq):
    assert seq % ATT_Q == 0 and ATT_Q % ATT_K == 0 and ATT_K % ATT_QPART == 0 and ATT_QPART % CHUNK == 0
    nq = seq // ATT_Q
    chains = N_HEADS * (ATT_Q // ATT_QPART)
    wide = N_HEADS * HEAD_PAD
    qspec = pl.BlockSpec((ATT_Q, wide), lambda b, i: (b * nq + i, 0))
    return pl.pallas_call(
        _attn_kernel,
        grid=(batch, nq),
        in_specs=[qspec, pl.BlockSpec((seq, wide), lambda b, i: (b, 0)),
                  pl.BlockSpec((wide, seq), lambda b, i: (0, b))],
        out_specs=pl.BlockSpec((ATT_Q, N_HEADS * V_DIM), lambda b, i: (b * nq + i, 0)),
        out_shape=jax.ShapeDtypeStruct((q.shape[0], N_HEADS * V_DIM), BF16),
        scratch_shapes=([pltpu.VMEM((1, ATT_QPART), F32)] * chains + [pltpu.VMEM((V_ROWS, ATT_QPART), F32)] * chains),
        compiler_params=_params("parallel", "arbitrary"),
        name="attention",
    )(q, k, vt)


def _conv_kernel(y_ref, w_ref, b_ref, lng_ref, lnb_ref, og_ref, o_ref, pad_ref, phase_ref):
    seq = y_ref.shape[0]
    pad_ref[0:CONV_HALO, :] = jnp.zeros((CONV_HALO, CONV_WIDTH), F32)
    pad_ref[CONV_HALO:, :] = y_ref[...]
    first = CONV_HALO - (CONV_TAPS - 1)

    def body(i, _):
        r0 = pl.multiple_of(i * CONV_ROWS, CONV_ROWS)
        win = pad_ref[pl.ds(r0, CONV_ROWS + CONV_HALO), :]
        acc = jnp.zeros((CONV_ROWS, CONV_WIDTH), F32)
        for s in range(SUBLANES):
            offs = [o for o in range(first, first + CONV_TAPS) if o % SUBLANES == s]
            span = max(offs) - s + CONV_ROWS
            if s:
                phase_ref[s, 0:span, :] = win[s:s + span, :]
            for o in offs:
                rows = (pad_ref[pl.ds(pl.multiple_of(r0 + o, SUBLANES), CONV_ROWS), :] if s == 0
                        else phase_ref[s, o - s:o - s + CONV_ROWS, :])
                acc = acc + w_ref[o - first:o - first + 1, :] * rows
        acc = acc + b_ref[...]
        xc = acc - jnp.mean(acc, axis=-1, keepdims=True)
        ln = xc * lax.rsqrt(jnp.mean(xc * xc, axis=-1, keepdims=True) + EPS) * lng_ref[...] + lnb_ref[...]
        z = ln * jax.nn.sigmoid(ln)
        o_ref[pl.ds(r0, CONV_ROWS), :] = _rms(z, og_ref[...]).astype(BF16)
        return 0

    lax.fori_loop(0, seq // CONV_ROWS, body, 0)


def _conv_branch(y, w, b, lng, lnb, og, batch, seq):
    spec = pl.BlockSpec((seq, CONV_WIDTH), lambda bi: (bi, 0))
    return pl.pallas_call(
        _conv_kernel,
        grid=(batch,),
        in_specs=[spec, _full(w.shape), _full(b.shape), _full(lng.shape), _full(lnb.shape), _full(og.shape)],
        out_specs=spec,
        out_shape=jax.ShapeDtypeStruct(y.shape, BF16),
        scratch_shapes=[pltpu.VMEM((seq + CONV_HALO, CONV_WIDTH), F32),
                        pltpu.VMEM((SUBLANES, CONV_ROWS + CONV_HALO, CONV_WIDTH), F32)],
        compiler_params=_params("parallel"),
        name="conv_branch",
    )(y, w, b, lng, lnb, og)


def _mix_kernel(attn_ref, conv_ref, x_ref, ga_ref, wo_ref, gffn_ref, rw2_ref, rb_ref,
                tri_ref, x1_ref, h2_ref, idx_ref, gate_ref, rank_ref, cnt_ref, base_ref):
    @pl.when(pl.program_id(0) == 0)
    def _():
        base_ref[...] = jnp.zeros(base_ref.shape, F32)

    size = MIX_ROWS // MIX_GROUPS
    groups = [slice(g * size, (g + 1) * size) for g in range(MIX_GROUPS)]
    dn = (((1,), (1,)), ((), ()))

    def project(rs):
        an = _rms(attn_ref[rs, :].astype(F32), ga_ref[...])
        mixed = jnp.concatenate([an.astype(BF16), conv_ref[rs, :]], axis=1)
        return x_ref[rs, :] + jnp.dot(mixed, wo_ref[...], preferred_element_type=F32)

    def norm_and_logits(g, rs, x1):
        x1_ref[rs, :] = x1
        h2 = _rms(x1, gffn_ref[...])
        _store_tile_rows(h2_ref, _pack_bf16_pairs(h2), PACKED_TILES, first=g * size)
        hi = h2.astype(BF16)
        lo = (h2 - hi.astype(F32)).astype(BF16)
        both = lax.dot_general(rw2_ref[...], hi, dn, preferred_element_type=F32)
        return (both[:N_EXPERTS] + both[N_EXPERTS:]
                + lax.dot_general(rw2_ref[:N_EXPERTS, :], lo, dn, preferred_element_type=F32)) + rb_ref[...]

    def top_k(rs, logits):
        eidx = lax.broadcasted_iota(jnp.int32, logits.shape, 0).astype(F32)
        work = logits
        sels, vals = [], []
        for k in range(TOP_K):
            mx = jnp.max(work, axis=0, keepdims=True)
            first = jnp.min(jnp.where(work == mx, eidx, float(N_EXPERTS)), axis=0, keepdims=True)
            sel = eidx == first
            work = jnp.where(sel, -jnp.inf, work)
            sels.append(sel)
            vals.append(mx)
            idx_ref[k:k + 1, rs] = first.astype(jnp.int32)
        exps = [jnp.exp(v - vals[0]) for v in vals]
        denom = exps[0] + exps[1] + exps[2] + exps[3]
        for k in range(TOP_K):
            gate_ref[k:k + 1, rs] = exps[k] / denom
        return sels

    x1s, logits, sels = {}, {}, {}
    for s in range(MIX_GROUPS + 2):
        if s < MIX_GROUPS:
            x1s[s] = project(groups[s])
        if 0 <= s - 1 < MIX_GROUPS:
            logits[s - 1] = norm_and_logits(s - 1, groups[s - 1], x1s.pop(s - 1))
        if 0 <= s - 2 < MIX_GROUPS:
            sels[s - 2] = top_k(groups[s - 2], logits.pop(s - 2))
    sels = [sels[g] for g in range(MIX_GROUPS)]

    sels = [jnp.concatenate([s[k] for s in sels], axis=1) for k in range(TOP_K)]
    member = jnp.where(sels[0] | sels[1] | sels[2] | sels[3], 1.0, 0.0)
    blocks = [member[:, b * LANES:(b + 1) * LANES] for b in range(MIX_ROWS // LANES)]
    inner = jnp.dot(jnp.concatenate(blocks, axis=0).astype(BF16), tri_ref[...], preferred_element_type=F32)
    offset = base_ref[:, 0:1]
    pieces = []
    for b, blk in enumerate(blocks):
        pieces.append(inner[b * N_EXPERTS:(b + 1) * N_EXPERTS, :] + offset)
        offset = offset + jnp.sum(blk, axis=1, keepdims=True)
    posn = jnp.concatenate(pieces, axis=1)
    for k in range(TOP_K):
        rank_ref[k:k + 1, :] = jnp.sum(jnp.where(sels[k], posn, 0.0), axis=0, keepdims=True).astype(jnp.int32)
    base_ref[...] = jnp.broadcast_to(offset, base_ref.shape)
    cnt_ref[...] = base_ref[...].astype(jnp.int32)


def _mix_stage(attn, conv, x2, ga, wo, gffn, rw2, rb, tri):
    t = x2.shape[0]
    proj = lambda i: (i, 0)
    row = lambda w: pl.BlockSpec((MIX_ROWS, w), proj)
    col = pl.BlockSpec((TOP_K, MIX_ROWS), lambda i: (0, i))
    return pl.pallas_call(
        _mix_kernel,
        grid=(t // MIX_ROWS,),
        in_specs=[row(attn.shape[1]), row(CONV_WIDTH), row(D_MODEL), _full(ga.shape), _full(wo.shape),
                  _full(gffn.shape), _full(rw2.shape), _full(rb.shape), _full(tri.shape)],
        out_specs=[row(D_MODEL), pl.BlockSpec((MIX_ROWS * PACKED_TILES, LANES), proj), col, col, col,
                   _full((N_EXPERTS, LANES))],
        out_shape=[jax.ShapeDtypeStruct((t, D_MODEL), F32),
                   jax.ShapeDtypeStruct((t * PACKED_TILES, LANES), jnp.uint32),
                   jax.ShapeDtypeStruct((TOP_K, t), jnp.int32), jax.ShapeDtypeStruct((TOP_K, t), F32),
                   jax.ShapeDtypeStruct((TOP_K, t), jnp.int32),
                   jax.ShapeDtypeStruct((N_EXPERTS, LANES), jnp.int32)],
        scratch_shapes=[pltpu.VMEM((N_EXPERTS, LANES), F32)],
        compiler_params=_params("arbitrary"),
        name="mix_router",
    )(attn, conv, x2, ga, wo, gffn, rw2, rb, tri)


def _dispatch_kernel(ends_ref, padded_ref, dest_ref, h_ref, xs_ref, zero_ref, sem, zsem):
    tiles = PACKED_TILES
    rows = h_ref.shape[0] // tiles
    block = MOE_STEP * tiles
    n_blocks = xs_ref.shape[0] // block

    @pl.when(pl.program_id(0) == 0)
    def _():
        zero_ref[...] = jnp.zeros(zero_ref.shape, zero_ref.dtype)

        def zero_copy(r0):
            return pltpu.make_async_copy(
                zero_ref, xs_ref.at[pl.ds(pl.multiple_of(r0 * tiles, block), block)], zsem)

        for start in (True, False):
            for e in range(N_EXPERTS):
                @pl.when(padded_ref[e] > 0)
                def _():
                    cp = zero_copy(ends_ref[e] - MOE_STEP)
                    cp.start() if start else cp.wait()

            def tail(b, _):
                cp = zero_copy(b * MOE_STEP)
                cp.start() if start else cp.wait()
                return 0

            lax.fori_loop(ends_ref[N_EXPERTS - 1] // MOE_STEP, n_blocks, tail, 0)

    def issue(t, _):
        for k in range(TOP_K):
            pltpu.make_async_copy(_tile_row(h_ref, t, tiles=tiles), _tile_row(xs_ref, dest_ref[k, t], tiles=tiles),
                                  sem).start(priority=k % DMA_PRIORITIES)
        return 0

    lax.fori_loop(0, rows, issue, 0, unroll=ISSUE_UNROLL)
    for _ in range(TOP_K):
        pltpu.make_async_copy(h_ref, xs_ref.at[pl.ds(0, rows * tiles)], sem).wait()


def _dispatch(ends, padded, dest, h2, n_blocks):
    t = h2.shape[0] // PACKED_TILES
    grid_spec = pltpu.PrefetchScalarGridSpec(
        num_scalar_prefetch=2,
        grid=(t // DISPATCH_ROWS,),
        in_specs=[pl.BlockSpec((TOP_K, DISPATCH_ROWS), lambda i, *_: (0, i), memory_space=pltpu.SMEM),
                  pl.BlockSpec((DISPATCH_ROWS * PACKED_TILES, LANES), lambda i, *_: (i, 0))],
        out_specs=pl.BlockSpec(memory_space=pl.ANY),
        scratch_shapes=[pltpu.VMEM((MOE_STEP * PACKED_TILES, LANES), h2.dtype), pltpu.SemaphoreType.DMA(()),
                        pltpu.SemaphoreType.DMA(())],
    )
    return pl.pallas_call(
        _dispatch_kernel,
        grid_spec=grid_spec,
        out_shape=jax.ShapeDtypeStruct((n_blocks * MOE_STEP * PACKED_TILES, LANES), h2.dtype),
        compiler_params=_params("arbitrary"),
        name="dispatch_rows",
    )(ends, padded, dest, h2)


def _moe_kernel(be_ref, nact_ref, passes_ref, xs_ref, wgu_ref, bgu_ref, wd_ref, bd_ref, ys_ref, wgu_bf, wd_bf):
    i = pl.program_id(0)
    half = MOE_ROWS // 2

    @pl.when((passes_ref[i] > 0) & ((i == 0) | (be_ref[i] != be_ref[jnp.maximum(i - 1, 0)])))
    def _():
        wgu_bf[...] = wgu_ref[0].astype(BF16)
        wd_bf[...] = wd_ref[0].astype(BF16)

    def blank(first, rows):
        ys_ref[first * ROW_TILES:(first + rows) * ROW_TILES, :] = jnp.zeros((rows * ROW_TILES, LANES), F32)

    def mlp(first, rows):
        x = _unpack_bf16_pairs(_load_tile_rows(xs_ref, rows, tiles=PACKED_TILES, first=first))
        gu = jnp.dot(x, wgu_bf[...], preferred_element_type=F32) + bgu_ref[0]
        gate = jnp.minimum(gu[:, :D_FF], SWIGLU_LIMIT)
        up = jnp.clip(gu[:, D_FF:], -SWIGLU_LIMIT, SWIGLU_LIMIT)
        mid = (up + 1.0) * (gate * jax.nn.sigmoid(gate * SWIGLU_ALPHA))
        y = jnp.dot(mid.astype(BF16), wd_bf[...], preferred_element_type=F32) + bd_ref[0]
        _store_tile_rows(ys_ref, y, first=first)

    for p in range(MOE_STEP // MOE_ROWS):
        first = p * MOE_ROWS

        @pl.when(passes_ref[i] >= 2 * p + 2)
        def _():
            mlp(first, MOE_ROWS)

        @pl.when(passes_ref[i] == 2 * p + 1)
        def _():
            mlp(first, half)
            blank(first + half, half)

        @pl.when(passes_ref[i] <= 2 * p)
        def _():
            blank(first, MOE_ROWS)


def _moe(block_expert, n_active, passes, xs, wgu, bgu, wd, bd):
    block = MOE_STEP * ROW_TILES
    n_blocks = xs.shape[0] // (MOE_STEP * PACKED_TILES)
    grid_spec = pltpu.PrefetchScalarGridSpec(
        num_scalar_prefetch=3,
        grid=(n_blocks,),
        in_specs=[
            pl.BlockSpec((MOE_STEP * PACKED_TILES, LANES), lambda i, be, na, ps: (jnp.minimum(i, na[0] - 1), 0)),
            pl.BlockSpec((1, D_MODEL, 2 * D_FF), lambda i, be, na, ps: (be[i], 0, 0)),
            pl.BlockSpec((1, 1, 2 * D_FF), lambda i, be, na, ps: (be[i], 0, 0)),
            pl.BlockSpec((1, D_FF, D_MODEL), lambda i, be, na, ps: (be[i], 0, 0)),
            pl.BlockSpec((1, 1, D_MODEL), lambda i, be, na, ps: (be[i], 0, 0)),
        ],
        out_specs=pl.BlockSpec((block, LANES), lambda i, be, na, ps: (i, 0)),
        scratch_shapes=[pltpu.VMEM((D_MODEL, 2 * D_FF), BF16), pltpu.VMEM((D_FF, D_MODEL), BF16)],
    )
    return pl.pallas_call(
        _moe_kernel,
        grid_spec=grid_spec,
        out_shape=jax.ShapeDtypeStruct((n_blocks * block, LANES), F32),
        compiler_params=_params("arbitrary"),
        name="expert_mlp",
    )(block_expert, n_active, passes, xs, wgu, bgu, wd, bd)


def _combine_kernel(dest_ref, dest_next_ref, x1_ref, gate_ref, ys_ref, o_ref, buf_ref, sems):
    i, n = pl.program_id(0), pl.num_programs(0)
    rows = x1_ref.shape[0]

    def gather(idx_ref, slot):
        def issue(t, _):
            for k in range(TOP_K):
                pltpu.make_async_copy(_tile_row(ys_ref, idx_ref[k * rows + t]), _tile_row(buf_ref, t, (slot, k)),
                                      sems.at[slot]).start(priority=k % DMA_PRIORITIES)
            return 0

        lax.fori_loop(0, rows, issue, 0, unroll=ISSUE_UNROLL)

    @pl.when(i == 0)
    def _():
        gather(dest_ref, 0)

    @pl.when(i + 1 < n)
    def _():
        gather(dest_next_ref, (i + 1) % 2)

    slot = i % 2
    for k in range(TOP_K):
        pltpu.make_async_copy(ys_ref.at[pl.ds(0, rows * ROW_TILES)], buf_ref.at[slot, k], sems.at[slot]).wait()
    acc = x1_ref[...]
    for k in range(TOP_K):
        acc = acc + gate_ref[:, k:k + 1] * _load_tile_rows(buf_ref, rows, (slot, k))
    o_ref[...] = acc


def _combine(dest, x1, gate_rows, ys):
    t = x1.shape[0]
    steps = t // COMBINE_ROWS
    row = lambda w: pl.BlockSpec((COMBINE_ROWS, w), lambda i: (i, 0))
    dest = dest.reshape(TOP_K, steps, COMBINE_ROWS).transpose(1, 0, 2).reshape(-1)
    slots = lambda nxt: pl.BlockSpec((TOP_K * COMBINE_ROWS,), lambda i: (jnp.minimum(i + nxt, steps - 1),),
                                     memory_space=pltpu.SMEM)
    return pl.pallas_call(
        _combine_kernel,
        grid=(steps,),
        in_specs=[slots(0), slots(1), row(D_MODEL), row(TOP_K), pl.BlockSpec(memory_space=pl.ANY)],
        out_specs=row(D_MODEL),
        out_shape=jax.ShapeDtypeStruct(x1.shape, F32),
        scratch_shapes=[pltpu.VMEM((2, TOP_K, COMBINE_ROWS * ROW_TILES, LANES), F32),
                        pltpu.SemaphoreType.DMA((2,))],
        compiler_params=_params("arbitrary"),
        name="combine_rows",
    )(dest, dest, x1, gate_rows, ys)


def _head_blocks(w, width):
    r = w.shape[0]
    w = w.reshape(r, N_HEADS, width)
    return jnp.pad(w, ((0, 0), (0, 0), (0, HEAD_PAD - width))).reshape(r, N_HEADS * HEAD_PAD)


def _lane_row(v, offset=0):
    return jnp.pad(v, (offset, HEAD_PAD - offset - v.shape[0])).reshape(1, HEAD_PAD)


def _swap_rope(w, sign):
    lo, hi = w[..., NOPE_DIM:NOPE_DIM + HALF_ROPE], w[..., NOPE_DIM + HALF_ROPE:QK_DIM]
    return jnp.concatenate([jnp.zeros_like(w[..., :NOPE_DIM]), sign * hi, lo], axis=-1)


def kernel(x, positions, norm_mix_g, w_in, q_latent_g, w_uq, kv_latent_g, w_ukv, q_head_g, k_head_g, conv_dw_w, conv_dw_b, conv_ln_g, conv_ln_b, attn_out_g, conv_out_g, w_out, norm_ffn_g, router_w, router_b, w_gate_up, b_gate_up, w_down, b_down):
    batch, seq, d_model = x.shape
    t = batch * seq
    assert d_model == D_MODEL and seq % max(ATT_Q, CONV_ROWS) == 0
    assert all(t % rows == 0 for rows in (IN_ROWS, MIX_ROWS, DISPATCH_ROWS, COMBINE_ROWS))
    depth = norm_mix_g.shape[0]
    x2 = x.reshape(t, D_MODEL)
    pos = positions.reshape(1, t)
    invf = (1.0 / (ROPE_THETA ** (jnp.arange(0, ROPE_DIM, 2, dtype=F32) / ROPE_DIM))).reshape(HALF_ROPE, 1)
    tri = jnp.triu(jnp.ones((LANES, LANES), BF16), 1)
    o_kv = Q_LORA
    o_pe = o_kv + KV_LORA
    o_u = o_pe + ROPE_DIM

    for l in range(depth):
        wi = w_in[l]
        w_pe = wi[:, o_pe:o_u]
        w_pe_sw = jnp.concatenate([-w_pe[:, HALF_ROPE:], w_pe[:, :HALF_ROPE]], axis=1)
        pe_block = lambda w: jnp.pad(w, ((0, 0), (NOPE_DIM, HEAD_PAD - QK_DIM)))
        win = jnp.concatenate([wi[:, :o_pe], pe_block(w_pe), pe_block(w_pe_sw), wi[:, o_u:]], axis=1).astype(BF16)
        wq = w_uq[l].reshape(Q_LORA, N_HEADS, QK_DIM)
        wuq = _head_blocks(w_uq[l], QK_DIM).astype(BF16)
        wuqs = _head_blocks(_swap_rope(wq, -1.0).reshape(Q_LORA, -1), QK_DIM).astype(BF16)
        wkv = w_ukv[l].reshape(KV_LORA, N_HEADS, NOPE_DIM + V_DIM)
        wuk = _head_blocks(wkv[:, :, :NOPE_DIM].reshape(KV_LORA, -1), NOPE_DIM).astype(BF16)
        wuv = _head_blocks(wkv[:, :, NOPE_DIM:].reshape(KV_LORA, -1), V_DIM).T.astype(BF16)
        ga = attn_out_g[l].reshape(1, -1)
        wo = w_out[l].astype(BF16)
        rwt = router_w[l].T
        rwh = rwt.astype(BF16)
        rw2 = jnp.concatenate([rwh, (rwt - rwh.astype(F32)).astype(BF16)], axis=0)

        q, k, v, y = _input_stage(
            x2, pos, invf, norm_mix_g[l].reshape(1, -1), win, q_latent_g[l].reshape(1, -1), wuq, wuqs,
            kv_latent_g[l].reshape(1, -1), wuk, wuv, _lane_row(q_head_g[l]),
            _lane_row(_swap_rope(q_head_g[l], 1.0)), _lane_row(k_head_g[l]), _lane_row(_swap_rope(k_head_g[l], 1.0)))
        attn = _attention(q, k, v, batch, seq)
        conv = _conv_branch(y, conv_dw_w[l], conv_dw_b[l].reshape(1, -1), conv_ln_g[l].reshape(1, -1),
                            conv_ln_b[l].reshape(1, -1), conv_out_g[l].reshape(1, -1), batch, seq)
        x1, h2, idx, gate, rank, cnt = _mix_stage(
            attn, conv, x2, ga, wo, norm_ffn_g[l].reshape(1, -1), rw2, router_b[l].reshape(-1, 1), tri)

        counts = cnt[:, 0]
        padded = (counts + MOE_STEP - 1) // MOE_STEP * MOE_STEP
        ends = jnp.cumsum(padded)
        starts = ends - padded
        experts = jnp.arange(N_EXPERTS, dtype=jnp.int32)
        dest = rank + jnp.sum(jnp.where(idx[None] == experts[:, None, None], starts[:, None, None], 0), axis=0)
        n_blocks = (t * TOP_K + N_EXPERTS * (MOE_STEP - 1)) // MOE_STEP
        n_active = (ends[-1] // MOE_STEP).astype(jnp.int32)
        step = jnp.arange(n_blocks, dtype=jnp.int32)
        blk = jnp.minimum(step, n_active - 1)
        be = jnp.minimum(jnp.sum((ends[None, :] <= (blk * MOE_STEP)[:, None]).astype(jnp.int32), axis=1),
                         N_EXPERTS - 1)
        real = jnp.clip((starts + counts)[be] - step * MOE_STEP, 0, MOE_STEP)
        passes = jnp.where(step < n_active, -(-real // (MOE_ROWS // 2)), 0).astype(jnp.int32)

        xs = _dispatch(ends.astype(jnp.int32), padded.astype(jnp.int32), dest, h2, n_blocks)
        ys = _moe(be, n_active.reshape(1), passes, xs, w_gate_up[l], b_gate_up[l].reshape(N_EXPERTS, 1, -1),
                  w_down[l], b_down[l].reshape(N_EXPERTS, 1, -1))
        x2 = _combine(dest, x1, gate.T, ys)
    return x2.reshape(batch, seq, D_MODEL)
```

```python
import jax
import jax.numpy as jnp
from jax import lax
from jax.experimental import pallas as pl
from jax.experimental.pallas import tpu as pltpu

D_MODEL = 1024
N_HEADS = 8
NOPE_DIM = 64
ROPE_DIM = 32
QK_DIM = NOPE_DIM + ROPE_DIM
V_DIM = 64
Q_LORA = 384
KV_LORA = 128
CONV_WIDTH = 512
CONV_TAPS = 31
N_EXPERTS = 32
TOP_K = 4
D_FF = 1024
CHUNK = 64
ROPE_THETA = 10000.0
EPS = 1e-6
SWIGLU_ALPHA = 1.702
SWIGLU_LIMIT = 7.0
LOG2_E = 1.4426950408889634

LANES = 128
SUBLANES = 8
HEAD_PAD = LANES
HALF_ROPE = ROPE_DIM // 2
V_ROWS = V_DIM + 2 * SUBLANES
VMEM_LIMIT_BYTES = 56 * 1024 * 1024

IN_ROWS = 1024
ATT_Q = 1024
ATT_K = 512
ATT_QPART = 256
ATT_AHEAD = 4
CONV_ROWS = 256
CONV_HALO = 32
MIX_ROWS = 1024
MIX_GROUPS = 2
DISPATCH_ROWS = 512
MOE_ROWS = 512
MOE_STEP = 2 * MOE_ROWS
COMBINE_ROWS = 512
COMBINE_SLOTS = 3
ISSUE_UNROLL = 8
DMA_PRIORITIES = 2

F32 = jnp.float32
BF16 = jnp.bfloat16


def _params(*semantics):
    return pltpu.CompilerParams(dimension_semantics=semantics, vmem_limit_bytes=VMEM_LIMIT_BYTES)


def _rms(x, g):
    return x * lax.rsqrt(jnp.mean(x * x, axis=-1, keepdims=True) + EPS) * g


def _full(shape):
    return pl.BlockSpec(shape, lambda *_: (0,) * len(shape))


ROW_TILES = D_MODEL // LANES
PACKED_TILES = ROW_TILES // 2
assert ROW_TILES == SUBLANES


def _load_tile_rows(ref, rows, lead=(), tiles=ROW_TILES, first=0):
    return jnp.concatenate(
        [ref[lead + (pl.ds(first * tiles + c, rows, stride=tiles), slice(None))] for c in range(tiles)], axis=1)


def _store_tile_rows(ref, value, tiles=ROW_TILES, first=0):
    rows = value.shape[0]
    for c in range(tiles):
        ref[pl.ds(first * tiles + c, rows, stride=tiles), :] = value[:, c * LANES:(c + 1) * LANES]


def _tile_row(ref, r, lead=(), tiles=ROW_TILES):
    return ref.at[lead + (pl.ds(pl.multiple_of(r * tiles, tiles), tiles),)]


def _pack_bf16_pairs(x):
    half = x.shape[1] // 2
    bits = lax.bitcast_convert_type(x.astype(BF16).astype(F32), jnp.uint32)
    return (bits[:, :half] >> 16) | bits[:, half:]


def _unpack_bf16_pairs(p):
    left = lax.bitcast_convert_type(p << 16, F32)
    right = lax.bitcast_convert_type(p & jnp.uint32(0xFFFF0000), F32)
    return jnp.concatenate([left, right], axis=1).astype(BF16)


def _in_kernel(x_ref, pos_ref, invf_ref, gmix_ref, win_ref, gql_ref, wuq_ref, wuqs_ref, gkvl_ref, wuk_ref,
               wuv_ref, gq_ref, gqs_ref, gk_ref, gks_ref, q_ref, k_ref, v_ref, y_ref):
    h = _rms(x_ref[...], gmix_ref[...]).astype(BF16)
    p = jnp.dot(h, win_ref[...], preferred_element_type=F32)
    o_kv = Q_LORA
    o_pe = o_kv + KV_LORA
    o_ps = o_pe + HEAD_PAD
    o_a = o_ps + HEAD_PAD
    o_g = o_a + CONV_WIDTH
    cq, ckv, kpe, kpe_sw = p[:, :o_kv], p[:, o_kv:o_pe], p[:, o_pe:o_ps], p[:, o_ps:o_a]
    y_ref[...] = p[:, o_a:o_g] * jax.nn.sigmoid(p[:, o_g:])

    cqn = _rms(cq, gql_ref[...]).astype(BF16)
    q = jnp.dot(cqn, wuq_ref[...], preferred_element_type=F32)
    q_sw = jnp.dot(cqn, wuqs_ref[...], preferred_element_type=F32)
    ckvn = _rms(ckv, gkvl_ref[...]).astype(BF16)
    kn = jnp.dot(ckvn, wuk_ref[...], preferred_element_type=F32)
    wide_row = lax.broadcasted_iota(jnp.int32, (N_HEADS * HEAD_PAD, 1), 0)
    ones_row = jnp.where(wide_row % HEAD_PAD == V_DIM, 1.0, 0.0)
    vt = lax.dot_general(wuv_ref[...], ckvn, (((1,), (1,)), ((), ())), preferred_element_type=F32)
    v_ref[...] = (vt + ones_row).astype(BF16)

    ang = invf_ref[...] * pos_ref[...].astype(F32)
    cos_t, sin_t = jnp.cos(ang), jnp.sin(ang)
    tab = jnp.concatenate([jnp.ones((NOPE_DIM, ang.shape[1]), F32), cos_t, cos_t, sin_t, sin_t], axis=0).T
    lane = lax.broadcasted_iota(jnp.int32, (1, HEAD_PAD), 1)
    ctab = jnp.where(lane < QK_DIM, tab, 0.0)
    stab = jnp.where((lane >= NOPE_DIM) & (lane < QK_DIM), pltpu.roll(tab, QK_DIM, 1), 0.0)
    scale = QK_DIM ** -0.5 * LOG2_E
    cq_tab = ctab * (gq_ref[...] * scale)
    sq_tab = stab * (gqs_ref[...] * scale)
    ck_tab = ctab * gk_ref[...]
    sk_tab = stab * gks_ref[...]
    k_pe = kpe * ck_tab + kpe_sw * sk_tab
    ss_pe = jnp.sum(kpe * kpe, axis=-1, keepdims=True)
    for hd in range(N_HEADS):
        sl = slice(hd * HEAD_PAD, (hd + 1) * HEAD_PAD)
        qh = q[:, sl]
        rq = lax.rsqrt(jnp.sum(qh * qh, axis=-1, keepdims=True) * (1.0 / QK_DIM) + EPS)
        q_ref[:, sl] = ((qh * cq_tab + q_sw[:, sl] * sq_tab) * rq).astype(BF16)
        kh = kn[:, sl]
        rk = lax.rsqrt((jnp.sum(kh * kh, axis=-1, keepdims=True) + ss_pe) * (1.0 / QK_DIM) + EPS)
        k_ref[:, sl] = ((kh * ck_tab + k_pe) * rk).astype(BF16)


def _input_stage(x2, pos, invf, gmix, win, gql, wuq, wuqs, gkvl, wuk, wuv, gq, gqs, gk, gks):
    t = x2.shape[0]
    wide = N_HEADS * HEAD_PAD
    row = lambda w: pl.BlockSpec((IN_ROWS, w), lambda i: (i, 0))
    consts = (invf, gmix, win, gql, wuq, wuqs, gkvl, wuk, wuv, gq, gqs, gk, gks)
    return pl.pallas_call(
        _in_kernel,
        grid=(t // IN_ROWS,),
        in_specs=[row(D_MODEL), pl.BlockSpec((1, IN_ROWS), lambda i: (0, i))] + [_full(a.shape) for a in consts],
        out_specs=[row(wide), row(wide), pl.BlockSpec((wide, IN_ROWS), lambda i: (0, i)), row(CONV_WIDTH)],
        out_shape=[jax.ShapeDtypeStruct((t, wide), BF16), jax.ShapeDtypeStruct((t, wide), BF16),
                   jax.ShapeDtypeStruct((wide, t), BF16), jax.ShapeDtypeStruct((t, CONV_WIDTH), F32)],
        compiler_params=_params("parallel"),
        name="input_stage",
    )(x2, pos, *consts)


def _attn_kernel(q_ref, k_ref, vt_ref, o_ref, *scratch):
    parts, block_parts, blocks = ATT_Q // ATT_QPART, ATT_K // ATT_QPART, ATT_Q // ATT_K
    units = [(hd, part) for hd in range(N_HEADS) for part in range(parts)]
    m_refs, acc_refs = scratch[:len(units)], scratch[len(units):]
    i = pl.program_id(1)
    dn = (((1,), (1,)), ((), ()))

    def step(r0, block, diagonal):
        todo = [u for u, (_, part) in enumerate(units) if block is None or part // block_parts == block]
        old = None if diagonal else {u: (m_refs[u][...], acc_refs[u][...]) for u in todo}
        new = {}

        def keys(part):
            return (part % block_parts + 1) * ATT_QPART if diagonal else ATT_K

        def scores(u):
            hd, part = units[u]
            sl = slice(hd * HEAD_PAD, (hd + 1) * HEAD_PAD)
            return lax.dot_general(k_ref[pl.ds(r0, keys(part)), sl],
                                   q_ref[part * ATT_QPART:(part + 1) * ATT_QPART, sl], dn,
                                   preferred_element_type=F32)

        ahead = [scores(u) for u in todo[:ATT_AHEAD]]
        for n, u in enumerate(todo):
            hd, part = units[u]
            s = ahead.pop(0)
            if n + ATT_AHEAD < len(todo):
                ahead.append(scores(todo[n + ATT_AHEAD]))
            if diagonal:
                kc = lax.broadcasted_iota(jnp.int32, s.shape, 0) // CHUNK
                qc = (lax.broadcasted_iota(jnp.int32, s.shape, 1) + part % block_parts * ATT_QPART) // CHUNK
                s = jnp.where(kc <= qc, s, -jnp.inf)
            m_new = jnp.max(s, axis=0, keepdims=True)
            if not diagonal:
                m_new = jnp.maximum(old[u][0], m_new)
            acc = jnp.dot(vt_ref[hd * HEAD_PAD:hd * HEAD_PAD + V_ROWS, pl.ds(r0, keys(part))],
                          jnp.exp2(s - m_new).astype(BF16), preferred_element_type=F32)
            if not diagonal:
                acc = jnp.exp2(old[u][0] - m_new) * old[u][1] + acc
            new[u] = (m_new, acc)
        for u in todo:
            m_refs[u][...], acc_refs[u][...] = new[u]

    tile0 = pl.multiple_of(i * ATT_Q, ATT_Q)
    for block in range(blocks):
        step(tile0 + block * ATT_K, block, True)
        for earlier in range(block):
            step(tile0 + earlier * ATT_K, block, False)

    def body(j, _):
        step(pl.multiple_of(j * ATT_K, ATT_K), None, False)
        return 0

    lax.fori_loop(0, i * blocks, body, 0)
    for pair in range(N_HEADS // 2):
        for part in range(parts):
            halves = []
            for hd in (2 * pair, 2 * pair + 1):
                acc = acc_refs[hd * parts + part][...]
                halves.append(acc[:V_DIM, :] / acc[V_DIM:V_DIM + 1, :])
            o_ref[part * ATT_QPART:(part + 1) * ATT_QPART, pair * LANES:(pair + 1) * LANES] = (
                jnp.concatenate(halves, axis=0).T.astype(BF16))


def _attention(q, k, vt, batch, seq):
    assert seq % ATT_Q == 0 and ATT_Q % ATT_K == 0 and ATT_K % ATT_QPART == 0 and ATT_QPART % CHUNK == 0
    nq = seq // ATT_Q
    chains = N_HEADS * (ATT_Q // ATT_QPART)
    wide = N_HEADS * HEAD_PAD
    qspec = pl.BlockSpec((ATT_Q, wide), lambda b, i: (b * nq + i, 0))
    return pl.pallas_call(
        _attn_kernel,
        grid=(batch, nq),
        in_specs=[qspec, pl.BlockSpec((seq, wide), lambda b, i: (b, 0)),
                  pl.BlockSpec((wide, seq), lambda b, i: (0, b))],
        out_specs=pl.BlockSpec((ATT_Q, N_HEADS * V_DIM), lambda b, i: (b * nq + i, 0)),
        out_shape=jax.ShapeDtypeStruct((q.shape[0], N_HEADS * V_DIM), BF16),
        scratch_shapes=([pltpu.VMEM((1, ATT_QPART), F32)] * chains + [pltpu.VMEM((V_ROWS, ATT_QPART), F32)] * chains),
        compiler_params=_params("parallel", "arbitrary"),
        name="attention",
    )(q, k, vt)


def _conv_kernel(y_ref, w_ref, b_ref, lng_ref, lnb_ref, og_ref, o_ref, pad_ref, phase_ref):
    seq = y_ref.shape[0]
    pad_ref[0:CONV_HALO, :] = jnp.zeros((CONV_HALO, CONV_WIDTH), F32)
    pad_ref[CONV_HALO:, :] = y_ref[...]
    first = CONV_HALO - (CONV_TAPS - 1)

    def body(i, _):
        r0 = pl.multiple_of(i * CONV_ROWS, CONV_ROWS)
        win = pad_ref[pl.ds(r0, CONV_ROWS + CONV_HALO), :]
        acc = jnp.zeros((CONV_ROWS, CONV_WIDTH), F32)
        for s in range(SUBLANES):
            offs = [o for o in range(first, first + CONV_TAPS) if o % SUBLANES == s]
            span = max(offs) - s + CONV_ROWS
            if s:
                phase_ref[s, 0:span, :] = win[s:s + span, :]
            for o in offs:
                rows = (pad_ref[pl.ds(pl.multiple_of(r0 + o, SUBLANES), CONV_ROWS), :] if s == 0
                        else phase_ref[s, o - s:o - s + CONV_ROWS, :])
                acc = acc + w_ref[o - first:o - first + 1, :] * rows
        acc = acc + b_ref[...]
        xc = acc - jnp.mean(acc, axis=-1, keepdims=True)
        ln = xc * lax.rsqrt(jnp.mean(xc * xc, axis=-1, keepdims=True) + EPS) * lng_ref[...] + lnb_ref[...]
        z = ln * jax.nn.sigmoid(ln)
        o_ref[pl.ds(r0, CONV_ROWS), :] = _rms(z, og_ref[...]).astype(BF16)
        return 0

    lax.fori_loop(0, seq // CONV_ROWS, body, 0)


def _conv_branch(y, w, b, lng, lnb, og, batch, seq):
    spec = pl.BlockSpec((seq, CONV_WIDTH), lambda bi: (bi, 0))
    return pl.pallas_call(
        _conv_kernel,
        grid=(batch,),
        in_specs=[spec, _full(w.shape), _full(b.shape), _full(lng.shape), _full(lnb.shape), _full(og.shape)],
        out_specs=spec,
        out_shape=jax.ShapeDtypeStruct(y.shape, BF16),
        scratch_shapes=[pltpu.VMEM((seq + CONV_HALO, CONV_WIDTH), F32),
                        pltpu.VMEM((SUBLANES, CONV_ROWS + CONV_HALO, CONV_WIDTH), F32)],
        compiler_params=_params("parallel"),
        name="conv_branch",
    )(y, w, b, lng, lnb, og)


def _mix_kernel(attn_ref, conv_ref, x_ref, ga_ref, wo_ref, gffn_ref, rw2_ref, rb_ref,
                tri_ref, x1_ref, h2_ref, idx_ref, gate_ref, rank_ref, cnt_ref, base_ref):
    @pl.when(pl.program_id(0) == 0)
    def _():
        base_ref[...] = jnp.zeros(base_ref.shape, F32)

    size = MIX_ROWS // MIX_GROUPS
    groups = [slice(g * size, (g + 1) * size) for g in range(MIX_GROUPS)]
    dn = (((1,), (1,)), ((), ()))

    def project(rs):
        an = _rms(attn_ref[rs, :].astype(F32), ga_ref[...])
        mixed = jnp.concatenate([an.astype(BF16), conv_ref[rs, :]], axis=1)
        return x_ref[rs, :] + jnp.dot(mixed, wo_ref[...], preferred_element_type=F32)

    def norm_and_logits(g, rs, x1):
        x1_ref[rs, :] = x1
        h2 = _rms(x1, gffn_ref[...])
        _store_tile_rows(h2_ref, _pack_bf16_pairs(h2), PACKED_TILES, first=g * size)
        hi = h2.astype(BF16)
        lo = (h2 - hi.astype(F32)).astype(BF16)
        both = lax.dot_general(rw2_ref[...], hi, dn, preferred_element_type=F32)
        return (both[:N_EXPERTS] + both[N_EXPERTS:]
                + lax.dot_general(rw2_ref[:N_EXPERTS, :], lo, dn, preferred_element_type=F32)) + rb_ref[...]

    def top_k(rs, logits):
        eidx = lax.broadcasted_iota(jnp.int32, logits.shape, 0).astype(F32)
        work = logits
        sels, vals = [], []
        for k in range(TOP_K):
            mx = jnp.max(work, axis=0, keepdims=True)
            first = jnp.min(jnp.where(work == mx, eidx, float(N_EXPERTS)), axis=0, keepdims=True)
            sel = eidx == first
            work = jnp.where(sel, -jnp.inf, work)
            sels.append(sel)
            vals.append(mx)
            idx_ref[k:k + 1, rs] = first.astype(jnp.int32)
        exps = [jnp.exp(v - vals[0]) for v in vals]
        denom = exps[0] + exps[1] + exps[2] + exps[3]
        for k in range(TOP_K):
            gate_ref[k:k + 1, rs] = exps[k] / denom
        return sels

    x1s, logits, sels = {}, {}, {}
    for s in range(MIX_GROUPS + 2):
        if s < MIX_GROUPS:
            x1s[s] = project(groups[s])
        if 0 <= s - 1 < MIX_GROUPS:
            logits[s - 1] = norm_and_logits(s - 1, groups[s - 1], x1s.pop(s - 1))
        if 0 <= s - 2 < MIX_GROUPS:
            sels[s - 2] = top_k(groups[s - 2], logits.pop(s - 2))
    sels = [sels[g] for g in range(MIX_GROUPS)]

    sels = [jnp.concatenate([s[k] for s in sels], axis=1) for k in range(TOP_K)]
    member = jnp.where(sels[0] | sels[1] | sels[2] | sels[3], 1.0, 0.0)
    blocks = [member[:, b * LANES:(b + 1) * LANES] for b in range(MIX_ROWS // LANES)]
    inner = jnp.dot(jnp.concatenate(blocks, axis=0).astype(BF16), tri_ref[...], preferred_element_type=F32)
    offset = base_ref[:, 0:1]
    pieces = []
    for b, blk in enumerate(blocks):
        pieces.append(inner[b * N_EXPERTS:(b + 1) * N_EXPERTS, :] + offset)
        offset = offset + jnp.sum(blk, axis=1, keepdims=True)
    posn = jnp.concatenate(pieces, axis=1)
    for k in range(TOP_K):
        rank_ref[k:k + 1, :] = jnp.sum(jnp.where(sels[k], posn, 0.0), axis=0, keepdims=True).astype(jnp.int32)
    base_ref[...] = jnp.broadcast_to(offset, base_ref.shape)
    cnt_ref[...] = base_ref[...].astype(jnp.int32)


def _mix_stage(attn, conv, x2, ga, wo, gffn, rw2, rb, tri):
    t = x2.shape[0]
    proj = lambda i: (i, 0)
    row = lambda w: pl.BlockSpec((MIX_ROWS, w), proj)
    col = pl.BlockSpec((TOP_K, MIX_ROWS), lambda i: (0, i))
    return pl.pallas_call(
        _mix_kernel,
        grid=(t // MIX_ROWS,),
        in_specs=[row(attn.shape[1]), row(CONV_WIDTH), row(D_MODEL), _full(ga.shape), _full(wo.shape),
                  _full(gffn.shape), _full(rw2.shape), _full(rb.shape), _full(tri.shape)],
        out_specs=[row(D_MODEL), pl.BlockSpec((MIX_ROWS * PACKED_TILES, LANES), proj), col, col, col,
                   _full((N_EXPERTS, LANES))],
        out_shape=[jax.ShapeDtypeStruct((t, D_MODEL), F32),
                   jax.ShapeDtypeStruct((t * PACKED_TILES, LANES), jnp.uint32),
                   jax.ShapeDtypeStruct((TOP_K, t), jnp.int32), jax.ShapeDtypeStruct((TOP_K, t), F32),
                   jax.ShapeDtypeStruct((TOP_K, t), jnp.int32),
                   jax.ShapeDtypeStruct((N_EXPERTS, LANES), jnp.int32)],
        scratch_shapes=[pltpu.VMEM((N_EXPERTS, LANES), F32)],
        compiler_params=_params("arbitrary"),
        name="mix_router",
    )(attn, conv, x2, ga, wo, gffn, rw2, rb, tri)


def _dispatch_kernel(ends_ref, padded_ref, dest_ref, h_ref, xs_ref, zero_ref, sem, zsem):
    tiles = PACKED_TILES
    rows = h_ref.shape[0] // tiles
    block = MOE_STEP * tiles
    n_blocks = xs_ref.shape[0] // block

    @pl.when(pl.program_id(0) == 0)
    def _():
        zero_ref[...] = jnp.zeros(zero_ref.shape, zero_ref.dtype)

        def zero_copy(r0):
            return pltpu.make_async_copy(
                zero_ref, xs_ref.at[pl.ds(pl.multiple_of(r0 * tiles, block), block)], zsem)

        for start in (True, False):
            for e in range(N_EXPERTS):
                @pl.when(padded_ref[e] > 0)
                def _():
                    cp = zero_copy(ends_ref[e] - MOE_STEP)
                    cp.start() if start else cp.wait()

            def tail(b, _):
                cp = zero_copy(b * MOE_STEP)
                cp.start() if start else cp.wait()
                return 0

            lax.fori_loop(ends_ref[N_EXPERTS - 1] // MOE_STEP, n_blocks, tail, 0)

    def issue(t, _):
        for k in range(TOP_K):
            pltpu.make_async_copy(_tile_row(h_ref, t, tiles=tiles), _tile_row(xs_ref, dest_ref[k, t], tiles=tiles),
                                  sem).start(priority=k % DMA_PRIORITIES)
        return 0

    lax.fori_loop(0, rows, issue, 0, unroll=ISSUE_UNROLL)
    for _ in range(TOP_K):
        pltpu.make_async_copy(h_ref, xs_ref.at[pl.ds(0, rows * tiles)], sem).wait()


def _dispatch(ends, padded, dest, h2, n_blocks):
    t = h2.shape[0] // PACKED_TILES
    grid_spec = pltpu.PrefetchScalarGridSpec(
        num_scalar_prefetch=2,
        grid=(t // DISPATCH_ROWS,),
        in_specs=[pl.BlockSpec((TOP_K, DISPATCH_ROWS), lambda i, *_: (0, i), memory_space=pltpu.SMEM),
                  pl.BlockSpec((DISPATCH_ROWS * PACKED_TILES, LANES), lambda i, *_: (i, 0))],
        out_specs=pl.BlockSpec(memory_space=pl.ANY),
        scratch_shapes=[pltpu.VMEM((MOE_STEP * PACKED_TILES, LANES), h2.dtype), pltpu.SemaphoreType.DMA(()),
                        pltpu.SemaphoreType.DMA(())],
    )
    return pl.pallas_call(
        _dispatch_kernel,
        grid_spec=grid_spec,
        out_shape=jax.ShapeDtypeStruct((n_blocks * MOE_STEP * PACKED_TILES, LANES), h2.dtype),
        compiler_params=_params("arbitrary"),
        name="dispatch_rows",
    )(ends, padded, dest, h2)


def _moe_kernel(be_ref, nact_ref, passes_ref, xs_ref, wgu_ref, bgu_ref, wd_ref, bd_ref, ys_ref, wgu_bf, wd_bf):
    i = pl.program_id(0)
    half = MOE_ROWS // 2

    @pl.when((passes_ref[i] > 0) & ((i == 0) | (be_ref[i] != be_ref[jnp.maximum(i - 1, 0)])))
    def _():
        wgu_bf[...] = wgu_ref[0].astype(BF16)
        wd_bf[...] = wd_ref[0].astype(BF16)

    def blank(first, rows):
        ys_ref[first * ROW_TILES:(first + rows) * ROW_TILES, :] = jnp.zeros((rows * ROW_TILES, LANES), F32)

    def mlp(first, rows):
        x = _unpack_bf16_pairs(_load_tile_rows(xs_ref, rows, tiles=PACKED_TILES, first=first))
        gu = jnp.dot(x, wgu_bf[...], preferred_element_type=F32) + bgu_ref[0]
        gate = jnp.minimum(gu[:, :D_FF], SWIGLU_LIMIT)
        up = jnp.clip(gu[:, D_FF:], -SWIGLU_LIMIT, SWIGLU_LIMIT)
        mid = (up + 1.0) * (gate * jax.nn.sigmoid(gate * SWIGLU_ALPHA))
        y = jnp.dot(mid.astype(BF16), wd_bf[...], preferred_element_type=F32) + bd_ref[0]
        _store_tile_rows(ys_ref, y, first=first)

    for p in range(MOE_STEP // MOE_ROWS):
        first = p * MOE_ROWS

        @pl.when(passes_ref[i] >= 2 * p + 2)
        def _():
            mlp(first, MOE_ROWS)

        @pl.when(passes_ref[i] == 2 * p + 1)
        def _():
            mlp(first, half)
            blank(first + half, half)

        @pl.when(passes_ref[i] <= 2 * p)
        def _():
            blank(first, MOE_ROWS)


def _moe(block_expert, n_active, passes, xs, wgu, bgu, wd, bd):
    block = MOE_STEP * ROW_TILES
    n_blocks = xs.shape[0] // (MOE_STEP * PACKED_TILES)
    grid_spec = pltpu.PrefetchScalarGridSpec(
        num_scalar_prefetch=3,
        grid=(n_blocks,),
        in_specs=[
            pl.BlockSpec((MOE_STEP * PACKED_TILES, LANES), lambda i, be, na, ps: (jnp.minimum(i, na[0] - 1), 0)),
            pl.BlockSpec((1, D_MODEL, 2 * D_FF), lambda i, be, na, ps: (be[i], 0, 0)),
            pl.BlockSpec((1, 1, 2 * D_FF), lambda i, be, na, ps: (be[i], 0, 0)),
            pl.BlockSpec((1, D_FF, D_MODEL), lambda i, be, na, ps: (be[i], 0, 0)),
            pl.BlockSpec((1, 1, D_MODEL), lambda i, be, na, ps: (be[i], 0, 0)),
        ],
        out_specs=pl.BlockSpec((block, LANES), lambda i, be, na, ps: (i, 0)),
        scratch_shapes=[pltpu.VMEM((D_MODEL, 2 * D_FF), BF16), pltpu.VMEM((D_FF, D_MODEL), BF16)],
    )
    return pl.pallas_call(
        _moe_kernel,
        grid_spec=grid_spec,
        out_shape=jax.ShapeDtypeStruct((n_blocks * block, LANES), F32),
        compiler_params=_params("arbitrary"),
        name="expert_mlp",
    )(block_expert, n_active, passes, xs, wgu, bgu, wd, bd)


def _combine_kernel(dest_ref, dest_next_ref, dest_ahead_ref, x1_ref, gate_ref, ys_ref, o_ref, *scratch):
    bufs, sems = scratch[:COMBINE_SLOTS], scratch[COMBINE_SLOTS]
    i, n = pl.program_id(0), pl.num_programs(0)
    rows = x1_ref.shape[0]
    ahead = COMBINE_SLOTS - 1

    def issue(idx_ref, to, t):
        for k in range(TOP_K):
            pltpu.make_async_copy(_tile_row(ys_ref, idx_ref[k * rows + t]), _tile_row(bufs[to], t, (k,)),
                                  sems.at[to]).start(priority=k % DMA_PRIORITIES)

    def consume(slot, t0):
        sl = pl.ds(t0, ISSUE_UNROLL)
        acc = x1_ref[sl, :]
        for k in range(TOP_K):
            got = bufs[slot].at[k, pl.ds(pl.multiple_of(t0 * ROW_TILES, ISSUE_UNROLL * ROW_TILES),
                                         ISSUE_UNROLL * ROW_TILES)]
            acc = acc + gate_ref[sl, k:k + 1] * _load_tile_rows(got, ISSUE_UNROLL)
        o_ref[sl, :] = acc

    def sweep(idx_ref, to, slot):
        def body(j, _):
            t0 = pl.multiple_of(j * ISSUE_UNROLL, ISSUE_UNROLL)
            if to is not None:
                for t in range(ISSUE_UNROLL):
                    issue(idx_ref, to, t0 + t)
            if slot is not None:
                consume(slot, t0)
            return 0

        lax.fori_loop(0, rows // ISSUE_UNROLL, body, 0, unroll=4)

    @pl.when(i == 0)
    def _():
        sweep(dest_ref, 0, None)
        sweep(dest_next_ref, 1, None)

    for slot in range(COMBINE_SLOTS):
        @pl.when(i % COMBINE_SLOTS == slot)
        def _():
            for k in range(TOP_K):
                pltpu.make_async_copy(ys_ref.at[pl.ds(0, rows * ROW_TILES)], bufs[slot].at[k], sems.at[slot]).wait()

            @pl.when(i + ahead < n)
            def _():
                sweep(dest_ahead_ref, (slot + ahead) % COMBINE_SLOTS, slot)

            @pl.when(i + ahead >= n)
            def _():
                sweep(None, None, slot)


def _combine(dest, x1, gate_rows, ys):
    t = x1.shape[0]
    steps = t // COMBINE_ROWS
    assert COMBINE_SLOTS == 3 and steps >= COMBINE_SLOTS and ISSUE_UNROLL % SUBLANES == 0
    row = lambda w: pl.BlockSpec((COMBINE_ROWS, w), lambda i: (i, 0))
    dest = dest.reshape(TOP_K, steps, COMBINE_ROWS).transpose(1, 0, 2).reshape(-1)
    slots = lambda nxt: pl.BlockSpec((TOP_K * COMBINE_ROWS,), lambda i: (jnp.minimum(i + nxt, steps - 1),),
                                     memory_space=pltpu.SMEM)
    return pl.pallas_call(
        _combine_kernel,
        grid=(steps,),
        in_specs=[slots(0), slots(1), slots(2), row(D_MODEL), row(TOP_K), pl.BlockSpec(memory_space=pl.ANY)],
        out_specs=row(D_MODEL),
        out_shape=jax.ShapeDtypeStruct(x1.shape, F32),
        scratch_shapes=([pltpu.VMEM((TOP_K, COMBINE_ROWS * ROW_TILES, LANES), F32)] * COMBINE_SLOTS
                        + [pltpu.SemaphoreType.DMA((COMBINE_SLOTS,))]),
        compiler_params=_params("arbitrary"),
        name="combine_rows",
    )(dest, dest, dest, x1, gate_rows, ys)


def _head_blocks(w, width):
    r = w.shape[0]
    w = w.reshape(r, N_HEADS, width)
    return jnp.pad(w, ((0, 0), (0, 0), (0, HEAD_PAD - width))).reshape(r, N_HEADS * HEAD_PAD)


def _lane_row(v, offset=0):
    return jnp.pad(v, (offset, HEAD_PAD - offset - v.shape[0])).reshape(1, HEAD_PAD)


def _swap_rope(w, sign):
    lo, hi = w[..., NOPE_DIM:NOPE_DIM + HALF_ROPE], w[..., NOPE_DIM + HALF_ROPE:QK_DIM]
    return jnp.concatenate([jnp.zeros_like(w[..., :NOPE_DIM]), sign * hi, lo], axis=-1)


def kernel(x, positions, norm_mix_g, w_in, q_latent_g, w_uq, kv_latent_g, w_ukv, q_head_g, k_head_g, conv_dw_w, conv_dw_b, conv_ln_g, conv_ln_b, attn_out_g, conv_out_g, w_out, norm_ffn_g, router_w, router_b, w_gate_up, b_gate_up, w_down, b_down):
    batch, seq, d_model = x.shape
    t = batch * seq
    assert d_model == D_MODEL and seq % max(ATT_Q, CONV_ROWS) == 0
    assert all(t % rows == 0 for rows in (IN_ROWS, MIX_ROWS, DISPATCH_ROWS, COMBINE_ROWS))
    depth = norm_mix_g.shape[0]
    x2 = x.reshape(t, D_MODEL)
    pos = positions.reshape(1, t)
    invf = (1.0 / (ROPE_THETA ** (jnp.arange(0, ROPE_DIM, 2, dtype=F32) / ROPE_DIM))).reshape(HALF_ROPE, 1)
    tri = jnp.triu(jnp.ones((LANES, LANES), BF16), 1)
    o_kv = Q_LORA
    o_pe = o_kv + KV_LORA
    o_u = o_pe + ROPE_DIM

    for l in range(depth):
        wi = w_in[l]
        w_pe = wi[:, o_pe:o_u]
        w_pe_sw = jnp.concatenate([-w_pe[:, HALF_ROPE:], w_pe[:, :HALF_ROPE]], axis=1)
        pe_block = lambda w: jnp.pad(w, ((0, 0), (NOPE_DIM, HEAD_PAD - QK_DIM)))
        win = jnp.concatenate([wi[:, :o_pe], pe_block(w_pe), pe_block(w_pe_sw), wi[:, o_u:]], axis=1).astype(BF16)
        wq = w_uq[l].reshape(Q_LORA, N_HEADS, QK_DIM)
        wuq = _head_blocks(w_uq[l], QK_DIM).astype(BF16)
        wuqs = _head_blocks(_swap_rope(wq, -1.0).reshape(Q_LORA, -1), QK_DIM).astype(BF16)
        wkv = w_ukv[l].reshape(KV_LORA, N_HEADS, NOPE_DIM + V_DIM)
        wuk = _head_blocks(wkv[:, :, :NOPE_DIM].reshape(KV_LORA, -1), NOPE_DIM).astype(BF16)
        wuv = _head_blocks(wkv[:, :, NOPE_DIM:].reshape(KV_LORA, -1), V_DIM).T.astype(BF16)
        ga = attn_out_g[l].reshape(1, -1)
        wo = w_out[l].astype(BF16)
        rwt = router_w[l].T
        rwh = rwt.astype(BF16)
        rw2 = jnp.concatenate([rwh, (rwt - rwh.astype(F32)).astype(BF16)], axis=0)

        q, k, v, y = _input_stage(
            x2, pos, invf, norm_mix_g[l].reshape(1, -1), win, q_latent_g[l].reshape(1, -1), wuq, wuqs,
            kv_latent_g[l].reshape(1, -1), wuk, wuv, _lane_row(q_head_g[l]),
            _lane_row(_swap_rope(q_head_g[l], 1.0)), _lane_row(k_head_g[l]), _lane_row(_swap_rope(k_head_g[l], 1.0)))
        attn = _attention(q, k, v, batch, seq)
        conv = _conv_branch(y, conv_dw_w[l], conv_dw_b[l].reshape(1, -1), conv_ln_g[l].reshape(1, -1),
                            conv_ln_b[l].reshape(1, -1), conv_out_g[l].reshape(1, -1), batch, seq)
        x1, h2, idx, gate, rank, cnt = _mix_stage(
            attn, conv, x2, ga, wo, norm_ffn_g[l].reshape(1, -1), rw2, router_b[l].reshape(-1, 1), tri)

        counts = cnt[:, 0]
        padded = (counts + MOE_STEP - 1) // MOE_STEP * MOE_STEP
        ends = jnp.cumsum(padded)
        starts = ends - padded
        experts = jnp.arange(N_EXPERTS, dtype=jnp.int32)
        dest = rank + jnp.sum(jnp.where(idx[None] == experts[:, None, None], starts[:, None, None], 0), axis=0)
        n_blocks = (t * TOP_K + N_EXPERTS * (MOE_STEP - 1)) // MOE_STEP
        n_active = (ends[-1] // MOE_STEP).astype(jnp.int32)
        step = jnp.arange(n_blocks, dtype=jnp.int32)
        blk = jnp.minimum(step, n_active - 1)
        be = jnp.minimum(jnp.sum((ends[None, :] <= (blk * MOE_STEP)[:, None]).astype(jnp.int32), axis=1),
                         N_EXPERTS - 1)
        real = jnp.clip((starts + counts)[be] - step * MOE_STEP, 0, MOE_STEP)
        passes = jnp.where(step < n_active, -(-real // (MOE_ROWS // 2)), 0).astype(jnp.int32)

        xs = _dispatch(ends.astype(jnp.int32), padded.astype(jnp.int32), dest, h2, n_blocks)
        ys = _moe(be, n_active.reshape(1), passes, xs, w_gate_up[l], b_gate_up[l].reshape(N_EXPERTS, 1, -1),
                  w_down[l], b_down[l].reshape(N_EXPERTS, 1, -1))
        x2 = _combine(dest, x1, gate.T, ys)
    return x2.reshape(batch, seq, D_MODEL)
```

```python
import jax
import jax.numpy as jnp
from jax import lax
from jax.experimental import pallas as pl
from jax.experimental.pallas import tpu as pltpu

D_MODEL = 1024
N_HEADS = 8
NOPE_DIM = 64
ROPE_DIM = 32
QK_DIM = NOPE_DIM + ROPE_DIM
V_DIM = 64
Q_LORA = 384
KV_LORA = 128
CONV_WIDTH = 512
CONV_TAPS = 31
N_EXPERTS = 32
TOP_K = 4
D_FF = 1024
CHUNK = 64
ROPE_THETA = 10000.0
EPS = 1e-6
SWIGLU_ALPHA = 1.702
SWIGLU_LIMIT = 7.0
LOG2_E = 1.4426950408889634

LANES = 128
SUBLANES = 8
HEAD_PAD = LANES
HALF_ROPE = ROPE_DIM // 2
V_ROWS = V_DIM + 2 * SUBLANES
VMEM_LIMIT_BYTES = 56 * 1024 * 1024

IN_ROWS = 1024
ATT_Q = 1024
ATT_K = 512
ATT_QPART = 256
ATT_AHEAD = 4
CONV_ROWS = 256
CONV_HALO = 32
MIX_ROWS = 1024
MIX_GROUPS = 2
DISPATCH_ROWS = 512
MOE_ROWS = 512
MOE_STEP = 2 * MOE_ROWS
DISPATCH_SLOTS = 3
COMBINE_ROWS = 512
COMBINE_SLOTS = 3
ISSUE_UNROLL = 8
DMA_PRIORITIES = 2

F32 = jnp.float32
BF16 = jnp.bfloat16


def _params(*semantics):
    return pltpu.CompilerParams(dimension_semantics=semantics, vmem_limit_bytes=VMEM_LIMIT_BYTES)


def _rms(x, g):
    return x * lax.rsqrt(jnp.mean(x * x, axis=-1, keepdims=True) + EPS) * g


def _full(shape):
    return pl.BlockSpec(shape, lambda *_: (0,) * len(shape))


ROW_TILES = D_MODEL // LANES
PACKED_TILES = ROW_TILES // 2
assert ROW_TILES == SUBLANES


def _load_tile_rows(ref, rows, lead=(), tiles=ROW_TILES, first=0):
    return jnp.concatenate(
        [ref[lead + (pl.ds(first * tiles + c, rows, stride=tiles), slice(None))] for c in range(tiles)], axis=1)


def _store_tile_rows(ref, value, tiles=ROW_TILES, first=0):
    rows = value.shape[0]
    for c in range(tiles):
        ref[pl.ds(first * tiles + c, rows, stride=tiles), :] = value[:, c * LANES:(c + 1) * LANES]


def _tile_row(ref, r, lead=(), tiles=ROW_TILES):
    return ref.at[lead + (pl.ds(pl.multiple_of(r * tiles, tiles), tiles),)]


def _pack_bf16_pairs(x):
    half = x.shape[1] // 2
    bits = lax.bitcast_convert_type(x.astype(BF16).astype(F32), jnp.uint32)
    return (bits[:, :half] >> 16) | bits[:, half:]


def _unpack_bf16_pairs(p):
    left = lax.bitcast_convert_type(p << 16, F32)
    right = lax.bitcast_convert_type(p & jnp.uint32(0xFFFF0000), F32)
    return jnp.concatenate([left, right], axis=1).astype(BF16)


def _in_kernel(x_ref, pos_ref, invf_ref, gmix_ref, win_ref, gql_ref, wuq_ref, wuqs_ref, gkvl_ref, wuk_ref,
               wuv_ref, gq_ref, gqs_ref, gk_ref, gks_ref, q_ref, k_ref, v_ref, y_ref):
    h = _rms(x_ref[...], gmix_ref[...]).astype(BF16)
    p = jnp.dot(h, win_ref[...], preferred_element_type=F32)
    o_kv = Q_LORA
    o_pe = o_kv + KV_LORA
    o_ps = o_pe + HEAD_PAD
    o_a = o_ps + HEAD_PAD
    o_g = o_a + CONV_WIDTH
    cq, ckv, kpe, kpe_sw = p[:, :o_kv], p[:, o_kv:o_pe], p[:, o_pe:o_ps], p[:, o_ps:o_a]
    y_ref[...] = p[:, o_a:o_g] * jax.nn.sigmoid(p[:, o_g:])

    cqn = _rms(cq, gql_ref[...]).astype(BF16)
    q = jnp.dot(cqn, wuq_ref[...], preferred_element_type=F32)
    q_sw = jnp.dot(cqn, wuqs_ref[...], preferred_element_type=F32)
    ckvn = _rms(ckv, gkvl_ref[...]).astype(BF16)
    kn = jnp.dot(ckvn, wuk_ref[...], preferred_element_type=F32)
    wide_row = lax.broadcasted_iota(jnp.int32, (N_HEADS * HEAD_PAD, 1), 0)
    ones_row = jnp.where(wide_row % HEAD_PAD == V_DIM, 1.0, 0.0)
    vt = lax.dot_general(wuv_ref[...], ckvn, (((1,), (1,)), ((), ())), preferred_element_type=F32)
    v_ref[...] = (vt + ones_row).astype(BF16)

    ang = invf_ref[...] * pos_ref[...].astype(F32)
    cos_t, sin_t = jnp.cos(ang), jnp.sin(ang)
    tab = jnp.concatenate([jnp.ones((NOPE_DIM, ang.shape[1]), F32), cos_t, cos_t, sin_t, sin_t], axis=0).T
    lane = lax.broadcasted_iota(jnp.int32, (1, HEAD_PAD), 1)
    ctab = jnp.where(lane < QK_DIM, tab, 0.0)
    stab = jnp.where((lane >= NOPE_DIM) & (lane < QK_DIM), pltpu.roll(tab, QK_DIM, 1), 0.0)
    scale = QK_DIM ** -0.5 * LOG2_E
    cq_tab = ctab * (gq_ref[...] * scale)
    sq_tab = stab * (gqs_ref[...] * scale)
    ck_tab = ctab * gk_ref[...]
    sk_tab = stab * gks_ref[...]
    k_pe = kpe * ck_tab + kpe_sw * sk_tab
    ss_pe = jnp.sum(kpe * kpe, axis=-1, keepdims=True)
    for hd in range(N_HEADS):
        sl = slice(hd * HEAD_PAD, (hd + 1) * HEAD_PAD)
        qh = q[:, sl]
        rq = lax.rsqrt(jnp.sum(qh * qh, axis=-1, keepdims=True) * (1.0 / QK_DIM) + EPS)
        q_ref[:, sl] = ((qh * cq_tab + q_sw[:, sl] * sq_tab) * rq).astype(BF16)
        kh = kn[:, sl]
        rk = lax.rsqrt((jnp.sum(kh * kh, axis=-1, keepdims=True) + ss_pe) * (1.0 / QK_DIM) + EPS)
        k_ref[:, sl] = ((kh * ck_tab + k_pe) * rk).astype(BF16)


def _input_stage(x2, pos, invf, gmix, win, gql, wuq, wuqs, gkvl, wuk, wuv, gq, gqs, gk, gks):
    t = x2.shape[0]
    wide = N_HEADS * HEAD_PAD
    row = lambda w: pl.BlockSpec((IN_ROWS, w), lambda i: (i, 0))
    consts = (invf, gmix, win, gql, wuq, wuqs, gkvl, wuk, wuv, gq, gqs, gk, gks)
    return pl.pallas_call(
        _in_kernel,
        grid=(t // IN_ROWS,),
        in_specs=[row(D_MODEL), pl.BlockSpec((1, IN_ROWS), lambda i: (0, i))] + [_full(a.shape) for a in consts],
        out_specs=[row(wide), row(wide), pl.BlockSpec((wide, IN_ROWS), lambda i: (0, i)), row(CONV_WIDTH)],
        out_shape=[jax.ShapeDtypeStruct((t, wide), BF16), jax.ShapeDtypeStruct((t, wide), BF16),
                   jax.ShapeDtypeStruct((wide, t), BF16), jax.ShapeDtypeStruct((t, CONV_WIDTH), F32)],
        compiler_params=_params("parallel"),
        name="input_stage",
    )(x2, pos, *consts)


def _attn_kernel(q_ref, k_ref, vt_ref, o_ref, *scratch):
    parts, block_parts, blocks = ATT_Q // ATT_QPART, ATT_K // ATT_QPART, ATT_Q // ATT_K
    units = [(hd, part) for hd in range(N_HEADS) for part in range(parts)]
    m_refs, acc_refs = scratch[:len(units)], scratch[len(units):]
    i = pl.program_id(1)
    dn = (((1,), (1,)), ((), ()))

    def step(r0, block, diagonal):
        todo = [u for u, (_, part) in enumerate(units) if block is None or part // block_parts == block]
        old = None if diagonal else {u: (m_refs[u][...], acc_refs[u][...]) for u in todo}
        new = {}

        def keys(part):
            return (part % block_parts + 1) * ATT_QPART if diagonal else ATT_K

        def scores(u):
            hd, part = units[u]
            sl = slice(hd * HEAD_PAD, (hd + 1) * HEAD_PAD)
            return lax.dot_general(k_ref[pl.ds(r0, keys(part)), sl],
                                   q_ref[part * ATT_QPART:(part + 1) * ATT_QPART, sl], dn,
                                   preferred_element_type=F32)

        ahead = [scores(u) for u in todo[:ATT_AHEAD]]
        for n, u in enumerate(todo):
            hd, part = units[u]
            s = ahead.pop(0)
            if n + ATT_AHEAD < len(todo):
                ahead.append(scores(todo[n + ATT_AHEAD]))
            if diagonal:
                kc = lax.broadcasted_iota(jnp.int32, s.shape, 0) // CHUNK
                qc = (lax.broadcasted_iota(jnp.int32, s.shape, 1) + part % block_parts * ATT_QPART) // CHUNK
                s = jnp.where(kc <= qc, s, -jnp.inf)
            m_new = jnp.max(s, axis=0, keepdims=True)
            if not diagonal:
                m_new = jnp.maximum(old[u][0], m_new)
            acc = jnp.dot(vt_ref[hd * HEAD_PAD:hd * HEAD_PAD + V_ROWS, pl.ds(r0, keys(part))],
                          jnp.exp2(s - m_new).astype(BF16), preferred_element_type=F32)
            if not diagonal:
                acc = jnp.exp2(old[u][0] - m_new) * old[u][1] + acc
            new[u] = (m_new, acc)
        for u in todo:
            m_refs[u][...], acc_refs[u][...] = new[u]

    tile0 = pl.multiple_of(i * ATT_Q, ATT_Q)
    for block in range(blocks):
        step(tile0 + block * ATT_K, block, True)
        for earlier in range(block):
            step(tile0 + earlier * ATT_K, block, False)

    def body(j, _):
        step(pl.multiple_of(j * ATT_K, ATT_K), None, False)
        return 0

    lax.fori_loop(0, i * blocks, body, 0)
    for pair in range(N_HEADS // 2):
        for part in range(parts):
            halves = []
            for hd in (2 * pair, 2 * pair + 1):
                acc = acc_refs[hd * parts + part][...]
                halves.append(acc[:V_DIM, :] / acc[V_DIM:V_DIM + 1, :])
            o_ref[part * ATT_QPART:(part + 1) * ATT_QPART, pair * LANES:(pair + 1) * LANES] = (
                jnp.concatenate(halves, axis=0).T.astype(BF16))


def _attention(q, k, vt, batch, seq):
    assert seq % ATT_Q == 0 and ATT_Q % ATT_K == 0 and ATT_K % ATT_QPART == 0 and ATT_QPART % CHUNK == 0
    nq = seq // ATT_Q
    chains = N_HEADS * (ATT_Q // ATT_QPART)
    wide = N_HEADS * HEAD_PAD
    qspec = pl.BlockSpec((ATT_Q, wide), lambda b, i: (b * nq + i, 0))
    return pl.pallas_call(
        _attn_kernel,
        grid=(batch, nq),
        in_specs=[qspec, pl.BlockSpec((seq, wide), lambda b, i: (b, 0)),
                  pl.BlockSpec((wide, seq), lambda b, i: (0, b))],
        out_specs=pl.BlockSpec((ATT_Q, N_HEADS * V_DIM), lambda b, i: (b * nq + i, 0)),
        out_shape=jax.ShapeDtypeStruct((q.shape[0], N_HEADS * V_DIM), BF16),
        scratch_shapes=([pltpu.VMEM((1, ATT_QPART), F32)] * chains + [pltpu.VMEM((V_ROWS, ATT_QPART), F32)] * chains),
        compiler_params=_params("parallel", "arbitrary"),
        name="attention",
    )(q, k, vt)


def _conv_kernel(y_ref, w_ref, b_ref, lng_ref, lnb_ref, og_ref, o_ref, pad_ref, phase_ref):
    seq = y_ref.shape[0]
    pad_ref[0:CONV_HALO, :] = jnp.zeros((CONV_HALO, CONV_WIDTH), F32)
    pad_ref[CONV_HALO:, :] = y_ref[...]
    first = CONV_HALO - (CONV_TAPS - 1)

    def body(i, _):
        r0 = pl.multiple_of(i * CONV_ROWS, CONV_ROWS)
        win = pad_ref[pl.ds(r0, CONV_ROWS + CONV_HALO), :]
        acc = jnp.zeros((CONV_ROWS, CONV_WIDTH), F32)
        for s in range(SUBLANES):
            offs = [o for o in range(first, first + CONV_TAPS) if o % SUBLANES == s]
            span = max(offs) - s + CONV_ROWS
            if s:
                phase_ref[s, 0:span, :] = win[s:s + span, :]
            for o in offs:
                rows = (pad_ref[pl.ds(pl.multiple_of(r0 + o, SUBLANES), CONV_ROWS), :] if s == 0
                        else phase_ref[s, o - s:o - s + CONV_ROWS, :])
                acc = acc + w_ref[o - first:o - first + 1, :] * rows
        acc = acc + b_ref[...]
        xc = acc - jnp.mean(acc, axis=-1, keepdims=True)
        ln = xc * lax.rsqrt(jnp.mean(xc * xc, axis=-1, keepdims=True) + EPS) * lng_ref[...] + lnb_ref[...]
        z = ln * jax.nn.sigmoid(ln)
        o_ref[pl.ds(r0, CONV_ROWS), :] = _rms(z, og_ref[...]).astype(BF16)
        return 0

    lax.fori_loop(0, seq // CONV_ROWS, body, 0)


def _conv_branch(y, w, b, lng, lnb, og, batch, seq):
    spec = pl.BlockSpec((seq, CONV_WIDTH), lambda bi: (bi, 0))
    return pl.pallas_call(
        _conv_kernel,
        grid=(batch,),
        in_specs=[spec, _full(w.shape), _full(b.shape), _full(lng.shape), _full(lnb.shape), _full(og.shape)],
        out_specs=spec,
        out_shape=jax.ShapeDtypeStruct(y.shape, BF16),
        scratch_shapes=[pltpu.VMEM((seq + CONV_HALO, CONV_WIDTH), F32),
                        pltpu.VMEM((SUBLANES, CONV_ROWS + CONV_HALO, CONV_WIDTH), F32)],
        compiler_params=_params("parallel"),
        name="conv_branch",
    )(y, w, b, lng, lnb, og)


def _mix_kernel(attn_ref, conv_ref, x_ref, ga_ref, wo_ref, gffn_ref, rw2_ref, rb_ref,
                tri_ref, x1_ref, h2_ref, idx_ref, gate_ref, rank_ref, cnt_ref, base_ref):
    @pl.when(pl.program_id(0) == 0)
    def _():
        base_ref[...] = jnp.zeros(base_ref.shape, F32)

    size = MIX_ROWS // MIX_GROUPS
    groups = [slice(g * size, (g + 1) * size) for g in range(MIX_GROUPS)]
    dn = (((1,), (1,)), ((), ()))

    def project(rs):
        an = _rms(attn_ref[rs, :].astype(F32), ga_ref[...])
        mixed = jnp.concatenate([an.astype(BF16), conv_ref[rs, :]], axis=1)
        return x_ref[rs, :] + jnp.dot(mixed, wo_ref[...], preferred_element_type=F32)

    def norm_and_logits(g, rs, x1):
        x1_ref[rs, :] = x1
        h2 = _rms(x1, gffn_ref[...])
        _store_tile_rows(h2_ref, _pack_bf16_pairs(h2), PACKED_TILES, first=g * size)
        hi = h2.astype(BF16)
        lo = (h2 - hi.astype(F32)).astype(BF16)
        both = lax.dot_general(rw2_ref[...], hi, dn, preferred_element_type=F32)
        return (both[:N_EXPERTS] + both[N_EXPERTS:]
                + lax.dot_general(rw2_ref[:N_EXPERTS, :], lo, dn, preferred_element_type=F32)) + rb_ref[...]

    def top_k(rs, logits):
        eidx = lax.broadcasted_iota(jnp.int32, logits.shape, 0).astype(F32)
        work = logits
        sels, vals = [], []
        for k in range(TOP_K):
            mx = jnp.max(work, axis=0, keepdims=True)
            first = jnp.min(jnp.where(work == mx, eidx, float(N_EXPERTS)), axis=0, keepdims=True)
            sel = eidx == first
            work = jnp.where(sel, -jnp.inf, work)
            sels.append(sel)
            vals.append(mx)
            idx_ref[k:k + 1, rs] = first.astype(jnp.int32)
        exps = [jnp.exp(v - vals[0]) for v in vals]
        denom = exps[0] + exps[1] + exps[2] + exps[3]
        for k in range(TOP_K):
            gate_ref[k:k + 1, rs] = exps[k] / denom
        return sels

    x1s, logits, sels = {}, {}, {}
    for s in range(MIX_GROUPS + 2):
        if s < MIX_GROUPS:
            x1s[s] = project(groups[s])
        if 0 <= s - 1 < MIX_GROUPS:
            logits[s - 1] = norm_and_logits(s - 1, groups[s - 1], x1s.pop(s - 1))
        if 0 <= s - 2 < MIX_GROUPS:
            sels[s - 2] = top_k(groups[s - 2], logits.pop(s - 2))
    sels = [sels[g] for g in range(MIX_GROUPS)]

    sels = [jnp.concatenate([s[k] for s in sels], axis=1) for k in range(TOP_K)]
    member = jnp.where(sels[0] | sels[1] | sels[2] | sels[3], 1.0, 0.0)
    blocks = [member[:, b * LANES:(b + 1) * LANES] for b in range(MIX_ROWS // LANES)]
    inner = jnp.dot(jnp.concatenate(blocks, axis=0).astype(BF16), tri_ref[...], preferred_element_type=F32)
    offset = base_ref[:, 0:1]
    pieces = []
    for b, blk in enumerate(blocks):
        pieces.append(inner[b * N_EXPERTS:(b + 1) * N_EXPERTS, :] + offset)
        offset = offset + jnp.sum(blk, axis=1, keepdims=True)
    posn = jnp.concatenate(pieces, axis=1)
    for k in range(TOP_K):
        rank_ref[k:k + 1, :] = jnp.sum(jnp.where(sels[k], posn, 0.0), axis=0, keepdims=True).astype(jnp.int32)
    base_ref[...] = jnp.broadcast_to(offset, base_ref.shape)
    cnt_ref[...] = base_ref[...].astype(jnp.int32)


def _mix_stage(attn, conv, x2, ga, wo, gffn, rw2, rb, tri):
    t = x2.shape[0]
    proj = lambda i: (i, 0)
    row = lambda w: pl.BlockSpec((MIX_ROWS, w), proj)
    col = pl.BlockSpec((TOP_K, MIX_ROWS), lambda i: (0, i))
    return pl.pallas_call(
        _mix_kernel,
        grid=(t // MIX_ROWS,),
        in_specs=[row(attn.shape[1]), row(CONV_WIDTH), row(D_MODEL), _full(ga.shape), _full(wo.shape),
                  _full(gffn.shape), _full(rw2.shape), _full(rb.shape), _full(tri.shape)],
        out_specs=[row(D_MODEL), pl.BlockSpec((MIX_ROWS * PACKED_TILES, LANES), proj), col, col, col,
                   _full((N_EXPERTS, LANES))],
        out_shape=[jax.ShapeDtypeStruct((t, D_MODEL), F32),
                   jax.ShapeDtypeStruct((t * PACKED_TILES, LANES), jnp.uint32),
                   jax.ShapeDtypeStruct((TOP_K, t), jnp.int32), jax.ShapeDtypeStruct((TOP_K, t), F32),
                   jax.ShapeDtypeStruct((TOP_K, t), jnp.int32),
                   jax.ShapeDtypeStruct((N_EXPERTS, LANES), jnp.int32)],
        scratch_shapes=[pltpu.VMEM((N_EXPERTS, LANES), F32)],
        compiler_params=_params("arbitrary"),
        name="mix_router",
    )(attn, conv, x2, ga, wo, gffn, rw2, rb, tri)


def _dispatch_kernel(ends_ref, padded_ref, dest_ref, h_ref, xs_ref, zero_ref, hbuf_ref, sems, hsems, zsem):
    tiles = PACKED_TILES
    rows = DISPATCH_ROWS
    block = MOE_STEP * tiles
    n_blocks = xs_ref.shape[0] // block
    i, n = pl.program_id(0), pl.num_programs(0)

    @pl.when(i == 0)
    def _():
        zero_ref[...] = jnp.zeros(zero_ref.shape, zero_ref.dtype)

        def zero_copy(r0):
            return pltpu.make_async_copy(
                zero_ref, xs_ref.at[pl.ds(pl.multiple_of(r0 * tiles, block), block)], zsem)

        for start in (True, False):
            for e in range(N_EXPERTS):
                @pl.when(padded_ref[e] > 0)
                def _():
                    cp = zero_copy(ends_ref[e] - MOE_STEP)
                    cp.start() if start else cp.wait()

            def tail(b, _):
                cp = zero_copy(b * MOE_STEP)
                cp.start() if start else cp.wait()
                return 0

            lax.fori_loop(ends_ref[N_EXPERTS - 1] // MOE_STEP, n_blocks, tail, 0)

    def fetch(step, slot):
        first = pl.multiple_of(step * (rows * tiles), rows * tiles)
        return pltpu.make_async_copy(h_ref.at[pl.ds(first, rows * tiles)], hbuf_ref.at[slot], hsems.at[slot])

    def drain(slot):
        for _ in range(TOP_K):
            pltpu.make_async_copy(hbuf_ref.at[slot], xs_ref.at[pl.ds(0, rows * tiles)], sems.at[slot]).wait()

    @pl.when(i == 0)
    def _():
        fetch(0, 0).start()
        fetch(1, 1).start()

    slot = i % DISPATCH_SLOTS
    fetch(i, slot).wait()

    def issue(t, _):
        for k in range(TOP_K):
            pltpu.make_async_copy(_tile_row(hbuf_ref, t, (slot,), tiles), _tile_row(xs_ref, dest_ref[k, t], tiles=tiles),
                                  sems.at[slot]).start(priority=k % DMA_PRIORITIES)
        return 0

    lax.fori_loop(0, rows, issue, 0, unroll=ISSUE_UNROLL)

    @pl.when(i > 0)
    def _():
        drain((i - 1) % DISPATCH_SLOTS)

    @pl.when(i + 2 < n)
    def _():
        fetch(i + 2, (i + 2) % DISPATCH_SLOTS).start()

    @pl.when(i == n - 1)
    def _():
        drain(slot)


def _dispatch(ends, padded, dest, h2, n_blocks):
    t = h2.shape[0] // PACKED_TILES
    assert DISPATCH_SLOTS == 3 and t // DISPATCH_ROWS >= DISPATCH_SLOTS
    grid_spec = pltpu.PrefetchScalarGridSpec(
        num_scalar_prefetch=2,
        grid=(t // DISPATCH_ROWS,),
        in_specs=[pl.BlockSpec((TOP_K, DISPATCH_ROWS), lambda i, *_: (0, i), memory_space=pltpu.SMEM),
                  pl.BlockSpec(memory_space=pl.ANY)],
        out_specs=pl.BlockSpec(memory_space=pl.ANY),
        scratch_shapes=[pltpu.VMEM((MOE_STEP * PACKED_TILES, LANES), h2.dtype),
                        pltpu.VMEM((DISPATCH_SLOTS, DISPATCH_ROWS * PACKED_TILES, LANES), h2.dtype),
                        pltpu.SemaphoreType.DMA((DISPATCH_SLOTS,)), pltpu.SemaphoreType.DMA((DISPATCH_SLOTS,)),
                        pltpu.SemaphoreType.DMA(())],
    )
    return pl.pallas_call(
        _dispatch_kernel,
        grid_spec=grid_spec,
        out_shape=jax.ShapeDtypeStruct((n_blocks * MOE_STEP * PACKED_TILES, LANES), h2.dtype),
        compiler_params=_params("arbitrary"),
        name="dispatch_rows",
    )(ends, padded, dest, h2)


def _moe_kernel(be_ref, nact_ref, passes_ref, xs_ref, wgu_ref, bgu_ref, wd_ref, bd_ref, ys_ref, wgu_bf, wd_bf):
    i = pl.program_id(0)
    half = MOE_ROWS // 2

    @pl.when((passes_ref[i] > 0) & ((i == 0) | (be_ref[i] != be_ref[jnp.maximum(i - 1, 0)])))
    def _():
        wgu_bf[...] = wgu_ref[0].astype(BF16)
        wd_bf[...] = wd_ref[0].astype(BF16)

    def blank(first, rows):
        ys_ref[first * ROW_TILES:(first + rows) * ROW_TILES, :] = jnp.zeros((rows * ROW_TILES, LANES), F32)

    def mlp(first, rows):
        x = _unpack_bf16_pairs(_load_tile_rows(xs_ref, rows, tiles=PACKED_TILES, first=first))
        gu = jnp.dot(x, wgu_bf[...], preferred_element_type=F32) + bgu_ref[0]
        gate = jnp.minimum(gu[:, :D_FF], SWIGLU_LIMIT)
        up = jnp.clip(gu[:, D_FF:], -SWIGLU_LIMIT, SWIGLU_LIMIT)
        mid = (up + 1.0) * (gate * jax.nn.sigmoid(gate * SWIGLU_ALPHA))
        y = jnp.dot(mid.astype(BF16), wd_bf[...], preferred_element_type=F32) + bd_ref[0]
        _store_tile_rows(ys_ref, y, first=first)

    for p in range(MOE_STEP // MOE_ROWS):
        first = p * MOE_ROWS

        @pl.when(passes_ref[i] >= 2 * p + 2)
        def _():
            mlp(first, MOE_ROWS)

        @pl.when(passes_ref[i] == 2 * p + 1)
        def _():
            mlp(first, half)
            blank(first + half, half)

        @pl.when(passes_ref[i] <= 2 * p)
        def _():
            blank(first, MOE_ROWS)


def _moe(block_expert, n_active, passes, xs, wgu, bgu, wd, bd):
    block = MOE_STEP * ROW_TILES
    n_blocks = xs.shape[0] // (MOE_STEP * PACKED_TILES)
    grid_spec = pltpu.PrefetchScalarGridSpec(
        num_scalar_prefetch=3,
        grid=(n_blocks,),
        in_specs=[
            pl.BlockSpec((MOE_STEP * PACKED_TILES, LANES), lambda i, be, na, ps: (jnp.minimum(i, na[0] - 1), 0)),
            pl.BlockSpec((1, D_MODEL, 2 * D_FF), lambda i, be, na, ps: (be[i], 0, 0)),
            pl.BlockSpec((1, 1, 2 * D_FF), lambda i, be, na, ps: (be[i], 0, 0)),
            pl.BlockSpec((1, D_FF, D_MODEL), lambda i, be, na, ps: (be[i], 0, 0)),
            pl.BlockSpec((1, 1, D_MODEL), lambda i, be, na, ps: (be[i], 0, 0)),
        ],
        out_specs=pl.BlockSpec((block, LANES), lambda i, be, na, ps: (i, 0)),
        scratch_shapes=[pltpu.VMEM((D_MODEL, 2 * D_FF), BF16), pltpu.VMEM((D_FF, D_MODEL), BF16)],
    )
    return pl.pallas_call(
        _moe_kernel,
        grid_spec=grid_spec,
        out_shape=jax.ShapeDtypeStruct((n_blocks * block, LANES), F32),
        compiler_params=_params("arbitrary"),
        name="expert_mlp",
    )(block_expert, n_active, passes, xs, wgu, bgu, wd, bd)


def _combine_kernel(dest_ref, dest_next_ref, dest_ahead_ref, x1_ref, gate_ref, ys_ref, o_ref, *scratch):
    bufs, sems = scratch[:COMBINE_SLOTS], scratch[COMBINE_SLOTS]
    i, n = pl.program_id(0), pl.num_programs(0)
    rows = x1_ref.shape[0]
    ahead = COMBINE_SLOTS - 1

    def issue(idx_ref, to, t):
        for k in range(TOP_K):
            pltpu.make_async_copy(_tile_row(ys_ref, idx_ref[k * rows + t]), _tile_row(bufs[to], t, (k,)),
                                  sems.at[to]).start(priority=k % DMA_PRIORITIES)

    def consume(slot, t0):
        sl = pl.ds(t0, ISSUE_UNROLL)
        acc = x1_ref[sl, :]
        for k in range(TOP_K):
            got = bufs[slot].at[k, pl.ds(pl.multiple_of(t0 * ROW_TILES, ISSUE_UNROLL * ROW_TILES),
                                         ISSUE_UNROLL * ROW_TILES)]
            acc = acc + gate_ref[sl, k:k + 1] * _load_tile_rows(got, ISSUE_UNROLL)
        o_ref[sl, :] = acc

    def sweep(idx_ref, to, slot):
        def body(j, _):
            t0 = pl.multiple_of(j * ISSUE_UNROLL, ISSUE_UNROLL)
            if to is not None:
                for t in range(ISSUE_UNROLL):
                    issue(idx_ref, to, t0 + t)
            if slot is not None:
                consume(slot, t0)
            return 0

        lax.fori_loop(0, rows // ISSUE_UNROLL, body, 0, unroll=4)

    @pl.when(i == 0)
    def _():
        sweep(dest_ref, 0, None)
        sweep(dest_next_ref, 1, None)

    for slot in range(COMBINE_SLOTS):
        @pl.when(i % COMBINE_SLOTS == slot)
        def _():
            for k in range(TOP_K):
                pltpu.make_async_copy(ys_ref.at[pl.ds(0, rows * ROW_TILES)], bufs[slot].at[k], sems.at[slot]).wait()

            @pl.when(i + ahead < n)
            def _():
                sweep(dest_ahead_ref, (slot + ahead) % COMBINE_SLOTS, slot)

            @pl.when(i + ahead >= n)
            def _():
                sweep(None, None, slot)


def _combine(dest, x1, gate_rows, ys):
    t = x1.shape[0]
    steps = t // COMBINE_ROWS
    assert COMBINE_SLOTS == 3 and steps >= COMBINE_SLOTS and ISSUE_UNROLL % SUBLANES == 0
    row = lambda w: pl.BlockSpec((COMBINE_ROWS, w), lambda i: (i, 0))
    dest = dest.reshape(TOP_K, steps, COMBINE_ROWS).transpose(1, 0, 2).reshape(-1)
    slots = lambda nxt: pl.BlockSpec((TOP_K * COMBINE_ROWS,), lambda i: (jnp.minimum(i + nxt, steps - 1),),
                                     memory_space=pltpu.SMEM)
    return pl.pallas_call(
        _combine_kernel,
        grid=(steps,),
        in_specs=[slots(0), slots(1), slots(2), row(D_MODEL), row(TOP_K), pl.BlockSpec(memory_space=pl.ANY)],
        out_specs=row(D_MODEL),
        out_shape=jax.ShapeDtypeStruct(x1.shape, F32),
        scratch_shapes=([pltpu.VMEM((TOP_K, COMBINE_ROWS * ROW_TILES, LANES), F32)] * COMBINE_SLOTS
                        + [pltpu.SemaphoreType.DMA((COMBINE_SLOTS,))]),
        compiler_params=_params("arbitrary"),
        name="combine_rows",
    )(dest, dest, dest, x1, gate_rows, ys)


def _head_blocks(w, width):
    r = w.shape[0]
    w = w.reshape(r, N_HEADS, width)
    return jnp.pad(w, ((0, 0), (0, 0), (0, HEAD_PAD - width))).reshape(r, N_HEADS * HEAD_PAD)


def _lane_row(v, offset=0):
    return jnp.pad(v, (offset, HEAD_PAD - offset - v.shape[0])).reshape(1, HEAD_PAD)


def _swap_rope(w, sign):
    lo, hi = w[..., NOPE_DIM:NOPE_DIM + HALF_ROPE], w[..., NOPE_DIM + HALF_ROPE:QK_DIM]
    return jnp.concatenate([jnp.zeros_like(w[..., :NOPE_DIM]), sign * hi, lo], axis=-1)


def kernel(x, positions, norm_mix_g, w_in, q_latent_g, w_uq, kv_latent_g, w_ukv, q_head_g, k_head_g, conv_dw_w, conv_dw_b, conv_ln_g, conv_ln_b, attn_out_g, conv_out_g, w_out, norm_ffn_g, router_w, router_b, w_gate_up, b_gate_up, w_down, b_down):
    batch, seq, d_model = x.shape
    t = batch * seq
    assert d_model == D_MODEL and seq % max(ATT_Q, CONV_ROWS) == 0
    assert all(t % rows == 0 for rows in (IN_ROWS, MIX_ROWS, DISPATCH_ROWS, COMBINE_ROWS))
    depth = norm_mix_g.shape[0]
    x2 = x.reshape(t, D_MODEL)
    pos = positions.reshape(1, t)
    invf = (1.0 / (ROPE_THETA ** (jnp.arange(0, ROPE_DIM, 2, dtype=F32) / ROPE_DIM))).reshape(HALF_ROPE, 1)
    tri = jnp.triu(jnp.ones((LANES, LANES), BF16), 1)
    o_kv = Q_LORA
    o_pe = o_kv + KV_LORA
    o_u = o_pe + ROPE_DIM

    for l in range(depth):
        wi = w_in[l]
        w_pe = wi[:, o_pe:o_u]
        w_pe_sw = jnp.concatenate([-w_pe[:, HALF_ROPE:], w_pe[:, :HALF_ROPE]], axis=1)
        pe_block = lambda w: jnp.pad(w, ((0, 0), (NOPE_DIM, HEAD_PAD - QK_DIM)))
        win = jnp.concatenate([wi[:, :o_pe], pe_block(w_pe), pe_block(w_pe_sw), wi[:, o_u:]], axis=1).astype(BF16)
        wq = w_uq[l].reshape(Q_LORA, N_HEADS, QK_DIM)
        wuq = _head_blocks(w_uq[l], QK_DIM).astype(BF16)
        wuqs = _head_blocks(_swap_rope(wq, -1.0).reshape(Q_LORA, -1), QK_DIM).astype(BF16)
        wkv = w_ukv[l].reshape(KV_LORA, N_HEADS, NOPE_DIM + V_DIM)
        wuk = _head_blocks(wkv[:, :, :NOPE_DIM].reshape(KV_LORA, -1), NOPE_DIM).astype(BF16)
        wuv = _head_blocks(wkv[:, :, NOPE_DIM:].reshape(KV_LORA, -1), V_DIM).T.astype(BF16)
        ga = attn_out_g[l].reshape(1, -1)
        wo = w_out[l].astype(BF16)
        rwt = router_w[l].T
        rwh = rwt.astype(BF16)
        rw2 = jnp.concatenate([rwh, (rwt - rwh.astype(F32)).astype(BF16)], axis=0)

        q, k, v, y = _input_stage(
            x2, pos, invf, norm_mix_g[l].reshape(1, -1), win, q_latent_g[l].reshape(1, -1), wuq, wuqs,
            kv_latent_g[l].reshape(1, -1), wuk, wuv, _lane_row(q_head_g[l]),
            _lane_row(_swap_rope(q_head_g[l], 1.0)), _lane_row(k_head_g[l]), _lane_row(_swap_rope(k_head_g[l], 1.0)))
        attn = _attention(q, k, v, batch, seq)
        conv = _conv_branch(y, conv_dw_w[l], conv_dw_b[l].reshape(1, -1), conv_ln_g[l].reshape(1, -1),
                            conv_ln_b[l].reshape(1, -1), conv_out_g[l].reshape(1, -1), batch, seq)
        x1, h2, idx, gate, rank, cnt = _mix_stage(
            attn, conv, x2, ga, wo, norm_ffn_g[l].reshape(1, -1), rw2, router_b[l].reshape(-1, 1), tri)

        counts = cnt[:, 0]
        padded = (counts + MOE_STEP - 1) // MOE_STEP * MOE_STEP
        ends = jnp.cumsum(padded)
        starts = ends - padded
        experts = jnp.arange(N_EXPERTS, dtype=jnp.int32)
        dest = rank + jnp.sum(jnp.where(idx[None] == experts[:, None, None], starts[:, None, None], 0), axis=0)
        n_blocks = (t * TOP_K + N_EXPERTS * (MOE_STEP - 1)) // MOE_STEP
        n_active = (ends[-1] // MOE_STEP).astype(jnp.int32)
        step = jnp.arange(n_blocks, dtype=jnp.int32)
        blk = jnp.minimum(step, n_active - 1)
        be = jnp.minimum(jnp.sum((ends[None, :] <= (blk * MOE_STEP)[:, None]).astype(jnp.int32), axis=1),
                         N_EXPERTS - 1)
        real = jnp.clip((starts + counts)[be] - step * MOE_STEP, 0, MOE_STEP)
        passes = jnp.where(step < n_active, -(-real // (MOE_ROWS // 2)), 0).astype(jnp.int32)

        xs = _dispatch(ends.astype(jnp.int32), padded.astype(jnp.int32), dest, h2, n_blocks)
        ys = _moe(be, n_active.reshape(1), passes, xs, w_gate_up[l], b_gate_up[l].reshape(N_EXPERTS, 1, -1),
                  w_down[l], b_down[l].reshape(N_EXPERTS, 1, -1))
        x2 = _combine(dest, x1, gate.T, ys)
    return x2.reshape(batch, seq, D_MODEL)
```

```python
import jax
import jax.numpy as jnp
from jax import lax
from jax.experimental import pallas as pl
from jax.experimental.pallas import tpu as pltpu

D_MODEL = 1024
N_HEADS = 8
NOPE_DIM = 64
ROPE_DIM = 32
QK_DIM = NOPE_DIM + ROPE_DIM
V_DIM = 64
Q_LORA = 384
KV_LORA = 128
CONV_WIDTH = 512
CONV_TAPS = 31
N_EXPERTS = 32
TOP_K = 4
D_FF = 1024
CHUNK = 64
ROPE_THETA = 10000.0
EPS = 1e-6
SWIGLU_ALPHA = 1.702
SWIGLU_LIMIT = 7.0
LOG2_E = 1.4426950408889634

LANES = 128
SUBLANES = 8
HEAD_PAD = LANES
HALF_ROPE = ROPE_DIM // 2
V_ROWS = V_DIM + 2 * SUBLANES
VMEM_LIMIT_BYTES = 56 * 1024 * 1024

IN_ROWS = 1024
ATT_Q = 1024
ATT_K = 512
ATT_QPART = 256
ATT_AHEAD = 4
CONV_ROWS = 256
CONV_HALO = 32
MIX_ROWS = 1024
MIX_GROUPS = 2
DISPATCH_ROWS = 512
MOE_ROWS = 512
MOE_STEP = 2 * MOE_ROWS
DISPATCH_SLOTS = 3
COMBINE_ROWS = 512
COMBINE_SLOTS = 3
ISSUE_UNROLL = 8
DMA_PRIORITIES = 2

F32 = jnp.float32
BF16 = jnp.bfloat16


def _params(*semantics):
    return pltpu.CompilerParams(dimension_semantics=semantics, vmem_limit_bytes=VMEM_LIMIT_BYTES)


def _rms(x, g):
    return x * lax.rsqrt(jnp.mean(x * x, axis=-1, keepdims=True) + EPS) * g


def _full(shape):
    return pl.BlockSpec(shape, lambda *_: (0,) * len(shape))


ROW_TILES = D_MODEL // LANES
PACKED_TILES = ROW_TILES // 2
assert ROW_TILES == SUBLANES


def _load_tile_rows(ref, rows, lead=(), tiles=ROW_TILES, first=0):
    return jnp.concatenate(
        [ref[lead + (pl.ds(first * tiles + c, rows, stride=tiles), slice(None))] for c in range(tiles)], axis=1)


def _store_tile_rows(ref, value, tiles=ROW_TILES, first=0):
    rows = value.shape[0]
    for c in range(tiles):
        ref[pl.ds(first * tiles + c, rows, stride=tiles), :] = value[:, c * LANES:(c + 1) * LANES]


def _tile_row(ref, r, lead=(), tiles=ROW_TILES):
    return ref.at[lead + (pl.ds(pl.multiple_of(r * tiles, tiles), tiles),)]


def _pack_bf16_pairs(x):
    half = x.shape[1] // 2
    bits = lax.bitcast_convert_type(x.astype(BF16).astype(F32), jnp.uint32)
    return (bits[:, :half] >> 16) | bits[:, half:]


def _unpack_bf16_pairs(p):
    left = lax.bitcast_convert_type(p << 16, F32)
    right = lax.bitcast_convert_type(p & jnp.uint32(0xFFFF0000), F32)
    return jnp.concatenate([left, right], axis=1).astype(BF16)


def _in_kernel(x_ref, pos_ref, invf_ref, gmix_ref, win_ref, gql_ref, wuq_ref, wuqs_ref, gkvl_ref, wuk_ref,
               wuv_ref, gq_ref, gqs_ref, gk_ref, gks_ref, q_ref, k_ref, v_ref, y_ref):
    h = _rms(x_ref[...], gmix_ref[...]).astype(BF16)
    p = jnp.dot(h, win_ref[...], preferred_element_type=F32)
    o_kv = Q_LORA
    o_pe = o_kv + KV_LORA
    o_ps = o_pe + HEAD_PAD
    o_a = o_ps + HEAD_PAD
    o_g = o_a + CONV_WIDTH
    cq, ckv, kpe, kpe_sw = p[:, :o_kv], p[:, o_kv:o_pe], p[:, o_pe:o_ps], p[:, o_ps:o_a]
    y_ref[...] = p[:, o_a:o_g] * jax.nn.sigmoid(p[:, o_g:])

    cqn = _rms(cq, gql_ref[...]).astype(BF16)
    q = jnp.dot(cqn, wuq_ref[...], preferred_element_type=F32)
    q_sw = jnp.dot(cqn, wuqs_ref[...], preferred_element_type=F32)
    ckvn = _rms(ckv, gkvl_ref[...]).astype(BF16)
    kn = jnp.dot(ckvn, wuk_ref[...], preferred_element_type=F32)
    wide_row = lax.broadcasted_iota(jnp.int32, (N_HEADS * HEAD_PAD, 1), 0)
    ones_row = jnp.where(wide_row % HEAD_PAD == V_DIM, 1.0, 0.0)
    vt = lax.dot_general(wuv_ref[...], ckvn, (((1,), (1,)), ((), ())), preferred_element_type=F32)
    v_ref[...] = (vt + ones_row).astype(BF16)

    ang = invf_ref[...] * pos_ref[...].astype(F32)
    cos_t, sin_t = jnp.cos(ang), jnp.sin(ang)
    tab = jnp.concatenate([jnp.ones((NOPE_DIM, ang.shape[1]), F32), cos_t, cos_t, sin_t, sin_t], axis=0).T
    lane = lax.broadcasted_iota(jnp.int32, (1, HEAD_PAD), 1)
    ctab = jnp.where(lane < QK_DIM, tab, 0.0)
    stab = jnp.where((lane >= NOPE_DIM) & (lane < QK_DIM), pltpu.roll(tab, QK_DIM, 1), 0.0)
    scale = QK_DIM ** -0.5 * LOG2_E
    cq_tab = ctab * (gq_ref[...] * scale)
    sq_tab = stab * (gqs_ref[...] * scale)
    ck_tab = ctab * gk_ref[...]
    sk_tab = stab * gks_ref[...]
    k_pe = kpe * ck_tab + kpe_sw * sk_tab
    ss_pe = jnp.sum(kpe * kpe, axis=-1, keepdims=True)
    for hd in range(N_HEADS):
        sl = slice(hd * HEAD_PAD, (hd + 1) * HEAD_PAD)
        qh = q[:, sl]
        rq = lax.rsqrt(jnp.sum(qh * qh, axis=-1, keepdims=True) * (1.0 / QK_DIM) + EPS)
        q_ref[:, sl] = ((qh * cq_tab + q_sw[:, sl] * sq_tab) * rq).astype(BF16)
        kh = kn[:, sl]
        rk = lax.rsqrt((jnp.sum(kh * kh, axis=-1, keepdims=True) + ss_pe) * (1.0 / QK_DIM) + EPS)
        k_ref[:, sl] = ((kh * ck_tab + k_pe) * rk).astype(BF16)


def _input_stage(x2, pos, invf, gmix, win, gql, wuq, wuqs, gkvl, wuk, wuv, gq, gqs, gk, gks):
    t = x2.shape[0]
    wide = N_HEADS * HEAD_PAD
    row = lambda w: pl.BlockSpec((IN_ROWS, w), lambda i: (i, 0))
    consts = (invf, gmix, win, gql, wuq, wuqs, gkvl, wuk, wuv, gq, gqs, gk, gks)
    return pl.pallas_call(
        _in_kernel,
        grid=(t // IN_ROWS,),
        in_specs=[row(D_MODEL), pl.BlockSpec((1, IN_ROWS), lambda i: (0, i))] + [_full(a.shape) for a in consts],
        out_specs=[row(wide), row(wide), pl.BlockSpec((wide, IN_ROWS), lambda i: (0, i)), row(CONV_WIDTH)],
        out_shape=[jax.ShapeDtypeStruct((t, wide), BF16), jax.ShapeDtypeStruct((t, wide), BF16),
                   jax.ShapeDtypeStruct((wide, t), BF16), jax.ShapeDtypeStruct((t, CONV_WIDTH), F32)],
        compiler_params=_params("parallel"),
        name="input_stage",
    )(x2, pos, *consts)


def _attn_kernel(q_ref, k_ref, vt_ref, o_ref, *scratch):
    parts, block_parts, blocks = ATT_Q // ATT_QPART, ATT_K // ATT_QPART, ATT_Q // ATT_K
    units = [(hd, part) for hd in range(N_HEADS) for part in range(parts)]
    m_refs, acc_refs = scratch[:len(units)], scratch[len(units):]
    i = pl.program_id(1)
    dn = (((1,), (1,)), ((), ()))

    def step(r0, block, diagonal):
        todo = [u for u, (_, part) in enumerate(units) if block is None or part // block_parts == block]
        old = None if diagonal else {u: (m_refs[u][...], acc_refs[u][...]) for u in todo}
        new = {}

        def keys(part):
            return (part % block_parts + 1) * ATT_QPART if diagonal else ATT_K

        def scores(u):
            hd, part = units[u]
            sl = slice(hd * HEAD_PAD, (hd + 1) * HEAD_PAD)
            return lax.dot_general(k_ref[pl.ds(r0, keys(part)), sl],
                                   q_ref[part * ATT_QPART:(part + 1) * ATT_QPART, sl], dn,
                                   preferred_element_type=F32)

        ahead = [scores(u) for u in todo[:ATT_AHEAD]]
        for n, u in enumerate(todo):
            hd, part = units[u]
            s = ahead.pop(0)
            if n + ATT_AHEAD < len(todo):
                ahead.append(scores(todo[n + ATT_AHEAD]))
            if diagonal:
                kc = lax.broadcasted_iota(jnp.int32, s.shape, 0) // CHUNK
                qc = (lax.broadcasted_iota(jnp.int32, s.shape, 1) + part % block_parts * ATT_QPART) // CHUNK
                s = jnp.where(kc <= qc, s, -jnp.inf)
            m_new = jnp.max(s, axis=0, keepdims=True)
            if not diagonal:
                m_new = jnp.maximum(old[u][0], m_new)
            acc = jnp.dot(vt_ref[hd * HEAD_PAD:hd * HEAD_PAD + V_ROWS, pl.ds(r0, keys(part))],
                          jnp.exp2(s - m_new).astype(BF16), preferred_element_type=F32)
            if not diagonal:
                acc = jnp.exp2(old[u][0] - m_new) * old[u][1] + acc
            new[u] = (m_new, acc)
        for u in todo:
            m_refs[u][...], acc_refs[u][...] = new[u]

    tile0 = pl.multiple_of(i * ATT_Q, ATT_Q)
    for block in range(blocks):
        step(tile0 + block * ATT_K, block, True)
        for earlier in range(block):
            step(tile0 + earlier * ATT_K, block, False)

    def body(j, _):
        step(pl.multiple_of(j * ATT_K, ATT_K), None, False)
        return 0

    lax.fori_loop(0, i * blocks, body, 0)
    for pair in range(N_HEADS // 2):
        for part in range(parts):
            halves = []
            for hd in (2 * pair, 2 * pair + 1):
                acc = acc_refs[hd * parts + part][...]
                halves.append(acc[:V_DIM, :] / acc[V_DIM:V_DIM + 1, :])
            o_ref[part * ATT_QPART:(part + 1) * ATT_QPART, pair * LANES:(pair + 1) * LANES] = (
                jnp.concatenate(halves, axis=0).T.astype(BF16))


def _attention(q, k, vt, batch, seq):
    assert seq % ATT_Q == 0 and ATT_Q % ATT_K == 0 and ATT_K % ATT_QPART == 0 and ATT_QPART % CHUNK == 0
    nq = seq // ATT_Q
    chains = N_HEADS * (ATT_Q // ATT_QPART)
    wide = N_HEADS * HEAD_PAD
    qspec = pl.BlockSpec((ATT_Q, wide), lambda b, i: (b * nq + i, 0))
    return pl.pallas_call(
        _attn_kernel,
        grid=(batch, nq),
        in_specs=[qspec, pl.BlockSpec((seq, wide), lambda b, i: (b, 0)),
                  pl.BlockSpec((wide, seq), lambda b, i: (0, b))],
        out_specs=pl.BlockSpec((ATT_Q, N_HEADS * V_DIM), lambda b, i: (b * nq + i, 0)),
        out_shape=jax.ShapeDtypeStruct((q.shape[0], N_HEADS * V_DIM), BF16),
        scratch_shapes=([pltpu.VMEM((1, ATT_QPART), F32)] * chains + [pltpu.VMEM((V_ROWS, ATT_QPART), F32)] * chains),
        compiler_params=_params("parallel", "arbitrary"),
        name="attention",
    )(q, k, vt)


def _conv_kernel(y_ref, w_ref, b_ref, lng_ref, lnb_ref, og_ref, o_ref, pad_ref, phase_ref):
    seq = y_ref.shape[0]
    pad_ref[0:CONV_HALO, :] = jnp.zeros((CONV_HALO, CONV_WIDTH), F32)
    pad_ref[CONV_HALO:, :] = y_ref[...]
    first = CONV_HALO - (CONV_TAPS - 1)

    def body(i, _):
        r0 = pl.multiple_of(i * CONV_ROWS, CONV_ROWS)
        win = pad_ref[pl.ds(r0, CONV_ROWS + CONV_HALO), :]
        acc = jnp.zeros((CONV_ROWS, CONV_WIDTH), F32)
        for s in range(SUBLANES):
            offs = [o for o in range(first, first + CONV_TAPS) if o % SUBLANES == s]
            span = max(offs) - s + CONV_ROWS
            if s:
                phase_ref[s, 0:span, :] = win[s:s + span, :]
            for o in offs:
                rows = (pad_ref[pl.ds(pl.multiple_of(r0 + o, SUBLANES), CONV_ROWS), :] if s == 0
                        else phase_ref[s, o - s:o - s + CONV_ROWS, :])
                acc = acc + w_ref[o - first:o - first + 1, :] * rows
        acc = acc + b_ref[...]
        xc = acc - jnp.mean(acc, axis=-1, keepdims=True)
        ln = xc * lax.rsqrt(jnp.mean(xc * xc, axis=-1, keepdims=True) + EPS) * lng_ref[...] + lnb_ref[...]
        z = ln * jax.nn.sigmoid(ln)
        o_ref[pl.ds(r0, CONV_ROWS), :] = _rms(z, og_ref[...]).astype(BF16)
        return 0

    lax.fori_loop(0, seq // CONV_ROWS, body, 0)


def _conv_branch(y, w, b, lng, lnb, og, batch, seq):
    spec = pl.BlockSpec((seq, CONV_WIDTH), lambda bi: (bi, 0))
    return pl.pallas_call(
        _conv_kernel,
        grid=(batch,),
        in_specs=[spec, _full(w.shape), _full(b.shape), _full(lng.shape), _full(lnb.shape), _full(og.shape)],
        out_specs=spec,
        out_shape=jax.ShapeDtypeStruct(y.shape, BF16),
        scratch_shapes=[pltpu.VMEM((seq + CONV_HALO, CONV_WIDTH), F32),
                        pltpu.VMEM((SUBLANES, CONV_ROWS + CONV_HALO, CONV_WIDTH), F32)],
        compiler_params=_params("parallel"),
        name="conv_branch",
    )(y, w, b, lng, lnb, og)


def _mix_kernel(attn_ref, conv_ref, x_ref, ga_ref, wo_ref, gffn_ref, rw2_ref, rb_ref,
                tri_ref, x1_ref, h2_ref, idx_ref, gate_ref, rank_ref, cnt_ref, base_ref):
    @pl.when(pl.program_id(0) == 0)
    def _():
        base_ref[...] = jnp.zeros(base_ref.shape, F32)

    size = MIX_ROWS // MIX_GROUPS
    groups = [slice(g * size, (g + 1) * size) for g in range(MIX_GROUPS)]
    dn = (((1,), (1,)), ((), ()))

    def project(rs):
        an = _rms(attn_ref[rs, :].astype(F32), ga_ref[...])
        mixed = jnp.concatenate([an.astype(BF16), conv_ref[rs, :]], axis=1)
        return x_ref[rs, :] + jnp.dot(mixed, wo_ref[...], preferred_element_type=F32)

    def norm_and_logits(g, rs, x1):
        x1_ref[rs, :] = x1
        h2 = _rms(x1, gffn_ref[...])
        _store_tile_rows(h2_ref, _pack_bf16_pairs(h2), PACKED_TILES, first=g * size)
        hi = h2.astype(BF16)
        lo = (h2 - hi.astype(F32)).astype(BF16)
        both = lax.dot_general(rw2_ref[...], hi, dn, preferred_element_type=F32)
        return (both[:N_EXPERTS] + both[N_EXPERTS:]
                + lax.dot_general(rw2_ref[:N_EXPERTS, :], lo, dn, preferred_element_type=F32)) + rb_ref[...]

    def top_k(rs, logits):
        eidx = lax.broadcasted_iota(jnp.int32, logits.shape, 0).astype(F32)
        work = logits
        sels, vals = [], []
        for k in range(TOP_K):
            mx = jnp.max(work, axis=0, keepdims=True)
            first = jnp.min(jnp.where(work == mx, eidx, float(N_EXPERTS)), axis=0, keepdims=True)
            sel = eidx == first
            work = jnp.where(sel, -jnp.inf, work)
            sels.append(sel)
            vals.append(mx)
            idx_ref[k:k + 1, rs] = first.astype(jnp.int32)
        exps = [jnp.exp(v - vals[0]) for v in vals]
        denom = exps[0] + exps[1] + exps[2] + exps[3]
        for k in range(TOP_K):
            gate_ref[k:k + 1, rs] = exps[k] / denom
        return sels

    x1s, logits, sels = {}, {}, {}
    for s in range(MIX_GROUPS + 2):
        if s < MIX_GROUPS:
            x1s[s] = project(groups[s])
        if 0 <= s - 1 < MIX_GROUPS:
            logits[s - 1] = norm_and_logits(s - 1, groups[s - 1], x1s.pop(s - 1))
        if 0 <= s - 2 < MIX_GROUPS:
            sels[s - 2] = top_k(groups[s - 2], logits.pop(s - 2))
    sels = [sels[g] for g in range(MIX_GROUPS)]

    sels = [jnp.concatenate([s[k] for s in sels], axis=1) for k in range(TOP_K)]
    member = jnp.where(sels[0] | sels[1] | sels[2] | sels[3], 1.0, 0.0)
    blocks = [member[:, b * LANES:(b + 1) * LANES] for b in range(MIX_ROWS // LANES)]
    inner = jnp.dot(jnp.concatenate(blocks, axis=0).astype(BF16), tri_ref[...], preferred_element_type=F32)
    offset = base_ref[:, 0:1]
    pieces = []
    for b, blk in enumerate(blocks):
        pieces.append(inner[b * N_EXPERTS:(b + 1) * N_EXPERTS, :] + offset)
        offset = offset + jnp.sum(blk, axis=1, keepdims=True)
    posn = jnp.concatenate(pieces, axis=1)
    for k in range(TOP_K):
        rank_ref[k:k + 1, :] = jnp.sum(jnp.where(sels[k], posn, 0.0), axis=0, keepdims=True).astype(jnp.int32)
    base_ref[...] = jnp.broadcast_to(offset, base_ref.shape)
    cnt_ref[...] = base_ref[...].astype(jnp.int32)


def _mix_stage(attn, conv, x2, ga, wo, gffn, rw2, rb, tri):
    t = x2.shape[0]
    proj = lambda i: (i, 0)
    row = lambda w: pl.BlockSpec((MIX_ROWS, w), proj)
    col = pl.BlockSpec((TOP_K, MIX_ROWS), lambda i: (0, i))
    return pl.pallas_call(
        _mix_kernel,
        grid=(t // MIX_ROWS,),
        in_specs=[row(attn.shape[1]), row(CONV_WIDTH), row(D_MODEL), _full(ga.shape), _full(wo.shape),
                  _full(gffn.shape), _full(rw2.shape), _full(rb.shape), _full(tri.shape)],
        out_specs=[row(D_MODEL), pl.BlockSpec((MIX_ROWS * PACKED_TILES, LANES), proj), col, col, col,
                   _full((N_EXPERTS, LANES))],
        out_shape=[jax.ShapeDtypeStruct((t, D_MODEL), F32),
                   jax.ShapeDtypeStruct((t * PACKED_TILES, LANES), jnp.uint32),
                   jax.ShapeDtypeStruct((TOP_K, t), jnp.int32), jax.ShapeDtypeStruct((TOP_K, t), F32),
                   jax.ShapeDtypeStruct((TOP_K, t), jnp.int32),
                   jax.ShapeDtypeStruct((N_EXPERTS, LANES), jnp.int32)],
        scratch_shapes=[pltpu.VMEM((N_EXPERTS, LANES), F32)],
        compiler_params=_params("arbitrary"),
        name="mix_router",
    )(attn, conv, x2, ga, wo, gffn, rw2, rb, tri)


def _dispatch_kernel(ends_ref, padded_ref, *refs):
    at_refs, (h_ref, xs_ref, zero_ref, hbuf_ref, sems, hsems, zsem) = refs[:TOP_K], refs[TOP_K:]
    tiles = PACKED_TILES
    rows = DISPATCH_ROWS
    block = MOE_STEP * tiles
    n_blocks = xs_ref.shape[0] // block
    i, n = pl.program_id(0), pl.num_programs(0)

    @pl.when(i == 0)
    def _():
        zero_ref[...] = jnp.zeros(zero_ref.shape, zero_ref.dtype)

        def zero_copy(r0):
            return pltpu.make_async_copy(
                zero_ref, xs_ref.at[pl.ds(pl.multiple_of(r0 * tiles, block), block)], zsem)

        for start in (True, False):
            for e in range(N_EXPERTS):
                @pl.when(padded_ref[e] > 0)
                def _():
                    cp = zero_copy(ends_ref[e] - MOE_STEP)
                    cp.start() if start else cp.wait()

            def tail(b, _):
                cp = zero_copy(b * MOE_STEP)
                cp.start() if start else cp.wait()
                return 0

            lax.fori_loop(ends_ref[N_EXPERTS - 1] // MOE_STEP, n_blocks, tail, 0)

    def fetch(step, slot):
        first = pl.multiple_of(step * (rows * tiles), rows * tiles)
        return pltpu.make_async_copy(h_ref.at[pl.ds(first, rows * tiles)], hbuf_ref.at[slot], hsems.at[slot])

    def drain(slot):
        for _ in range(TOP_K):
            pltpu.make_async_copy(hbuf_ref.at[slot], xs_ref.at[pl.ds(0, rows * tiles)], sems.at[slot]).wait()

    @pl.when(i == 0)
    def _():
        fetch(0, 0).start()
        fetch(1, 1).start()

    slot = i % DISPATCH_SLOTS
    fetch(i, slot).wait()

    def issue(t, _):
        for k in range(TOP_K):
            to = xs_ref.at[pl.ds(pl.multiple_of(at_refs[k][t], tiles), tiles)]
            pltpu.make_async_copy(_tile_row(hbuf_ref, t, (slot,), tiles), to,
                                  sems.at[slot]).start(priority=k % DMA_PRIORITIES)
        return 0

    lax.fori_loop(0, rows, issue, 0, unroll=ISSUE_UNROLL)

    @pl.when(i > 0)
    def _():
        drain((i - 1) % DISPATCH_SLOTS)

    @pl.when(i + 2 < n)
    def _():
        fetch(i + 2, (i + 2) % DISPATCH_SLOTS).start()

    @pl.when(i == n - 1)
    def _():
        drain(slot)


def _dispatch(ends, padded, dest, h2, n_blocks):
    t = h2.shape[0] // PACKED_TILES
    assert DISPATCH_SLOTS == 3 and t // DISPATCH_ROWS >= DISPATCH_SLOTS
    grid_spec = pltpu.PrefetchScalarGridSpec(
        num_scalar_prefetch=2,
        grid=(t // DISPATCH_ROWS,),
        in_specs=([pl.BlockSpec((DISPATCH_ROWS,), lambda i, *_: (i,), memory_space=pltpu.SMEM)] * TOP_K
                  + [pl.BlockSpec(memory_space=pl.ANY)]),
        out_specs=pl.BlockSpec(memory_space=pl.ANY),
        scratch_shapes=[pltpu.VMEM((MOE_STEP * PACKED_TILES, LANES), h2.dtype),
                        pltpu.VMEM((DISPATCH_SLOTS, DISPATCH_ROWS * PACKED_TILES, LANES), h2.dtype),
                        pltpu.SemaphoreType.DMA((DISPATCH_SLOTS,)), pltpu.SemaphoreType.DMA((DISPATCH_SLOTS,)),
                        pltpu.SemaphoreType.DMA(())],
    )
    return pl.pallas_call(
        _dispatch_kernel,
        grid_spec=grid_spec,
        out_shape=jax.ShapeDtypeStruct((n_blocks * MOE_STEP * PACKED_TILES, LANES), h2.dtype),
        compiler_params=_params("arbitrary"),
        name="dispatch_rows",
    )(ends, padded, *[dest[k] * PACKED_TILES for k in range(TOP_K)], h2)


def _moe_kernel(be_ref, nact_ref, passes_ref, xs_ref, wgu_ref, bgu_ref, wd_ref, bd_ref, ys_ref, wgu_bf, wd_bf):
    i = pl.program_id(0)
    half = MOE_ROWS // 2

    @pl.when((passes_ref[i] > 0) & ((i == 0) | (be_ref[i] != be_ref[jnp.maximum(i - 1, 0)])))
    def _():
        wgu_bf[...] = wgu_ref[0].astype(BF16)
        wd_bf[...] = wd_ref[0].astype(BF16)

    def blank(first, rows):
        ys_ref[first * ROW_TILES:(first + rows) * ROW_TILES, :] = jnp.zeros((rows * ROW_TILES, LANES), F32)

    def mlp(first, rows):
        x = _unpack_bf16_pairs(_load_tile_rows(xs_ref, rows, tiles=PACKED_TILES, first=first))
        gu = jnp.dot(x, wgu_bf[...], preferred_element_type=F32) + bgu_ref[0]
        gate = jnp.minimum(gu[:, :D_FF], SWIGLU_LIMIT)
        up = jnp.clip(gu[:, D_FF:], -SWIGLU_LIMIT, SWIGLU_LIMIT)
        mid = (up + 1.0) * (gate * jax.nn.sigmoid(gate * SWIGLU_ALPHA))
        y = jnp.dot(mid.astype(BF16), wd_bf[...], preferred_element_type=F32) + bd_ref[0]
        _store_tile_rows(ys_ref, y, first=first)

    for p in range(MOE_STEP // MOE_ROWS):
        first = p * MOE_ROWS

        @pl.when(passes_ref[i] >= 2 * p + 2)
        def _():
            mlp(first, MOE_ROWS)

        @pl.when(passes_ref[i] == 2 * p + 1)
        def _():
            mlp(first, half)
            blank(first + half, half)

        @pl.when(passes_ref[i] <= 2 * p)
        def _():
            blank(first, MOE_ROWS)


def _moe(block_expert, n_active, passes, xs, wgu, bgu, wd, bd):
    block = MOE_STEP * ROW_TILES
    n_blocks = xs.shape[0] // (MOE_STEP * PACKED_TILES)
    grid_spec = pltpu.PrefetchScalarGridSpec(
        num_scalar_prefetch=3,
        grid=(n_blocks,),
        in_specs=[
            pl.BlockSpec((MOE_STEP * PACKED_TILES, LANES), lambda i, be, na, ps: (jnp.minimum(i, na[0] - 1), 0)),
            pl.BlockSpec((1, D_MODEL, 2 * D_FF), lambda i, be, na, ps: (be[i], 0, 0)),
            pl.BlockSpec((1, 1, 2 * D_FF), lambda i, be, na, ps: (be[i], 0, 0)),
            pl.BlockSpec((1, D_FF, D_MODEL), lambda i, be, na, ps: (be[i], 0, 0)),
            pl.BlockSpec((1, 1, D_MODEL), lambda i, be, na, ps: (be[i], 0, 0)),
        ],
        out_specs=pl.BlockSpec((block, LANES), lambda i, be, na, ps: (i, 0)),
        scratch_shapes=[pltpu.VMEM((D_MODEL, 2 * D_FF), BF16), pltpu.VMEM((D_FF, D_MODEL), BF16)],
    )
    return pl.pallas_call(
        _moe_kernel,
        grid_spec=grid_spec,
        out_shape=jax.ShapeDtypeStruct((n_blocks * block, LANES), F32),
        compiler_params=_params("arbitrary"),
        name="expert_mlp",
    )(block_expert, n_active, passes, xs, wgu, bgu, wd, bd)


def _combine_kernel(*refs):
    at_refs = [refs[s * TOP_K:(s + 1) * TOP_K] for s in range(COMBINE_SLOTS)]
    x1_ref, gate_ref, ys_ref, o_ref = refs[COMBINE_SLOTS * TOP_K:COMBINE_SLOTS * TOP_K + 4]
    scratch = refs[COMBINE_SLOTS * TOP_K + 4:]
    dest_ref, dest_next_ref, dest_ahead_ref = at_refs
    bufs, sems = scratch[:COMBINE_SLOTS], scratch[COMBINE_SLOTS]
    i, n = pl.program_id(0), pl.num_programs(0)
    rows = x1_ref.shape[0]
    ahead = COMBINE_SLOTS - 1

    def issue(idx_refs, to, t):
        for k in range(TOP_K):
            source = ys_ref.at[pl.ds(pl.multiple_of(idx_refs[k][t], ROW_TILES), ROW_TILES)]
            pltpu.make_async_copy(source, _tile_row(bufs[to], t, (k,)),
                                  sems.at[to]).start(priority=k % DMA_PRIORITIES)

    def consume(slot, t0):
        sl = pl.ds(t0, ISSUE_UNROLL)
        acc = x1_ref[sl, :]
        for k in range(TOP_K):
            got = bufs[slot].at[k, pl.ds(pl.multiple_of(t0 * ROW_TILES, ISSUE_UNROLL * ROW_TILES),
                                         ISSUE_UNROLL * ROW_TILES)]
            acc = acc + gate_ref[sl, k:k + 1] * _load_tile_rows(got, ISSUE_UNROLL)
        o_ref[sl, :] = acc

    def sweep(idx_ref, to, slot):
        def body(j, _):
            t0 = pl.multiple_of(j * ISSUE_UNROLL, ISSUE_UNROLL)
            if to is not None:
                for t in range(ISSUE_UNROLL):
                    issue(idx_ref, to, t0 + t)
            if slot is not None:
                consume(slot, t0)
            return 0

        lax.fori_loop(0, rows // ISSUE_UNROLL, body, 0, unroll=4)

    @pl.when(i == 0)
    def _():
        sweep(dest_ref, 0, None)
        sweep(dest_next_ref, 1, None)

    for slot in range(COMBINE_SLOTS):
        @pl.when(i % COMBINE_SLOTS == slot)
        def _():
            for k in range(TOP_K):
                pltpu.make_async_copy(ys_ref.at[pl.ds(0, rows * ROW_TILES)], bufs[slot].at[k], sems.at[slot]).wait()

            @pl.when(i + ahead < n)
            def _():
                sweep(dest_ahead_ref, (slot + ahead) % COMBINE_SLOTS, slot)

            @pl.when(i + ahead >= n)
            def _():
                sweep(None, None, slot)


def _combine(dest, x1, gate_rows, ys):
    t = x1.shape[0]
    steps = t // COMBINE_ROWS
    assert COMBINE_SLOTS == 3 and steps >= COMBINE_SLOTS and ISSUE_UNROLL % SUBLANES == 0
    row = lambda w: pl.BlockSpec((COMBINE_ROWS, w), lambda i: (i, 0))
    at = [dest[k] * ROW_TILES for k in range(TOP_K)]
    slots = lambda nxt: [pl.BlockSpec((COMBINE_ROWS,), lambda i: (jnp.minimum(i + nxt, steps - 1),),
                                      memory_space=pltpu.SMEM)] * TOP_K
    return pl.pallas_call(
        _combine_kernel,
        grid=(steps,),
        in_specs=slots(0) + slots(1) + slots(2) + [row(D_MODEL), row(TOP_K), pl.BlockSpec(memory_space=pl.ANY)],
        out_specs=row(D_MODEL),
        out_shape=jax.ShapeDtypeStruct(x1.shape, F32),
        scratch_shapes=([pltpu.VMEM((TOP_K, COMBINE_ROWS * ROW_TILES, LANES), F32)] * COMBINE_SLOTS
                        + [pltpu.SemaphoreType.DMA((COMBINE_SLOTS,))]),
        compiler_params=_params("arbitrary"),
        name="combine_rows",
    )(*(at * COMBINE_SLOTS), x1, gate_rows, ys)


def _head_blocks(w, width):
    r = w.shape[0]
    w = w.reshape(r, N_HEADS, width)
    return jnp.pad(w, ((0, 0), (0, 0), (0, HEAD_PAD - width))).reshape(r, N_HEADS * HEAD_PAD)


def _lane_row(v, offset=0):
    return jnp.pad(v, (offset, HEAD_PAD - offset - v.shape[0])).reshape(1, HEAD_PAD)


def _swap_rope(w, sign):
    lo, hi = w[..., NOPE_DIM:NOPE_DIM + HALF_ROPE], w[..., NOPE_DIM + HALF_ROPE:QK_DIM]
    return jnp.concatenate([jnp.zeros_like(w[..., :NOPE_DIM]), sign * hi, lo], axis=-1)


def kernel(x, positions, norm_mix_g, w_in, q_latent_g, w_uq, kv_latent_g, w_ukv, q_head_g, k_head_g, conv_dw_w, conv_dw_b, conv_ln_g, conv_ln_b, attn_out_g, conv_out_g, w_out, norm_ffn_g, router_w, router_b, w_gate_up, b_gate_up, w_down, b_down):
    batch, seq, d_model = x.shape
    t = batch * seq
    assert d_model == D_MODEL and seq % max(ATT_Q, CONV_ROWS) == 0
    assert all(t % rows == 0 for rows in (IN_ROWS, MIX_ROWS, DISPATCH_ROWS, COMBINE_ROWS))
    depth = norm_mix_g.shape[0]
    x2 = x.reshape(t, D_MODEL)
    pos = positions.reshape(1, t)
    invf = (1.0 / (ROPE_THETA ** (jnp.arange(0, ROPE_DIM, 2, dtype=F32) / ROPE_DIM))).reshape(HALF_ROPE, 1)
    tri = jnp.triu(jnp.ones((LANES, LANES), BF16), 1)
    o_kv = Q_LORA
    o_pe = o_kv + KV_LORA
    o_u = o_pe + ROPE_DIM

    for l in range(depth):
        wi = w_in[l]
        w_pe = wi[:, o_pe:o_u]
        w_pe_sw = jnp.concatenate([-w_pe[:, HALF_ROPE:], w_pe[:, :HALF_ROPE]], axis=1)
        pe_block = lambda w: jnp.pad(w, ((0, 0), (NOPE_DIM, HEAD_PAD - QK_DIM)))
        win = jnp.concatenate([wi[:, :o_pe], pe_block(w_pe), pe_block(w_pe_sw), wi[:, o_u:]], axis=1).astype(BF16)
        wq = w_uq[l].reshape(Q_LORA, N_HEADS, QK_DIM)
        wuq = _head_blocks(w_uq[l], QK_DIM).astype(BF16)
        wuqs = _head_blocks(_swap_rope(wq, -1.0).reshape(Q_LORA, -1), QK_DIM).astype(BF16)
        wkv = w_ukv[l].reshape(KV_LORA, N_HEADS, NOPE_DIM + V_DIM)
        wuk = _head_blocks(wkv[:, :, :NOPE_DIM].reshape(KV_LORA, -1), NOPE_DIM).astype(BF16)
        wuv = _head_blocks(wkv[:, :, NOPE_DIM:].reshape(KV_LORA, -1), V_DIM).T.astype(BF16)
        ga = attn_out_g[l].reshape(1, -1)
        wo = w_out[l].astype(BF16)
        rwt = router_w[l].T
        rwh = rwt.astype(BF16)
        rw2 = jnp.concatenate([rwh, (rwt - rwh.astype(F32)).astype(BF16)], axis=0)

        q, k, v, y = _input_stage(
            x2, pos, invf, norm_mix_g[l].reshape(1, -1), win, q_latent_g[l].reshape(1, -1), wuq, wuqs,
            kv_latent_g[l].reshape(1, -1), wuk, wuv, _lane_row(q_head_g[l]),
            _lane_row(_swap_rope(q_head_g[l], 1.0)), _lane_row(k_head_g[l]), _lane_row(_swap_rope(k_head_g[l], 1.0)))
        attn = _attention(q, k, v, batch, seq)
        conv = _conv_branch(y, conv_dw_w[l], conv_dw_b[l].reshape(1, -1), conv_ln_g[l].reshape(1, -1),
                            conv_ln_b[l].reshape(1, -1), conv_out_g[l].reshape(1, -1), batch, seq)
        x1, h2, idx, gate, rank, cnt = _mix_stage(
            attn, conv, x2, ga, wo, norm_ffn_g[l].reshape(1, -1), rw2, router_b[l].reshape(-1, 1), tri)

        counts = cnt[:, 0]
        padded = (counts + MOE_STEP - 1) // MOE_STEP * MOE_STEP
        ends = jnp.cumsum(padded)
        starts = ends - padded
        experts = jnp.arange(N_EXPERTS, dtype=jnp.int32)
        dest = rank + jnp.sum(jnp.where(idx[None] == experts[:, None, None], starts[:, None, None], 0), axis=0)
        n_blocks = (t * TOP_K + N_EXPERTS * (MOE_STEP - 1)) // MOE_STEP
        n_active = (ends[-1] // MOE_STEP).astype(jnp.int32)
        step = jnp.arange(n_blocks, dtype=jnp.int32)
        blk = jnp.minimum(step, n_active - 1)
        be = jnp.minimum(jnp.sum((ends[None, :] <= (blk * MOE_STEP)[:, None]).astype(jnp.int32), axis=1),
                         N_EXPERTS - 1)
        real = jnp.clip((starts + counts)[be] - step * MOE_STEP, 0, MOE_STEP)
        passes = jnp.where(step < n_active, -(-real // (MOE_ROWS // 2)), 0).astype(jnp.int32)

        xs = _dispatch(ends.astype(jnp.int32), padded.astype(jnp.int32), dest, h2, n_blocks)
        ys = _moe(be, n_active.reshape(1), passes, xs, w_gate_up[l], b_gate_up[l].reshape(N_EXPERTS, 1, -1),
                  w_down[l], b_down[l].reshape(N_EXPERTS, 1, -1))
        x2 = _combine(dest, x1, gate.T, ys)
    return x2.reshape(batch, seq, D_MODEL)
```

```python
import jax
import jax.numpy as jnp
from jax import lax
from jax.experimental import pallas as pl
from jax.experimental.pallas import tpu as pltpu

D_MODEL = 1024
N_HEADS = 8
NOPE_DIM = 64
ROPE_DIM = 32
QK_DIM = NOPE_DIM + ROPE_DIM
V_DIM = 64
Q_LORA = 384
KV_LORA = 128
CONV_WIDTH = 512
CONV_TAPS = 31
N_EXPERTS = 32
TOP_K = 4
D_FF = 1024
CHUNK = 64
ROPE_THETA = 10000.0
EPS = 1e-6
SWIGLU_ALPHA = 1.702
SWIGLU_LIMIT = 7.0
LOG2_E = 1.4426950408889634

LANES = 128
SUBLANES = 8
HEAD_PAD = LANES
HALF_ROPE = ROPE_DIM // 2
V_ROWS = V_DIM + 2 * SUBLANES
VMEM_LIMIT_BYTES = 56 * 1024 * 1024

IN_ROWS = 1024
ATT_Q = 1024
ATT_K = 512
ATT_QPART = 256
ATT_AHEAD = 4
CONV_ROWS = 256
CONV_HALO = 32
MIX_ROWS = 1024
MIX_GROUPS = 2
DISPATCH_ROWS = 512
MOE_ROWS = 512
MOE_STEP = 2 * MOE_ROWS
DISPATCH_SLOTS = 3
COMBINE_ROWS = 512
COMBINE_SLOTS = 3
ISSUE_UNROLL = 8
DMA_PRIORITIES = 2

F32 = jnp.float32
BF16 = jnp.bfloat16


def _params(*semantics):
    return pltpu.CompilerParams(dimension_semantics=semantics, vmem_limit_bytes=VMEM_LIMIT_BYTES)


def _rms(x, g):
    return x * lax.rsqrt(jnp.mean(x * x, axis=-1, keepdims=True) + EPS) * g


def _full(shape):
    return pl.BlockSpec(shape, lambda *_: (0,) * len(shape))


ROW_TILES = D_MODEL // LANES
PACKED_TILES = ROW_TILES // 2
assert ROW_TILES == SUBLANES


def _load_tile_rows(ref, rows, lead=(), tiles=ROW_TILES, first=0):
    return jnp.concatenate(
        [ref[lead + (pl.ds(first * tiles + c, rows, stride=tiles), slice(None))] for c in range(tiles)], axis=1)


def _store_tile_rows(ref, value, tiles=ROW_TILES, first=0):
    rows = value.shape[0]
    for c in range(tiles):
        ref[pl.ds(first * tiles + c, rows, stride=tiles), :] = value[:, c * LANES:(c + 1) * LANES]


def _tile_row(ref, r, lead=(), tiles=ROW_TILES):
    return ref.at[lead + (pl.ds(pl.multiple_of(r * tiles, tiles), tiles),)]


def _pack_bf16_pairs(x):
    half = x.shape[1] // 2
    bits = lax.bitcast_convert_type(x.astype(BF16).astype(F32), jnp.uint32)
    return (bits[:, :half] >> 16) | bits[:, half:]


def _unpack_bf16_pairs(p):
    left = lax.bitcast_convert_type(p << 16, F32)
    right = lax.bitcast_convert_type(p & jnp.uint32(0xFFFF0000), F32)
    return jnp.concatenate([left, right], axis=1).astype(BF16)


def _in_kernel(x_ref, pos_ref, invf_ref, gmix_ref, win_ref, gql_ref, wuq_ref, wuqs_ref, gkvl_ref, wuk_ref,
               wuv_ref, gq_ref, gqs_ref, gk_ref, gks_ref, q_ref, k_ref, v_ref, y_ref):
    h = _rms(x_ref[...], gmix_ref[...]).astype(BF16)
    p = jnp.dot(h, win_ref[...], preferred_element_type=F32)
    o_kv = Q_LORA
    o_pe = o_kv + KV_LORA
    o_ps = o_pe + HEAD_PAD
    o_a = o_ps + HEAD_PAD
    o_g = o_a + CONV_WIDTH
    cq, ckv, kpe, kpe_sw = p[:, :o_kv], p[:, o_kv:o_pe], p[:, o_pe:o_ps], p[:, o_ps:o_a]
    y_ref[...] = p[:, o_a:o_g] * jax.nn.sigmoid(p[:, o_g:])

    cqn = _rms(cq, gql_ref[...]).astype(BF16)
    q = jnp.dot(cqn, wuq_ref[...], preferred_element_type=F32)
    q_sw = jnp.dot(cqn, wuqs_ref[...], preferred_element_type=F32)
    ckvn = _rms(ckv, gkvl_ref[...]).astype(BF16)
    kn = jnp.dot(ckvn, wuk_ref[...], preferred_element_type=F32)
    wide_row = lax.broadcasted_iota(jnp.int32, (N_HEADS * HEAD_PAD, 1), 0)
    ones_row = jnp.where(wide_row % HEAD_PAD == V_DIM, 1.0, 0.0)
    vt = lax.dot_general(wuv_ref[...], ckvn, (((1,), (1,)), ((), ())), preferred_element_type=F32)
    v_ref[...] = (vt + ones_row).astype(BF16)

    ang = invf_ref[...] * pos_ref[...].astype(F32)
    cos_t, sin_t = jnp.cos(ang), jnp.sin(ang)
    tab = jnp.concatenate([jnp.ones((NOPE_DIM, ang.shape[1]), F32), cos_t, cos_t, sin_t, sin_t], axis=0).T
    lane = lax.broadcasted_iota(jnp.int32, (1, HEAD_PAD), 1)
    ctab = jnp.where(lane < QK_DIM, tab, 0.0)
    stab = jnp.where((lane >= NOPE_DIM) & (lane < QK_DIM), pltpu.roll(tab, QK_DIM, 1), 0.0)
    scale = QK_DIM ** -0.5 * LOG2_E
    cq_tab = ctab * (gq_ref[...] * scale)
    sq_tab = stab * (gqs_ref[...] * scale)
    ck_tab = ctab * gk_ref[...]
    sk_tab = stab * gks_ref[...]
    k_pe = kpe * ck_tab + kpe_sw * sk_tab
    ss_pe = jnp.sum(kpe * kpe, axis=-1, keepdims=True)
    for hd in range(N_HEADS):
        sl = slice(hd * HEAD_PAD, (hd + 1) * HEAD_PAD)
        qh = q[:, sl]
        rq = lax.rsqrt(jnp.sum(qh * qh, axis=-1, keepdims=True) * (1.0 / QK_DIM) + EPS)
        q_ref[:, sl] = ((qh * cq_tab + q_sw[:, sl] * sq_tab) * rq).astype(BF16)
        kh = kn[:, sl]
        rk = lax.rsqrt((jnp.sum(kh * kh, axis=-1, keepdims=True) + ss_pe) * (1.0 / QK_DIM) + EPS)
        k_ref[:, sl] = ((kh * ck_tab + k_pe) * rk).astype(BF16)


def _input_stage(x2, pos, invf, gmix, win, gql, wuq, wuqs, gkvl, wuk, wuv, gq, gqs, gk, gks):
    t = x2.shape[0]
    wide = N_HEADS * HEAD_PAD
    row = lambda w: pl.BlockSpec((IN_ROWS, w), lambda i: (i, 0))
    consts = (invf, gmix, win, gql, wuq, wuqs, gkvl, wuk, wuv, gq, gqs, gk, gks)
    return pl.pallas_call(
        _in_kernel,
        grid=(t // IN_ROWS,),
        in_specs=[row(D_MODEL), pl.BlockSpec((1, IN_ROWS), lambda i: (0, i))] + [_full(a.shape) for a in consts],
        out_specs=[row(wide), row(wide), pl.BlockSpec((wide, IN_ROWS), lambda i: (0, i)), row(CONV_WIDTH)],
        out_shape=[jax.ShapeDtypeStruct((t, wide), BF16), jax.ShapeDtypeStruct((t, wide), BF16),
                   jax.ShapeDtypeStruct((wide, t), BF16), jax.ShapeDtypeStruct((t, CONV_WIDTH), F32)],
        compiler_params=_params("parallel"),
        name="input_stage",
    )(x2, pos, *consts)


def _attn_kernel(q_ref, k_ref, vt_ref, o_ref, *scratch):
    parts, block_parts, blocks = ATT_Q // ATT_QPART, ATT_K // ATT_QPART, ATT_Q // ATT_K
    units = [(hd, part) for hd in range(N_HEADS) for part in range(parts)]
    m_refs, acc_refs = scratch[:len(units)], scratch[len(units):]
    i = pl.program_id(1)
    dn = (((1,), (1,)), ((), ()))

    def step(r0, block, diagonal):
        todo = [u for u, (_, part) in enumerate(units) if block is None or part // block_parts == block]
        old = None if diagonal else {u: (m_refs[u][...], acc_refs[u][...]) for u in todo}
        new = {}

        def keys(part):
            return (part % block_parts + 1) * ATT_QPART if diagonal else ATT_K

        def scores(u):
            hd, part = units[u]
            sl = slice(hd * HEAD_PAD, (hd + 1) * HEAD_PAD)
            return lax.dot_general(k_ref[pl.ds(r0, keys(part)), sl],
                                   q_ref[part * ATT_QPART:(part + 1) * ATT_QPART, sl], dn,
                                   preferred_element_type=F32)

        ahead = [scores(u) for u in todo[:ATT_AHEAD]]
        for n, u in enumerate(todo):
            hd, part = units[u]
            s = ahead.pop(0)
            if n + ATT_AHEAD < len(todo):
                ahead.append(scores(todo[n + ATT_AHEAD]))
            if diagonal:
                kc = lax.broadcasted_iota(jnp.int32, s.shape, 0) // CHUNK
                qc = (lax.broadcasted_iota(jnp.int32, s.shape, 1) + part % block_parts * ATT_QPART) // CHUNK
                s = jnp.where(kc <= qc, s, -jnp.inf)
            m_new = jnp.max(s, axis=0, keepdims=True)
            if not diagonal:
                m_new = jnp.maximum(old[u][0], m_new)
            acc = jnp.dot(vt_ref[hd * HEAD_PAD:hd * HEAD_PAD + V_ROWS, pl.ds(r0, keys(part))],
                          jnp.exp2(s - m_new).astype(BF16), preferred_element_type=F32)
            if not diagonal:
                acc = jnp.exp2(old[u][0] - m_new) * old[u][1] + acc
            new[u] = (m_new, acc)
        for u in todo:
            m_refs[u][...], acc_refs[u][...] = new[u]

    tile0 = pl.multiple_of(i * ATT_Q, ATT_Q)
    for block in range(blocks):
        step(tile0 + block * ATT_K, block, True)
        for earlier in range(block):
            step(tile0 + earlier * ATT_K, block, False)

    def body(j, _):
        step(pl.multiple_of(j * ATT_K, ATT_K), None, False)
        return 0

    lax.fori_loop(0, i * blocks, body, 0)
    for pair in range(N_HEADS // 2):
        for part in range(parts):
            halves = []
            for hd in (2 * pair, 2 * pair + 1):
                acc = acc_refs[hd * parts + part][...]
                halves.append(acc[:V_DIM, :] / acc[V_DIM:V_DIM + 1, :])
            o_ref[part * ATT_QPART:(part + 1) * ATT_QPART, pair * LANES:(pair + 1) * LANES] = (
                jnp.concatenate(halves, axis=0).T.astype(BF16))


def _attention(q, k, vt, batch, seq):
    assert seq % ATT_Q == 0 and ATT_Q % ATT_K == 0 and ATT_K % ATT_QPART == 0 and ATT_QPART % CHUNK == 0
    nq = seq // ATT_Q
    chains = N_HEADS * (ATT_Q // ATT_QPART)
    wide = N_HEADS * HEAD_PAD
    qspec = pl.BlockSpec((ATT_Q, wide), lambda b, i: (b * nq + i, 0))
    return pl.pallas_call(
        _attn_kernel,
        grid=(batch, nq),
        in_specs=[qspec, pl.BlockSpec((seq, wide), lambda b, i: (b, 0)),
                  pl.BlockSpec((wide, seq), lambda b, i: (0, b))],
        out_specs=pl.BlockSpec((ATT_Q, N_HEADS * V_DIM), lambda b, i: (b * nq + i, 0)),
        out_shape=jax.ShapeDtypeStruct((q.shape[0], N_HEADS * V_DIM), BF16),
        scratch_shapes=([pltpu.VMEM((1, ATT_QPART), F32)] * chains + [pltpu.VMEM((V_ROWS, ATT_QPART), F32)] * chains),
        compiler_params=_params("parallel", "arbitrary"),
        name="attention",
    )(q, k, vt)


def _conv_kernel(y_ref, w_ref, b_ref, lng_ref, lnb_ref, og_ref, o_ref, pad_ref, phase_ref):
    seq = y_ref.shape[0]
    pad_ref[0:CONV_HALO, :] = jnp.zeros((CONV_HALO, CONV_WIDTH), F32)
    pad_ref[CONV_HALO:, :] = y_ref[...]
    first = CONV_HALO - (CONV_TAPS - 1)

    def body(i, _):
        r0 = pl.multiple_of(i * CONV_ROWS, CONV_ROWS)
        win = pad_ref[pl.ds(r0, CONV_ROWS + CONV_HALO), :]
        acc = jnp.zeros((CONV_ROWS, CONV_WIDTH), F32)
        for s in range(SUBLANES):
            offs = [o for o in range(first, first + CONV_TAPS) if o % SUBLANES == s]
            span = max(offs) - s + CONV_ROWS
            if s:
                phase_ref[s, 0:span, :] = win[s:s + span, :]
            for o in offs:
                rows = (pad_ref[pl.ds(pl.multiple_of(r0 + o, SUBLANES), CONV_ROWS), :] if s == 0
                        else phase_ref[s, o - s:o - s + CONV_ROWS, :])
                acc = acc + w_ref[o - first:o - first + 1, :] * rows
        acc = acc + b_ref[...]
        xc = acc - jnp.mean(acc, axis=-1, keepdims=True)
        ln = xc * lax.rsqrt(jnp.mean(xc * xc, axis=-1, keepdims=True) + EPS) * lng_ref[...] + lnb_ref[...]
        z = ln * jax.nn.sigmoid(ln)
        o_ref[pl.ds(r0, CONV_ROWS), :] = _rms(z, og_ref[...]).astype(BF16)
        return 0

    lax.fori_loop(0, seq // CONV_ROWS, body, 0)


def _conv_branch(y, w, b, lng, lnb, og, batch, seq):
    spec = pl.BlockSpec((seq, CONV_WIDTH), lambda bi: (bi, 0))
    return pl.pallas_call(
        _conv_kernel,
        grid=(batch,),
        in_specs=[spec, _full(w.shape), _full(b.shape), _full(lng.shape), _full(lnb.shape), _full(og.shape)],
        out_specs=spec,
        out_shape=jax.ShapeDtypeStruct(y.shape, BF16),
        scratch_shapes=[pltpu.VMEM((seq + CONV_HALO, CONV_WIDTH), F32),
                        pltpu.VMEM((SUBLANES, CONV_ROWS + CONV_HALO, CONV_WIDTH), F32)],
        compiler_params=_params("parallel"),
        name="conv_branch",
    )(y, w, b, lng, lnb, og)


def _mix_kernel(attn_ref, conv_ref, x_ref, ga_ref, wo_ref, gffn_ref, rw2_ref, rb_ref,
                tri_ref, x1_ref, h2_ref, idx_ref, gate_ref, rank_ref, cnt_ref, base_ref):
    @pl.when(pl.program_id(0) == 0)
    def _():
        base_ref[...] = jnp.zeros(base_ref.shape, F32)

    size = MIX_ROWS // MIX_GROUPS
    groups = [slice(g * size, (g + 1) * size) for g in range(MIX_GROUPS)]
    dn = (((1,), (1,)), ((), ()))

    def project(rs):
        an = _rms(attn_ref[rs, :].astype(F32), ga_ref[...])
        mixed = jnp.concatenate([an.astype(BF16), conv_ref[rs, :]], axis=1)
        return x_ref[rs, :] + jnp.dot(mixed, wo_ref[...], preferred_element_type=F32)

    def norm_and_logits(g, rs, x1):
        x1_ref[rs, :] = x1
        h2 = _rms(x1, gffn_ref[...])
        _store_tile_rows(h2_ref, _pack_bf16_pairs(h2), PACKED_TILES, first=g * size)
        hi = h2.astype(BF16)
        lo = (h2 - hi.astype(F32)).astype(BF16)
        both = lax.dot_general(rw2_ref[...], hi, dn, preferred_element_type=F32)
        return (both[:N_EXPERTS] + both[N_EXPERTS:]
                + lax.dot_general(rw2_ref[:N_EXPERTS, :], lo, dn, preferred_element_type=F32)) + rb_ref[...]

    def top_k(rs, logits):
        eidx = lax.broadcasted_iota(jnp.int32, logits.shape, 0).astype(F32)
        work = logits
        sels, vals = [], []
        for k in range(TOP_K):
            mx = jnp.max(work, axis=0, keepdims=True)
            first = jnp.min(jnp.where(work == mx, eidx, float(N_EXPERTS)), axis=0, keepdims=True)
            sel = eidx == first
            work = jnp.where(sel, -jnp.inf, work)
            sels.append(sel)
            vals.append(mx)
            idx_ref[k:k + 1, rs] = first.astype(jnp.int32)
        exps = [jnp.exp(v - vals[0]) for v in vals]
        denom = exps[0] + exps[1] + exps[2] + exps[3]
        for k in range(TOP_K):
            gate_ref[k:k + 1, rs] = exps[k] / denom
        return sels

    x1s, logits, sels = {}, {}, {}
    for s in range(MIX_GROUPS + 2):
        if s < MIX_GROUPS:
            x1s[s] = project(groups[s])
        if 0 <= s - 1 < MIX_GROUPS:
            logits[s - 1] = norm_and_logits(s - 1, groups[s - 1], x1s.pop(s - 1))
        if 0 <= s - 2 < MIX_GROUPS:
            sels[s - 2] = top_k(groups[s - 2], logits.pop(s - 2))
    sels = [sels[g] for g in range(MIX_GROUPS)]

    sels = [jnp.concatenate([s[k] for s in sels], axis=1) for k in range(TOP_K)]
    member = jnp.where(sels[0] | sels[1] | sels[2] | sels[3], 1.0, 0.0)
    blocks = [member[:, b * LANES:(b + 1) * LANES] for b in range(MIX_ROWS // LANES)]
    inner = jnp.dot(jnp.concatenate(blocks, axis=0).astype(BF16), tri_ref[...], preferred_element_type=F32)
    offset = base_ref[:, 0:1]
    pieces = []
    for b, blk in enumerate(blocks):
        pieces.append(inner[b * N_EXPERTS:(b + 1) * N_EXPERTS, :] + offset)
        offset = offset + jnp.sum(blk, axis=1, keepdims=True)
    posn = jnp.concatenate(pieces, axis=1)
    for k in range(TOP_K):
        rank_ref[k:k + 1, :] = jnp.sum(jnp.where(sels[k], posn, 0.0), axis=0, keepdims=True).astype(jnp.int32)
    base_ref[...] = jnp.broadcast_to(offset, base_ref.shape)
    cnt_ref[...] = base_ref[...].astype(jnp.int32)


def _mix_stage(attn, conv, x2, ga, wo, gffn, rw2, rb, tri):
    t = x2.shape[0]
    proj = lambda i: (i, 0)
    row = lambda w: pl.BlockSpec((MIX_ROWS, w), proj)
    col = pl.BlockSpec((TOP_K, MIX_ROWS), lambda i: (0, i))
    return pl.pallas_call(
        _mix_kernel,
        grid=(t // MIX_ROWS,),
        in_specs=[row(attn.shape[1]), row(CONV_WIDTH), row(D_MODEL), _full(ga.shape), _full(wo.shape),
                  _full(gffn.shape), _full(rw2.shape), _full(rb.shape), _full(tri.shape)],
        out_specs=[row(D_MODEL), pl.BlockSpec((MIX_ROWS * PACKED_TILES, LANES), proj), col, col, col,
                   _full((N_EXPERTS, LANES))],
        out_shape=[jax.ShapeDtypeStruct((t, D_MODEL), F32),
                   jax.ShapeDtypeStruct((t * PACKED_TILES, LANES), jnp.uint32),
                   jax.ShapeDtypeStruct((TOP_K, t), jnp.int32), jax.ShapeDtypeStruct((TOP_K, t), F32),
                   jax.ShapeDtypeStruct((TOP_K, t), jnp.int32),
                   jax.ShapeDtypeStruct((N_EXPERTS, LANES), jnp.int32)],
        scratch_shapes=[pltpu.VMEM((N_EXPERTS, LANES), F32)],
        compiler_params=_params("arbitrary"),
        name="mix_router",
    )(attn, conv, x2, ga, wo, gffn, rw2, rb, tri)


def _dispatch_kernel(ends_ref, padded_ref, dest_ref, h_ref, xs_ref, zero_ref, hbuf_ref, sems, hsems, zsem):
    tiles = PACKED_TILES
    rows = DISPATCH_ROWS
    block = MOE_STEP * tiles
    n_blocks = xs_ref.shape[0] // block
    i, n = pl.program_id(0), pl.num_programs(0)

    @pl.when(i == 0)
    def _():
        zero_ref[...] = jnp.zeros(zero_ref.shape, zero_ref.dtype)

        def zero_copy(r0):
            return pltpu.make_async_copy(
                zero_ref, xs_ref.at[pl.ds(pl.multiple_of(r0 * tiles, block), block)], zsem)

        for start in (True, False):
            for e in range(N_EXPERTS):
                @pl.when(padded_ref[e] > 0)
                def _():
                    cp = zero_copy(ends_ref[e] - MOE_STEP)
                    cp.start() if start else cp.wait()

            def tail(b, _):
                cp = zero_copy(b * MOE_STEP)
                cp.start() if start else cp.wait()
                return 0

            lax.fori_loop(ends_ref[N_EXPERTS - 1] // MOE_STEP, n_blocks, tail, 0)

    def fetch(step, slot):
        first = pl.multiple_of(step * (rows * tiles), rows * tiles)
        return pltpu.make_async_copy(h_ref.at[pl.ds(first, rows * tiles)], hbuf_ref.at[slot], hsems.at[slot])

    def drain(slot):
        for _ in range(TOP_K):
            pltpu.make_async_copy(hbuf_ref.at[slot], xs_ref.at[pl.ds(0, rows * tiles)], sems.at[slot]).wait()

    @pl.when(i == 0)
    def _():
        fetch(0, 0).start()
        fetch(1, 1).start()

    slot = i % DISPATCH_SLOTS
    fetch(i, slot).wait()

    def issue(t, _):
        for k in range(TOP_K):
            pltpu.make_async_copy(_tile_row(hbuf_ref, t, (slot,), tiles), _tile_row(xs_ref, dest_ref[k, t], tiles=tiles),
                                  sems.at[slot]).start(priority=k % DMA_PRIORITIES)
        return 0

    lax.fori_loop(0, rows, issue, 0, unroll=ISSUE_UNROLL)

    @pl.when(i > 0)
    def _():
        drain((i - 1) % DISPATCH_SLOTS)

    @pl.when(i + 2 < n)
    def _():
        fetch(i + 2, (i + 2) % DISPATCH_SLOTS).start()

    @pl.when(i == n - 1)
    def _():
        drain(slot)


def _dispatch(ends, padded, dest, h2, n_blocks):
    t = h2.shape[0] // PACKED_TILES
    assert DISPATCH_SLOTS == 3 and t // DISPATCH_ROWS >= DISPATCH_SLOTS
    grid_spec = pltpu.PrefetchScalarGridSpec(
        num_scalar_prefetch=2,
        grid=(t // DISPATCH_ROWS,),
        in_specs=[pl.BlockSpec((TOP_K, DISPATCH_ROWS), lambda i, *_: (0, i), memory_space=pltpu.SMEM),
                  pl.BlockSpec(memory_space=pl.ANY)],
        out_specs=pl.BlockSpec(memory_space=pl.ANY),
        scratch_shapes=[pltpu.VMEM((MOE_STEP * PACKED_TILES, LANES), h2.dtype),
                        pltpu.VMEM((DISPATCH_SLOTS, DISPATCH_ROWS * PACKED_TILES, LANES), h2.dtype),
                        pltpu.SemaphoreType.DMA((DISPATCH_SLOTS,)), pltpu.SemaphoreType.DMA((DISPATCH_SLOTS,)),
                        pltpu.SemaphoreType.DMA(())],
    )
    return pl.pallas_call(
        _dispatch_kernel,
        grid_spec=grid_spec,
        out_shape=jax.ShapeDtypeStruct((n_blocks * MOE_STEP * PACKED_TILES, LANES), h2.dtype),
        compiler_params=_params("arbitrary"),
        name="dispatch_rows",
    )(ends, padded, dest, h2)


def _moe_kernel(be_ref, nact_ref, passes_ref, xs_ref, wgu_ref, bgu_ref, wd_ref, bd_ref, ys_ref, wgu_bf, wd_bf):
    i = pl.program_id(0)
    half = MOE_ROWS // 2

    @pl.when((passes_ref[i] > 0) & ((i == 0) | (be_ref[i] != be_ref[jnp.maximum(i - 1, 0)])))
    def _():
        wgu_bf[...] = wgu_ref[0].astype(BF16)
        wd_bf[...] = wd_ref[0].astype(BF16)

    def blank(first, rows):
        ys_ref[first * ROW_TILES:(first + rows) * ROW_TILES, :] = jnp.zeros((rows * ROW_TILES, LANES), F32)

    def mlp(first, rows):
        x = _unpack_bf16_pairs(_load_tile_rows(xs_ref, rows, tiles=PACKED_TILES, first=first))
        gu = jnp.dot(x, wgu_bf[...], preferred_element_type=F32) + bgu_ref[0]
        gate = jnp.minimum(gu[:, :D_FF], SWIGLU_LIMIT)
        up = jnp.clip(gu[:, D_FF:], -SWIGLU_LIMIT, SWIGLU_LIMIT)
        mid = (up + 1.0) * (gate * jax.nn.sigmoid(gate * SWIGLU_ALPHA))
        y = jnp.dot(mid.astype(BF16), wd_bf[...], preferred_element_type=F32) + bd_ref[0]
        _store_tile_rows(ys_ref, y, first=first)

    full = MOE_STEP // half

    @pl.when(passes_ref[i] == full)
    def _():
        for p in range(MOE_STEP // MOE_ROWS):
            mlp(p * MOE_ROWS, MOE_ROWS)

    for p in range(MOE_STEP // MOE_ROWS):
        first = p * MOE_ROWS

        if 2 * p + 2 < full:
            @pl.when((passes_ref[i] >= 2 * p + 2) & (passes_ref[i] < full))
            def _():
                mlp(first, MOE_ROWS)

        @pl.when(passes_ref[i] == 2 * p + 1)
        def _():
            mlp(first, half)
            blank(first + half, half)

        @pl.when(passes_ref[i] <= 2 * p)
        def _():
            blank(first, MOE_ROWS)


def _moe(block_expert, n_active, passes, xs, wgu, bgu, wd, bd):
    block = MOE_STEP * ROW_TILES
    n_blocks = xs.shape[0] // (MOE_STEP * PACKED_TILES)
    grid_spec = pltpu.PrefetchScalarGridSpec(
        num_scalar_prefetch=3,
        grid=(n_blocks,),
        in_specs=[
            pl.BlockSpec((MOE_STEP * PACKED_TILES, LANES), lambda i, be, na, ps: (jnp.minimum(i, na[0] - 1), 0)),
            pl.BlockSpec((1, D_MODEL, 2 * D_FF), lambda i, be, na, ps: (be[i], 0, 0)),
            pl.BlockSpec((1, 1, 2 * D_FF), lambda i, be, na, ps: (be[i], 0, 0)),
            pl.BlockSpec((1, D_FF, D_MODEL), lambda i, be, na, ps: (be[i], 0, 0)),
            pl.BlockSpec((1, 1, D_MODEL), lambda i, be, na, ps: (be[i], 0, 0)),
        ],
        out_specs=pl.BlockSpec((block, LANES), lambda i, be, na, ps: (i, 0)),
        scratch_shapes=[pltpu.VMEM((D_MODEL, 2 * D_FF), BF16), pltpu.VMEM((D_FF, D_MODEL), BF16)],
    )
    return pl.pallas_call(
        _moe_kernel,
        grid_spec=grid_spec,
        out_shape=jax.ShapeDtypeStruct((n_blocks * block, LANES), F32),
        compiler_params=_params("arbitrary"),
        name="expert_mlp",
    )(block_expert, n_active, passes, xs, wgu, bgu, wd, bd)


def _combine_kernel(dest_ref, dest_next_ref, dest_ahead_ref, x1_ref, gate_ref, ys_ref, o_ref, *scratch):
    bufs, sems = scratch[:COMBINE_SLOTS], scratch[COMBINE_SLOTS]
    i, n = pl.program_id(0), pl.num_programs(0)
    rows = x1_ref.shape[0]
    ahead = COMBINE_SLOTS - 1

    def issue(idx_ref, to, t):
        for k in range(TOP_K):
            pltpu.make_async_copy(_tile_row(ys_ref, idx_ref[k * rows + t]), _tile_row(bufs[to], t, (k,)),
                                  sems.at[to]).start(priority=k % DMA_PRIORITIES)

    def consume(slot, t0):
        sl = pl.ds(t0, ISSUE_UNROLL)
        acc = x1_ref[sl, :]
        for k in range(TOP_K):
            got = bufs[slot].at[k, pl.ds(pl.multiple_of(t0 * ROW_TILES, ISSUE_UNROLL * ROW_TILES),
                                         ISSUE_UNROLL * ROW_TILES)]
            acc = acc + gate_ref[sl, k:k + 1] * _load_tile_rows(got, ISSUE_UNROLL)
        o_ref[sl, :] = acc

    def sweep(idx_ref, to, slot):
        def body(j, _):
            t0 = pl.multiple_of(j * ISSUE_UNROLL, ISSUE_UNROLL)
            if to is not None:
                for t in range(ISSUE_UNROLL):
                    issue(idx_ref, to, t0 + t)
            if slot is not None:
                consume(slot, t0)
            return 0

        lax.fori_loop(0, rows // ISSUE_UNROLL, body, 0, unroll=4)

    @pl.when(i == 0)
    def _():
        sweep(dest_ref, 0, None)
        sweep(dest_next_ref, 1, None)

    for slot in range(COMBINE_SLOTS):
        @pl.when(i % COMBINE_SLOTS == slot)
        def _():
            for k in range(TOP_K):
                pltpu.make_async_copy(ys_ref.at[pl.ds(0, rows * ROW_TILES)], bufs[slot].at[k], sems.at[slot]).wait()

            @pl.when(i + ahead < n)
            def _():
                sweep(dest_ahead_ref, (slot + ahead) % COMBINE_SLOTS, slot)

            @pl.when(i + ahead >= n)
            def _():
                sweep(None, None, slot)


def _combine(dest, x1, gate_rows, ys):
    t = x1.shape[0]
    steps = t // COMBINE_ROWS
    assert COMBINE_SLOTS == 3 and steps >= COMBINE_SLOTS and ISSUE_UNROLL % SUBLANES == 0
    row = lambda w: pl.BlockSpec((COMBINE_ROWS, w), lambda i: (i, 0))
    dest = dest.reshape(TOP_K, steps, COMBINE_ROWS).transpose(1, 0, 2).reshape(-1)
    slots = lambda nxt: pl.BlockSpec((TOP_K * COMBINE_ROWS,), lambda i: (jnp.minimum(i + nxt, steps - 1),),
                                     memory_space=pltpu.SMEM)
    return pl.pallas_call(
        _combine_kernel,
        grid=(steps,),
        in_specs=[slots(0), slots(1), slots(2), row(D_MODEL), row(TOP_K), pl.BlockSpec(memory_space=pl.ANY)],
        out_specs=row(D_MODEL),
        out_shape=jax.ShapeDtypeStruct(x1.shape, F32),
        scratch_shapes=([pltpu.VMEM((TOP_K, COMBINE_ROWS * ROW_TILES, LANES), F32)] * COMBINE_SLOTS
                        + [pltpu.SemaphoreType.DMA((COMBINE_SLOTS,))]),
        compiler_params=_params("arbitrary"),
        name="combine_rows",
    )(dest, dest, dest, x1, gate_rows, ys)


def _head_blocks(w, width):
    r = w.shape[0]
    w = w.reshape(r, N_HEADS, width)
    return jnp.pad(w, ((0, 0), (0, 0), (0, HEAD_PAD - width))).reshape(r, N_HEADS * HEAD_PAD)


def _lane_row(v, offset=0):
    return jnp.pad(v, (offset, HEAD_PAD - offset - v.shape[0])).reshape(1, HEAD_PAD)


def _swap_rope(w, sign):
    lo, hi = w[..., NOPE_DIM:NOPE_DIM + HALF_ROPE], w[..., NOPE_DIM + HALF_ROPE:QK_DIM]
    return jnp.concatenate([jnp.zeros_like(w[..., :NOPE_DIM]), sign * hi, lo], axis=-1)


def kernel(x, positions, norm_mix_g, w_in, q_latent_g, w_uq, kv_latent_g, w_ukv, q_head_g, k_head_g, conv_dw_w, conv_dw_b, conv_ln_g, conv_ln_b, attn_out_g, conv_out_g, w_out, norm_ffn_g, router_w, router_b, w_gate_up, b_gate_up, w_down, b_down):
    batch, seq, d_model = x.shape
    t = batch * seq
    assert d_model == D_MODEL and seq % max(ATT_Q, CONV_ROWS) == 0
    assert all(t % rows == 0 for rows in (IN_ROWS, MIX_ROWS, DISPATCH_ROWS, COMBINE_ROWS))
    depth = norm_mix_g.shape[0]
    x2 = x.reshape(t, D_MODEL)
    pos = positions.reshape(1, t)
    invf = (1.0 / (ROPE_THETA ** (jnp.arange(0, ROPE_DIM, 2, dtype=F32) / ROPE_DIM))).reshape(HALF_ROPE, 1)
    tri = jnp.triu(jnp.ones((LANES, LANES), BF16), 1)
    o_kv = Q_LORA
    o_pe = o_kv + KV_LORA
    o_u = o_pe + ROPE_DIM

    for l in range(depth):
        wi = w_in[l]
        w_pe = wi[:, o_pe:o_u]
        w_pe_sw = jnp.concatenate([-w_pe[:, HALF_ROPE:], w_pe[:, :HALF_ROPE]], axis=1)
        pe_block = lambda w: jnp.pad(w, ((0, 0), (NOPE_DIM, HEAD_PAD - QK_DIM)))
        win = jnp.concatenate([wi[:, :o_pe], pe_block(w_pe), pe_block(w_pe_sw), wi[:, o_u:]], axis=1).astype(BF16)
        wq = w_uq[l].reshape(Q_LORA, N_HEADS, QK_DIM)
        wuq = _head_blocks(w_uq[l], QK_DIM).astype(BF16)
        wuqs = _head_blocks(_swap_rope(wq, -1.0).reshape(Q_LORA, -1), QK_DIM).astype(BF16)
        wkv = w_ukv[l].reshape(KV_LORA, N_HEADS, NOPE_DIM + V_DIM)
        wuk = _head_blocks(wkv[:, :, :NOPE_DIM].reshape(KV_LORA, -1), NOPE_DIM).astype(BF16)
        wuv = _head_blocks(wkv[:, :, NOPE_DIM:].reshape(KV_LORA, -1), V_DIM).T.astype(BF16)
        ga = attn_out_g[l].reshape(1, -1)
        wo = w_out[l].astype(BF16)
        rwt = router_w[l].T
        rwh = rwt.astype(BF16)
        rw2 = jnp.concatenate([rwh, (rwt - rwh.astype(F32)).astype(BF16)], axis=0)

        q, k, v, y = _input_stage(
            x2, pos, invf, norm_mix_g[l].reshape(1, -1), win, q_latent_g[l].reshape(1, -1), wuq, wuqs,
            kv_latent_g[l].reshape(1, -1), wuk, wuv, _lane_row(q_head_g[l]),
            _lane_row(_swap_rope(q_head_g[l], 1.0)), _lane_row(k_head_g[l]), _lane_row(_swap_rope(k_head_g[l], 1.0)))
        attn = _attention(q, k, v, batch, seq)
        conv = _conv_branch(y, conv_dw_w[l], conv_dw_b[l].reshape(1, -1), conv_ln_g[l].reshape(1, -1),
                            conv_ln_b[l].reshape(1, -1), conv_out_g[l].reshape(1, -1), batch, seq)
        x1, h2, idx, gate, rank, cnt = _mix_stage(
            attn, conv, x2, ga, wo, norm_ffn_g[l].reshape(1, -1), rw2, router_b[l].reshape(-1, 1), tri)

        counts = cnt[:, 0]
        padded = (counts + MOE_STEP - 1) // MOE_STEP * MOE_STEP
        ends = jnp.cumsum(padded)
        starts = ends - padded
        experts = jnp.arange(N_EXPERTS, dtype=jnp.int32)
        dest = rank + jnp.sum(jnp.where(idx[None] == experts[:, None, None], starts[:, None, None], 0), axis=0)
        n_blocks = (t * TOP_K + N_EXPERTS * (MOE_STEP - 1)) // MOE_STEP
        n_active = (ends[-1] // MOE_STEP).astype(jnp.int32)
        step = jnp.arange(n_blocks, dtype=jnp.int32)
        blk = jnp.minimum(step, n_active - 1)
        be = jnp.minimum(jnp.sum((ends[None, :] <= (blk * MOE_STEP)[:, None]).astype(jnp.int32), axis=1),
                         N_EXPERTS - 1)
        real = jnp.clip((starts + counts)[be] - step * MOE_STEP, 0, MOE_STEP)
        passes = jnp.where(step < n_active, -(-real // (MOE_ROWS // 2)), 0).astype(jnp.int32)

        xs = _dispatch(ends.astype(jnp.int32), padded.astype(jnp.int32), dest, h2, n_blocks)
        ys = _moe(be, n_active.reshape(1), passes, xs, w_gate_up[l], b_gate_up[l].reshape(N_EXPERTS, 1, -1),
                  w_down[l], b_down[l].reshape(N_EXPERTS, 1, -1))
        x2 = _combine(dest, x1, gate.T, ys)
    return x2.reshape(batch, seq, D_MODEL)
```

```python
import jax
import jax.numpy as jnp
from jax import lax
from jax.experimental import pallas as pl
from jax.experimental.pallas import tpu as pltpu

D_MODEL = 1024
N_HEADS = 8
NOPE_DIM = 64
ROPE_DIM = 32
QK_DIM = NOPE_DIM + ROPE_DIM
V_DIM = 64
Q_LORA = 384
KV_LORA = 128
CONV_WIDTH = 512
CONV_TAPS = 31
N_EXPERTS = 32
TOP_K = 4
D_FF = 1024
CHUNK = 64
ROPE_THETA = 10000.0
EPS = 1e-6
SWIGLU_ALPHA = 1.702
SWIGLU_LIMIT = 7.0
LOG2_E = 1.4426950408889634

LANES = 128
SUBLANES = 8
HEAD_PAD = LANES
HALF_ROPE = ROPE_DIM // 2
V_ROWS = V_DIM + 2 * SUBLANES
VMEM_LIMIT_BYTES = 56 * 1024 * 1024

IN_ROWS = 1024
ATT_Q = 1024
ATT_K = 512
ATT_QPART = 256
ATT_AHEAD = 4
CONV_ROWS = 256
CONV_HALO = 32
MIX_ROWS = 1024
MIX_GROUPS = 2
DISPATCH_ROWS = 512
MOE_ROWS = 512
MOE_STEP = 2 * MOE_ROWS
DISPATCH_SLOTS = 3
COMBINE_ROWS = 512
COMBINE_SLOTS = 3
ISSUE_UNROLL = 8
DMA_PRIORITIES = 2

F32 = jnp.float32
BF16 = jnp.bfloat16


def _params(*semantics):
    return pltpu.CompilerParams(dimension_semantics=semantics, vmem_limit_bytes=VMEM_LIMIT_BYTES)


def _rms(x, g):
    return x * lax.rsqrt(jnp.mean(x * x, axis=-1, keepdims=True) + EPS) * g


def _full(shape):
    return pl.BlockSpec(shape, lambda *_: (0,) * len(shape))


ROW_TILES = D_MODEL // LANES
PACKED_TILES = ROW_TILES // 2
assert ROW_TILES == SUBLANES


def _load_tile_rows(ref, rows, lead=(), tiles=ROW_TILES, first=0):
    return jnp.concatenate(
        [ref[lead + (pl.ds(first * tiles + c, rows, stride=tiles), slice(None))] for c in range(tiles)], axis=1)


def _store_tile_rows(ref, value, tiles=ROW_TILES, first=0):
    rows = value.shape[0]
    for c in range(tiles):
        ref[pl.ds(first * tiles + c, rows, stride=tiles), :] = value[:, c * LANES:(c + 1) * LANES]


def _tile_row(ref, r, lead=(), tiles=ROW_TILES):
    return ref.at[lead + (pl.ds(pl.multiple_of(r * tiles, tiles), tiles),)]


def _pack_bf16_pairs(x):
    half = x.shape[1] // 2
    bits = lax.bitcast_convert_type(x.astype(BF16).astype(F32), jnp.uint32)
    return (bits[:, :half] >> 16) | bits[:, half:]


def _unpack_bf16_pairs(p):
    left = lax.bitcast_convert_type(p << 16, F32)
    right = lax.bitcast_convert_type(p & jnp.uint32(0xFFFF0000), F32)
    return jnp.concatenate([left, right], axis=1).astype(BF16)


def _in_kernel(x_ref, pos_ref, invf_ref, gmix_ref, win_ref, gql_ref, wuq_ref, wuqs_ref, gkvl_ref, wuk_ref,
               wuv_ref, gq_ref, gqs_ref, gk_ref, gks_ref, q_ref, k_ref, v_ref, y_ref):
    h = _rms(x_ref[...], gmix_ref[...]).astype(BF16)
    p = jnp.dot(h, win_ref[...], preferred_element_type=F32)
    o_kv = Q_LORA
    o_pe = o_kv + KV_LORA
    o_ps = o_pe + HEAD_PAD
    o_a = o_ps + HEAD_PAD
    o_g = o_a + CONV_WIDTH
    cq, ckv, kpe, kpe_sw = p[:, :o_kv], p[:, o_kv:o_pe], p[:, o_pe:o_ps], p[:, o_ps:o_a]
    y_ref[...] = p[:, o_a:o_g] * jax.nn.sigmoid(p[:, o_g:])

    cqn = _rms(cq, gql_ref[...]).astype(BF16)
    q = jnp.dot(cqn, wuq_ref[...], preferred_element_type=F32)
    q_sw = jnp.dot(cqn, wuqs_ref[...], preferred_element_type=F32)
    ckvn = _rms(ckv, gkvl_ref[...]).astype(BF16)
    kn = jnp.dot(ckvn, wuk_ref[...], preferred_element_type=F32)
    wide_row = lax.broadcasted_iota(jnp.int32, (N_HEADS * HEAD_PAD, 1), 0)
    ones_row = jnp.where(wide_row % HEAD_PAD == V_DIM, 1.0, 0.0)
    vt = lax.dot_general(wuv_ref[...], ckvn, (((1,), (1,)), ((), ())), preferred_element_type=F32)
    v_ref[...] = (vt + ones_row).astype(BF16)

    ang = invf_ref[...] * pos_ref[...].astype(F32)
    cos_t, sin_t = jnp.cos(ang), jnp.sin(ang)
    tab = jnp.concatenate([jnp.ones((NOPE_DIM, ang.shape[1]), F32), cos_t, cos_t, sin_t, sin_t], axis=0).T
    lane = lax.broadcasted_iota(jnp.int32, (1, HEAD_PAD), 1)
    ctab = jnp.where(lane < QK_DIM, tab, 0.0)
    stab = jnp.where((lane >= NOPE_DIM) & (lane < QK_DIM), pltpu.roll(tab, QK_DIM, 1), 0.0)
    scale = QK_DIM ** -0.5 * LOG2_E
    cq_tab = ctab * (gq_ref[...] * scale)
    sq_tab = stab * (gqs_ref[...] * scale)
    ck_tab = ctab * gk_ref[...]
    sk_tab = stab * gks_ref[...]
    k_pe = kpe * ck_tab + kpe_sw * sk_tab
    ss_pe = jnp.sum(kpe * kpe, axis=-1, keepdims=True)
    for hd in range(N_HEADS):
        sl = slice(hd * HEAD_PAD, (hd + 1) * HEAD_PAD)
        qh = q[:, sl]
        rq = lax.rsqrt(jnp.sum(qh * qh, axis=-1, keepdims=True) * (1.0 / QK_DIM) + EPS)
        q_ref[:, sl] = ((qh * cq_tab + q_sw[:, sl] * sq_tab) * rq).astype(BF16)
        kh = kn[:, sl]
        rk = lax.rsqrt((jnp.sum(kh * kh, axis=-1, keepdims=True) + ss_pe) * (1.0 / QK_DIM) + EPS)
        k_ref[:, sl] = ((kh * ck_tab + k_pe) * rk).astype(BF16)


def _input_stage(x2, pos, invf, gmix, win, gql, wuq, wuqs, gkvl, wuk, wuv, gq, gqs, gk, gks):
    t = x2.shape[0]
    wide = N_HEADS * HEAD_PAD
    row = lambda w: pl.BlockSpec((IN_ROWS, w), lambda i: (i, 0))
    consts = (invf, gmix, win, gql, wuq, wuqs, gkvl, wuk, wuv, gq, gqs, gk, gks)
    return pl.pallas_call(
        _in_kernel,
        grid=(t // IN_ROWS,),
        in_specs=[row(D_MODEL), pl.BlockSpec((1, IN_ROWS), lambda i: (0, i))] + [_full(a.shape) for a in consts],
        out_specs=[row(wide), row(wide), pl.BlockSpec((wide, IN_ROWS), lambda i: (0, i)), row(CONV_WIDTH)],
        out_shape=[jax.ShapeDtypeStruct((t, wide), BF16), jax.ShapeDtypeStruct((t, wide), BF16),
                   jax.ShapeDtypeStruct((wide, t), BF16), jax.ShapeDtypeStruct((t, CONV_WIDTH), F32)],
        compiler_params=_params("parallel"),
        name="input_stage",
    )(x2, pos, *consts)


def _attn_kernel(q_ref, k_ref, vt_ref, o_ref, *scratch):
    parts, block_parts, blocks = ATT_Q // ATT_QPART, ATT_K // ATT_QPART, ATT_Q // ATT_K
    units = [(hd, part) for hd in range(N_HEADS) for part in range(parts)]
    m_refs, acc_refs = scratch[:len(units)], scratch[len(units):]
    i = pl.program_id(1)
    dn = (((1,), (1,)), ((), ()))

    def step(r0, block, diagonal):
        todo = [u for u, (_, part) in enumerate(units) if block is None or part // block_parts == block]
        old = None if diagonal else {u: (m_refs[u][...], acc_refs[u][...]) for u in todo}
        new = {}

        def keys(part):
            return (part % block_parts + 1) * ATT_QPART if diagonal else ATT_K

        def scores(u):
            hd, part = units[u]
            sl = slice(hd * HEAD_PAD, (hd + 1) * HEAD_PAD)
            return lax.dot_general(k_ref[pl.ds(r0, keys(part)), sl],
                                   q_ref[part * ATT_QPART:(part + 1) * ATT_QPART, sl], dn,
                                   preferred_element_type=F32)

        ahead = [scores(u) for u in todo[:ATT_AHEAD]]
        for n, u in enumerate(todo):
            hd, part = units[u]
            s = ahead.pop(0)
            if n + ATT_AHEAD < len(todo):
                ahead.append(scores(todo[n + ATT_AHEAD]))
            if diagonal:
                kc = lax.broadcasted_iota(jnp.int32, s.shape, 0) // CHUNK
                qc = (lax.broadcasted_iota(jnp.int32, s.shape, 1) + part % block_parts * ATT_QPART) // CHUNK
                s = jnp.where(kc <= qc, s, -jnp.inf)
            m_new = jnp.max(s, axis=0, keepdims=True)
            if not diagonal:
                m_new = jnp.maximum(old[u][0], m_new)
            acc = jnp.dot(vt_ref[hd * HEAD_PAD:hd * HEAD_PAD + V_ROWS, pl.ds(r0, keys(part))],
                          jnp.exp2(s - m_new).astype(BF16), preferred_element_type=F32)
            if not diagonal:
                acc = jnp.exp2(old[u][0] - m_new) * old[u][1] + acc
            new[u] = (m_new, acc)
        for u in todo:
            m_refs[u][...], acc_refs[u][...] = new[u]

    tile0 = pl.multiple_of(i * ATT_Q, ATT_Q)
    for block in range(blocks):
        step(tile0 + block * ATT_K, block, True)
        for earlier in range(block):
            step(tile0 + earlier * ATT_K, block, False)

    def body(j, _):
        step(pl.multiple_of(j * ATT_K, ATT_K), None, False)
        return 0

    lax.fori_loop(0, i * blocks, body, 0)
    for pair in range(N_HEADS // 2):
        for part in range(parts):
            halves = []
            for hd in (2 * pair, 2 * pair + 1):
                acc = acc_refs[hd * parts + part][...]
                halves.append(acc[:V_DIM, :] / acc[V_DIM:V_DIM + 1, :])
            o_ref[part * ATT_QPART:(part + 1) * ATT_QPART, pair * LANES:(pair + 1) * LANES] = (
                jnp.concatenate(halves, axis=0).T.astype(BF16))


def _attention(q, k, vt, batch, seq):
    assert seq % ATT_Q == 0 and ATT_Q % ATT_K == 0 and ATT_K % ATT_QPART == 0 and ATT_QPART % CHUNK == 0
    nq = seq // ATT_Q
    chains = N_HEADS * (ATT_Q // ATT_QPART)
    wide = N_HEADS * HEAD_PAD
    qspec = pl.BlockSpec((ATT_Q, wide), lambda b, i: (b * nq + i, 0))
    return pl.pallas_call(
        _attn_kernel,
        grid=(batch, nq),
        in_specs=[qspec, pl.BlockSpec((seq, wide), lambda b, i: (b, 0)),
                  pl.BlockSpec((wide, seq), lambda b, i: (0, b))],
        out_specs=pl.BlockSpec((ATT_Q, N_HEADS * V_DIM), lambda b, i: (b * nq + i, 0)),
        out_shape=jax.ShapeDtypeStruct((q.shape[0], N_HEADS * V_DIM), BF16),
        scratch_shapes=([pltpu.VMEM((1, ATT_QPART), F32)] * chains + [pltpu.VMEM((V_ROWS, ATT_QPART), F32)] * chains),
        compiler_params=_params("parallel", "arbitrary"),
        name="attention",
    )(q, k, vt)


def _conv_kernel(y_ref, w_ref, b_ref, lng_ref, lnb_ref, og_ref, o_ref, pad_ref, phase_ref):
    seq = y_ref.shape[0]
    pad_ref[0:CONV_HALO, :] = jnp.zeros((CONV_HALO, CONV_WIDTH), F32)
    pad_ref[CONV_HALO:, :] = y_ref[...]
    first = CONV_HALO - (CONV_TAPS - 1)

    def body(i, _):
        r0 = pl.multiple_of(i * CONV_ROWS, CONV_ROWS)
        win = pad_ref[pl.ds(r0, CONV_ROWS + CONV_HALO), :]
        acc = jnp.zeros((CONV_ROWS, CONV_WIDTH), F32)
        for s in range(SUBLANES):
            offs = [o for o in range(first, first + CONV_TAPS) if o % SUBLANES == s]
            span = max(offs) - s + CONV_ROWS
            if s:
                phase_ref[s, 0:span, :] = win[s:s + span, :]
            for o in offs:
                rows = (pad_ref[pl.ds(pl.multiple_of(r0 + o, SUBLANES), CONV_ROWS), :] if s == 0
                        else phase_ref[s, o - s:o - s + CONV_ROWS, :])
                acc = acc + w_ref[o - first:o - first + 1, :] * rows
        acc = acc + b_ref[...]
        xc = acc - jnp.mean(acc, axis=-1, keepdims=True)
        ln = xc * lax.rsqrt(jnp.mean(xc * xc, axis=-1, keepdims=True) + EPS) * lng_ref[...] + lnb_ref[...]
        z = ln * jax.nn.sigmoid(ln)
        o_ref[pl.ds(r0, CONV_ROWS), :] = _rms(z, og_ref[...]).astype(BF16)
        return 0

    lax.fori_loop(0, seq // CONV_ROWS, body, 0)


def _conv_branch(y, w, b, lng, lnb, og, batch, seq):
    spec = pl.BlockSpec((seq, CONV_WIDTH), lambda bi: (bi, 0))
    return pl.pallas_call(
        _conv_kernel,
        grid=(batch,),
        in_specs=[spec, _full(w.shape), _full(b.shape), _full(lng.shape), _full(lnb.shape), _full(og.shape)],
        out_specs=spec,
        out_shape=jax.ShapeDtypeStruct(y.shape, BF16),
        scratch_shapes=[pltpu.VMEM((seq + CONV_HALO, CONV_WIDTH), F32),
                        pltpu.VMEM((SUBLANES, CONV_ROWS + CONV_HALO, CONV_WIDTH), F32)],
        compiler_params=_params("parallel"),
        name="conv_branch",
    )(y, w, b, lng, lnb, og)


def _mix_kernel(attn_ref, conv_ref, x_ref, ga_ref, wo_ref, gffn_ref, rw2_ref, rb_ref,
                tri_ref, x1_ref, h2_ref, idx_ref, gate_ref, rank_ref, cnt_ref, base_ref):
    @pl.when(pl.program_id(0) == 0)
    def _():
        base_ref[...] = jnp.zeros(base_ref.shape, F32)

    size = MIX_ROWS // MIX_GROUPS
    groups = [slice(g * size, (g + 1) * size) for g in range(MIX_GROUPS)]
    dn = (((1,), (1,)), ((), ()))

    def project(rs):
        an = _rms(attn_ref[rs, :].astype(F32), ga_ref[...])
        mixed = jnp.concatenate([an.astype(BF16), conv_ref[rs, :]], axis=1)
        return x_ref[rs, :] + jnp.dot(mixed, wo_ref[...], preferred_element_type=F32)

    def norm_and_logits(g, rs, x1):
        x1_ref[rs, :] = x1
        h2 = _rms(x1, gffn_ref[...])
        _store_tile_rows(h2_ref, _pack_bf16_pairs(h2), PACKED_TILES, first=g * size)
        hi = h2.astype(BF16)
        lo = (h2 - hi.astype(F32)).astype(BF16)
        both = lax.dot_general(rw2_ref[...], hi, dn, preferred_element_type=F32)
        return (both[:N_EXPERTS] + both[N_EXPERTS:]
                + lax.dot_general(rw2_ref[:N_EXPERTS, :], lo, dn, preferred_element_type=F32)) + rb_ref[...]

    def top_k(rs, logits):
        eidx = lax.broadcasted_iota(jnp.int32, logits.shape, 0).astype(F32)
        work = logits
        sels, vals = [], []
        for k in range(TOP_K):
            mx = jnp.max(work, axis=0, keepdims=True)
            first = jnp.min(jnp.where(work == mx, eidx, float(N_EXPERTS)), axis=0, keepdims=True)
            sel = eidx == first
            work = jnp.where(sel, -jnp.inf, work)
            sels.append(sel)
            vals.append(mx)
            idx_ref[k:k + 1, rs] = first.astype(jnp.int32)
        exps = [jnp.exp(v - vals[0]) for v in vals]
        denom = exps[0] + exps[1] + exps[2] + exps[3]
        for k in range(TOP_K):
            gate_ref[k:k + 1, rs] = exps[k] / denom
        return sels

    x1s, logits, sels = {}, {}, {}
    for s in range(MIX_GROUPS + 2):
        if s < MIX_GROUPS:
            x1s[s] = project(groups[s])
        if 0 <= s - 1 < MIX_GROUPS:
            logits[s - 1] = norm_and_logits(s - 1, groups[s - 1], x1s.pop(s - 1))
        if 0 <= s - 2 < MIX_GROUPS:
            sels[s - 2] = top_k(groups[s - 2], logits.pop(s - 2))
    sels = [sels[g] for g in range(MIX_GROUPS)]

    sels = [jnp.concatenate([s[k] for s in sels], axis=1) for k in range(TOP_K)]
    member = jnp.where(sels[0] | sels[1] | sels[2] | sels[3], 1.0, 0.0)
    blocks = [member[:, b * LANES:(b + 1) * LANES] for b in range(MIX_ROWS // LANES)]
    inner = jnp.dot(jnp.concatenate(blocks, axis=0).astype(BF16), tri_ref[...], preferred_element_type=F32)
    offset = base_ref[:, 0:1]
    pieces = []
    for b, blk in enumerate(blocks):
        pieces.append(inner[b * N_EXPERTS:(b + 1) * N_EXPERTS, :] + offset)
        offset = offset + jnp.sum(blk, axis=1, keepdims=True)
    posn = jnp.concatenate(pieces, axis=1)
    for k in range(TOP_K):
        rank_ref[k:k + 1, :] = jnp.sum(jnp.where(sels[k], posn, 0.0), axis=0, keepdims=True).astype(jnp.int32)
    base_ref[...] = jnp.broadcast_to(offset, base_ref.shape)
    cnt_ref[...] = base_ref[...].astype(jnp.int32)


def _mix_stage(attn, conv, x2, ga, wo, gffn, rw2, rb, tri):
    t = x2.shape[0]
    proj = lambda i: (i, 0)
    row = lambda w: pl.BlockSpec((MIX_ROWS, w), proj)
    col = pl.BlockSpec((TOP_K, MIX_ROWS), lambda i: (0, i))
    return pl.pallas_call(
        _mix_kernel,
        grid=(t // MIX_ROWS,),
        in_specs=[row(attn.shape[1]), row(CONV_WIDTH), row(D_MODEL), _full(ga.shape), _full(wo.shape),
                  _full(gffn.shape), _full(rw2.shape), _full(rb.shape), _full(tri.shape)],
        out_specs=[row(D_MODEL), pl.BlockSpec((MIX_ROWS * PACKED_TILES, LANES), proj), col, col, col,
                   _full((N_EXPERTS, LANES))],
        out_shape=[jax.ShapeDtypeStruct((t, D_MODEL), F32),
                   jax.ShapeDtypeStruct((t * PACKED_TILES, LANES), jnp.uint32),
                   jax.ShapeDtypeStruct((TOP_K, t), jnp.int32), jax.ShapeDtypeStruct((TOP_K, t), F32),
                   jax.ShapeDtypeStruct((TOP_K, t), jnp.int32),
                   jax.ShapeDtypeStruct((N_EXPERTS, LANES), jnp.int32)],
        scratch_shapes=[pltpu.VMEM((N_EXPERTS, LANES), F32)],
        compiler_params=_params("arbitrary"),
        name="mix_router",
    )(attn, conv, x2, ga, wo, gffn, rw2, rb, tri)


def _dispatch_kernel(ends_ref, padded_ref, dest_ref, h_ref, xs_ref, zero_ref, hbuf_ref, sems, hsems, zsem):
    tiles = PACKED_TILES
    rows = DISPATCH_ROWS
    block = MOE_STEP * tiles
    n_blocks = xs_ref.shape[0] // block
    i, n = pl.program_id(0), pl.num_programs(0)

    @pl.when(i == 0)
    def _():
        zero_ref[...] = jnp.zeros(zero_ref.shape, zero_ref.dtype)

        def zero_copy(r0):
            return pltpu.make_async_copy(
                zero_ref, xs_ref.at[pl.ds(pl.multiple_of(r0 * tiles, block), block)], zsem)

        for start in (True, False):
            for e in range(N_EXPERTS):
                @pl.when(padded_ref[e] > 0)
                def _():
                    cp = zero_copy(ends_ref[e] - MOE_STEP)
                    cp.start() if start else cp.wait()

            def tail(b, _):
                cp = zero_copy(b * MOE_STEP)
                cp.start() if start else cp.wait()
                return 0

            lax.fori_loop(ends_ref[N_EXPERTS - 1] // MOE_STEP, n_blocks, tail, 0)

    def fetch(step, slot):
        first = pl.multiple_of(step * (rows * tiles), rows * tiles)
        return pltpu.make_async_copy(h_ref.at[pl.ds(first, rows * tiles)], hbuf_ref.at[slot], hsems.at[slot])

    def drain(slot):
        for _ in range(TOP_K):
            pltpu.make_async_copy(hbuf_ref.at[slot], xs_ref.at[pl.ds(0, rows * tiles)], sems.at[slot]).wait()

    @pl.when(i == 0)
    def _():
        fetch(0, 0).start()
        fetch(1, 1).start()

    slot = i % DISPATCH_SLOTS
    fetch(i, slot).wait()

    def issue(t, _):
        for k in range(TOP_K):
            pltpu.make_async_copy(_tile_row(hbuf_ref, t, (slot,), tiles), _tile_row(xs_ref, dest_ref[k, t], tiles=tiles),
                                  sems.at[slot]).start(priority=k % DMA_PRIORITIES)
        return 0

    lax.fori_loop(0, rows, issue, 0, unroll=ISSUE_UNROLL)

    @pl.when(i > 0)
    def _():
        drain((i - 1) % DISPATCH_SLOTS)

    @pl.when(i + 2 < n)
    def _():
        fetch(i + 2, (i + 2) % DISPATCH_SLOTS).start()

    @pl.when(i == n - 1)
    def _():
        drain(slot)


def _dispatch(ends, padded, dest, h2, n_blocks):
    t = h2.shape[0] // PACKED_TILES
    assert DISPATCH_SLOTS == 3 and t // DISPATCH_ROWS >= DISPATCH_SLOTS
    grid_spec = pltpu.PrefetchScalarGridSpec(
        num_scalar_prefetch=2,
        grid=(t // DISPATCH_ROWS,),
        in_specs=[pl.BlockSpec((TOP_K, DISPATCH_ROWS), lambda i, *_: (0, i), memory_space=pltpu.SMEM),
                  pl.BlockSpec(memory_space=pl.ANY)],
        out_specs=pl.BlockSpec(memory_space=pl.ANY),
        scratch_shapes=[pltpu.VMEM((MOE_STEP * PACKED_TILES, LANES), h2.dtype),
                        pltpu.VMEM((DISPATCH_SLOTS, DISPATCH_ROWS * PACKED_TILES, LANES), h2.dtype),
                        pltpu.SemaphoreType.DMA((DISPATCH_SLOTS,)), pltpu.SemaphoreType.DMA((DISPATCH_SLOTS,)),
                        pltpu.SemaphoreType.DMA(())],
    )
    return pl.pallas_call(
        _dispatch_kernel,
        grid_spec=grid_spec,
        out_shape=jax.ShapeDtypeStruct((n_blocks * MOE_STEP * PACKED_TILES, LANES), h2.dtype),
        compiler_params=_params("arbitrary"),
        name="dispatch_rows",
    )(ends, padded, dest, h2)


def _moe_kernel(be_ref, nact_ref, passes_ref, xs_ref, wgu_ref, bgu_ref, wd_ref, bd_ref, ys_ref, wgu_bf, wd_bf):
    i = pl.program_id(0)
    half = MOE_ROWS // 2

    full = MOE_STEP // half
    fresh = (passes_ref[i] > 0) & ((i == 0) | (be_ref[i] != be_ref[jnp.maximum(i - 1, 0)]))

    def cast():
        wgu_bf[...] = wgu_ref[0].astype(BF16)
        wd_bf[...] = wd_ref[0].astype(BF16)

    pl.when(fresh & (passes_ref[i] != full))(cast)

    def blank(first, rows):
        ys_ref[first * ROW_TILES:(first + rows) * ROW_TILES, :] = jnp.zeros((rows * ROW_TILES, LANES), F32)

    def mlp(first, rows):
        x = _unpack_bf16_pairs(_load_tile_rows(xs_ref, rows, tiles=PACKED_TILES, first=first))
        gu = jnp.dot(x, wgu_bf[...], preferred_element_type=F32) + bgu_ref[0]
        gate = jnp.minimum(gu[:, :D_FF], SWIGLU_LIMIT)
        up = jnp.clip(gu[:, D_FF:], -SWIGLU_LIMIT, SWIGLU_LIMIT)
        mid = (up + 1.0) * (gate * jax.nn.sigmoid(gate * SWIGLU_ALPHA))
        y = jnp.dot(mid.astype(BF16), wd_bf[...], preferred_element_type=F32) + bd_ref[0]
        _store_tile_rows(ys_ref, y, first=first)

    def full_step(with_cast):
        if with_cast:
            cast()
        for p in range(MOE_STEP // MOE_ROWS):
            mlp(p * MOE_ROWS, MOE_ROWS)

    pl.when((passes_ref[i] == full) & fresh)(lambda: full_step(True))
    pl.when((passes_ref[i] == full) & jnp.logical_not(fresh))(lambda: full_step(False))

    for p in range(MOE_STEP // MOE_ROWS):
        first = p * MOE_ROWS

        if 2 * p + 2 < full:
            @pl.when((passes_ref[i] >= 2 * p + 2) & (passes_ref[i] < full))
            def _():
                mlp(first, MOE_ROWS)

        @pl.when(passes_ref[i] == 2 * p + 1)
        def _():
            mlp(first, half)
            blank(first + half, half)

        @pl.when(passes_ref[i] <= 2 * p)
        def _():
            blank(first, MOE_ROWS)


def _moe(block_expert, n_active, passes, xs, wgu, bgu, wd, bd):
    block = MOE_STEP * ROW_TILES
    n_blocks = xs.shape[0] // (MOE_STEP * PACKED_TILES)
    grid_spec = pltpu.PrefetchScalarGridSpec(
        num_scalar_prefetch=3,
        grid=(n_blocks,),
        in_specs=[
            pl.BlockSpec((MOE_STEP * PACKED_TILES, LANES), lambda i, be, na, ps: (jnp.minimum(i, na[0] - 1), 0)),
            pl.BlockSpec((1, D_MODEL, 2 * D_FF), lambda i, be, na, ps: (be[i], 0, 0)),
            pl.BlockSpec((1, 1, 2 * D_FF), lambda i, be, na, ps: (be[i], 0, 0)),
            pl.BlockSpec((1, D_FF, D_MODEL), lambda i, be, na, ps: (be[i], 0, 0)),
            pl.BlockSpec((1, 1, D_MODEL), lambda i, be, na, ps: (be[i], 0, 0)),
        ],
        out_specs=pl.BlockSpec((block, LANES), lambda i, be, na, ps: (i, 0)),
        scratch_shapes=[pltpu.VMEM((D_MODEL, 2 * D_FF), BF16), pltpu.VMEM((D_FF, D_MODEL), BF16)],
    )
    return pl.pallas_call(
        _moe_kernel,
        grid_spec=grid_spec,
        out_shape=jax.ShapeDtypeStruct((n_blocks * block, LANES), F32),
        compiler_params=_params("arbitrary"),
        name="expert_mlp",
    )(block_expert, n_active, passes, xs, wgu, bgu, wd, bd)


def _combine_kernel(dest_ref, dest_next_ref, dest_ahead_ref, x1_ref, gate_ref, ys_ref, o_ref, *scratch):
    bufs, sems = scratch[:COMBINE_SLOTS], scratch[COMBINE_SLOTS]
    i, n = pl.program_id(0), pl.num_programs(0)
    rows = x1_ref.shape[0]
    ahead = COMBINE_SLOTS - 1

    def issue(idx_ref, to, t):
        for k in range(TOP_K):
            pltpu.make_async_copy(_tile_row(ys_ref, idx_ref[k * rows + t]), _tile_row(bufs[to], t, (k,)),
                                  sems.at[to]).start(priority=k % DMA_PRIORITIES)

    def consume(slot, t0):
        sl = pl.ds(t0, ISSUE_UNROLL)
        acc = x1_ref[sl, :]
        for k in range(TOP_K):
            got = bufs[slot].at[k, pl.ds(pl.multiple_of(t0 * ROW_TILES, ISSUE_UNROLL * ROW_TILES),
                                         ISSUE_UNROLL * ROW_TILES)]
            acc = acc + gate_ref[sl, k:k + 1] * _load_tile_rows(got, ISSUE_UNROLL)
        o_ref[sl, :] = acc

    def sweep(idx_ref, to, slot):
        def body(j, _):
            t0 = pl.multiple_of(j * ISSUE_UNROLL, ISSUE_UNROLL)
            if to is not None:
                for t in range(ISSUE_UNROLL):
                    issue(idx_ref, to, t0 + t)
            if slot is not None:
                consume(slot, t0)
            return 0

        lax.fori_loop(0, rows // ISSUE_UNROLL, body, 0, unroll=4)

    @pl.when(i == 0)
    def _():
        sweep(dest_ref, 0, None)
        sweep(dest_next_ref, 1, None)

    for slot in range(COMBINE_SLOTS):
        @pl.when(i % COMBINE_SLOTS == slot)
        def _():
            for k in range(TOP_K):
                pltpu.make_async_copy(ys_ref.at[pl.ds(0, rows * ROW_TILES)], bufs[slot].at[k], sems.at[slot]).wait()

            @pl.when(i + ahead < n)
            def _():
                sweep(dest_ahead_ref, (slot + ahead) % COMBINE_SLOTS, slot)

            @pl.when(i + ahead >= n)
            def _():
                sweep(None, None, slot)


def _combine(dest, x1, gate_rows, ys):
    t = x1.shape[0]
    steps = t // COMBINE_ROWS
    assert COMBINE_SLOTS == 3 and steps >= COMBINE_SLOTS and ISSUE_UNROLL % SUBLANES == 0
    row = lambda w: pl.BlockSpec((COMBINE_ROWS, w), lambda i: (i, 0))
    dest = dest.reshape(TOP_K, steps, COMBINE_ROWS).transpose(1, 0, 2).reshape(-1)
    slots = lambda nxt: pl.BlockSpec((TOP_K * COMBINE_ROWS,), lambda i: (jnp.minimum(i + nxt, steps - 1),),
                                     memory_space=pltpu.SMEM)
    return pl.pallas_call(
        _combine_kernel,
        grid=(steps,),
        in_specs=[slots(0), slots(1), slots(2), row(D_MODEL), row(TOP_K), pl.BlockSpec(memory_space=pl.ANY)],
        out_specs=row(D_MODEL),
        out_shape=jax.ShapeDtypeStruct(x1.shape, F32),
        scratch_shapes=([pltpu.VMEM((TOP_K, COMBINE_ROWS * ROW_TILES, LANES), F32)] * COMBINE_SLOTS
                        + [pltpu.SemaphoreType.DMA((COMBINE_SLOTS,))]),
        compiler_params=_params("arbitrary"),
        name="combine_rows",
    )(dest, dest, dest, x1, gate_rows, ys)


def _head_blocks(w, width):
    r = w.shape[0]
    w = w.reshape(r, N_HEADS, width)
    return jnp.pad(w, ((0, 0), (0, 0), (0, HEAD_PAD - width))).reshape(r, N_HEADS * HEAD_PAD)


def _lane_row(v, offset=0):
    return jnp.pad(v, (offset, HEAD_PAD - offset - v.shape[0])).reshape(1, HEAD_PAD)


def _swap_rope(w, sign):
    lo, hi = w[..., NOPE_DIM:NOPE_DIM + HALF_ROPE], w[..., NOPE_DIM + HALF_ROPE:QK_DIM]
    return jnp.concatenate([jnp.zeros_like(w[..., :NOPE_DIM]), sign * hi, lo], axis=-1)


def kernel(x, positions, norm_mix_g, w_in, q_latent_g, w_uq, kv_latent_g, w_ukv, q_head_g, k_head_g, conv_dw_w, conv_dw_b, conv_ln_g, conv_ln_b, attn_out_g, conv_out_g, w_out, norm_ffn_g, router_w, router_b, w_gate_up, b_gate_up, w_down, b_down):
    batch, seq, d_model = x.shape
    t = batch * seq
    assert d_model == D_MODEL and seq % max(ATT_Q, CONV_ROWS) == 0
    assert all(t % rows == 0 for rows in (IN_ROWS, MIX_ROWS, DISPATCH_ROWS, COMBINE_ROWS))
    depth = norm_mix_g.shape[0]
    x2 = x.reshape(t, D_MODEL)
    pos = positions.reshape(1, t)
    invf = (1.0 / (ROPE_THETA ** (jnp.arange(0, ROPE_DIM, 2, dtype=F32) / ROPE_DIM))).reshape(HALF_ROPE, 1)
    tri = jnp.triu(jnp.ones((LANES, LANES), BF16), 1)
    o_kv = Q_LORA
    o_pe = o_kv + KV_LORA
    o_u = o_pe + ROPE_DIM

    for l in range(depth):
        wi = w_in[l]
        w_pe = wi[:, o_pe:o_u]
        w_pe_sw = jnp.concatenate([-w_pe[:, HALF_ROPE:], w_pe[:, :HALF_ROPE]], axis=1)
        pe_block = lambda w: jnp.pad(w, ((0, 0), (NOPE_DIM, HEAD_PAD - QK_DIM)))
        win = jnp.concatenate([wi[:, :o_pe], pe_block(w_pe), pe_block(w_pe_sw), wi[:, o_u:]], axis=1).astype(BF16)
        wq = w_uq[l].reshape(Q_LORA, N_HEADS, QK_DIM)
        wuq = _head_blocks(w_uq[l], QK_DIM).astype(BF16)
        wuqs = _head_blocks(_swap_rope(wq, -1.0).reshape(Q_LORA, -1), QK_DIM).astype(BF16)
        wkv = w_ukv[l].reshape(KV_LORA, N_HEADS, NOPE_DIM + V_DIM)
        wuk = _head_blocks(wkv[:, :, :NOPE_DIM].reshape(KV_LORA, -1), NOPE_DIM).astype(BF16)
        wuv = _head_blocks(wkv[:, :, NOPE_DIM:].reshape(KV_LORA, -1), V_DIM).T.astype(BF16)
        ga = attn_out_g[l].reshape(1, -1)
        wo = w_out[l].astype(BF16)
        rwt = router_w[l].T
        rwh = rwt.astype(BF16)
        rw2 = jnp.concatenate([rwh, (rwt - rwh.astype(F32)).astype(BF16)], axis=0)

        q, k, v, y = _input_stage(
            x2, pos, invf, norm_mix_g[l].reshape(1, -1), win, q_latent_g[l].reshape(1, -1), wuq, wuqs,
            kv_latent_g[l].reshape(1, -1), wuk, wuv, _lane_row(q_head_g[l]),
            _lane_row(_swap_rope(q_head_g[l], 1.0)), _lane_row(k_head_g[l]), _lane_row(_swap_rope(k_head_g[l], 1.0)))
        attn = _attention(q, k, v, batch, seq)
        conv = _conv_branch(y, conv_dw_w[l], conv_dw_b[l].reshape(1, -1), conv_ln_g[l].reshape(1, -1),
                            conv_ln_b[l].reshape(1, -1), conv_out_g[l].reshape(1, -1), batch, seq)
        x1, h2, idx, gate, rank, cnt = _mix_stage(
            attn, conv, x2, ga, wo, norm_ffn_g[l].reshape(1, -1), rw2, router_b[l].reshape(-1, 1), tri)

        counts = cnt[:, 0]
        padded = (counts + MOE_STEP - 1) // MOE_STEP * MOE_STEP
        ends = jnp.cumsum(padded)
        starts = ends - padded
        experts = jnp.arange(N_EXPERTS, dtype=jnp.int32)
        dest = rank + jnp.sum(jnp.where(idx[None] == experts[:, None, None], starts[:, None, None], 0), axis=0)
        n_blocks = (t * TOP_K + N_EXPERTS * (MOE_STEP - 1)) // MOE_STEP
        n_active = (ends[-1] // MOE_STEP).astype(jnp.int32)
        step = jnp.arange(n_blocks, dtype=jnp.int32)
        blk = jnp.minimum(step, n_active - 1)
        be = jnp.minimum(jnp.sum((ends[None, :] <= (blk * MOE_STEP)[:, None]).astype(jnp.int32), axis=1),
                         N_EXPERTS - 1)
        real = jnp.clip((starts + counts)[be] - step * MOE_STEP, 0, MOE_STEP)
        passes = jnp.where(step < n_active, -(-real // (MOE_ROWS // 2)), 0).astype(jnp.int32)

        xs = _dispatch(ends.astype(jnp.int32), padded.astype(jnp.int32), dest, h2, n_blocks)
        ys = _moe(be, n_active.reshape(1), passes, xs, w_gate_up[l], b_gate_up[l].reshape(N_EXPERTS, 1, -1),
                  w_down[l], b_down[l].reshape(N_EXPERTS, 1, -1))
        x2 = _combine(dest, x1, gate.T, ys)
    return x2.reshape(batch, seq, D_MODEL)
```
